```python
import math
import jax, jax.numpy as jnp
from jax import lax
import numpy as np

D_MODEL = 1024
BATCH = 8
SEQ = 2048
DEPTH = 2

GRID_W = 64
CTX_LEN = 256
N_EVEN = (DEPTH + 1) // 2
N_ODD = DEPTH // 2

DIFF_HEADS = 4
DIFF_HEAD_DIM = 64
DIFF_V_DIM = 2 * DIFF_HEAD_DIM
DIFF_QK_W = DIFF_HEADS * 2 * DIFF_HEAD_DIM
DIFF_V_W = DIFF_HEADS * DIFF_V_DIM
DIFF_IN_W = 2 * DIFF_QK_W + DIFF_V_W
HGRN_HEADS = 4
HGRN_K_DIM = 128
HGRN_V_DIM = 128
HGRN_CHUNK = 16
HGRN_KW = HGRN_HEADS * HGRN_K_DIM
HGRN_VW = HGRN_HEADS * HGRN_V_DIM
HGRN_IN_W = 3 * HGRN_KW + 2 * HGRN_VW
EVEN_IN_W = DIFF_IN_W + HGRN_IN_W
EVEN_OUT_W = DIFF_V_W + HGRN_VW
MLA_HEADS = 8
MLA_NOPE = 128
MLA_ROPE = 64
MLA_V = 128
MLA_Q_LORA = 384
MLA_KV_LORA = 256
ODD_IN_W = MLA_Q_LORA + MLA_KV_LORA + MLA_ROPE
ODD_OUT_W = MLA_HEADS * MLA_V
N_EXPERTS = 16
N_GROUPS = 4
EXPERTS_PER_GROUP = N_EXPERTS // N_GROUPS
TOP_K = 2
EXPERT_FF = 512
SHARED_FF = 512

ROPE_BASE = 10000.0
Q_BLOCK = 128
EPS = 1e-6

kernel_name = 'hybrid_diffattn_hgrn2_mla_grouped_moe_dit'


def rmsnorm(x, gain):
    xf = x.astype(jnp.float32)
    y = xf * lax.rsqrt(jnp.mean(xf * xf, axis=-1, keepdims=True) + EPS)
    return y.astype(x.dtype) * gain


def axial_rope_tables(n_tokens, rot_dim):
    rows = n_tokens // GRID_W
    row = jnp.repeat(jnp.arange(rows, dtype=jnp.int32), GRID_W)
    col = jnp.tile(jnp.arange(GRID_W, dtype=jnp.int32), rows)
    axis_dim = rot_dim // 2
    inv_freq = ROPE_BASE ** (-jnp.arange(0, axis_dim, 2, dtype=jnp.float32) / axis_dim)
    ang_r = row.astype(jnp.float32)[:, None] * inv_freq
    ang_c = col.astype(jnp.float32)[:, None] * inv_freq
    return (jnp.cos(ang_r), jnp.sin(ang_r), jnp.cos(ang_c), jnp.sin(ang_c))


def _rope_1d(x, cos, sin):
    x1, x2 = jnp.split(x, 2, axis=-1)
    cos = cos.astype(x.dtype)
    sin = sin.astype(x.dtype)
    return jnp.concatenate([x1 * cos - x2 * sin, x1 * sin + x2 * cos], axis=-1)


def apply_axial_rope(x, tables):
    cos_r, sin_r, cos_c, sin_c = tables
    xr, xc = jnp.split(x, 2, axis=-1)
    return jnp.concatenate([_rope_1d(xr, cos_r, sin_r), _rope_1d(xc, cos_c, sin_c)], axis=-1)


def sweep_query_blocks(fn, qs, axis):
    n_q = qs[0].shape[axis]
    n_blocks = n_q // Q_BLOCK

    def split(q):
        q = q.reshape(q.shape[:axis] + (n_blocks, Q_BLOCK) + q.shape[axis + 1:])
        return jnp.moveaxis(q, axis, 0)

    out = lax.map(fn, tuple(split(q) for q in qs))
    out = jnp.moveaxis(out, 0, 2)
    return out.reshape(out.shape[:2] + (n_q, out.shape[-1]))


def merge_heads(o):
    b, h, t, dv = o.shape
    return o.transpose(0, 2, 1, 3).reshape(b, t, h * dv)


def diff_heads(p, q_gain, k_gain):
    b, t, _ = p.shape
    q, k, v = jnp.split(p, [DIFF_QK_W, 2 * DIFF_QK_W], axis=-1)
    q = rmsnorm(q.reshape(b, t, DIFF_HEADS, 2, DIFF_HEAD_DIM), q_gain).transpose(0, 2, 3, 1, 4)
    k = rmsnorm(k.reshape(b, t, DIFF_HEADS, 2, DIFF_HEAD_DIM), k_gain).transpose(0, 2, 3, 1, 4)
    v = v.reshape(b, t, DIFF_HEADS, DIFF_V_DIM).transpose(0, 2, 1, 3)
    return q, k, v


def diff_attention(q, k, v, lam):
    scale = DIFF_HEAD_DIM ** -0.5

    def block(qs):
        (qb,) = qs
        s = jnp.einsum('bhcqd,bhckd->bhcqk', qb, k).astype(jnp.float32) * scale
        p = jax.nn.softmax(s, axis=-1)
        a = p[:, :, 0] - lam * p[:, :, 1]
        return jnp.einsum('bhqk,bhkv->bhqv', a.astype(v.dtype), v)

    return sweep_query_blocks(block, (q,), axis=3)


def diff_output(o, subln, lam_init):
    return merge_heads(rmsnorm(o, subln) * (1.0 - lam_init))


def gla_chunked(q, k, v, log_f, s0):
    b, h, t, dk = q.shape
    dv = v.shape[-1]
    c = HGRN_CHUNK
    n = t // c
    f32 = jnp.float32
    q = q.astype(f32).reshape(b, h, n, c, dk) * (dk ** -0.5)
    k = k.astype(f32).reshape(b, h, n, c, dk)
    v = v.astype(f32).reshape(b, h, n, c, dv)
    cum = jnp.cumsum(log_f.astype(f32).reshape(b, h, n, c, dk), axis=3)
    lower = jnp.tril(jnp.ones((c, c), dtype=bool))[:, :, None]
    decay = jnp.exp(jnp.where(lower, cum[:, :, :, :, None, :] - cum[:, :, :, None, :, :], -jnp.inf))
    scores = jnp.einsum('bhntk,bhnsk,bhntsk->bhnts', q, k, decay)
    o_intra = jnp.einsum('bhnts,bhnsv->bhntv', scores, v)
    last = cum[:, :, :, -1:, :]
    d_state = jnp.einsum('bhnsk,bhnsv->bhnkv', k * jnp.exp(last - cum), v)
    chunk_decay = jnp.exp(last[:, :, :, 0, :])

    def step(state, inp):
        a, ds = inp
        return a[..., None] * state + ds, state

    s_final, s_before = lax.scan(step, s0.astype(f32), (jnp.moveaxis(chunk_decay, 2, 0), jnp.moveaxis(d_state, 2, 0)))
    s_before = jnp.moveaxis(s_before, 0, 2)
    o_inter = jnp.einsum('bhntk,bhnkv->bhntv', q * jnp.exp(cum), s_before)
    return (o_intra + o_inter).reshape(b, h, t, dv), s_final


def hgrn_parse(p, lb_fwd, lb_bwd):
    b, t, _ = p.shape
    q, f_fw, f_bw, i, g = jnp.split(p, [HGRN_KW, 2 * HGRN_KW, 3 * HGRN_KW, 3 * HGRN_KW + HGRN_VW], axis=-1)

    def heads(z):
        return z.reshape(b, t, HGRN_HEADS, -1).transpose(0, 2, 1, 3)

    def forget(z, lb):
        lb = lb.reshape(HGRN_HEADS, 1, HGRN_K_DIM)
        f = lb + (1.0 - lb) * jax.nn.sigmoid(heads(z).astype(jnp.float32))
        return jnp.log(f), 1.0 - f

    return heads(jax.nn.silu(q)), heads(i), forget(f_fw, lb_fwd), forget(f_bw, lb_bwd), g


def hgrn_bidir(q, v, fwd, bwd, s_fwd0, s_bwd0):
    log_f_fw, k_fw = fwd
    log_f_bw, k_bw = bwd
    o_fw, s_fw = gla_chunked(q, k_fw, v, log_f_fw, s_fwd0)

    def rev(z):
        return jnp.flip(z, axis=2)

    o_bw, s_bw = gla_chunked(rev(q), rev(k_bw), rev(v), rev(log_f_bw), s_bwd0)
    return o_fw + rev(o_bw), s_fw, s_bw


def hgrn_output(o, g, gain):
    b, h, t, dv = o.shape
    o = rmsnorm(o.transpose(0, 2, 1, 3).astype(g.dtype), gain)
    return (o * jax.nn.silu(g).reshape(b, t, h, dv)).reshape(b, t, h * dv)


def even_mixer(h_ctx, h_lat, w_in, w_out, q_gain, k_gain, lam_vecs, subln, lam_init, lb_fwd, lb_bwd, o_gain, rope, with_ctx_out):
    p_ctx = h_ctx @ w_in
    p_lat = h_lat @ w_in
    qa_c, ka_c, va_c = diff_heads(p_ctx[..., :DIFF_IN_W], q_gain, k_gain)
    qa_l, ka_l, va_l = diff_heads(p_lat[..., :DIFF_IN_W], q_gain, k_gain)
    qa_l = apply_axial_rope(qa_l, rope)
    ka_l = apply_axial_rope(ka_l, rope)
    lv = lam_vecs.astype(jnp.float32)
    lam = jnp.exp(jnp.sum(lv[0] * lv[1])) - jnp.exp(jnp.sum(lv[2] * lv[3])) + lam_init
    oa_l = diff_attention(qa_l, jnp.concatenate([ka_c, ka_l], axis=3), jnp.concatenate([va_c, va_l], axis=2), lam)
    qb_c, vb_c, fw_c, bw_c, g_c = hgrn_parse(p_ctx[..., DIFF_IN_W:], lb_fwd, lb_bwd)
    qb_l, vb_l, fw_l, bw_l, g_l = hgrn_parse(p_lat[..., DIFF_IN_W:], lb_fwd, lb_bwd)
    s0 = jnp.zeros((h_ctx.shape[0], HGRN_HEADS, HGRN_K_DIM, HGRN_V_DIM), jnp.float32)
    ob_c, s_fw, s_bw = hgrn_bidir(qb_c, vb_c, fw_c, bw_c, s0, s0)
    ob_l, _, _ = hgrn_bidir(qb_l, vb_l, fw_l, bw_l, s_fw, s_bw)
    out_lat = jnp.concatenate([diff_output(oa_l, subln, lam_init), hgrn_output(ob_l, g_l, o_gain)], axis=-1) @ w_out
    out_ctx = None
    if with_ctx_out:
        oa_c = diff_attention(qa_c, ka_c, va_c, lam)
        out_ctx = jnp.concatenate([diff_output(oa_c, subln, lam_init), hgrn_output(ob_c, g_c, o_gain)], axis=-1) @ w_out
    return out_ctx, out_lat


def mla_queries(c_q, q_a_gain, w_uq, qn_gain, qr_gain):
    b, t, _ = c_q.shape
    q = (rmsnorm(c_q, q_a_gain) @ w_uq).reshape(b, t, MLA_HEADS, MLA_NOPE + MLA_ROPE).transpose(0, 2, 1, 3)
    return rmsnorm(q[..., :MLA_NOPE], qn_gain), rmsnorm(q[..., MLA_NOPE:], qr_gain)


def mla_keys_values(c_kv, k_rope, kv_a_gain, w_ukv, kn_gain, kr_gain):
    b, t, _ = c_kv.shape
    kv = (rmsnorm(c_kv, kv_a_gain) @ w_ukv).reshape(b, t, MLA_HEADS, MLA_NOPE + MLA_V).transpose(0, 2, 1, 3)
    return rmsnorm(kv[..., :MLA_NOPE], kn_gain), rmsnorm(k_rope, kr_gain), kv[..., MLA_NOPE:]


def mla_attention(q_nope, q_rope, k_nope, k_rope, v):
    scale = (MLA_NOPE + MLA_ROPE) ** -0.5

    def block(qs):
        qn, qr = qs
        s = jnp.einsum('bhqd,bhkd->bhqk', qn, k_nope) + jnp.einsum('bhqd,bkd->bhqk', qr, k_rope)
        p = jax.nn.softmax(s.astype(jnp.float32) * scale, axis=-1)
        return jnp.einsum('bhqk,bhkv->bhqv', p.astype(v.dtype), v)

    return sweep_query_blocks(block, (q_nope, q_rope), axis=2)


def odd_mixer(h_ctx, h_lat, w_in, q_a_gain, kv_a_gain, w_uq, w_ukv, qn_gain, qr_gain, kn_gain, kr_gain, w_out, rope, with_ctx_out):
    ckv_c, kr_c = jnp.split(h_ctx @ w_in[:, MLA_Q_LORA:], [MLA_KV_LORA], axis=-1)
    kn_c, kr_c, v_c = mla_keys_values(ckv_c, kr_c, kv_a_gain, w_ukv, kn_gain, kr_gain)
    cq_l, ckv_l, kr_l = jnp.split(h_lat @ w_in, [MLA_Q_LORA, MLA_Q_LORA + MLA_KV_LORA], axis=-1)
    kn_l, kr_l, v_l = mla_keys_values(ckv_l, kr_l, kv_a_gain, w_ukv, kn_gain, kr_gain)
    kr_l = apply_axial_rope(kr_l, rope)
    qn_l, qr_l = mla_queries(cq_l, q_a_gain, w_uq, qn_gain, qr_gain)
    qr_l = apply_axial_rope(qr_l, rope)
    o_l = mla_attention(qn_l, qr_l, jnp.concatenate([kn_c, kn_l], axis=2), jnp.concatenate([kr_c, kr_l], axis=1), jnp.concatenate([v_c, v_l], axis=2))
    out_lat = merge_heads(o_l) @ w_out
    out_ctx = None
    if with_ctx_out:
        qn_c, qr_c = mla_queries(h_ctx @ w_in[:, :MLA_Q_LORA], q_a_gain, w_uq, qn_gain, qr_gain)
        out_ctx = merge_heads(mla_attention(qn_c, qr_c, kn_c, kr_c, v_c)) @ w_out
    return out_ctx, out_lat


def swiglu(t, w_gate, w_up, w_down):
    return (jax.nn.silu(t @ w_gate) * (t @ w_up)) @ w_down


def moe_ffn(h, router_w, router_bias, w_gate, w_up, w_down, sw_gate, sw_up, sw_down):
    shape = h.shape
    t = h.reshape(-1, shape[-1])
    scores = jax.nn.sigmoid((t @ router_w).astype(jnp.float32))
    biased = scores + router_bias.astype(jnp.float32)
    grouped = biased.reshape(-1, N_GROUPS, EXPERTS_PER_GROUP)
    group_score = jnp.sum(lax.top_k(grouped, TOP_K)[0], axis=-1)
    best_group = jnp.argmax(group_score, axis=-1)
    in_group = (jnp.arange(N_GROUPS) == best_group[:, None])[:, :, None]
    masked = jnp.where(in_group, grouped, -jnp.inf).reshape(-1, N_EXPERTS)
    _, top_idx = lax.top_k(masked, TOP_K)
    top_w = jnp.take_along_axis(scores, top_idx, axis=-1)
    top_w = top_w / jnp.sum(top_w, axis=-1, keepdims=True)
    gates = jnp.sum(jax.nn.one_hot(top_idx, N_EXPERTS, dtype=jnp.float32) * top_w[..., None], axis=1).astype(t.dtype)
    out = swiglu(t, sw_gate, sw_up, sw_down)
    for e in range(N_EXPERTS):
        out = out + gates[:, e:e + 1] * swiglu(t, w_gate[e], w_up[e], w_down[e])
    return out.reshape(shape)


def setup_inputs(seed: int = 0) -> dict:
    key = jax.random.key(seed)
    ks = jax.random.split(key, 40)
    f32 = jnp.float32

    def nrm(k, shape, scale=1.0):
        return jax.random.normal(k, shape, f32) * scale

    def gain(k, shape):
        return 1.0 + 0.05 * jax.random.normal(k, shape, f32)

    D = D_MODEL
    return {
        'x': nrm(ks[0], (BATCH, SEQ, D)),
        'c': nrm(ks[1], (BATCH, D)),
        'ctx': nrm(ks[2], (BATCH, CTX_LEN, D)),
        'c_ctx': nrm(ks[3], (D,)),
        'mod_w': nrm(ks[4], (DEPTH, D, 6 * D), 0.5 * D ** -0.5),
        'mod_b': nrm(ks[5], (DEPTH, 6 * D), 0.02),
        'norm_mix': gain(ks[6], (DEPTH, D)),
        'norm_ffn': gain(ks[7], (DEPTH, D)),
        'even_w_in': nrm(ks[8], (N_EVEN, D, EVEN_IN_W), D ** -0.5),
        'even_w_out': nrm(ks[9], (N_EVEN, EVEN_OUT_W, D), EVEN_OUT_W ** -0.5),
        'diff_q_gain': gain(ks[10], (N_EVEN, DIFF_HEAD_DIM)),
        'diff_k_gain': gain(ks[11], (N_EVEN, DIFF_HEAD_DIM)),
        'diff_lambda': nrm(ks[12], (N_EVEN, 4, DIFF_HEAD_DIM), 0.1),
        'diff_subln': gain(ks[13], (N_EVEN, DIFF_V_DIM)),
        'hgrn_lb_logits': nrm(ks[14], (N_EVEN + 1, 2, HGRN_KW), 0.5),
        'hgrn_out_gain': gain(ks[15], (N_EVEN, HGRN_V_DIM)),
        'odd_w_in': nrm(ks[16], (N_ODD, D, ODD_IN_W), D ** -0.5),
        'mla_q_a_gain': gain(ks[17], (N_ODD, MLA_Q_LORA)),
        'mla_kv_a_gain': gain(ks[18], (N_ODD, MLA_KV_LORA)),
        'mla_w_uq': nrm(ks[19], (N_ODD, MLA_Q_LORA, MLA_HEADS * (MLA_NOPE + MLA_ROPE)), MLA_Q_LORA ** -0.5),
        'mla_w_ukv': nrm(ks[20], (N_ODD, MLA_KV_LORA, MLA_HEADS * (MLA_NOPE + MLA_V)), MLA_KV_LORA ** -0.5),
        'mla_q_nope_gain': gain(ks[21], (N_ODD, MLA_NOPE)),
        'mla_q_rope_gain': gain(ks[22], (N_ODD, MLA_ROPE)),
        'mla_k_nope_gain': gain(ks[23], (N_ODD, MLA_NOPE)),
        'mla_k_rope_gain': gain(ks[24], (N_ODD, MLA_ROPE)),
        'odd_w_out': nrm(ks[25], (N_ODD, ODD_OUT_W, D), ODD_OUT_W ** -0.5),
        'router_w': nrm(ks[26], (D, N_EXPERTS), D ** -0.5),
        'router_bias': nrm(ks[27], (N_EXPERTS,), 0.01),
        'expert_w_gate': nrm(ks[28], (DEPTH, N_EXPERTS, D, EXPERT_FF), D ** -0.5),
        'expert_w_up': nrm(ks[29], (DEPTH, N_EXPERTS, D, EXPERT_FF), D ** -0.5),
        'expert_w_down': nrm(ks[30], (DEPTH, N_EXPERTS, EXPERT_FF, D), EXPERT_FF ** -0.5),
        'shared_w_gate': nrm(ks[31], (DEPTH, D, SHARED_FF), D ** -0.5),
        'shared_w_up': nrm(ks[32], (DEPTH, D, SHARED_FF), D ** -0.5),
        'shared_w_down': nrm(ks[33], (DEPTH, SHARED_FF, D), SHARED_FF ** -0.5),
    }


def reference(x, c, ctx, c_ctx, mod_w, mod_b, norm_mix, norm_ffn,
              even_w_in, even_w_out, diff_q_gain, diff_k_gain, diff_lambda, diff_subln, hgrn_lb_logits, hgrn_out_gain,
              odd_w_in, mla_q_a_gain, mla_kv_a_gain, mla_w_uq, mla_w_ukv, mla_q_nope_gain, mla_q_rope_gain,
              mla_k_nope_gain, mla_k_rope_gain, odd_w_out,
              router_w, router_bias, expert_w_gate, expert_w_up, expert_w_down, shared_w_gate, shared_w_up, shared_w_down):
    n_lat = x.shape[1]
    rope = axial_rope_tables(n_lat, DIFF_HEAD_DIM)
    lb_all = jnp.cumsum(jax.nn.softmax(hgrn_lb_logits.astype(jnp.float32), axis=0), axis=0)
    s_c = jax.nn.silu(c)
    s_ctx = jax.nn.silu(c_ctx)
    for layer in range(DEPTH):
        with_ctx_out = layer < DEPTH - 1
        j = layer // 2
        mod_lat = (s_c @ mod_w[layer] + mod_b[layer])[:, None, :]
        mod_ctx = s_ctx @ mod_w[layer] + mod_b[layer]
        sh_m, sc_m, g_m, sh_f, sc_f, g_f = jnp.split(mod_lat, 6, axis=-1)
        csh_m, csc_m, cg_m, csh_f, csc_f, cg_f = jnp.split(mod_ctx, 6, axis=-1)
        h_lat = rmsnorm(x, norm_mix[layer]) * (1.0 + sc_m) + sh_m
        h_ctx = rmsnorm(ctx, norm_mix[layer]) * (1.0 + csc_m) + csh_m
        if layer % 2 == 0:
            lam_init = 0.8 - 0.6 * math.exp(-0.3 * layer)
            mix_ctx, mix_lat = even_mixer(h_ctx, h_lat, even_w_in[j], even_w_out[j], diff_q_gain[j], diff_k_gain[j],
                                          diff_lambda[j], diff_subln[j], lam_init, lb_all[j, 0], lb_all[j, 1],
                                          hgrn_out_gain[j], rope, with_ctx_out)
        else:
            mix_ctx, mix_lat = odd_mixer(h_ctx, h_lat, odd_w_in[j], mla_q_a_gain[j], mla_kv_a_gain[j], mla_w_uq[j],
                                         mla_w_ukv[j], mla_q_nope_gain[j], mla_q_rope_gain[j], mla_k_nope_gain[j],
                                         mla_k_rope_gain[j], odd_w_out[j], rope, with_ctx_out)
        x = x + g_m * mix_lat
        hf = rmsnorm(x, norm_ffn[layer]) * (1.0 + sc_f) + sh_f
        x = x + g_f * moe_ffn(hf, router_w, router_bias, expert_w_gate[layer], expert_w_up[layer], expert_w_down[layer],
                              shared_w_gate[layer], shared_w_up[layer], shared_w_down[layer])
        if with_ctx_out:
            ctx = ctx + cg_m * mix_ctx
            hfc = rmsnorm(ctx, norm_ffn[layer]) * (1.0 + csc_f) + csh_f
            ctx = ctx + cg_f * moe_ffn(hfc, router_w, router_bias, expert_w_gate[layer], expert_w_up[layer],
                                       expert_w_down[layer], shared_w_gate[layer], shared_w_up[layer], shared_w_down[layer])
    return x
```

```python
import functools
import math

import numpy as np
import jax
import jax.numpy as jnp
from jax import lax
from jax.experimental import pallas as pl
from jax.experimental.pallas import tpu as pltpu

F32 = jnp.float32
BF16 = jnp.bfloat16

LANES = 128
VMEM_LIMIT = 56 * 1024 * 1024

GRID_W = 64
DIFF_HEADS = 4
DIFF_HEAD_DIM = 64
HGRN_HEADS = 4
HGRN_K_DIM = 128
MLA_HEADS = 8
MLA_NOPE = 128
MLA_ROPE = 64
MLA_V = 128
MLA_Q_LORA = 384
MLA_KV_LORA = 256
N_EXPERTS = 16
N_GROUPS = 4
EXPERTS_PER_GROUP = 4
ROPE_BASE = 10000.0
EPS = 1e-6
GLA_CHUNK = 64

SH_M, SC_M, G_M, SH_F, SC_F, G_F = range(6)


def _sigmoid(x):
    return 1.0 / (1.0 + jnp.exp(-x))


def _dot(a, b):
    return jnp.dot(a, b, preferred_element_type=F32)


def _dot_nt(a, b):
    return lax.dot_general(a, b, (((1,), (1,)), ((), ())), preferred_element_type=F32)


def _dot_tn(a, b):
    return lax.dot_general(a, b, (((0,), (0,)), ((), ())), preferred_element_type=F32)


def _split3(x):
    hi = x.astype(BF16)
    r = x - hi.astype(F32)
    mid = r.astype(BF16)
    lo = (r - mid.astype(F32)).astype(BF16)
    return hi, mid, lo


def _rms(x, width=None):
    n = x.shape[-1] if width is None else width
    return x * lax.rsqrt(jnp.sum(x * x, axis=-1, keepdims=True) * (1.0 / n) + EPS)


def _cparams(sem):
    return pltpu.CompilerParams(dimension_semantics=sem, vmem_limit_bytes=VMEM_LIMIT)


def _mod_kernel(c_ref, w_ref, b_ref, o_ref):
    c = c_ref[...]
    s = c * _sigmoid(c)
    o_ref[...] = _dot(s.astype(BF16), w_ref[...].astype(BF16)) + b_ref[...]


def modulation(c_rows, mod_w, mod_b):
    n_layers, d, n = mod_w.shape
    rows = c_rows.shape[0]
    tn = 1536
    return pl.pallas_call(
        _mod_kernel,
        grid=(n_layers, n // tn),
        in_specs=[
            pl.BlockSpec((rows, d), lambda l, j: (0, 0)),
            pl.BlockSpec((None, d, tn), lambda l, j: (l, 0, j)),
            pl.BlockSpec((None, 1, tn), lambda l, j: (l, 0, j)),
        ],
        out_specs=pl.BlockSpec((None, rows, tn), lambda l, j: (l, 0, j)),
        out_shape=jax.ShapeDtypeStruct((n_layers, rows, n), F32),
        compiler_params=_cparams(("arbitrary", "arbitrary")),
        name="modulation",
    )(c_rows, mod_w, mod_b.reshape(n_layers, 1, n))


def _norm_matmul_kernel(*refs, has_mod):
    if has_mod:
        x_ref, m_ref, g_ref, w_ref, o_ref = refs
    else:
        x_ref, g_ref, w_ref, o_ref = refs
    h = _rms(x_ref[...].astype(F32)) * g_ref[...]
    if has_mod:
        h = h * (1.0 + m_ref[SC_M:SC_M + 1, :]) + m_ref[SH_M:SH_M + 1, :]
    o_ref[...] = _dot(h.astype(BF16), w_ref[...]).astype(o_ref.dtype)


def norm_matmul(x, gain, w, *, tm, col_block=0, row_tile0=0, n_row_tiles=None, mods=None, n_ctx_tiles=0):
    b, t, _ = x.shape
    k, n = w.shape
    if n_row_tiles is None:
        n_row_tiles = t // tm - row_tile0
    in_specs = [pl.BlockSpec((None, tm, k), lambda bi, ti: (bi, ti + row_tile0, col_block))]
    args = [x]
    if mods is not None:
        in_specs.append(pl.BlockSpec((None, None, 6, k),
                                     lambda bi, ti: (bi, ((ti + row_tile0) >= n_ctx_tiles).astype(jnp.int32), 0, 0)))
        args.append(mods)
    in_specs += [pl.BlockSpec((1, k), lambda bi, ti: (0, 0)), pl.BlockSpec((k, n), lambda bi, ti: (0, 0))]
    args += [gain.reshape(1, k).astype(F32), w]
    return pl.pallas_call(
        functools.partial(_norm_matmul_kernel, has_mod=mods is not None),
        grid=(b, n_row_tiles),
        in_specs=in_specs,
        out_specs=pl.BlockSpec((None, tm, n), lambda bi, ti: (bi, ti, 0)),
        out_shape=jax.ShapeDtypeStruct((b, n_row_tiles * tm, n), BF16),
        compiler_params=_cparams(("arbitrary", "arbitrary")),
        name="norm_matmul",
    )(*args)


def rope_lane_tables(n_ctx, n_lat):
    rows = n_lat // GRID_W
    row = np.repeat(np.arange(rows), GRID_W).astype(np.float32)
    col = np.tile(np.arange(GRID_W), rows).astype(np.float32)
    axis_dim = DIFF_HEAD_DIM // 2
    inv_freq = jnp.asarray(ROPE_BASE, F32) ** (-jnp.arange(0, axis_dim, 2, dtype=F32) / axis_dim)
    ang_r = jnp.asarray(row)[:, None] * inv_freq
    ang_c = jnp.asarray(col)[:, None] * inv_freq
    lane = np.arange(LANES)
    freq_idx = lane % 16
    use_col = (lane % 64) >= 32
    first = (lane % 32) < 16
    ang = jnp.where(use_col[None, :], ang_c[:, freq_idx], ang_r[:, freq_idx])
    cos, sin = jnp.cos(ang), jnp.sin(ang)
    c = jnp.concatenate([jnp.ones((n_ctx, LANES), F32), cos], axis=0)
    sa = jnp.concatenate([jnp.zeros((n_ctx, LANES), F32), jnp.where(first[None, :], -sin, 0.0)], axis=0)
    sb = jnp.concatenate([jnp.zeros((n_ctx, LANES), F32), jnp.where(first[None, :], 0.0, sin)], axis=0)
    return c, sa, sb


def _rope(x, c, sa, sb):
    return x * c + pltpu.roll(x, LANES - 16, 1) * sa + pltpu.roll(x, 16, 1) * sb


def _half_mean_matrix():
    r = lax.broadcasted_iota(jnp.int32, (LANES, LANES), 0) // 64
    c = lax.broadcasted_iota(jnp.int32, (LANES, LANES), 1) // 64
    return (r == c).astype(BF16)


def _halfnorm(x, bd):
    xx = x * x
    hi = xx.astype(BF16)
    lo = (xx - hi.astype(F32)).astype(BF16)
    ms = (_dot(hi, bd) + _dot(lo, bd)) * (1.0 / 64)
    return x * lax.rsqrt(ms + EPS)


def _softmax_parts(s):
    e = jnp.exp(s - jnp.max(s, axis=-1, keepdims=True))
    return e, jnp.sum(e, axis=-1, keepdims=True)


def _diff_attn_kernel(lam_ref, q_ref, k_ref, v_ref, cq_ref, saq_ref, sbq_ref, ck_ref, sak_ref, sbk_ref,
                      qg_ref, kg_ref, sub_ref, o_ref, ks_ref, *, n_ctx, n_ctx_tiles, lam_init):
    qi = pl.program_id(2)
    bd = _half_mean_matrix()

    @pl.when(qi == 0)
    def _():
        k = _halfnorm(k_ref[...].astype(F32), bd) * kg_ref[...]
        ks_ref[...] = _rope(k, ck_ref[...], sak_ref[...], sbk_ref[...]).astype(BF16)

    q = _halfnorm(q_ref[...].astype(F32), bd) * qg_ref[...]
    q = _rope(q, cq_ref[...], saq_ref[...], sbq_ref[...]) * (DIFF_HEAD_DIM ** -0.5)
    lane = lax.broadcasted_iota(jnp.int32, q.shape, 1)
    q1 = jnp.where(lane < DIFF_HEAD_DIM, q, 0.0).astype(BF16)
    q2 = jnp.where(lane >= DIFF_HEAD_DIM, q, 0.0).astype(BF16)
    lv = lam_ref[...]
    lam = (jnp.exp(jnp.sum(lv[0:1] * lv[1:2], axis=-1, keepdims=True))
           - jnp.exp(jnp.sum(lv[2:3] * lv[3:4], axis=-1, keepdims=True)) + lam_init)

    def attend(nk):
        ks = ks_ref[0:nk, :]
        e1, l1 = _softmax_parts(_dot_nt(q1, ks))
        e2, l2 = _softmax_parts(_dot_nt(q2, ks))
        a = e1 * (1.0 / l1) - e2 * (lam / l2)
        o = _dot(a.astype(BF16), v_ref[0:nk, :])
        o_ref[...] = (_rms(o) * sub_ref[...] * (1.0 - lam_init)).astype(o_ref.dtype)

    @pl.when(qi < n_ctx_tiles)
    def _():
        attend(n_ctx)

    @pl.when(qi >= n_ctx_tiles)
    def _():
        attend(k_ref.shape[0])


def diff_attention(p, lam_vecs, q_gain, k_gain, subln, rope, *, tq, n_ctx, lam_init, q_blk0, k_blk0, v_blk0):
    b, t, _ = p.shape
    c, sa, sb = rope
    nq = t // tq
    row_q = lambda bi, h, qi: (qi, 0)
    full = lambda bi, h, qi: (0, 0)
    tile2 = lambda g: jnp.tile(g.astype(F32), 2).reshape(1, LANES)
    return pl.pallas_call(
        functools.partial(_diff_attn_kernel, n_ctx=n_ctx, n_ctx_tiles=n_ctx // tq, lam_init=lam_init),
        grid=(b, DIFF_HEADS, nq),
        in_specs=[
            pl.BlockSpec((4, DIFF_HEAD_DIM), full),
            pl.BlockSpec((None, tq, LANES), lambda bi, h, qi: (bi, qi, q_blk0 + h)),
            pl.BlockSpec((None, t, LANES), lambda bi, h, qi: (bi, 0, k_blk0 + h)),
            pl.BlockSpec((None, t, LANES), lambda bi, h, qi: (bi, 0, v_blk0 + h)),
            pl.BlockSpec((tq, LANES), row_q), pl.BlockSpec((tq, LANES), row_q), pl.BlockSpec((tq, LANES), row_q),
            pl.BlockSpec((t, LANES), full), pl.BlockSpec((t, LANES), full), pl.BlockSpec((t, LANES), full),
            pl.BlockSpec((1, LANES), full), pl.BlockSpec((1, LANES), full), pl.BlockSpec((1, LANES), full),
        ],
        out_specs=pl.BlockSpec((None, tq, LANES), lambda bi, h, qi: (bi, qi, h)),
        out_shape=jax.ShapeDtypeStruct((b, t, DIFF_HEADS * LANES), BF16),
        scratch_shapes=[pltpu.VMEM((t, LANES), BF16)],
        compiler_params=_cparams(("arbitrary", "arbitrary", "arbitrary")),
        name="diff_attention",
    )(lam_vecs.astype(F32), p, p, p, c, sa, sb, c, sa, sb, tile2(q_gain), tile2(k_gain),
      subln.astype(F32).reshape(1, LANES))


def _gla_constants(c):
    levels = int(math.log2(c))
    t = np.arange(c)[:, None]
    u = np.arange(c)[None, :]
    fwd = [(u <= t)]
    bwd = [(u >= t)]
    for lv in range(1, levels + 1):
        base = (t >> lv) << lv
        half = 1 << (lv - 1)
        fwd.append(u <= base + half - 1)
        bwd.append(u >= base + half)
    stack = lambda ms: jnp.asarray(np.concatenate(ms, axis=0).astype(np.float32), BF16)
    return stack(fwd), stack(bwd)


def _gla_chunk(q, k, v, g, s_t, mstack, *, reverse):
    c = q.shape[0]
    levels = int(math.log2(c))
    ghi, gmid, glo = _split3(g)
    cs = _dot(mstack, ghi) + _dot(mstack, gmid) + _dot(mstack, glo)
    cum = cs[0:c]
    tot = cum[0:1] if reverse else cum[c - 1:c]
    row = lax.broadcasted_iota(jnp.int32, (c, q.shape[1]), 0)
    tr = lax.broadcasted_iota(jnp.int32, (c, c), 0)
    tc = lax.broadcasted_iota(jnp.int32, (c, c), 1)
    a = jnp.where(tr == tc, _dot_nt(q.astype(BF16), k.astype(BF16)), 0.0)
    for lv in range(1, levels + 1):
        e = jnp.exp(-jnp.abs(cum - cs[lv * c:(lv + 1) * c]))
        upper = ((row >> (lv - 1)) & 1) == 1
        q_side = jnp.logical_not(upper) if reverse else upper
        ql = jnp.where(q_side, q * e, 0.0).astype(BF16)
        kl = jnp.where(q_side, 0.0, k * e).astype(BF16)
        a = a + jnp.where((tr >> lv) == (tc >> lv), _dot_nt(ql, kl), 0.0)
    o = _dot(a.astype(BF16), v)
    o = o + _dot_nt((q * jnp.exp(cum)).astype(BF16), s_t.astype(BF16))
    ks = (k * jnp.exp(tot - cum)).astype(BF16)
    s_new = s_t * jnp.exp(tot) + _dot_tn(v, ks)
    return o, s_new


def _hgrn_kernel(qz_ref, zf_ref, zb_ref, v_ref, gz_ref, lbf_ref, lbb_ref, og_ref, mf_ref, mb_ref,
                 o_ref, of_ref, ob_ref, st_ref, *, n_ctx_chunks):
    c = GLA_CHUNK
    t = qz_ref.shape[0]
    n = t // c
    st_ref[...] = jnp.zeros_like(st_ref)
    mf = mf_ref[...]
    mb = mb_ref[...]
    lbf = lbf_ref[...]
    lbb = lbb_ref[...]

    def load(z_ref, lb, r0):
        qz = qz_ref[pl.ds(r0, c), :].astype(F32)
        q = qz * _sigmoid(qz) * (HGRN_K_DIM ** -0.5)
        f = lb + (1.0 - lb) * _sigmoid(z_ref[pl.ds(r0, c), :].astype(F32))
        return q, 1.0 - f, v_ref[pl.ds(r0, c), :], jnp.log(f)

    def body(i, carry):
        rf = pl.multiple_of(i * c, c)
        cb = jnp.where(i < n_ctx_chunks, n_ctx_chunks - 1 - i, n - 1 - i + n_ctx_chunks)
        rb = pl.multiple_of(cb * c, c)
        q, k, v, g = load(zf_ref, lbf, rf)
        o, s_new = _gla_chunk(q, k, v, g, st_ref[0], mf, reverse=False)
        of_ref[pl.ds(rf, c), :] = o
        st_ref[0] = s_new
        q, k, v, g = load(zb_ref, lbb, rb)
        o, s_new = _gla_chunk(q, k, v, g, st_ref[1], mb, reverse=True)
        ob_ref[pl.ds(rb, c), :] = o
        st_ref[1] = s_new
        return carry

    lax.fori_loop(0, n, body, 0)
    gz = gz_ref[...].astype(F32)
    o = _rms(of_ref[...] + ob_ref[...]) * og_ref[...]
    o_ref[...] = (o * (gz * _sigmoid(gz))).astype(o_ref.dtype)


def hgrn_bidirectional(p, lb_fwd, lb_bwd, out_gain, *, n_ctx, blk0):
    b, t, _ = p.shape
    c = GLA_CHUNK
    mf, mb = _gla_constants(c)
    h = HGRN_HEADS
    seg = lambda s: pl.BlockSpec((None, t, LANES), lambda bi, hi: (bi, 0, blk0 + s * h + hi))
    per_head = pl.BlockSpec((None, 1, LANES), lambda bi, hi: (hi, 0, 0))
    const = lambda a: pl.BlockSpec(a.shape, lambda bi, hi: (0, 0))
    return pl.pallas_call(
        functools.partial(_hgrn_kernel, n_ctx_chunks=n_ctx // c),
        grid=(b, h),
        in_specs=[seg(0), seg(1), seg(2), seg(3), seg(4), per_head, per_head,
                  pl.BlockSpec((1, LANES), lambda bi, hi: (0, 0)), const(mf), const(mb)],
        out_specs=pl.BlockSpec((None, t, LANES), lambda bi, hi: (bi, 0, hi)),
        out_shape=jax.ShapeDtypeStruct((b, t, h * LANES), BF16),
        scratch_shapes=[pltpu.VMEM((t, LANES), F32), pltpu.VMEM((t, LANES), F32),
                        pltpu.VMEM((2, LANES, LANES), F32)],
        compiler_params=_cparams(("arbitrary", "arbitrary")),
        name="hgrn_bidirectional",
    )(p, p, p, p, p, lb_fwd.reshape(h, 1, LANES), lb_bwd.reshape(h, 1, LANES),
      out_gain.astype(F32).reshape(1, LANES), mf, mb)


def _mla_attn_kernel(q_ref, kv_ref, kr_ref, cq_ref, saq_ref, sbq_ref, ck_ref, sak_ref, sbk_ref,
                     qn_ref, qr_ref, kn_ref, krg_ref, o_ref, kc_ref):
    qi = pl.program_id(2)

    @pl.when(qi == 0)
    def _():
        kn = _rms(kv_ref[:, 0:MLA_NOPE].astype(F32)) * kn_ref[...]
        kr = _rms(kr_ref[...].astype(F32), MLA_ROPE) * krg_ref[...]
        kr = _rope(kr, ck_ref[...], sak_ref[...], sbk_ref[...])
        kc_ref[:, 0:LANES] = kn.astype(BF16)
        kc_ref[:, LANES:2 * LANES] = kr.astype(BF16)

    scale = (MLA_NOPE + MLA_ROPE) ** -0.5
    qn = _rms(q_ref[:, 0:MLA_NOPE].astype(F32)) * (qn_ref[...] * scale)
    qr = _rms(q_ref[:, MLA_NOPE:2 * LANES].astype(F32), MLA_ROPE) * qr_ref[...]
    qr = _rope(qr, cq_ref[...], saq_ref[...], sbq_ref[...]) * scale
    qc = jnp.concatenate([qn.astype(BF16), qr.astype(BF16)], axis=-1)
    e, l = _softmax_parts(_dot_nt(qc, kc_ref[...]))
    o = _dot(e.astype(BF16), kv_ref[:, MLA_NOPE:MLA_NOPE + MLA_V]) * (1.0 / l)
    o_ref[...] = o.astype(o_ref.dtype)


def mla_attention(q, kv, p1, kr_blk, rope, qn_gain, qr_gain, kn_gain, kr_gain, *, tq, n_ctx):
    b, n_lat, _ = q.shape
    t = kv.shape[1]
    c, sa, sb = rope
    q_tile0 = n_ctx // tq
    row_q = lambda bi, h, qi: (qi + q_tile0, 0)
    full = lambda bi, h, qi: (0, 0)
    pad = lambda g: jnp.concatenate([g.astype(F32), jnp.zeros((LANES - g.shape[0],), F32)]).reshape(1, LANES)
    return pl.pallas_call(
        _mla_attn_kernel,
        grid=(b, MLA_HEADS, n_lat // tq),
        in_specs=[
            pl.BlockSpec((None, tq, 2 * LANES), lambda bi, h, qi: (bi, qi, h)),
            pl.BlockSpec((None, t, 2 * LANES), lambda bi, h, qi: (bi, 0, h)),
            pl.BlockSpec((None, t, LANES), lambda bi, h, qi: (bi, 0, kr_blk)),
            pl.BlockSpec((tq, LANES), row_q), pl.BlockSpec((tq, LANES), row_q), pl.BlockSpec((tq, LANES), row_q),
            pl.BlockSpec((t, LANES), full), pl.BlockSpec((t, LANES), full), pl.BlockSpec((t, LANES), full),
            pl.BlockSpec((1, LANES), full), pl.BlockSpec((1, LANES), full),
            pl.BlockSpec((1, LANES), full), pl.BlockSpec((1, LANES), full),
        ],
        out_specs=pl.BlockSpec((None, tq, LANES), lambda bi, h, qi: (bi, qi, h)),
        out_shape=jax.ShapeDtypeStruct((b, n_lat, MLA_HEADS * MLA_V), BF16),
        scratch_shapes=[pltpu.VMEM((t, 2 * LANES), BF16)],
        compiler_params=_cparams(("arbitrary", "arbitrary", "arbitrary")),
        name="mla_attention",
    )(q, kv, p1, c, sa, sb, c, sa, sb, pad(qn_gain), pad(qr_gain), pad(kn_gain), pad(kr_gain))


def _proj_residual_kernel(*refs, n_in):
    a_refs = refs[:n_in]
    w_refs = refs[n_in:2 * n_in]
    x_ref, m_ref, o_ref = refs[2 * n_in:]
    acc = _dot(a_refs[0][...], w_refs[0][...])
    for a_ref, w_ref in zip(a_refs[1:], w_refs[1:]):
        acc = acc + _dot(a_ref[...], w_ref[...])
    o_ref[...] = x_ref[...] + m_ref[G_M:G_M + 1, :] * acc


def proj_residual(acts, weights, xs, mods, *, tm, n_ctx_tiles, row_tile0=0):
    b, r, _ = acts[0].shape
    d = xs.shape[-1]
    in_specs = [pl.BlockSpec((None, tm, a.shape[-1]), lambda bi, ti: (bi, ti, 0)) for a in acts]
    in_specs += [pl.BlockSpec(w.shape, lambda bi, ti: (0, 0)) for w in weights]
    in_specs += [
        pl.BlockSpec((None, tm, d), lambda bi, ti: (bi, ti + row_tile0, 0)),
        pl.BlockSpec((None, None, 6, d), lambda bi, ti: (bi, ((ti + row_tile0) >= n_ctx_tiles).astype(jnp.int32), 0, 0)),
    ]
    return pl.pallas_call(
        functools.partial(_proj_residual_kernel, n_in=len(acts)),
        grid=(b, r // tm),
        in_specs=in_specs,
        out_specs=pl.BlockSpec((None, tm, d), lambda bi, ti: (bi, ti, 0)),
        out_shape=jax.ShapeDtypeStruct((b, r, d), F32),
        compiler_params=_cparams(("arbitrary", "arbitrary")),
        name="proj_residual",
    )(*acts, *weights, xs, mods)


def _route_kernel(x_ref, m_ref, g_ref, rw_ref, rb_ref, hf_ref, gates_ref):
    h = _rms(x_ref[...]) * g_ref[...]
    h = h * (1.0 + m_ref[SC_F:SC_F + 1, :]) + m_ref[SH_F:SH_F + 1, :]
    hf_ref[...] = h.astype(hf_ref.dtype)
    h1, h2, h3 = _split3(h)
    w1, w2, w3 = _split3(rw_ref[...])
    logits = (_dot_nt(w1, h1) + (_dot_nt(w1, h2) + _dot_nt(w2, h1))
              + (_dot_nt(w1, h3) + _dot_nt(w2, h2) + _dot_nt(w3, h1)))
    s = _sigmoid(logits)
    biased = s + rb_ref[...]
    row = [biased[e:e + 1, :] for e in range(N_EXPERTS)]
    gscore = []
    for g in range(N_GROUPS):
        m = row[4 * g:4 * g + 4]
        pairs = [m[i] + m[j] for i in range(4) for j in range(i + 1, 4)]
        gscore.append(functools.reduce(jnp.maximum, pairs))
    sel_rows = []
    for g in range(N_GROUPS):
        best = None
        for g2 in range(N_GROUPS):
            if g2 == g:
                continue
            wins = (gscore[g] > gscore[g2]) if g2 < g else (gscore[g] >= gscore[g2])
            best = wins if best is None else jnp.logical_and(best, wins)
        for i in range(4):
            rank = None
            for j in range(4):
                if j == i:
                    continue
                mi, mj = row[4 * g + i], row[4 * g + j]
                beats = (mj >= mi) if j < i else (mj > mi)
                rank = beats.astype(jnp.int32) if rank is None else rank + beats.astype(jnp.int32)
            e = 4 * g + i
            sel_rows.append(jnp.where(jnp.logical_and(best, rank < 2), s[e:e + 1, :], 0.0))
    picked = jnp.concatenate(sel_rows, axis=0)
    gates_ref[...] = picked / jnp.sum(picked, axis=0, keepdims=True)


def route(xs, mods, gain, router_w, router_bias, *, tm, n_ctx_tiles, row_tile0=0):
    b, t, d = xs.shape
    nt = t // tm - row_tile0
    return pl.pallas_call(
        _route_kernel,
        grid=(b, nt),
        in_specs=[
            pl.BlockSpec((None, tm, d), lambda bi, ti: (bi, ti + row_tile0, 0)),
            pl.BlockSpec((None, None, 6, d), lambda bi, ti: (bi, ((ti + row_tile0) >= n_ctx_tiles).astype(jnp.int32), 0, 0)),
            pl.BlockSpec((1, d), lambda bi, ti: (0, 0)),
            pl.BlockSpec((N_EXPERTS, d), lambda bi, ti: (0, 0)),
            pl.BlockSpec((N_EXPERTS, 1), lambda bi, ti: (0, 0)),
        ],
        out_specs=[
            pl.BlockSpec((None, tm, d), lambda bi, ti: (bi, ti, 0)),
            pl.BlockSpec((None, None, N_EXPERTS, tm), lambda bi, ti: (bi, ti, 0, 0)),
        ],
        out_shape=[jax.ShapeDtypeStruct((b, nt * tm, d), BF16),
                   jax.ShapeDtypeStruct((b, nt, N_EXPERTS, tm), F32)],
        compiler_params=_cparams(("arbitrary", "arbitrary")),
        name="route",
    )(xs, mods, gain.reshape(1, d).astype(F32), router_w.T.astype(F32),
      router_bias.reshape(N_EXPERTS, 1).astype(F32))


def _moe_dense_kernel(hf_ref, gates_ref, wg_ref, wu_ref, wd_ref, x_ref, m_ref, o_ref, acc_ref):
    e = pl.program_id(2)

    @pl.when(e == 0)
    def _():
        acc_ref[...] = jnp.zeros_like(acc_ref)

    h = hf_ref[...]
    a = _dot(h, wg_ref[...])
    u = _dot(h, wu_ref[...])
    y = _dot((a * _sigmoid(a) * u).astype(BF16), wd_ref[...])
    lane = lax.broadcasted_iota(jnp.int32, gates_ref.shape, 1)
    gate = jnp.sum(jnp.where(lane == e, gates_ref[...], 0.0), axis=1, keepdims=True)
    acc_ref[...] += gate * y

    @pl.when(e == pl.num_programs(2) - 1)
    def _():
        o_ref[...] = x_ref[...] + m_ref[G_F:G_F + 1, :] * acc_ref[...]


def moe_dense(hf, gates, wg, wu, wd, xs, mods, *, tm, n_ctx_tiles, row_tile0=0):
    b, r, d = hf.shape
    ne, _, ff = wg.shape
    return pl.pallas_call(
        _moe_dense_kernel,
        grid=(b, r // tm, ne),
        in_specs=[
            pl.BlockSpec((None, tm, d), lambda bi, ti, e: (bi, ti, 0)),
            pl.BlockSpec((None, tm, LANES), lambda bi, ti, e: (bi, ti, 0)),
            pl.BlockSpec((None, d, ff), lambda bi, ti, e: (e, 0, 0)),
            pl.BlockSpec((None, d, ff), lambda bi, ti, e: (e, 0, 0)),
            pl.BlockSpec((None, ff, d), lambda bi, ti, e: (e, 0, 0)),
            pl.BlockSpec((None, tm, d), lambda bi, ti, e: (bi, ti + row_tile0, 0)),
            pl.BlockSpec((None, None, 6, d),
                         lambda bi, ti, e: (bi, ((ti + row_tile0) >= n_ctx_tiles).astype(jnp.int32), 0, 0)),
        ],
        out_specs=pl.BlockSpec((None, tm, d), lambda bi, ti, e: (bi, ti, 0)),
        out_shape=jax.ShapeDtypeStruct((b, r, d), F32),
        scratch_shapes=[pltpu.VMEM((tm, d), F32)],
        compiler_params=_cparams(("arbitrary", "arbitrary", "arbitrary")),
        name="moe_dense",
    )(hf, gates, wg, wu, wd, xs, mods)


def moe_block(xs, mods, gain, router_w, router_bias, ew_gate, ew_up, ew_down, sw_gate, sw_up, sw_down,
              *, tm, n_ctx_tiles, row_tile0=0):
    b = xs.shape[0]
    hf, gates_t = route(xs, mods, gain, router_w, router_bias, tm=tm, n_ctx_tiles=n_ctx_tiles, row_tile0=row_tile0)
    r = hf.shape[1]
    gates = gates_t.transpose(0, 1, 3, 2).reshape(b, r, N_EXPERTS)
    gates = jnp.concatenate([gates, jnp.ones((b, r, 1), F32), jnp.zeros((b, r, LANES - N_EXPERTS - 1), F32)], axis=-1)
    wg = jnp.concatenate([ew_gate, sw_gate[None]], axis=0).astype(BF16)
    wu = jnp.concatenate([ew_up, sw_up[None]], axis=0).astype(BF16)
    wd = jnp.concatenate([ew_down, sw_down[None]], axis=0).astype(BF16)
    return moe_dense(hf, gates, wg, wu, wd, xs, mods, tm=tm, n_ctx_tiles=n_ctx_tiles, row_tile0=row_tile0)


def kernel(x, c, ctx, c_ctx, mod_w, mod_b, norm_mix, norm_ffn, even_w_in, even_w_out, diff_q_gain, diff_k_gain, diff_lambda, diff_subln, hgrn_lb_logits, hgrn_out_gain, odd_w_in, mla_q_a_gain, mla_kv_a_gain, mla_w_uq, mla_w_ukv, mla_q_nope_gain, mla_q_rope_gain, mla_k_nope_gain, mla_k_rope_gain, odd_w_out, router_w, router_bias, expert_w_gate, expert_w_up, expert_w_down, shared_w_gate, shared_w_up, shared_w_down):
    b, n_lat, d = x.shape
    n_ctx = ctx.shape[1]
    depth = mod_w.shape[0]
    tm = 256 if n_ctx % 256 == 0 else 128
    n_ctx_tiles = n_ctx // tm
    rope = rope_lane_tables(n_ctx, n_lat)

    mod_rows = 16
    c_rows = jnp.concatenate([c, c_ctx[None, :], jnp.zeros((mod_rows - b - 1, d), F32)], axis=0)
    mod_all = modulation(c_rows, mod_w, mod_b).reshape(depth, mod_rows, 6, d)
    lb_all = jnp.cumsum(jax.nn.softmax(hgrn_lb_logits.astype(F32), axis=0), axis=0)

    xs = jnp.concatenate([ctx, x], axis=1)
    for layer in range(depth):
        last = layer == depth - 1
        j = layer // 2
        mods = jnp.stack([jnp.broadcast_to(mod_all[layer, b], (b, 6, d)), mod_all[layer, :b]], axis=1)
        row_tile0 = n_ctx_tiles if last else 0
        if layer % 2 == 0:
            lam_init = 0.8 - 0.6 * math.exp(-0.3 * layer)
            p = norm_matmul(xs, norm_mix[layer], even_w_in[j].astype(BF16), tm=tm, mods=mods, n_ctx_tiles=n_ctx_tiles)
            oa = diff_attention(p, diff_lambda[j], diff_q_gain[j], diff_k_gain[j], diff_subln[j], rope,
                                tq=tm, n_ctx=n_ctx, lam_init=lam_init, q_blk0=0, k_blk0=4, v_blk0=8)
            ob = hgrn_bidirectional(p, lb_all[j, 0], lb_all[j, 1], hgrn_out_gain[j], n_ctx=n_ctx, blk0=12)
            if last:
                oa, ob = oa[:, n_ctx:], ob[:, n_ctx:]
            w_out = even_w_out[j].astype(BF16)
            half = oa.shape[-1]
            xs_mid = proj_residual([oa, ob], [w_out[:half], w_out[half:]], xs, mods, tm=tm,
                                   n_ctx_tiles=n_ctx_tiles, row_tile0=row_tile0)
        else:
            w_in = odd_w_in[j]
            zpad = jnp.zeros((d, LANES - MLA_ROPE), F32)
            w_in_r = jnp.concatenate([w_in[:, MLA_Q_LORA:MLA_Q_LORA + MLA_KV_LORA], w_in[:, MLA_Q_LORA + MLA_KV_LORA:],
                                      zpad, w_in[:, :MLA_Q_LORA]], axis=1).astype(BF16)
            p1 = norm_matmul(xs, norm_mix[layer], w_in_r, tm=tm, mods=mods, n_ctx_tiles=n_ctx_tiles)
            w_uq = mla_w_uq[j].reshape(MLA_Q_LORA, MLA_HEADS, MLA_NOPE + MLA_ROPE)
            w_uq = jnp.concatenate([w_uq, jnp.zeros((MLA_Q_LORA, MLA_HEADS, LANES - MLA_ROPE), F32)], axis=-1)
            w_uq = w_uq.reshape(MLA_Q_LORA, MLA_HEADS * 2 * LANES).astype(BF16)
            q = norm_matmul(p1, mla_q_a_gain[j], w_uq, tm=tm, col_block=1, row_tile0=row_tile0)
            kv = norm_matmul(p1, mla_kv_a_gain[j], mla_w_ukv[j].astype(BF16), tm=tm, col_block=0)
            if not last:
                raise NotImplementedError("context queries for a non-final latent-attention layer")
            o = mla_attention(q, kv, p1, 2, rope, mla_q_nope_gain[j], mla_q_rope_gain[j], mla_k_nope_gain[j],
                              mla_k_rope_gain[j], tq=tm, n_ctx=n_ctx)
            xs_mid = proj_residual([o], [odd_w_out[j].astype(BF16)], xs, mods, tm=tm,
                                   n_ctx_tiles=n_ctx_tiles, row_tile0=row_tile0)
        xs = moe_block(xs_mid, mods, norm_ffn[layer], router_w, router_bias,
                       expert_w_gate[layer], expert_w_up[layer], expert_w_down[layer],
                       shared_w_gate[layer], shared_w_up[layer], shared_w_down[layer],
                       tm=tm, n_ctx_tiles=0 if last else n_ctx_tiles)
    return xs if xs.shape[1] == n_lat else xs[:, n_ctx:]
```

```python
import functools
import math

import numpy as np
import jax
import jax.numpy as jnp
from jax import lax
from jax.experimental import pallas as pl
from jax.experimental.pallas import tpu as pltpu

F32 = jnp.float32
BF16 = jnp.bfloat16

LANES = 128
VMEM_LIMIT = 56 * 1024 * 1024

GRID_W = 64
DIFF_HEADS = 4
DIFF_HEAD_DIM = 64
HGRN_HEADS = 4
HGRN_K_DIM = 128
MLA_HEADS = 8
MLA_NOPE = 128
MLA_ROPE = 64
MLA_V = 128
MLA_Q_LORA = 384
MLA_KV_LORA = 256
N_EXPERTS = 16
N_GROUPS = 4
EXPERTS_PER_GROUP = 4
ROPE_BASE = 10000.0
EPS = 1e-6
GLA_CHUNK = 64

SH_M, SC_M, G_M, SH_F, SC_F, G_F = range(6)


def _sigmoid(x):
    return 1.0 / (1.0 + jnp.exp(-x))


def _dot(a, b):
    return jnp.dot(a, b, preferred_element_type=F32)


def _dot_nt(a, b):
    return lax.dot_general(a, b, (((1,), (1,)), ((), ())), preferred_element_type=F32)


def _dot_tn(a, b):
    return lax.dot_general(a, b, (((0,), (0,)), ((), ())), preferred_element_type=F32)


def _split3(x):
    hi = x.astype(BF16)
    r = x - hi.astype(F32)
    mid = r.astype(BF16)
    lo = (r - mid.astype(F32)).astype(BF16)
    return hi, mid, lo


def _rms(x, width=None):
    n = x.shape[-1] if width is None else width
    return x * lax.rsqrt(jnp.sum(x * x, axis=-1, keepdims=True) * (1.0 / n) + EPS)


def _cparams(sem):
    return pltpu.CompilerParams(dimension_semantics=sem, vmem_limit_bytes=VMEM_LIMIT)


def _mod_kernel(c_ref, w_ref, b_ref, o_ref):
    c = c_ref[...]
    s = c * _sigmoid(c)
    o_ref[...] = _dot(s.astype(BF16), w_ref[...].astype(BF16)) + b_ref[...]


def modulation(c_rows, mod_w, mod_b):
    n_layers, d, n = mod_w.shape
    rows = c_rows.shape[0]
    tn = 1536
    return pl.pallas_call(
        _mod_kernel,
        grid=(n_layers, n // tn),
        in_specs=[
            pl.BlockSpec((rows, d), lambda l, j: (0, 0)),
            pl.BlockSpec((None, d, tn), lambda l, j: (l, 0, j)),
            pl.BlockSpec((None, 1, tn), lambda l, j: (l, 0, j)),
        ],
        out_specs=pl.BlockSpec((None, rows, tn), lambda l, j: (l, 0, j)),
        out_shape=jax.ShapeDtypeStruct((n_layers, rows, n), F32),
        compiler_params=_cparams(("arbitrary", "arbitrary")),
        name="modulation",
    )(c_rows, mod_w, mod_b.reshape(n_layers, 1, n))


def _norm_matmul_kernel(*refs, has_mod):
    if has_mod:
        x_ref, m_ref, g_ref, w_ref, o_ref = refs
    else:
        x_ref, g_ref, w_ref, o_ref = refs
    h = _rms(x_ref[...].astype(F32)) * g_ref[...]
    if has_mod:
        h = h * (1.0 + m_ref[SC_M:SC_M + 1, :]) + m_ref[SH_M:SH_M + 1, :]
    o_ref[...] = _dot(h.astype(BF16), w_ref[...]).astype(o_ref.dtype)


def norm_matmul(x, gain, w, *, tm, col_block=0, row_tile0=0, n_row_tiles=None, mods=None, n_ctx_tiles=0):
    b, t, _ = x.shape
    k, n = w.shape
    if n_row_tiles is None:
        n_row_tiles = t // tm - row_tile0
    in_specs = [pl.BlockSpec((None, tm, k), lambda bi, ti: (bi, ti + row_tile0, col_block))]
    args = [x]
    if mods is not None:
        in_specs.append(pl.BlockSpec((None, None, 6, k),
                                     lambda bi, ti: (bi, ((ti + row_tile0) >= n_ctx_tiles).astype(jnp.int32), 0, 0)))
        args.append(mods)
    in_specs += [pl.BlockSpec((1, k), lambda bi, ti: (0, 0)), pl.BlockSpec((k, n), lambda bi, ti: (0, 0))]
    args += [gain.reshape(1, k).astype(F32), w]
    return pl.pallas_call(
        functools.partial(_norm_matmul_kernel, has_mod=mods is not None),
        grid=(b, n_row_tiles),
        in_specs=in_specs,
        out_specs=pl.BlockSpec((None, tm, n), lambda bi, ti: (bi, ti, 0)),
        out_shape=jax.ShapeDtypeStruct((b, n_row_tiles * tm, n), BF16),
        compiler_params=_cparams(("arbitrary", "arbitrary")),
        name="norm_matmul",
    )(*args)


def rope_lane_tables(n_ctx, n_lat):
    rows = n_lat // GRID_W
    row = np.repeat(np.arange(rows), GRID_W).astype(np.float32)
    col = np.tile(np.arange(GRID_W), rows).astype(np.float32)
    axis_dim = DIFF_HEAD_DIM // 2
    inv_freq = jnp.asarray(ROPE_BASE, F32) ** (-jnp.arange(0, axis_dim, 2, dtype=F32) / axis_dim)
    ang_r = jnp.asarray(row)[:, None] * inv_freq
    ang_c = jnp.asarray(col)[:, None] * inv_freq
    lane = np.arange(LANES)
    freq_idx = lane % 16
    use_col = (lane % 64) >= 32
    first = (lane % 32) < 16
    ang = jnp.where(use_col[None, :], ang_c[:, freq_idx], ang_r[:, freq_idx])
    cos, sin = jnp.cos(ang), jnp.sin(ang)
    c = jnp.concatenate([jnp.ones((n_ctx, LANES), F32), cos], axis=0)
    sa = jnp.concatenate([jnp.zeros((n_ctx, LANES), F32), jnp.where(first[None, :], -sin, 0.0)], axis=0)
    sb = jnp.concatenate([jnp.zeros((n_ctx, LANES), F32), jnp.where(first[None, :], 0.0, sin)], axis=0)
    return c, sa, sb


def _rope(x, c, sa, sb):
    return x * c + pltpu.roll(x, LANES - 16, 1) * sa + pltpu.roll(x, 16, 1) * sb


def _half_mean_matrix():
    r = lax.broadcasted_iota(jnp.int32, (LANES, LANES), 0) // 64
    c = lax.broadcasted_iota(jnp.int32, (LANES, LANES), 1) // 64
    return (r == c).astype(BF16)


def _halfnorm(x, bd):
    xx = x * x
    hi = xx.astype(BF16)
    lo = (xx - hi.astype(F32)).astype(BF16)
    ms = (_dot(hi, bd) + _dot(lo, bd)) * (1.0 / 64)
    return x * lax.rsqrt(ms + EPS)


def _softmax_parts(s):
    e = jnp.exp(s - jnp.max(s, axis=-1, keepdims=True))
    return e, jnp.sum(e, axis=-1, keepdims=True)


def _diff_attn_kernel(lam_ref, q_ref, k_ref, v_ref, cq_ref, saq_ref, sbq_ref, ck_ref, sak_ref, sbk_ref,
                      qg_ref, kg_ref, sub_ref, o_ref, ks_ref, *, n_ctx, n_ctx_tiles, lam_init):
    qi = pl.program_id(2)
    bd = _half_mean_matrix()

    @pl.when(qi == 0)
    def _():
        k = _halfnorm(k_ref[...].astype(F32), bd) * kg_ref[...]
        ks_ref[...] = _rope(k, ck_ref[...], sak_ref[...], sbk_ref[...]).astype(BF16)

    q = _halfnorm(q_ref[...].astype(F32), bd) * qg_ref[...]
    q = _rope(q, cq_ref[...], saq_ref[...], sbq_ref[...]) * (DIFF_HEAD_DIM ** -0.5)
    lane = lax.broadcasted_iota(jnp.int32, q.shape, 1)
    q1 = jnp.where(lane < DIFF_HEAD_DIM, q, 0.0).astype(BF16)
    q2 = jnp.where(lane >= DIFF_HEAD_DIM, q, 0.0).astype(BF16)
    lv = lam_ref[...]
    lam = (jnp.exp(jnp.sum(lv[0:1] * lv[1:2], axis=-1, keepdims=True))
           - jnp.exp(jnp.sum(lv[2:3] * lv[3:4], axis=-1, keepdims=True)) + lam_init)

    def attend(nk):
        ks = ks_ref[0:nk, :]
        e1, l1 = _softmax_parts(_dot_nt(q1, ks))
        e2, l2 = _softmax_parts(_dot_nt(q2, ks))
        a = e1 * (1.0 / l1) - e2 * (lam / l2)
        o = _dot(a.astype(BF16), v_ref[0:nk, :])
        o_ref[...] = (_rms(o) * sub_ref[...] * (1.0 - lam_init)).astype(o_ref.dtype)

    @pl.when(qi < n_ctx_tiles)
    def _():
        attend(n_ctx)

    @pl.when(qi >= n_ctx_tiles)
    def _():
        attend(k_ref.shape[0])


def diff_attention(p, lam_vecs, q_gain, k_gain, subln, rope, *, tq, n_ctx, lam_init, q_blk0, k_blk0, v_blk0):
    b, t, _ = p.shape
    c, sa, sb = rope
    nq = t // tq
    row_q = lambda bi, h, qi: (qi, 0)
    full = lambda bi, h, qi: (0, 0)
    tile2 = lambda g: jnp.tile(g.astype(F32), 2).reshape(1, LANES)
    return pl.pallas_call(
        functools.partial(_diff_attn_kernel, n_ctx=n_ctx, n_ctx_tiles=n_ctx // tq, lam_init=lam_init),
        grid=(b, DIFF_HEADS, nq),
        in_specs=[
            pl.BlockSpec((4, DIFF_HEAD_DIM), full),
            pl.BlockSpec((None, tq, LANES), lambda bi, h, qi: (bi, qi, q_blk0 + h)),
            pl.BlockSpec((None, t, LANES), lambda bi, h, qi: (bi, 0, k_blk0 + h)),
            pl.BlockSpec((None, t, LANES), lambda bi, h, qi: (bi, 0, v_blk0 + h)),
            pl.BlockSpec((tq, LANES), row_q), pl.BlockSpec((tq, LANES), row_q), pl.BlockSpec((tq, LANES), row_q),
            pl.BlockSpec((t, LANES), full), pl.BlockSpec((t, LANES), full), pl.BlockSpec((t, LANES), full),
            pl.BlockSpec((1, LANES), full), pl.BlockSpec((1, LANES), full), pl.BlockSpec((1, LANES), full),
        ],
        out_specs=pl.BlockSpec((None, tq, LANES), lambda bi, h, qi: (bi, qi, h)),
        out_shape=jax.ShapeDtypeStruct((b, t, DIFF_HEADS * LANES), BF16),
        scratch_shapes=[pltpu.VMEM((t, LANES), BF16)],
        compiler_params=_cparams(("arbitrary", "arbitrary", "arbitrary")),
        name="diff_attention",
    )(lam_vecs.astype(F32), p, p, p, c, sa, sb, c, sa, sb, tile2(q_gain), tile2(k_gain),
      subln.astype(F32).reshape(1, LANES))


def _gla_constants(c):
    levels = int(math.log2(c))
    t = np.arange(c)[:, None]
    u = np.arange(c)[None, :]
    fwd = [(u <= t)]
    bwd = [(u >= t)]
    for lv in range(1, levels + 1):
        base = (t >> lv) << lv
        half = 1 << (lv - 1)
        fwd.append(u <= base + half - 1)
        bwd.append(u >= base + half)
    stack = lambda ms: jnp.asarray(np.concatenate(ms, axis=0).astype(np.float32), BF16)
    return stack(fwd), stack(bwd)


def _gla_chunk(q, k, v, g, s_t, mstack, *, reverse):
    c = q.shape[0]
    levels = int(math.log2(c))
    ghi, gmid, glo = _split3(g)
    cs = _dot(mstack, ghi) + _dot(mstack, gmid) + _dot(mstack, glo)
    cum = cs[0:c]
    tot = cum[0:1] if reverse else cum[c - 1:c]
    row = lax.broadcasted_iota(jnp.int32, (c, q.shape[1]), 0)
    tr = lax.broadcasted_iota(jnp.int32, (c, c), 0)
    tc = lax.broadcasted_iota(jnp.int32, (c, c), 1)
    a = jnp.where(tr == tc, _dot_nt(q.astype(BF16), k.astype(BF16)), 0.0)
    for lv in range(1, levels + 1):
        e = jnp.exp(-jnp.abs(cum - cs[lv * c:(lv + 1) * c]))
        upper = ((row >> (lv - 1)) & 1) == 1
        q_side = jnp.logical_not(upper) if reverse else upper
        ql = jnp.where(q_side, q * e, 0.0).astype(BF16)
        kl = jnp.where(q_side, 0.0, k * e).astype(BF16)
        a = a + jnp.where((tr >> lv) == (tc >> lv), _dot_nt(ql, kl), 0.0)
    o = _dot(a.astype(BF16), v)
    o = o + _dot_nt((q * jnp.exp(cum)).astype(BF16), s_t.astype(BF16))
    ks = (k * jnp.exp(tot - cum)).astype(BF16)
    s_new = s_t * jnp.exp(tot) + _dot_tn(v, ks)
    return o, s_new


def _hgrn_kernel(qz_ref, zf_ref, zb_ref, v_ref, gz_ref, lbf_ref, lbb_ref, og_ref, mf_ref, mb_ref,
                 o_ref, of_ref, ob_ref, st_ref, *, n_ctx_chunks):
    c = GLA_CHUNK
    t = qz_ref.shape[0]
    n = t // c
    st_ref[...] = jnp.zeros_like(st_ref)
    mf = mf_ref[...]
    mb = mb_ref[...]
    lbf = lbf_ref[...]
    lbb = lbb_ref[...]

    def load(z_ref, lb, r0):
        qz = qz_ref[pl.ds(r0, c), :].astype(F32)
        q = qz * _sigmoid(qz) * (HGRN_K_DIM ** -0.5)
        f = lb + (1.0 - lb) * _sigmoid(z_ref[pl.ds(r0, c), :].astype(F32))
        return q, 1.0 - f, v_ref[pl.ds(r0, c), :], jnp.log(f)

    def body(i, carry):
        rf = pl.multiple_of(i * c, c)
        cb = jnp.where(i < n_ctx_chunks, n_ctx_chunks - 1 - i, n - 1 - i + n_ctx_chunks)
        rb = pl.multiple_of(cb * c, c)
        q, k, v, g = load(zf_ref, lbf, rf)
        o, s_new = _gla_chunk(q, k, v, g, st_ref[0], mf, reverse=False)
        of_ref[pl.ds(rf, c), :] = o
        st_ref[0] = s_new
        q, k, v, g = load(zb_ref, lbb, rb)
        o, s_new = _gla_chunk(q, k, v, g, st_ref[1], mb, reverse=True)
        ob_ref[pl.ds(rb, c), :] = o
        st_ref[1] = s_new
        return carry

    lax.fori_loop(0, n, body, 0)
    gz = gz_ref[...].astype(F32)
    o = _rms(of_ref[...] + ob_ref[...]) * og_ref[...]
    o_ref[...] = (o * (gz * _sigmoid(gz))).astype(o_ref.dtype)


def hgrn_bidirectional(p, lb_fwd, lb_bwd, out_gain, *, n_ctx, blk0):
    b, t, _ = p.shape
    c = GLA_CHUNK
    mf, mb = _gla_constants(c)
    h = HGRN_HEADS
    seg = lambda s: pl.BlockSpec((None, t, LANES), lambda bi, hi: (bi, 0, blk0 + s * h + hi))
    per_head = pl.BlockSpec((None, 1, LANES), lambda bi, hi: (hi, 0, 0))
    const = lambda a: pl.BlockSpec(a.shape, lambda bi, hi: (0, 0))
    return pl.pallas_call(
        functools.partial(_hgrn_kernel, n_ctx_chunks=n_ctx // c),
        grid=(b, h),
        in_specs=[seg(0), seg(1), seg(2), seg(3), seg(4), per_head, per_head,
                  pl.BlockSpec((1, LANES), lambda bi, hi: (0, 0)), const(mf), const(mb)],
        out_specs=pl.BlockSpec((None, t, LANES), lambda bi, hi: (bi, 0, hi)),
        out_shape=jax.ShapeDtypeStruct((b, t, h * LANES), BF16),
        scratch_shapes=[pltpu.VMEM((t, LANES), F32), pltpu.VMEM((t, LANES), F32),
                        pltpu.VMEM((2, LANES, LANES), F32)],
        compiler_params=_cparams(("arbitrary", "arbitrary")),
        name="hgrn_bidirectional",
    )(p, p, p, p, p, lb_fwd.reshape(h, 1, LANES), lb_bwd.reshape(h, 1, LANES),
      out_gain.astype(F32).reshape(1, LANES), mf, mb)


def _mla_attn_kernel(q_ref, kv_ref, kr_ref, cq_ref, saq_ref, sbq_ref, ck_ref, sak_ref, sbk_ref,
                     qn_ref, qr_ref, kn_ref, krg_ref, o_ref, kc_ref):
    qi = pl.program_id(2)

    @pl.when(qi == 0)
    def _():
        kn = _rms(kv_ref[:, 0:MLA_NOPE].astype(F32)) * kn_ref[...]
        kr = _rms(kr_ref[...].astype(F32), MLA_ROPE) * krg_ref[...]
        kr = _rope(kr, ck_ref[...], sak_ref[...], sbk_ref[...])
        kc_ref[:, 0:LANES] = kn.astype(BF16)
        kc_ref[:, LANES:2 * LANES] = kr.astype(BF16)

    scale = (MLA_NOPE + MLA_ROPE) ** -0.5
    qn = _rms(q_ref[:, 0:MLA_NOPE].astype(F32)) * (qn_ref[...] * scale)
    qr = _rms(q_ref[:, MLA_NOPE:2 * LANES].astype(F32), MLA_ROPE) * qr_ref[...]
    qr = _rope(qr, cq_ref[...], saq_ref[...], sbq_ref[...]) * scale
    qc = jnp.concatenate([qn.astype(BF16), qr.astype(BF16)], axis=-1)
    e, l = _softmax_parts(_dot_nt(qc, kc_ref[...]))
    o = _dot(e.astype(BF16), kv_ref[:, MLA_NOPE:MLA_NOPE + MLA_V]) * (1.0 / l)
    o_ref[...] = o.astype(o_ref.dtype)


def mla_attention(q, kv, p1, kr_blk, rope, qn_gain, qr_gain, kn_gain, kr_gain, *, tq, n_ctx):
    b, n_lat, _ = q.shape
    t = kv.shape[1]
    c, sa, sb = rope
    q_tile0 = n_ctx // tq
    row_q = lambda bi, h, qi: (qi + q_tile0, 0)
    full = lambda bi, h, qi: (0, 0)
    pad = lambda g: jnp.concatenate([g.astype(F32), jnp.zeros((LANES - g.shape[0],), F32)]).reshape(1, LANES)
    return pl.pallas_call(
        _mla_attn_kernel,
        grid=(b, MLA_HEADS, n_lat // tq),
        in_specs=[
            pl.BlockSpec((None, tq, 2 * LANES), lambda bi, h, qi: (bi, qi, h)),
            pl.BlockSpec((None, t, 2 * LANES), lambda bi, h, qi: (bi, 0, h)),
            pl.BlockSpec((None, t, LANES), lambda bi, h, qi: (bi, 0, kr_blk)),
            pl.BlockSpec((tq, LANES), row_q), pl.BlockSpec((tq, LANES), row_q), pl.BlockSpec((tq, LANES), row_q),
            pl.BlockSpec((t, LANES), full), pl.BlockSpec((t, LANES), full), pl.BlockSpec((t, LANES), full),
            pl.BlockSpec((1, LANES), full), pl.BlockSpec((1, LANES), full),
            pl.BlockSpec((1, LANES), full), pl.BlockSpec((1, LANES), full),
        ],
        out_specs=pl.BlockSpec((None, tq, LANES), lambda bi, h, qi: (bi, qi, h)),
        out_shape=jax.ShapeDtypeStruct((b, n_lat, MLA_HEADS * MLA_V), BF16),
        scratch_shapes=[pltpu.VMEM((t, 2 * LANES), BF16)],
        compiler_params=_cparams(("arbitrary", "arbitrary", "arbitrary")),
        name="mla_attention",
    )(q, kv, p1, c, sa, sb, c, sa, sb, pad(qn_gain), pad(qr_gain), pad(kn_gain), pad(kr_gain))


def _proj_residual_kernel(*refs, n_in):
    a_refs = refs[:n_in]
    w_refs = refs[n_in:2 * n_in]
    x_ref, m_ref, o_ref = refs[2 * n_in:]
    acc = _dot(a_refs[0][...], w_refs[0][...])
    for a_ref, w_ref in zip(a_refs[1:], w_refs[1:]):
        acc = acc + _dot(a_ref[...], w_ref[...])
    o_ref[...] = x_ref[...] + m_ref[G_M:G_M + 1, :] * acc


def proj_residual(acts, weights, xs, mods, *, tm, n_ctx_tiles, row_tile0=0):
    b, r, _ = acts[0].shape
    d = xs.shape[-1]
    in_specs = [pl.BlockSpec((None, tm, a.shape[-1]), lambda bi, ti: (bi, ti, 0)) for a in acts]
    in_specs += [pl.BlockSpec(w.shape, lambda bi, ti: (0, 0)) for w in weights]
    in_specs += [
        pl.BlockSpec((None, tm, d), lambda bi, ti: (bi, ti + row_tile0, 0)),
        pl.BlockSpec((None, None, 6, d), lambda bi, ti: (bi, ((ti + row_tile0) >= n_ctx_tiles).astype(jnp.int32), 0, 0)),
    ]
    return pl.pallas_call(
        functools.partial(_proj_residual_kernel, n_in=len(acts)),
        grid=(b, r // tm),
        in_specs=in_specs,
        out_specs=pl.BlockSpec((None, tm, d), lambda bi, ti: (bi, ti, 0)),
        out_shape=jax.ShapeDtypeStruct((b, r, d), F32),
        compiler_params=_cparams(("arbitrary", "arbitrary")),
        name="proj_residual",
    )(*acts, *weights, xs, mods)


PAIRS = [(i, j) for i in range(EXPERTS_PER_GROUP) for j in range(i + 1, EXPERTS_PER_GROUP)]
N_CLASSES = N_GROUPS * len(PAIRS)
CLASS_LO = np.array([EXPERTS_PER_GROUP * g + i for g in range(N_GROUPS) for (i, j) in PAIRS], np.int32)
CLASS_HI = np.array([EXPERTS_PER_GROUP * g + j for g in range(N_GROUPS) for (i, j) in PAIRS], np.int32)
EXPERT_TILE = 256
ROW_SLABS = 8
DMA_UNROLL = 8


def _route_kernel(x_ref, m_ref, g_ref, rw_ref, rb_ref, hf_ref, info_ref, count_ref, cnt_ref):
    h = _rms(x_ref[...]) * g_ref[...]
    h = h * (1.0 + m_ref[SC_F:SC_F + 1, :]) + m_ref[SH_F:SH_F + 1, :]
    hf_ref[...] = h
    h1, h2, h3 = _split3(h)
    w1, w2, w3 = _split3(rw_ref[...])
    logits = (_dot_nt(w1, h1) + (_dot_nt(w1, h2) + _dot_nt(w2, h1))
              + (_dot_nt(w1, h3) + _dot_nt(w2, h2) + _dot_nt(w3, h1)))
    biased = _sigmoid(logits) + rb_ref[...]
    row = [biased[e:e + 1, :] for e in range(N_EXPERTS)]
    gscore = []
    for g in range(N_GROUPS):
        m = row[4 * g:4 * g + 4]
        gscore.append(functools.reduce(jnp.maximum, [m[i] + m[j] for (i, j) in PAIRS]))
    hits = []
    for g in range(N_GROUPS):
        best = None
        for g2 in range(N_GROUPS):
            if g2 == g:
                continue
            wins = (gscore[g] > gscore[g2]) if g2 < g else (gscore[g] >= gscore[g2])
            best = wins if best is None else jnp.logical_and(best, wins)
        chosen = []
        for i in range(EXPERTS_PER_GROUP):
            rank = None
            for j in range(EXPERTS_PER_GROUP):
                if j == i:
                    continue
                mi, mj = row[4 * g + i], row[4 * g + j]
                ahead = ((mj >= mi) if j < i else (mj > mi)).astype(jnp.int32)
                rank = ahead if rank is None else rank + ahead
            chosen.append(rank < 2)
        for (i, j) in PAIRS:
            hits.append(jnp.where(best & chosen[i] & chosen[j], 1.0, 0.0))
    onehot = jnp.concatenate(hits, axis=0)
    tm = onehot.shape[1]
    upper = (lax.broadcasted_iota(jnp.int32, (tm, tm), 0) <= lax.broadcasted_iota(jnp.int32, (tm, tm), 1))
    prefix = _dot(onehot.astype(BF16), upper.astype(BF16))

    @pl.when((pl.program_id(0) == 0) & (pl.program_id(1) == 0))
    def _():
        cnt_ref[...] = jnp.zeros_like(cnt_ref)

    seen = cnt_ref[...]
    cls_id = lax.broadcasted_iota(jnp.int32, onehot.shape, 0).astype(F32)
    cls = jnp.sum(onehot * cls_id, axis=0, keepdims=True)
    rank = jnp.sum(onehot * (seen[:, 0:1] + prefix - 1.0), axis=0, keepdims=True)
    info_ref[...] = jnp.concatenate([cls, rank, jnp.zeros((6, tm), F32)], axis=0).astype(jnp.int32)
    seen = seen + jnp.sum(onehot, axis=1, keepdims=True)
    cnt_ref[...] = seen
    count_ref[...] = seen


def route(xs, mods, gain, router_w, router_bias, *, tm, n_ctx_tiles):
    b, t, d = xs.shape
    nt = t // tm
    return pl.pallas_call(
        _route_kernel,
        grid=(b, nt),
        in_specs=[
            pl.BlockSpec((None, tm, d), lambda bi, ti: (bi, ti, 0)),
            pl.BlockSpec((None, None, 6, d), lambda bi, ti: (bi, (ti >= n_ctx_tiles).astype(jnp.int32), 0, 0)),
            pl.BlockSpec((1, d), lambda bi, ti: (0, 0)),
            pl.BlockSpec((N_EXPERTS, d), lambda bi, ti: (0, 0)),
            pl.BlockSpec((N_EXPERTS, 1), lambda bi, ti: (0, 0)),
        ],
        out_specs=[
            pl.BlockSpec((None, tm, d), lambda bi, ti: (bi, ti, 0)),
            pl.BlockSpec((None, None, 8, tm), lambda bi, ti: (bi, ti, 0, 0)),
            pl.BlockSpec((N_CLASSES, LANES), lambda bi, ti: (0, 0)),
        ],
        out_shape=[jax.ShapeDtypeStruct((b, t, d), F32),
                   jax.ShapeDtypeStruct((b, nt, 8, tm), jnp.int32),
                   jax.ShapeDtypeStruct((N_CLASSES, LANES), F32)],
        scratch_shapes=[pltpu.VMEM((N_CLASSES, LANES), F32)],
        compiler_params=_cparams(("arbitrary", "arbitrary")),
        name="route",
    )(xs, mods, gain.reshape(1, d).astype(F32), router_w.T.astype(F32),
      router_bias.reshape(N_EXPERTS, 1).astype(F32))


def _for_rows(n, fn):
    def body(k, carry):
        for u in range(DMA_UNROLL):
            fn(k * DMA_UNROLL + u)
        return carry
    lax.fori_loop(0, n // DMA_UNROLL, body, 0)


def _dispatch_kernel(dest_ref, hf_ref, xinit_ref, xs_ref, buf_ref, sem_ref):
    del xinit_ref
    i = pl.program_id(0)
    n = pl.num_programs(0)
    tm = hf_ref.shape[0]
    slot = i % 2

    def wait_rows(s):
        pltpu.make_async_copy(buf_ref.at[s], xs_ref.at[pl.ds(0, tm)], sem_ref.at[s]).wait()

    @pl.when(i >= 2)
    def _():
        wait_rows(slot)

    for k in range(ROW_SLABS):
        buf_ref[slot, :, k, :] = hf_ref[:, k * LANES:(k + 1) * LANES]
    base = i * tm
    _for_rows(tm, lambda t: pltpu.make_async_copy(
        buf_ref.at[slot, t], xs_ref.at[dest_ref[base + t]], sem_ref.at[slot]).start())

    @pl.when(i == n - 1)
    def _():
        wait_rows(slot)

    @pl.when((i == n - 1) & (n > 1))
    def _():
        wait_rows(1 - slot)


def dispatch(dest, hf2d, n_rows_out, *, tm):
    n, d = hf2d.shape
    xinit = jnp.zeros((n_rows_out, ROW_SLABS, LANES), F32)
    return pl.pallas_call(
        _dispatch_kernel,
        grid_spec=pltpu.PrefetchScalarGridSpec(
            num_scalar_prefetch=1,
            grid=(n // tm,),
            in_specs=[pl.BlockSpec((tm, d), lambda i, dest: (i, 0)),
                      pl.BlockSpec(memory_space=pl.ANY)],
            out_specs=pl.BlockSpec(memory_space=pl.ANY),
            scratch_shapes=[pltpu.VMEM((2, tm, ROW_SLABS, LANES), F32), pltpu.SemaphoreType.DMA((2,))],
        ),
        out_shape=jax.ShapeDtypeStruct((n_rows_out, ROW_SLABS, LANES), F32),
        input_output_aliases={2: 0},
        compiler_params=_cparams(("arbitrary",)),
        name="moe_dispatch",
    )(dest, hf2d, xinit)


def _expert_kernel(te_ref, tv_ref, x_ref, rwl_ref, rwh_ref, wg_ref, wu_ref, wd_ref, y_ref, xb_ref, w_ref, acc_ref):
    i = pl.program_id(0)
    s = pl.program_id(1)
    valid = tv_ref[i] != 0

    @pl.when(valid & (s == 0))
    def _():
        x = jnp.concatenate([x_ref[:, k, :] for k in range(ROW_SLABS)], axis=-1)
        xb_ref[...] = x.astype(BF16)
        s_lo = _sigmoid(jnp.sum(x * rwl_ref[...], axis=-1, keepdims=True))
        s_hi = _sigmoid(jnp.sum(x * rwh_ref[...], axis=-1, keepdims=True))
        inv = 1.0 / (s_lo + s_hi)
        w_ref[0] = jnp.broadcast_to(s_lo * inv, w_ref.shape[1:])
        w_ref[1] = jnp.broadcast_to(s_hi * inv, w_ref.shape[1:])

    def ffn():
        xb = xb_ref[...]
        a = _dot(xb, wg_ref[...])
        u = _dot(xb, wu_ref[...])
        return _dot((a * _sigmoid(a) * u).astype(BF16), wd_ref[...])

    @pl.when(valid & (s == 0))
    def _():
        acc_ref[...] = w_ref[0][:, 0:1] * ffn()

    @pl.when(valid & (s == 1))
    def _():
        y = acc_ref[...] + w_ref[1][:, 0:1] * ffn()
        for k in range(ROW_SLABS):
            y_ref[:, k, :] = y[:, k * LANES:(k + 1) * LANES]

    @pl.when(jnp.logical_not(valid) & (s == 1))
    def _():
        y_ref[...] = jnp.zeros_like(y_ref)


def experts(tile_expert, tile_valid, x_sorted, router_w_t, wg, wu, wd):
    rows = x_sorted.shape[0]
    tg = EXPERT_TILE
    n_tiles = rows // tg
    _, d, ff = wg.shape
    wspec = lambda shape: pl.BlockSpec((None,) + shape, lambda i, s, te, tv: (te[s * n_tiles + i], 0, 0))
    return pl.pallas_call(
        _expert_kernel,
        grid_spec=pltpu.PrefetchScalarGridSpec(
            num_scalar_prefetch=2,
            grid=(n_tiles, 2),
            in_specs=[
                pl.BlockSpec((tg, ROW_SLABS, LANES), lambda i, s, te, tv: (i, 0, 0)),
                pl.BlockSpec((None, 1, d), lambda i, s, te, tv: (te[i], 0, 0)),
                pl.BlockSpec((None, 1, d), lambda i, s, te, tv: (te[n_tiles + i], 0, 0)),
                wspec((d, ff)), wspec((d, ff)), wspec((ff, d)),
            ],
            out_specs=pl.BlockSpec((tg, ROW_SLABS, LANES), lambda i, s, te, tv: (i, 0, 0)),
            scratch_shapes=[pltpu.VMEM((tg, d), BF16), pltpu.VMEM((2, tg, LANES), F32), pltpu.VMEM((tg, d), F32)],
        ),
        out_shape=jax.ShapeDtypeStruct((rows, ROW_SLABS, LANES), F32),
        compiler_params=_cparams(("arbitrary", "arbitrary")),
        name="moe_experts",
    )(tile_expert, tile_valid, x_sorted, router_w_t, router_w_t, wg, wu, wd)


def _combine_kernel(dest_ref, hf_ref, x_ref, m_ref, wg_ref, wu_ref, wd_ref, y_hbm, o_ref, ybuf_ref, sem_ref):
    i = pl.program_id(0)
    n = pl.num_programs(0)
    tm = hf_ref.shape[0]
    slot = i % 2

    def fetch(tile, s):
        base = tile * tm
        _for_rows(tm, lambda t: pltpu.make_async_copy(
            y_hbm.at[dest_ref[base + t]], ybuf_ref.at[s, t], sem_ref.at[s]).start())

    @pl.when(i == 0)
    def _():
        fetch(0, 0)

    @pl.when(i + 1 < n)
    def _():
        fetch(i + 1, 1 - slot)

    h = hf_ref[...].astype(BF16)
    a = _dot(h, wg_ref[...])
    u = _dot(h, wu_ref[...])
    shared = _dot((a * _sigmoid(a) * u).astype(BF16), wd_ref[...])
    pltpu.make_async_copy(y_hbm.at[pl.ds(0, tm)], ybuf_ref.at[slot], sem_ref.at[slot]).wait()
    for k in range(ROW_SLABS):
        cols = slice(k * LANES, (k + 1) * LANES)
        o_ref[:, cols] = x_ref[:, cols] + m_ref[G_F:G_F + 1, cols] * (shared[:, cols] + ybuf_ref[slot, :, k, :])


def combine(dest, hf, xs, mods, wg, wu, wd, y_sorted, *, tm, n_ctx_tiles):
    b, r, d = hf.shape
    nt = r // tm
    ff = wg.shape[1]
    row = lambda i, dest: (i // nt, i % nt, 0)
    const = lambda i, dest: (0, 0)
    return pl.pallas_call(
        _combine_kernel,
        grid_spec=pltpu.PrefetchScalarGridSpec(
            num_scalar_prefetch=1,
            grid=(b * nt,),
            in_specs=[
                pl.BlockSpec((None, tm, d), row),
                pl.BlockSpec((None, tm, d), row),
                pl.BlockSpec((None, None, 6, d),
                             lambda i, dest: (i // nt, ((i % nt) >= n_ctx_tiles).astype(jnp.int32), 0, 0)),
                pl.BlockSpec((d, ff), const), pl.BlockSpec((d, ff), const), pl.BlockSpec((ff, d), const),
                pl.BlockSpec(memory_space=pl.ANY),
            ],
            out_specs=pl.BlockSpec((None, tm, d), row),
            scratch_shapes=[pltpu.VMEM((2, tm, ROW_SLABS, LANES), F32), pltpu.SemaphoreType.DMA((2,))],
        ),
        out_shape=jax.ShapeDtypeStruct((b, r, d), F32),
        compiler_params=_cparams(("arbitrary",)),
        name="moe_combine",
    )(dest, hf, xs, mods, wg, wu, wd, y_sorted)


def moe_block(xs, mods, gain, router_w, router_bias, ew_gate, ew_up, ew_down, sw_gate, sw_up, sw_down,
              *, tm, n_ctx_tiles):
    b, r, d = xs.shape
    n = b * r
    tg = EXPERT_TILE
    hf, info, counts = route(xs, mods, gain, router_w, router_bias, tm=tm, n_ctx_tiles=n_ctx_tiles)
    cls = info[:, :, 0, :].reshape(n)
    rank = info[:, :, 1, :].reshape(n)
    padded = ((counts[:, 0].astype(jnp.int32) + tg - 1) // tg) * tg
    ends = jnp.cumsum(padded)
    dest = (ends - padded)[cls] + rank
    n_tiles = n // tg + N_CLASSES
    tile_start = jnp.arange(n_tiles, dtype=jnp.int32) * tg
    tile_valid = tile_start < ends[-1]
    last_start = jnp.maximum(ends[-1] - tg, 0)
    start = jnp.where(tile_valid, tile_start, last_start)
    tile_cls = jnp.sum((ends[None, :] <= start[:, None]).astype(jnp.int32), axis=1)
    tile_cls = jnp.minimum(tile_cls, N_CLASSES - 1)
    tile_expert = jnp.concatenate([jnp.asarray(CLASS_LO)[tile_cls], jnp.asarray(CLASS_HI)[tile_cls]])
    x_sorted = dispatch(dest, hf.reshape(n, d), n_tiles * tg, tm=tm)
    y_sorted = experts(tile_expert, tile_valid.astype(jnp.int32), x_sorted,
                       router_w.T.astype(F32).reshape(N_EXPERTS, 1, d),
                       ew_gate.astype(BF16), ew_up.astype(BF16), ew_down.astype(BF16))
    return combine(dest, hf, xs, mods, sw_gate.astype(BF16), sw_up.astype(BF16), sw_down.astype(BF16), y_sorted,
                   tm=tm, n_ctx_tiles=n_ctx_tiles)


def kernel(x, c, ctx, c_ctx, mod_w, mod_b, norm_mix, norm_ffn, even_w_in, even_w_out, diff_q_gain, diff_k_gain, diff_lambda, diff_subln, hgrn_lb_logits, hgrn_out_gain, odd_w_in, mla_q_a_gain, mla_kv_a_gain, mla_w_uq, mla_w_ukv, mla_q_nope_gain, mla_q_rope_gain, mla_k_nope_gain, mla_k_rope_gain, odd_w_out, router_w, router_bias, expert_w_gate, expert_w_up, expert_w_down, shared_w_gate, shared_w_up, shared_w_down):
    b, n_lat, d = x.shape
    n_ctx = ctx.shape[1]
    depth = mod_w.shape[0]
    tm = 256 if n_ctx % 256 == 0 else 128
    n_ctx_tiles = n_ctx // tm
    rope = rope_lane_tables(n_ctx, n_lat)

    mod_rows = 16
    c_rows = jnp.concatenate([c, c_ctx[None, :], jnp.zeros((mod_rows - b - 1, d), F32)], axis=0)
    mod_all = modulation(c_rows, mod_w, mod_b).reshape(depth, mod_rows, 6, d)
    lb_all = jnp.cumsum(jax.nn.softmax(hgrn_lb_logits.astype(F32), axis=0), axis=0)

    xs = jnp.concatenate([ctx, x], axis=1)
    for layer in range(depth):
        last = layer == depth - 1
        j = layer // 2
        mods = jnp.stack([jnp.broadcast_to(mod_all[layer, b], (b, 6, d)), mod_all[layer, :b]], axis=1)
        row_tile0 = n_ctx_tiles if last else 0
        if layer % 2 == 0:
            lam_init = 0.8 - 0.6 * math.exp(-0.3 * layer)
            p = norm_matmul(xs, norm_mix[layer], even_w_in[j].astype(BF16), tm=tm, mods=mods, n_ctx_tiles=n_ctx_tiles)
            oa = diff_attention(p, diff_lambda[j], diff_q_gain[j], diff_k_gain[j], diff_subln[j], rope,
                                tq=tm, n_ctx=n_ctx, lam_init=lam_init, q_blk0=0, k_blk0=4, v_blk0=8)
            ob = hgrn_bidirectional(p, lb_all[j, 0], lb_all[j, 1], hgrn_out_gain[j], n_ctx=n_ctx, blk0=12)
            if last:
                oa, ob = oa[:, n_ctx:], ob[:, n_ctx:]
            w_out = even_w_out[j].astype(BF16)
            half = oa.shape[-1]
            xs_mid = proj_residual([oa, ob], [w_out[:half], w_out[half:]], xs, mods, tm=tm,
                                   n_ctx_tiles=n_ctx_tiles, row_tile0=row_tile0)
        else:
            w_in = odd_w_in[j]
            zpad = jnp.zeros((d, LANES - MLA_ROPE), F32)
            w_in_r = jnp.concatenate([w_in[:, MLA_Q_LORA:MLA_Q_LORA + MLA_KV_LORA], w_in[:, MLA_Q_LORA + MLA_KV_LORA:],
                                      zpad, w_in[:, :MLA_Q_LORA]], axis=1).astype(BF16)
            p1 = norm_matmul(xs, norm_mix[layer], w_in_r, tm=tm, mods=mods, n_ctx_tiles=n_ctx_tiles)
            w_uq = mla_w_uq[j].reshape(MLA_Q_LORA, MLA_HEADS, MLA_NOPE + MLA_ROPE)
            w_uq = jnp.concatenate([w_uq, jnp.zeros((MLA_Q_LORA, MLA_HEADS, LANES - MLA_ROPE), F32)], axis=-1)
            w_uq = w_uq.reshape(MLA_Q_LORA, MLA_HEADS * 2 * LANES).astype(BF16)
            q = norm_matmul(p1, mla_q_a_gain[j], w_uq, tm=tm, col_block=1, row_tile0=row_tile0)
            kv = norm_matmul(p1, mla_kv_a_gain[j], mla_w_ukv[j].astype(BF16), tm=tm, col_block=0)
            if not last:
                raise NotImplementedError("context queries for a non-final latent-attention layer")
            o = mla_attention(q, kv, p1, 2, rope, mla_q_nope_gain[j], mla_q_rope_gain[j], mla_k_nope_gain[j],
                              mla_k_rope_gain[j], tq=tm, n_ctx=n_ctx)
            xs_mid = proj_residual([o], [odd_w_out[j].astype(BF16)], xs, mods, tm=tm,
                                   n_ctx_tiles=n_ctx_tiles, row_tile0=row_tile0)
        xs = moe_block(xs_mid, mods, norm_ffn[layer], router_w, router_bias,
                       expert_w_gate[layer], expert_w_up[layer], expert_w_down[layer],
                       shared_w_gate[layer], shared_w_up[layer], shared_w_down[layer],
                       tm=tm, n_ctx_tiles=0 if last else n_ctx_tiles)
    return xs if xs.shape[1] == n_lat else xs[:, n_ctx:]
```

```python
import functools
import math

import numpy as np
import jax
import jax.numpy as jnp
from jax import lax
from jax.experimental import pallas as pl
from jax.experimental.pallas import tpu as pltpu

F32 = jnp.float32
BF16 = jnp.bfloat16

LANES = 128
VMEM_LIMIT = 56 * 1024 * 1024

GRID_W = 64
DIFF_HEADS = 4
DIFF_HEAD_DIM = 64
HGRN_HEADS = 4
HGRN_K_DIM = 128
MLA_HEADS = 8
MLA_NOPE = 128
MLA_ROPE = 64
MLA_V = 128
MLA_Q_LORA = 384
MLA_KV_LORA = 256
N_EXPERTS = 16
N_GROUPS = 4
EXPERTS_PER_GROUP = 4
ROPE_BASE = 10000.0
EPS = 1e-6
LOG2E = 1.4426950408889634
GLA_CHUNK = 64
ATTN_KEY_CHUNK = 1152
MLA_HEADS_PER_STEP = 2
DIFF_HEADS_PER_STEP = 2
HGRN_HEADS_PER_STEP = 2

SH_M, SC_M, G_M, SH_F, SC_F, G_F = range(6)


def _sigmoid(x):
    return 1.0 / (1.0 + jnp.exp(-x))


def _dot(a, b):
    return jnp.dot(a, b, preferred_element_type=F32)


def _dot_nt(a, b):
    return lax.dot_general(a, b, (((1,), (1,)), ((), ())), preferred_element_type=F32)


def _dot_tn(a, b):
    return lax.dot_general(a, b, (((0,), (0,)), ((), ())), preferred_element_type=F32)


def _split3(x):
    hi = x.astype(BF16)
    r = x - hi.astype(F32)
    mid = r.astype(BF16)
    lo = (r - mid.astype(F32)).astype(BF16)
    return hi, mid, lo


def _rms(x, width=None):
    n = x.shape[-1] if width is None else width
    return x * lax.rsqrt(jnp.sum(x * x, axis=-1, keepdims=True) * (1.0 / n) + EPS)


def _cparams(sem):
    return pltpu.CompilerParams(dimension_semantics=sem, vmem_limit_bytes=VMEM_LIMIT)


def _mod_kernel(c_ref, w_ref, b_ref, o_ref):
    c = c_ref[...]
    s = c * _sigmoid(c)
    o_ref[...] = _dot(s.astype(BF16), w_ref[...].astype(BF16)) + b_ref[...]


def modulation(c_rows, mod_w, mod_b):
    n_layers, d, n = mod_w.shape
    rows = c_rows.shape[0]
    tn = 1536
    return pl.pallas_call(
        _mod_kernel,
        grid=(n_layers, n // tn),
        in_specs=[
            pl.BlockSpec((rows, d), lambda l, j: (0, 0)),
            pl.BlockSpec((None, d, tn), lambda l, j: (l, 0, j)),
            pl.BlockSpec((None, 1, tn), lambda l, j: (l, 0, j)),
        ],
        out_specs=pl.BlockSpec((None, rows, tn), lambda l, j: (l, 0, j)),
        out_shape=jax.ShapeDtypeStruct((n_layers, rows, n), F32),
        compiler_params=_cparams(("arbitrary", "arbitrary")),
        name="modulation",
    )(c_rows, mod_w, mod_b.reshape(n_layers, 1, n))


def _norm_matmul_kernel(*refs, has_mod):
    if has_mod:
        x_ref, m_ref, g_ref, w_ref, o_ref = refs
    else:
        x_ref, g_ref, w_ref, o_ref = refs
    h = _rms(x_ref[...].astype(F32)) * g_ref[...]
    if has_mod:
        h = h * (1.0 + m_ref[SC_M:SC_M + 1, :]) + m_ref[SH_M:SH_M + 1, :]
    o_ref[...] = _dot(h.astype(BF16), w_ref[...]).astype(o_ref.dtype)


def norm_matmul(x, gain, w, *, tm, col_block=0, row_tile0=0, n_row_tiles=None, mods=None, n_ctx_tiles=0):
    b, t, _ = x.shape
    k, n = w.shape
    if n_row_tiles is None:
        n_row_tiles = t // tm - row_tile0
    in_specs = [pl.BlockSpec((None, tm, k), lambda bi, ti: (bi, ti + row_tile0, col_block))]
    args = [x]
    if mods is not None:
        in_specs.append(pl.BlockSpec((None, None, 6, k),
                                     lambda bi, ti: (bi, ((ti + row_tile0) >= n_ctx_tiles).astype(jnp.int32), 0, 0)))
        args.append(mods)
    in_specs += [pl.BlockSpec((1, k), lambda bi, ti: (0, 0)), pl.BlockSpec((k, n), lambda bi, ti: (0, 0))]
    args += [gain.reshape(1, k).astype(F32), w]
    return pl.pallas_call(
        functools.partial(_norm_matmul_kernel, has_mod=mods is not None),
        grid=(b, n_row_tiles),
        in_specs=in_specs,
        out_specs=pl.BlockSpec((None, tm, n), lambda bi, ti: (bi, ti, 0)),
        out_shape=jax.ShapeDtypeStruct((b, n_row_tiles * tm, n), BF16),
        compiler_params=_cparams(("arbitrary", "arbitrary")),
        name="norm_matmul",
    )(*args)


def rope_lane_tables(n_ctx, n_lat):
    rows = n_lat // GRID_W
    row = np.repeat(np.arange(rows), GRID_W).astype(np.float32)
    col = np.tile(np.arange(GRID_W), rows).astype(np.float32)
    axis_dim = DIFF_HEAD_DIM // 2
    inv_freq = jnp.asarray(ROPE_BASE, F32) ** (-jnp.arange(0, axis_dim, 2, dtype=F32) / axis_dim)
    ang_r = jnp.asarray(row)[:, None] * inv_freq
    ang_c = jnp.asarray(col)[:, None] * inv_freq
    lane = np.arange(LANES)
    freq_idx = lane % 16
    use_col = (lane % 64) >= 32
    first = (lane % 32) < 16
    ang = jnp.where(use_col[None, :], ang_c[:, freq_idx], ang_r[:, freq_idx])
    cos, sin = jnp.cos(ang), jnp.sin(ang)
    c = jnp.concatenate([jnp.ones((n_ctx, LANES), F32), cos], axis=0)
    sa = jnp.concatenate([jnp.zeros((n_ctx, LANES), F32), jnp.where(first[None, :], -sin, 0.0)], axis=0)
    sb = jnp.concatenate([jnp.zeros((n_ctx, LANES), F32), jnp.where(first[None, :], 0.0, sin)], axis=0)
    return c, sa, sb


def _rope(x, c, sa, sb):
    return x * c + pltpu.roll(x, LANES - 16, 1) * sa + pltpu.roll(x, 16, 1) * sb


def _block_ones(width):
    r = lax.broadcasted_iota(jnp.int32, (LANES, LANES), 0) // width
    c = lax.broadcasted_iota(jnp.int32, (LANES, LANES), 1) // width
    return (r == c).astype(BF16)


def _block_rms(x, ones, width):
    xx = x * x
    hi = xx.astype(BF16)
    lo = (xx - hi.astype(F32)).astype(BF16)
    ms = (_dot(hi, ones) + _dot(lo, ones)) * (1.0 / width)
    return x * lax.rsqrt(ms + EPS)


def _attend_streams(qs, kt_refs, vs, nk):
    n = len(qs)
    chunk = ATTN_KEY_CHUNK if nk % ATTN_KEY_CHUNK == 0 else nk
    nc = nk // chunk
    s = [[None] * nc for _ in range(n)]
    p = [[None] * nc for _ in range(n)]
    m, l, o = [None] * n, [None] * n, [None] * n

    def logits(i, c):
        s[i][c] = _dot(qs[i], kt_refs[i][:, c * chunk:(c + 1) * chunk])
        mc = jnp.max(s[i][c], axis=-1, keepdims=True)
        m[i] = mc if c == 0 else jnp.maximum(m[i], mc)

    def exps(i, c):
        e = jnp.exp2(s[i][c] - m[i])
        lc = jnp.sum(e, axis=-1, keepdims=True)
        l[i] = lc if c == 0 else l[i] + lc
        p[i][c] = e.astype(BF16)

    def values(i, c):
        ref, col0 = vs[i]
        oc = _dot(p[i][c], ref[c * chunk:(c + 1) * chunk, col0:col0 + LANES])
        o[i] = oc if c == 0 else o[i] + oc

    for step in range(n + 2):
        for c in range(nc):
            if step < n:
                logits(step, c)
            if 0 <= step - 1 < n:
                exps(step - 1, c)
            if 0 <= step - 2 < n:
                values(step - 2, c)
    return list(zip(o, l))


def _diff_attn_kernel(lam_ref, q_ref, k_ref, v_ref, cq_ref, saq_ref, sbq_ref, ck_ref, sak_ref, sbk_ref,
                      qg_ref, kg_ref, sub_ref, o_ref, kt_ref, *, n_ctx, n_ctx_tiles, lam_init):
    qi = pl.program_id(2)
    ones = _block_ones(DIFF_HEAD_DIM)
    heads = range(DIFF_HEADS_PER_STEP)

    @pl.when(qi == 0)
    def _():
        for h in heads:
            k = _block_rms(k_ref[:, h * LANES:(h + 1) * LANES].astype(F32), ones, DIFF_HEAD_DIM) * kg_ref[...]
            kt_ref[h] = _rope(k, ck_ref[...], sak_ref[...], sbk_ref[...]).T.astype(BF16)

    qs = []
    for h in heads:
        q = _block_rms(q_ref[:, h * LANES:(h + 1) * LANES].astype(F32), ones, DIFF_HEAD_DIM) * qg_ref[...]
        q = _rope(q, cq_ref[...], saq_ref[...], sbq_ref[...]) * (DIFF_HEAD_DIM ** -0.5 * LOG2E)
        lane = lax.broadcasted_iota(jnp.int32, q.shape, 1)
        qs.append(jnp.where(lane < DIFF_HEAD_DIM, q, 0.0).astype(BF16))
        qs.append(jnp.where(lane >= DIFF_HEAD_DIM, q, 0.0).astype(BF16))
    lv = lam_ref[...]
    lam = (jnp.exp(jnp.sum(lv[0:1] * lv[1:2], axis=-1, keepdims=True))
           - jnp.exp(jnp.sum(lv[2:3] * lv[3:4], axis=-1, keepdims=True)) + lam_init)

    def attend(nk):
        outs = _attend_streams(qs, [kt_ref.at[h] for h in heads for _ in range(2)],
                               [(v_ref, h * LANES) for h in heads for _ in range(2)], nk)
        for h in heads:
            (o1, l1), (o2, l2) = outs[2 * h], outs[2 * h + 1]
            o = o1 * (1.0 / l1) - o2 * (lam / l2)
            o_ref[:, h * LANES:(h + 1) * LANES] = (_rms(o) * sub_ref[...] * (1.0 - lam_init)).astype(o_ref.dtype)

    @pl.when(qi < n_ctx_tiles)
    def _():
        attend(n_ctx)

    @pl.when(qi >= n_ctx_tiles)
    def _():
        attend(k_ref.shape[0])


def diff_attention(p, lam_vecs, q_gain, k_gain, subln, rope, *, tq, n_ctx, lam_init, q_blk0, k_blk0, v_blk0):
    b, t, _ = p.shape
    c, sa, sb = rope
    nq = t // tq
    hps = DIFF_HEADS_PER_STEP
    assert q_blk0 % hps == 0 and k_blk0 % hps == 0 and v_blk0 % hps == 0
    row_q = lambda bi, h, qi: (qi, 0)
    full = lambda bi, h, qi: (0, 0)
    tile2 = lambda g: jnp.tile(g.astype(F32), 2).reshape(1, LANES)
    return pl.pallas_call(
        functools.partial(_diff_attn_kernel, n_ctx=n_ctx, n_ctx_tiles=n_ctx // tq, lam_init=lam_init),
        grid=(b, DIFF_HEADS // hps, nq),
        in_specs=[
            pl.BlockSpec((4, DIFF_HEAD_DIM), full),
            pl.BlockSpec((None, tq, hps * LANES), lambda bi, h, qi: (bi, qi, q_blk0 // hps + h)),
            pl.BlockSpec((None, t, hps * LANES), lambda bi, h, qi: (bi, 0, k_blk0 // hps + h)),
            pl.BlockSpec((None, t, hps * LANES), lambda bi, h, qi: (bi, 0, v_blk0 // hps + h)),
            pl.BlockSpec((tq, LANES), row_q), pl.BlockSpec((tq, LANES), row_q), pl.BlockSpec((tq, LANES), row_q),
            pl.BlockSpec((t, LANES), full), pl.BlockSpec((t, LANES), full), pl.BlockSpec((t, LANES), full),
            pl.BlockSpec((1, LANES), full), pl.BlockSpec((1, LANES), full), pl.BlockSpec((1, LANES), full),
        ],
        out_specs=pl.BlockSpec((None, tq, hps * LANES), lambda bi, h, qi: (bi, qi, h)),
        out_shape=jax.ShapeDtypeStruct((b, t, DIFF_HEADS * LANES), BF16),
        scratch_shapes=[pltpu.VMEM((hps, LANES, t), BF16)],
        compiler_params=_cparams(("arbitrary", "arbitrary", "arbitrary")),
        name="diff_attention",
    )(lam_vecs.astype(F32), p, p, p, c, sa, sb, c, sa, sb, tile2(q_gain), tile2(k_gain),
      subln.astype(F32).reshape(1, LANES))


def _gla_constants(c):
    levels = int(math.log2(c))
    t = np.arange(c)[:, None]
    u = np.arange(c)[None, :]
    stack = [[u <= t], [u >= t]]
    qside = [[], []]
    pair = [[t == u], [t == u]]
    for lv in range(1, levels + 1):
        base = (t >> lv) << lv
        half = 1 << (lv - 1)
        stack[0].append(u <= base + half - 1)
        stack[1].append(u >= base + half)
        up_t = ((t >> (lv - 1)) & 1) == 1
        up_u = ((u >> (lv - 1)) & 1) == 1
        same = (t >> lv) == (u >> lv)
        qside[0].append(np.broadcast_to(up_t, (c, LANES)))
        qside[1].append(np.broadcast_to(~up_t, (c, LANES)))
        pair[0].append(same & up_t & ~up_u)
        pair[1].append(same & ~up_t & up_u)
    f32 = lambda x: np.asarray(x, np.float32)
    stack = np.stack([np.tile(np.concatenate(f32(m), axis=0), (1, 3)) for m in stack])
    return (jnp.asarray(stack, BF16), jnp.asarray(np.stack([f32(m) for m in qside])),
            jnp.asarray(np.stack([f32(m) for m in pair])))


def _gla_chunk(q, k, v, g2, s, stack, qside_ref, pair_ref, d):
    c = q.shape[0]
    levels = int(math.log2(c))
    gcat = jnp.concatenate(_split3(g2), axis=0)
    cs = _dot(stack, gcat)
    tot_col = _dot_tn(gcat, jnp.ones((3 * c, LANES), BF16))
    diag = jnp.sum(q * k, axis=-1, keepdims=True)
    yield
    cum = cs[0:c]
    tot = cum[0:1] if d == 1 else cum[c - 1:c]
    zz = []
    for lv in range(1, levels + 1):
        e = jnp.exp2(-jnp.abs(cum - cs[lv * c:(lv + 1) * c]))
        z = (jnp.where(qside_ref[d, lv - 1] > 0.5, q, k) * e).astype(BF16)
        zz.append(_dot_nt(z, z))
    q_in = (q * jnp.exp2(cum)).astype(BF16)
    ks = (k * jnp.exp2(tot - cum)).astype(BF16)
    ds = _dot_tn(ks, v)
    yield
    a = pair_ref[d, 0] * diag
    for lv in range(1, levels + 1):
        a = a + pair_ref[d, lv] * zz[lv - 1]
    lhs = jnp.concatenate([q_in, a.astype(BF16)], axis=1)
    o = _dot(lhs, jnp.concatenate([s.astype(BF16), v], axis=0))
    s_new = s * jnp.exp2(tot_col) + ds
    yield
    return o, s_new


def _run_interleaved(gens):
    results = [None] * len(gens)
    live = list(range(len(gens)))
    while live:
        for i in list(live):
            try:
                next(gens[i])
            except StopIteration as stop:
                results[i] = stop.value
                live.remove(i)
    return results


def _hgrn_kernel(qz_ref, zf_ref, zb_ref, v_ref, gz_ref, lbf_ref, lbb_ref, og_ref, stack_ref, qside_ref, pair_ref,
                 o_ref, q_ref, kf_ref, gf_ref, kb_ref, gb_ref, of_ref, ob_ref, st_ref, *, n_ctx_chunks):
    c = GLA_CHUNK
    t = qz_ref.shape[0]
    n = t // c
    heads = range(HGRN_HEADS_PER_STEP)
    qz = qz_ref[...].astype(F32)
    q_ref[...] = qz * _sigmoid(qz) * (HGRN_K_DIM ** -0.5)
    for z_ref, lb_ref, k_ref, g_ref in ((zf_ref, lbf_ref, kf_ref, gf_ref), (zb_ref, lbb_ref, kb_ref, gb_ref)):
        lb = jnp.concatenate([lb_ref[h] for h in heads], axis=-1)
        f = lb + (1.0 - lb) * _sigmoid(z_ref[...].astype(F32))
        k_ref[...] = 1.0 - f
        g_ref[...] = jnp.log(f) * LOG2E
    st_ref[...] = jnp.zeros_like(st_ref)

    def body(i, carry):
        rf = pl.multiple_of(i * c, c)
        cb = jnp.where(i < n_ctx_chunks, n_ctx_chunks - 1 - i, n - 1 - i + n_ctx_chunks)
        rb = pl.multiple_of(cb * c, c)
        chains = [(h, d, pl.ds(r0, c), k_ref, g_ref, out_ref) for h in heads
                  for d, r0, k_ref, g_ref, out_ref in ((0, rf, kf_ref, gf_ref, of_ref), (1, rb, kb_ref, gb_ref, ob_ref))]
        col = lambda h: slice(h * LANES, (h + 1) * LANES)
        outs = _run_interleaved([
            _gla_chunk(q_ref[rows, col(h)], k_ref[rows, col(h)], v_ref[rows, col(h)], g_ref[rows, col(h)],
                       st_ref[h, d], stack_ref[d], qside_ref, pair_ref, d)
            for (h, d, rows, k_ref, g_ref, _) in chains])
        for (h, d, rows, _, _, out_ref), (o, s_new) in zip(chains, outs):
            out_ref[rows, col(h)] = o
            st_ref[h, d] = s_new
        return carry

    lax.fori_loop(0, n, body, 0)
    gz = gz_ref[...].astype(F32)
    for h in heads:
        cols = slice(h * LANES, (h + 1) * LANES)
        o = _rms(of_ref[:, cols] + ob_ref[:, cols]) * og_ref[...]
        o_ref[:, cols] = (o * (gz[:, cols] * _sigmoid(gz[:, cols]))).astype(o_ref.dtype)


def hgrn_bidirectional(p, lb_fwd, lb_bwd, out_gain, *, n_ctx, blk0):
    b, t, _ = p.shape
    c = GLA_CHUNK
    consts = _gla_constants(c)
    h = HGRN_HEADS
    hps = HGRN_HEADS_PER_STEP
    assert blk0 % hps == 0 and h % hps == 0
    seg = lambda s: pl.BlockSpec((None, t, hps * LANES), lambda bi, hi: (bi, 0, (blk0 + s * h) // hps + hi))
    per_head = pl.BlockSpec((hps, 1, LANES), lambda bi, hi: (hi, 0, 0))
    const = lambda a: pl.BlockSpec(a.shape, lambda bi, hi: (0,) * a.ndim)
    seq = pltpu.VMEM((t, hps * LANES), F32)
    return pl.pallas_call(
        functools.partial(_hgrn_kernel, n_ctx_chunks=n_ctx // c),
        grid=(b, h // hps),
        in_specs=[seg(0), seg(1), seg(2), seg(3), seg(4), per_head, per_head,
                  pl.BlockSpec((1, LANES), lambda bi, hi: (0, 0))] + [const(a) for a in consts],
        out_specs=pl.BlockSpec((None, t, hps * LANES), lambda bi, hi: (bi, 0, hi)),
        out_shape=jax.ShapeDtypeStruct((b, t, h * LANES), BF16),
        scratch_shapes=[seq, seq, seq, seq, seq, seq, seq, pltpu.VMEM((hps, 2, LANES, LANES), F32)],
        compiler_params=_cparams(("arbitrary", "arbitrary")),
        name="hgrn_bidirectional",
    )(p, p, p, p, p, lb_fwd.reshape(h, 1, LANES), lb_bwd.reshape(h, 1, LANES),
      out_gain.astype(F32).reshape(1, LANES), *consts)


def _mla_attn_kernel(q_ref, kv_ref, kr_ref, cq_ref, saq_ref, sbq_ref, ck_ref, sak_ref, sbk_ref,
                     qn_ref, qr_ref, kn_ref, krg_ref, o_ref, kt_ref):
    qi = pl.program_id(2)
    ones_nope = _block_ones(MLA_NOPE)
    ones_rope = _block_ones(MLA_ROPE)
    hw = 2 * LANES

    @pl.when(qi == 0)
    def _():
        kr = _block_rms(kr_ref[...].astype(F32), ones_rope, MLA_ROPE) * krg_ref[...]
        kr = _rope(kr, ck_ref[...], sak_ref[...], sbk_ref[...]).T.astype(BF16)
        for h in range(MLA_HEADS_PER_STEP):
            kn = _block_rms(kv_ref[:, h * hw:h * hw + MLA_NOPE].astype(F32), ones_nope, MLA_NOPE) * kn_ref[...]
            kt_ref[h, 0:LANES, :] = kn.T.astype(BF16)
            kt_ref[h, LANES:hw, :] = kr

    scale = (MLA_NOPE + MLA_ROPE) ** -0.5 * LOG2E
    qs = []
    for h in range(MLA_HEADS_PER_STEP):
        qn = _block_rms(q_ref[:, h * hw:h * hw + MLA_NOPE].astype(F32), ones_nope, MLA_NOPE) * (qn_ref[...] * scale)
        qr = _block_rms(q_ref[:, h * hw + MLA_NOPE:(h + 1) * hw].astype(F32), ones_rope, MLA_ROPE) * qr_ref[...]
        qr = _rope(qr, cq_ref[...], saq_ref[...], sbq_ref[...]) * scale
        qs.append(jnp.concatenate([qn.astype(BF16), qr.astype(BF16)], axis=-1))
    heads = range(MLA_HEADS_PER_STEP)
    outs = _attend_streams(qs, [kt_ref.at[h] for h in heads], [(kv_ref, h * hw + MLA_NOPE) for h in heads],
                           kt_ref.shape[2])
    for h, (o, l) in enumerate(outs):
        o_ref[:, h * MLA_V:(h + 1) * MLA_V] = (o * (1.0 / l)).astype(o_ref.dtype)


def mla_attention(q, kv, p1, kr_blk, rope, qn_gain, qr_gain, kn_gain, kr_gain, *, tq, n_ctx):
    b, n_lat, _ = q.shape
    t = kv.shape[1]
    c, sa, sb = rope
    hps = MLA_HEADS_PER_STEP
    q_tile0 = n_ctx // tq
    row_q = lambda bi, h, qi: (qi + q_tile0, 0)
    full = lambda bi, h, qi: (0, 0)
    pad = lambda g: jnp.concatenate([g.astype(F32), jnp.zeros((LANES - g.shape[0],), F32)]).reshape(1, LANES)
    return pl.pallas_call(
        _mla_attn_kernel,
        grid=(b, MLA_HEADS // hps, n_lat // tq),
        in_specs=[
            pl.BlockSpec((None, tq, hps * 2 * LANES), lambda bi, h, qi: (bi, qi, h)),
            pl.BlockSpec((None, t, hps * 2 * LANES), lambda bi, h, qi: (bi, 0, h)),
            pl.BlockSpec((None, t, LANES), lambda bi, h, qi: (bi, 0, kr_blk)),
            pl.BlockSpec((tq, LANES), row_q), pl.BlockSpec((tq, LANES), row_q), pl.BlockSpec((tq, LANES), row_q),
            pl.BlockSpec((t, LANES), full), pl.BlockSpec((t, LANES), full), pl.BlockSpec((t, LANES), full),
            pl.BlockSpec((1, LANES), full), pl.BlockSpec((1, LANES), full),
            pl.BlockSpec((1, LANES), full), pl.BlockSpec((1, LANES), full),
        ],
        out_specs=pl.BlockSpec((None, tq, hps * MLA_V), lambda bi, h, qi: (bi, qi, h)),
        out_shape=jax.ShapeDtypeStruct((b, n_lat, MLA_HEADS * MLA_V), BF16),
        scratch_shapes=[pltpu.VMEM((hps, 2 * LANES, t), BF16)],
        compiler_params=_cparams(("arbitrary", "arbitrary", "arbitrary")),
        name="mla_attention",
    )(q, kv, p1, c, sa, sb, c, sa, sb, pad(qn_gain), pad(qr_gain), pad(kn_gain), pad(kr_gain))


def _proj_residual_kernel(*refs, n_in):
    a_refs = refs[:n_in]
    w_refs = refs[n_in:2 * n_in]
    x_ref, m_ref, o_ref = refs[2 * n_in:]
    acc = _dot(a_refs[0][...], w_refs[0][...])
    for a_ref, w_ref in zip(a_refs[1:], w_refs[1:]):
        acc = acc + _dot(a_ref[...], w_ref[...])
    o_ref[...] = x_ref[...] + m_ref[G_M:G_M + 1, :] * acc


def proj_residual(acts, weights, xs, mods, *, tm, n_ctx_tiles, row_tile0=0):
    b, r, _ = acts[0].shape
    d = xs.shape[-1]
    in_specs = [pl.BlockSpec((None, tm, a.shape[-1]), lambda bi, ti: (bi, ti, 0)) for a in acts]
    in_specs += [pl.BlockSpec(w.shape, lambda bi, ti: (0, 0)) for w in weights]
    in_specs += [
        pl.BlockSpec((None, tm, d), lambda bi, ti: (bi, ti + row_tile0, 0)),
        pl.BlockSpec((None, None, 6, d), lambda bi, ti: (bi, ((ti + row_tile0) >= n_ctx_tiles).astype(jnp.int32), 0, 0)),
    ]
    return pl.pallas_call(
        functools.partial(_proj_residual_kernel, n_in=len(acts)),
        grid=(b, r // tm),
        in_specs=in_specs,
        out_specs=pl.BlockSpec((None, tm, d), lambda bi, ti: (bi, ti, 0)),
        out_shape=jax.ShapeDtypeStruct((b, r, d), F32),
        compiler_params=_cparams(("arbitrary", "arbitrary")),
        name="proj_residual",
    )(*acts, *weights, xs, mods)


PAIRS = [(i, j) for i in range(EXPERTS_PER_GROUP) for j in range(i + 1, EXPERTS_PER_GROUP)]
N_CLASSES = N_GROUPS * len(PAIRS)
CLASS_LO = np.array([EXPERTS_PER_GROUP * g + i for g in range(N_GROUPS) for (i, j) in PAIRS], np.int32)
CLASS_HI = np.array([EXPERTS_PER_GROUP * g + j for g in range(N_GROUPS) for (i, j) in PAIRS], np.int32)
EXPERT_TILE = 256
ROW_SLABS = 8
DMA_UNROLL = 8


def _route_kernel(x_ref, m_ref, g_ref, rw_ref, rb_ref, hf_ref, info_ref, count_ref, cnt_ref):
    h = _rms(x_ref[...]) * g_ref[...]
    h = h * (1.0 + m_ref[SC_F:SC_F + 1, :]) + m_ref[SH_F:SH_F + 1, :]
    hf_ref[...] = h
    h1, h2, h3 = _split3(h)
    w1, w2, w3 = _split3(rw_ref[...])
    logits = (_dot_nt(w1, h1) + (_dot_nt(w1, h2) + _dot_nt(w2, h1))
              + (_dot_nt(w1, h3) + _dot_nt(w2, h2) + _dot_nt(w3, h1)))
    biased = _sigmoid(logits) + rb_ref[...]
    row = [biased[e:e + 1, :] for e in range(N_EXPERTS)]
    gscore = []
    for g in range(N_GROUPS):
        m = row[4 * g:4 * g + 4]
        gscore.append(functools.reduce(jnp.maximum, [m[i] + m[j] for (i, j) in PAIRS]))
    hits = []
    for g in range(N_GROUPS):
        best = None
        for g2 in range(N_GROUPS):
            if g2 == g:
                continue
            wins = (gscore[g] > gscore[g2]) if g2 < g else (gscore[g] >= gscore[g2])
            best = wins if best is None else jnp.logical_and(best, wins)
        chosen = []
        for i in range(EXPERTS_PER_GROUP):
            rank = None
            for j in range(EXPERTS_PER_GROUP):
                if j == i:
                    continue
                mi, mj = row[4 * g + i], row[4 * g + j]
                ahead = ((mj >= mi) if j < i else (mj > mi)).astype(jnp.int32)
                rank = ahead if rank is None else rank + ahead
            chosen.append(rank < 2)
        for (i, j) in PAIRS:
            hits.append(jnp.where(best & chosen[i] & chosen[j], 1.0, 0.0))
    onehot = jnp.concatenate(hits, axis=0)
    tm = onehot.shape[1]
    upper = (lax.broadcasted_iota(jnp.int32, (tm, tm), 0) <= lax.broadcasted_iota(jnp.int32, (tm, tm), 1))
    prefix = _dot(onehot.astype(BF16), upper.astype(BF16))

    @pl.when((pl.program_id(0) == 0) & (pl.program_id(1) == 0))
    def _():
        cnt_ref[...] = jnp.zeros_like(cnt_ref)

    seen = cnt_ref[...]
    cls_id = lax.broadcasted_iota(jnp.int32, onehot.shape, 0).astype(F32)
    cls = jnp.sum(onehot * cls_id, axis=0, keepdims=True)
    rank = jnp.sum(onehot * (seen[:, 0:1] + prefix - 1.0), axis=0, keepdims=True)
    info_ref[...] = jnp.concatenate([cls, rank, jnp.zeros((6, tm), F32)], axis=0).astype(jnp.int32)
    seen = seen + jnp.sum(onehot, axis=1, keepdims=True)
    cnt_ref[...] = seen
    count_ref[...] = seen


def route(xs, mods, gain, router_w, router_bias, *, tm, n_ctx_tiles):
    b, t, d = xs.shape
    nt = t // tm
    return pl.pallas_call(
        _route_kernel,
        grid=(b, nt),
        in_specs=[
            pl.BlockSpec((None, tm, d), lambda bi, ti: (bi, ti, 0)),
            pl.BlockSpec((None, None, 6, d), lambda bi, ti: (bi, (ti >= n_ctx_tiles).astype(jnp.int32), 0, 0)),
            pl.BlockSpec((1, d), lambda bi, ti: (0, 0)),
            pl.BlockSpec((N_EXPERTS, d), lambda bi, ti: (0, 0)),
            pl.BlockSpec((N_EXPERTS, 1), lambda bi, ti: (0, 0)),
        ],
        out_specs=[
            pl.BlockSpec((None, tm, d), lambda bi, ti: (bi, ti, 0)),
            pl.BlockSpec((None, None, 8, tm), lambda bi, ti: (bi, ti, 0, 0)),
            pl.BlockSpec((N_CLASSES, LANES), lambda bi, ti: (0, 0)),
        ],
        out_shape=[jax.ShapeDtypeStruct((b, t, d), F32),
                   jax.ShapeDtypeStruct((b, nt, 8, tm), jnp.int32),
                   jax.ShapeDtypeStruct((N_CLASSES, LANES), F32)],
        scratch_shapes=[pltpu.VMEM((N_CLASSES, LANES), F32)],
        compiler_params=_cparams(("arbitrary", "arbitrary")),
        name="route",
    )(xs, mods, gain.reshape(1, d).astype(F32), router_w.T.astype(F32),
      router_bias.reshape(N_EXPERTS, 1).astype(F32))


def _for_rows(n, fn):
    def body(k, carry):
        for u in range(DMA_UNROLL):
            fn(k * DMA_UNROLL + u)
        return carry
    lax.fori_loop(0, n // DMA_UNROLL, body, 0)


def _dispatch_kernel(dest_ref, hf_ref, xinit_ref, xs_ref, buf_ref, sem_ref):
    del xinit_ref
    i = pl.program_id(0)
    n = pl.num_programs(0)
    tm = hf_ref.shape[0]
    slot = i % 2

    def wait_rows(s):
        pltpu.make_async_copy(buf_ref.at[s], xs_ref.at[pl.ds(0, tm)], sem_ref.at[s]).wait()

    @pl.when(i >= 2)
    def _():
        wait_rows(slot)

    for k in range(ROW_SLABS):
        buf_ref[slot, :, k, :] = hf_ref[:, k * LANES:(k + 1) * LANES]
    base = i * tm
    _for_rows(tm, lambda t: pltpu.make_async_copy(
        buf_ref.at[slot, t], xs_ref.at[dest_ref[base + t]], sem_ref.at[slot]).start())

    @pl.when(i == n - 1)
    def _():
        wait_rows(slot)

    @pl.when((i == n - 1) & (n > 1))
    def _():
        wait_rows(1 - slot)


def dispatch(dest, hf2d, n_rows_out, *, tm):
    n, d = hf2d.shape
    xinit = jnp.zeros((n_rows_out, ROW_SLABS, LANES), F32)
    return pl.pallas_call(
        _dispatch_kernel,
        grid_spec=pltpu.PrefetchScalarGridSpec(
            num_scalar_prefetch=1,
            grid=(n // tm,),
            in_specs=[pl.BlockSpec((tm, d), lambda i, dest: (i, 0)),
                      pl.BlockSpec(memory_space=pl.ANY)],
            out_specs=pl.BlockSpec(memory_space=pl.ANY),
            scratch_shapes=[pltpu.VMEM((2, tm, ROW_SLABS, LANES), F32), pltpu.SemaphoreType.DMA((2,))],
        ),
        out_shape=jax.ShapeDtypeStruct((n_rows_out, ROW_SLABS, LANES), F32),
        input_output_aliases={2: 0},
        compiler_params=_cparams(("arbitrary",)),
        name="moe_dispatch",
    )(dest, hf2d, xinit)


def _expert_kernel(te_ref, tv_ref, x_ref, rwl_ref, rwh_ref, wg_ref, wu_ref, wd_ref, y_ref, xb_ref, w_ref, acc_ref):
    i = pl.program_id(0)
    s = pl.program_id(1)
    valid = tv_ref[i] != 0

    @pl.when(valid & (s == 0))
    def _():
        x = jnp.concatenate([x_ref[:, k, :] for k in range(ROW_SLABS)], axis=-1)
        xb_ref[...] = x.astype(BF16)
        s_lo = _sigmoid(jnp.sum(x * rwl_ref[...], axis=-1, keepdims=True))
        s_hi = _sigmoid(jnp.sum(x * rwh_ref[...], axis=-1, keepdims=True))
        inv = 1.0 / (s_lo + s_hi)
        w_ref[0] = jnp.broadcast_to(s_lo * inv, w_ref.shape[1:])
        w_ref[1] = jnp.broadcast_to(s_hi * inv, w_ref.shape[1:])

    def ffn():
        xb = xb_ref[...]
        a = _dot(xb, wg_ref[...])
        u = _dot(xb, wu_ref[...])
        return _dot((a * _sigmoid(a) * u).astype(BF16), wd_ref[...])

    @pl.when(valid & (s == 0))
    def _():
        acc_ref[...] = w_ref[0][:, 0:1] * ffn()

    @pl.when(valid & (s == 1))
    def _():
        y = acc_ref[...] + w_ref[1][:, 0:1] * ffn()
        for k in range(ROW_SLABS):
            y_ref[:, k, :] = y[:, k * LANES:(k + 1) * LANES]

    @pl.when(jnp.logical_not(valid) & (s == 1))
    def _():
        y_ref[...] = jnp.zeros_like(y_ref)


def experts(tile_expert, tile_valid, x_sorted, router_w_t, wg, wu, wd):
    rows = x_sorted.shape[0]
    tg = EXPERT_TILE
    n_tiles = rows // tg
    _, d, ff = wg.shape
    wspec = lambda shape: pl.BlockSpec((None,) + shape, lambda i, s, te, tv: (te[s * n_tiles + i], 0, 0))
    return pl.pallas_call(
        _expert_kernel,
        grid_spec=pltpu.PrefetchScalarGridSpec(
            num_scalar_prefetch=2,
            grid=(n_tiles, 2),
            in_specs=[
                pl.BlockSpec((tg, ROW_SLABS, LANES), lambda i, s, te, tv: (i, 0, 0)),
                pl.BlockSpec((None, 1, d), lambda i, s, te, tv: (te[i], 0, 0)),
                pl.BlockSpec((None, 1, d), lambda i, s, te, tv: (te[n_tiles + i], 0, 0)),
                wspec((d, ff)), wspec((d, ff)), wspec((ff, d)),
            ],
            out_specs=pl.BlockSpec((tg, ROW_SLABS, LANES), lambda i, s, te, tv: (i, 0, 0)),
            scratch_shapes=[pltpu.VMEM((tg, d), BF16), pltpu.VMEM((2, tg, LANES), F32), pltpu.VMEM((tg, d), F32)],
        ),
        out_shape=jax.ShapeDtypeStruct((rows, ROW_SLABS, LANES), F32),
        compiler_params=_cparams(("arbitrary", "arbitrary")),
        name="moe_experts",
    )(tile_expert, tile_valid, x_sorted, router_w_t, router_w_t, wg, wu, wd)


def _combine_kernel(dest_ref, hf_ref, x_ref, m_ref, wg_ref, wu_ref, wd_ref, y_hbm, o_ref, ybuf_ref, sem_ref):
    i = pl.program_id(0)
    n = pl.num_programs(0)
    tm = hf_ref.shape[0]
    slot = i % 2

    def fetch(tile, s):
        base = tile * tm
        _for_rows(tm, lambda t: pltpu.make_async_copy(
            y_hbm.at[dest_ref[base + t]], ybuf_ref.at[s, t], sem_ref.at[s]).start())

    @pl.when(i == 0)
    def _():
        fetch(0, 0)

    @pl.when(i + 1 < n)
    def _():
        fetch(i + 1, 1 - slot)

    h = hf_ref[...].astype(BF16)
    a = _dot(h, wg_ref[...])
    u = _dot(h, wu_ref[...])
    shared = _dot((a * _sigmoid(a) * u).astype(BF16), wd_ref[...])
    pltpu.make_async_copy(y_hbm.at[pl.ds(0, tm)], ybuf_ref.at[slot], sem_ref.at[slot]).wait()
    for k in range(ROW_SLABS):
        cols = slice(k * LANES, (k + 1) * LANES)
        o_ref[:, cols] = x_ref[:, cols] + m_ref[G_F:G_F + 1, cols] * (shared[:, cols] + ybuf_ref[slot, :, k, :])


def combine(dest, hf, xs, mods, wg, wu, wd, y_sorted, *, tm, n_ctx_tiles):
    b, r, d = hf.shape
    nt = r // tm
    ff = wg.shape[1]
    row = lambda i, dest: (i // nt, i % nt, 0)
    const = lambda i, dest: (0, 0)
    return pl.pallas_call(
        _combine_kernel,
        grid_spec=pltpu.PrefetchScalarGridSpec(
            num_scalar_prefetch=1,
            grid=(b * nt,),
            in_specs=[
                pl.BlockSpec((None, tm, d), row),
                pl.BlockSpec((None, tm, d), row),
                pl.BlockSpec((None, None, 6, d),
                             lambda i, dest: (i // nt, ((i % nt) >= n_ctx_tiles).astype(jnp.int32), 0, 0)),
                pl.BlockSpec((d, ff), const), pl.BlockSpec((d, ff), const), pl.BlockSpec((ff, d), const),
                pl.BlockSpec(memory_space=pl.ANY),
            ],
            out_specs=pl.BlockSpec((None, tm, d), row),
            scratch_shapes=[pltpu.VMEM((2, tm, ROW_SLABS, LANES), F32), pltpu.SemaphoreType.DMA((2,))],
        ),
        out_shape=jax.ShapeDtypeStruct((b, r, d), F32),
        compiler_params=_cparams(("arbitrary",)),
        name="moe_combine",
    )(dest, hf, xs, mods, wg, wu, wd, y_sorted)


def moe_block(xs, mods, gain, router_w, router_bias, ew_gate, ew_up, ew_down, sw_gate, sw_up, sw_down,
              *, tm, n_ctx_tiles):
    b, r, d = xs.shape
    n = b * r
    tg = EXPERT_TILE
    hf, info, counts = route(xs, mods, gain, router_w, router_bias, tm=tm, n_ctx_tiles=n_ctx_tiles)
    cls = info[:, :, 0, :].reshape(n)
    rank = info[:, :, 1, :].reshape(n)
    padded = ((counts[:, 0].astype(jnp.int32) + tg - 1) // tg) * tg
    ends = jnp.cumsum(padded)
    dest = (ends - padded)[cls] + rank
    n_tiles = n // tg + N_CLASSES
    tile_start = jnp.arange(n_tiles, dtype=jnp.int32) * tg
    tile_valid = tile_start < ends[-1]
    last_start = jnp.maximum(ends[-1] - tg, 0)
    start = jnp.where(tile_valid, tile_start, last_start)
    tile_cls = jnp.sum((ends[None, :] <= start[:, None]).astype(jnp.int32), axis=1)
    tile_cls = jnp.minimum(tile_cls, N_CLASSES - 1)
    tile_expert = jnp.concatenate([jnp.asarray(CLASS_LO)[tile_cls], jnp.asarray(CLASS_HI)[tile_cls]])
    x_sorted = dispatch(dest, hf.reshape(n, d), n_tiles * tg, tm=tm)
    y_sorted = experts(tile_expert, tile_valid.astype(jnp.int32), x_sorted,
                       router_w.T.astype(F32).reshape(N_EXPERTS, 1, d),
                       ew_gate.astype(BF16), ew_up.astype(BF16), ew_down.astype(BF16))
    return combine(dest, hf, xs, mods, sw_gate.astype(BF16), sw_up.astype(BF16), sw_down.astype(BF16), y_sorted,
                   tm=tm, n_ctx_tiles=n_ctx_tiles)


def kernel(x, c, ctx, c_ctx, mod_w, mod_b, norm_mix, norm_ffn, even_w_in, even_w_out, diff_q_gain, diff_k_gain, diff_lambda, diff_subln, hgrn_lb_logits, hgrn_out_gain, odd_w_in, mla_q_a_gain, mla_kv_a_gain, mla_w_uq, mla_w_ukv, mla_q_nope_gain, mla_q_rope_gain, mla_k_nope_gain, mla_k_rope_gain, odd_w_out, router_w, router_bias, expert_w_gate, expert_w_up, expert_w_down, shared_w_gate, shared_w_up, shared_w_down):
    b, n_lat, d = x.shape
    n_ctx = ctx.shape[1]
    depth = mod_w.shape[0]
    tm = 256 if n_ctx % 256 == 0 else 128
    n_ctx_tiles = n_ctx // tm
    rope = rope_lane_tables(n_ctx, n_lat)

    mod_rows = 16
    c_rows = jnp.concatenate([c, c_ctx[None, :], jnp.zeros((mod_rows - b - 1, d), F32)], axis=0)
    mod_all = modulation(c_rows, mod_w, mod_b).reshape(depth, mod_rows, 6, d)
    lb_all = jnp.cumsum(jax.nn.softmax(hgrn_lb_logits.astype(F32), axis=0), axis=0)

    xs = jnp.concatenate([ctx, x], axis=1)
    for layer in range(depth):
        last = layer == depth - 1
        j = layer // 2
        mods = jnp.stack([jnp.broadcast_to(mod_all[layer, b], (b, 6, d)), mod_all[layer, :b]], axis=1)
        row_tile0 = n_ctx_tiles if last else 0
        if layer % 2 == 0:
            lam_init = 0.8 - 0.6 * math.exp(-0.3 * layer)
            p = norm_matmul(xs, norm_mix[layer], even_w_in[j].astype(BF16), tm=tm, mods=mods, n_ctx_tiles=n_ctx_tiles)
            oa = diff_attention(p, diff_lambda[j], diff_q_gain[j], diff_k_gain[j], diff_subln[j], rope,
                                tq=tm, n_ctx=n_ctx, lam_init=lam_init, q_blk0=0, k_blk0=4, v_blk0=8)
            ob = hgrn_bidirectional(p, lb_all[j, 0], lb_all[j, 1], hgrn_out_gain[j], n_ctx=n_ctx, blk0=12)
            if last:
                oa, ob = oa[:, n_ctx:], ob[:, n_ctx:]
            w_out = even_w_out[j].astype(BF16)
            half = oa.shape[-1]
            xs_mid = proj_residual([oa, ob], [w_out[:half], w_out[half:]], xs, mods, tm=tm,
                                   n_ctx_tiles=n_ctx_tiles, row_tile0=row_tile0)
        else:
            w_in = odd_w_in[j]
            zpad = jnp.zeros((d, LANES - MLA_ROPE), F32)
            w_in_r = jnp.concatenate([w_in[:, MLA_Q_LORA:MLA_Q_LORA + MLA_KV_LORA], w_in[:, MLA_Q_LORA + MLA_KV_LORA:],
                                      zpad, w_in[:, :MLA_Q_LORA]], axis=1).astype(BF16)
            p1 = norm_matmul(xs, norm_mix[layer], w_in_r, tm=tm, mods=mods, n_ctx_tiles=n_ctx_tiles)
            w_uq = mla_w_uq[j].reshape(MLA_Q_LORA, MLA_HEADS, MLA_NOPE + MLA_ROPE)
            w_uq = jnp.concatenate([w_uq, jnp.zeros((MLA_Q_LORA, MLA_HEADS, LANES - MLA_ROPE), F32)], axis=-1)
            w_uq = w_uq.reshape(MLA_Q_LORA, MLA_HEADS * 2 * LANES).astype(BF16)
            q = norm_matmul(p1, mla_q_a_gain[j], w_uq, tm=tm, col_block=1, row_tile0=row_tile0)
            kv = norm_matmul(p1, mla_kv_a_gain[j], mla_w_ukv[j].astype(BF16), tm=tm, col_block=0)
            if not last:
                raise NotImplementedError("context queries for a non-final latent-attention layer")
            o = mla_attention(q, kv, p1, 2, rope, mla_q_nope_gain[j], mla_q_rope_gain[j], mla_k_nope_gain[j],
                              mla_k_rope_gain[j], tq=tm, n_ctx=n_ctx)
            xs_mid = proj_residual([o], [odd_w_out[j].astype(BF16)], xs, mods, tm=tm,
                                   n_ctx_tiles=n_ctx_tiles, row_tile0=row_tile0)
        xs = moe_block(xs_mid, mods, norm_ffn[layer], router_w, router_bias,
                       expert_w_gate[layer], expert_w_up[layer], expert_w_down[layer],
                       shared_w_gate[layer], shared_w_up[layer], shared_w_down[layer],
                       tm=tm, n_ctx_tiles=0 if last else n_ctx_tiles)
    return xs if xs.shape[1] == n_lat else xs[:, n_ctx:]
```

```python
import functools
import math

import numpy as np
import jax
import jax.numpy as jnp
from jax import lax
from jax.experimental import pallas as pl
from jax.experimental.pallas import tpu as pltpu

F32 = jnp.float32
BF16 = jnp.bfloat16

LANES = 128
VMEM_LIMIT = 56 * 1024 * 1024

GRID_W = 64
DIFF_HEADS = 4
DIFF_HEAD_DIM = 64
HGRN_HEADS = 4
HGRN_K_DIM = 128
MLA_HEADS = 8
MLA_NOPE = 128
MLA_ROPE = 64
MLA_V = 128
MLA_Q_LORA = 384
MLA_KV_LORA = 256
N_EXPERTS = 16
N_GROUPS = 4
EXPERTS_PER_GROUP = 4
ROPE_BASE = 10000.0
EPS = 1e-6
LOG2E = 1.4426950408889634
GLA_CHUNK = 64
ATTN_KEY_CHUNK = 1152
MLA_HEADS_PER_STEP = 2
DIFF_HEADS_PER_STEP = 2
HGRN_HEADS_PER_STEP = 2

SH_M, SC_M, G_M, SH_F, SC_F, G_F = range(6)


def _sigmoid(x):
    return 1.0 / (1.0 + jnp.exp(-x))


def _dot(a, b):
    return jnp.dot(a, b, preferred_element_type=F32)


def _dot_nt(a, b):
    return lax.dot_general(a, b, (((1,), (1,)), ((), ())), preferred_element_type=F32)


def _dot_tn(a, b):
    return lax.dot_general(a, b, (((0,), (0,)), ((), ())), preferred_element_type=F32)


def _split3(x):
    hi = x.astype(BF16)
    r = x - hi.astype(F32)
    mid = r.astype(BF16)
    lo = (r - mid.astype(F32)).astype(BF16)
    return hi, mid, lo


def _rms(x, width=None):
    n = x.shape[-1] if width is None else width
    return x * lax.rsqrt(jnp.sum(x * x, axis=-1, keepdims=True) * (1.0 / n) + EPS)


def _cparams(sem):
    return pltpu.CompilerParams(dimension_semantics=sem, vmem_limit_bytes=VMEM_LIMIT)


def _mod_kernel(c_ref, w_ref, b_ref, o_ref):
    c = c_ref[...]
    s = c * _sigmoid(c)
    o_ref[...] = _dot(s.astype(BF16), w_ref[...].astype(BF16)) + b_ref[...]


def modulation(c_rows, mod_w, mod_b):
    n_layers, d, n = mod_w.shape
    rows = c_rows.shape[0]
    tn = 1536
    return pl.pallas_call(
        _mod_kernel,
        grid=(n_layers, n // tn),
        in_specs=[
            pl.BlockSpec((rows, d), lambda l, j: (0, 0)),
            pl.BlockSpec((None, d, tn), lambda l, j: (l, 0, j)),
            pl.BlockSpec((None, 1, tn), lambda l, j: (l, 0, j)),
        ],
        out_specs=pl.BlockSpec((None, rows, tn), lambda l, j: (l, 0, j)),
        out_shape=jax.ShapeDtypeStruct((n_layers, rows, n), F32),
        compiler_params=_cparams(("arbitrary", "arbitrary")),
        name="modulation",
    )(c_rows, mod_w, mod_b.reshape(n_layers, 1, n))


def _norm_matmul_kernel(*refs, has_mod):
    if has_mod:
        x_ref, m_ref, g_ref, w_ref, o_ref = refs
    else:
        x_ref, g_ref, w_ref, o_ref = refs
    h = _rms(x_ref[...].astype(F32)) * g_ref[...]
    if has_mod:
        h = h * (1.0 + m_ref[SC_M:SC_M + 1, :]) + m_ref[SH_M:SH_M + 1, :]
    o_ref[...] = _dot(h.astype(BF16), w_ref[...]).astype(o_ref.dtype)


def norm_matmul(x, gain, w, *, tm, col_block=0, row_tile0=0, n_row_tiles=None, mods=None, n_ctx_tiles=0):
    b, t, _ = x.shape
    k, n = w.shape
    if n_row_tiles is None:
        n_row_tiles = t // tm - row_tile0
    in_specs = [pl.BlockSpec((None, tm, k), lambda bi, ti: (bi, ti + row_tile0, col_block))]
    args = [x]
    if mods is not None:
        in_specs.append(pl.BlockSpec((None, None, 6, k),
                                     lambda bi, ti: (bi, ((ti + row_tile0) >= n_ctx_tiles).astype(jnp.int32), 0, 0)))
        args.append(mods)
    in_specs += [pl.BlockSpec((1, k), lambda bi, ti: (0, 0)), pl.BlockSpec((k, n), lambda bi, ti: (0, 0))]
    args += [gain.reshape(1, k).astype(F32), w]
    return pl.pallas_call(
        functools.partial(_norm_matmul_kernel, has_mod=mods is not None),
        grid=(b, n_row_tiles),
        in_specs=in_specs,
        out_specs=pl.BlockSpec((None, tm, n), lambda bi, ti: (bi, ti, 0)),
        out_shape=jax.ShapeDtypeStruct((b, n_row_tiles * tm, n), BF16),
        compiler_params=_cparams(("arbitrary", "arbitrary")),
        name="norm_matmul",
    )(*args)


def rope_lane_tables(n_ctx, n_lat):
    rows = n_lat // GRID_W
    row = np.repeat(np.arange(rows), GRID_W).astype(np.float32)
    col = np.tile(np.arange(GRID_W), rows).astype(np.float32)
    axis_dim = DIFF_HEAD_DIM // 2
    inv_freq = jnp.asarray(ROPE_BASE, F32) ** (-jnp.arange(0, axis_dim, 2, dtype=F32) / axis_dim)
    ang_r = jnp.asarray(row)[:, None] * inv_freq
    ang_c = jnp.asarray(col)[:, None] * inv_freq
    lane = np.arange(LANES)
    freq_idx = lane % 16
    use_col = (lane % 64) >= 32
    first = (lane % 32) < 16
    ang = jnp.where(use_col[None, :], ang_c[:, freq_idx], ang_r[:, freq_idx])
    cos, sin = jnp.cos(ang), jnp.sin(ang)
    c = jnp.concatenate([jnp.ones((n_ctx, LANES), F32), cos], axis=0)
    sa = jnp.concatenate([jnp.zeros((n_ctx, LANES), F32), jnp.where(first[None, :], -sin, 0.0)], axis=0)
    sb = jnp.concatenate([jnp.zeros((n_ctx, LANES), F32), jnp.where(first[None, :], 0.0, sin)], axis=0)
    return c, sa, sb


def _rope(x, c, sa, sb):
    return x * c + pltpu.roll(x, LANES - 16, 1) * sa + pltpu.roll(x, 16, 1) * sb


def _block_ones(width):
    r = lax.broadcasted_iota(jnp.int32, (LANES, LANES), 0) // width
    c = lax.broadcasted_iota(jnp.int32, (LANES, LANES), 1) // width
    return (r == c).astype(BF16)


def _block_rms(x, ones, width):
    xx = x * x
    hi = xx.astype(BF16)
    lo = (xx - hi.astype(F32)).astype(BF16)
    ms = (_dot(hi, ones) + _dot(lo, ones)) * (1.0 / width)
    return x * lax.rsqrt(ms + EPS)


def _attend_streams(qs, kt_refs, vs, nk):
    n = len(qs)
    chunk = ATTN_KEY_CHUNK if nk % ATTN_KEY_CHUNK == 0 else nk
    nc = nk // chunk
    s = [[None] * nc for _ in range(n)]
    p = [[None] * nc for _ in range(n)]
    m, l, o = [None] * n, [None] * n, [None] * n

    def logits(i, c):
        s[i][c] = _dot(qs[i], kt_refs[i][:, c * chunk:(c + 1) * chunk])
        mc = jnp.max(s[i][c], axis=-1, keepdims=True)
        m[i] = mc if c == 0 else jnp.maximum(m[i], mc)

    def exps(i, c):
        e = jnp.exp2(s[i][c] - m[i])
        lc = jnp.sum(e, axis=-1, keepdims=True)
        l[i] = lc if c == 0 else l[i] + lc
        p[i][c] = e.astype(BF16)

    def values(i, c):
        ref, col0 = vs[i]
        oc = _dot(p[i][c], ref[c * chunk:(c + 1) * chunk, col0:col0 + LANES])
        o[i] = oc if c == 0 else o[i] + oc

    for step in range(n + 2):
        for c in range(nc):
            if step < n:
                logits(step, c)
            if 0 <= step - 1 < n:
                exps(step - 1, c)
            if 0 <= step - 2 < n:
                values(step - 2, c)
    return list(zip(o, l))


def _diff_attn_kernel(lam_ref, q_ref, k_ref, v_ref, cq_ref, saq_ref, sbq_ref, ck_ref, sak_ref, sbk_ref,
                      qg_ref, kg_ref, sub_ref, o_ref, kt_ref, *, n_ctx, n_ctx_tiles, lam_init):
    qi = pl.program_id(2)
    ones = _block_ones(DIFF_HEAD_DIM)
    heads = range(DIFF_HEADS_PER_STEP)

    @pl.when(qi == 0)
    def _():
        for h in heads:
            k = _block_rms(k_ref[:, h * LANES:(h + 1) * LANES].astype(F32), ones, DIFF_HEAD_DIM) * kg_ref[...]
            kt_ref[h] = _rope(k, ck_ref[...], sak_ref[...], sbk_ref[...]).T.astype(BF16)

    qs = []
    for h in heads:
        q = _block_rms(q_ref[:, h * LANES:(h + 1) * LANES].astype(F32), ones, DIFF_HEAD_DIM) * qg_ref[...]
        q = _rope(q, cq_ref[...], saq_ref[...], sbq_ref[...]) * (DIFF_HEAD_DIM ** -0.5 * LOG2E)
        lane = lax.broadcasted_iota(jnp.int32, q.shape, 1)
        qs.append(jnp.where(lane < DIFF_HEAD_DIM, q, 0.0).astype(BF16))
        qs.append(jnp.where(lane >= DIFF_HEAD_DIM, q, 0.0).astype(BF16))
    lv = lam_ref[...]
    lam = (jnp.exp(jnp.sum(lv[0:1] * lv[1:2], axis=-1, keepdims=True))
           - jnp.exp(jnp.sum(lv[2:3] * lv[3:4], axis=-1, keepdims=True)) + lam_init)

    def attend(nk):
        outs = _attend_streams(qs, [kt_ref.at[h] for h in heads for _ in range(2)],
                               [(v_ref, h * LANES) for h in heads for _ in range(2)], nk)
        for h in heads:
            (o1, l1), (o2, l2) = outs[2 * h], outs[2 * h + 1]
            o = o1 * (1.0 / l1) - o2 * (lam / l2)
            o_ref[:, h * LANES:(h + 1) * LANES] = (_rms(o) * sub_ref[...] * (1.0 - lam_init)).astype(o_ref.dtype)

    @pl.when(qi < n_ctx_tiles)
    def _():
        attend(n_ctx)

    @pl.when(qi >= n_ctx_tiles)
    def _():
        attend(k_ref.shape[0])


def diff_attention(p, lam_vecs, q_gain, k_gain, subln, rope, *, tq, n_ctx, lam_init, q_blk0, k_blk0, v_blk0):
    b, t, _ = p.shape
    c, sa, sb = rope
    nq = t // tq
    hps = DIFF_HEADS_PER_STEP
    assert q_blk0 % hps == 0 and k_blk0 % hps == 0 and v_blk0 % hps == 0
    row_q = lambda bi, h, qi: (qi, 0)
    full = lambda bi, h, qi: (0, 0)
    tile2 = lambda g: jnp.tile(g.astype(F32), 2).reshape(1, LANES)
    return pl.pallas_call(
        functools.partial(_diff_attn_kernel, n_ctx=n_ctx, n_ctx_tiles=n_ctx // tq, lam_init=lam_init),
        grid=(b, DIFF_HEADS // hps, nq),
        in_specs=[
            pl.BlockSpec((4, DIFF_HEAD_DIM), full),
            pl.BlockSpec((None, tq, hps * LANES), lambda bi, h, qi: (bi, qi, q_blk0 // hps + h)),
            pl.BlockSpec((None, t, hps * LANES), lambda bi, h, qi: (bi, 0, k_blk0 // hps + h)),
            pl.BlockSpec((None, t, hps * LANES), lambda bi, h, qi: (bi, 0, v_blk0 // hps + h)),
            pl.BlockSpec((tq, LANES), row_q), pl.BlockSpec((tq, LANES), row_q), pl.BlockSpec((tq, LANES), row_q),
            pl.BlockSpec((t, LANES), full), pl.BlockSpec((t, LANES), full), pl.BlockSpec((t, LANES), full),
            pl.BlockSpec((1, LANES), full), pl.BlockSpec((1, LANES), full), pl.BlockSpec((1, LANES), full),
        ],
        out_specs=pl.BlockSpec((None, tq, hps * LANES), lambda bi, h, qi: (bi, qi, h)),
        out_shape=jax.ShapeDtypeStruct((b, t, DIFF_HEADS * LANES), BF16),
        scratch_shapes=[pltpu.VMEM((hps, LANES, t), BF16)],
        compiler_params=_cparams(("arbitrary", "arbitrary", "arbitrary")),
        name="diff_attention",
    )(lam_vecs.astype(F32), p, p, p, c, sa, sb, c, sa, sb, tile2(q_gain), tile2(k_gain),
      subln.astype(F32).reshape(1, LANES))


def _gla_constants(c):
    levels = int(math.log2(c))
    t = np.arange(c)[:, None]
    u = np.arange(c)[None, :]
    stack = [[u <= t], [u >= t]]
    qside = [[], []]
    pair = [[t == u], [t == u]]
    for lv in range(1, levels + 1):
        base = (t >> lv) << lv
        half = 1 << (lv - 1)
        stack[0].append(u <= base + half - 1)
        stack[1].append(u >= base + half)
        up_t = ((t >> (lv - 1)) & 1) == 1
        up_u = ((u >> (lv - 1)) & 1) == 1
        same = (t >> lv) == (u >> lv)
        qside[0].append(np.broadcast_to(up_t, (c, LANES)))
        qside[1].append(np.broadcast_to(~up_t, (c, LANES)))
        pair[0].append(same & up_t & ~up_u)
        pair[1].append(same & ~up_t & up_u)
    f32 = lambda x: np.asarray(x, np.float32)
    stack = np.stack([np.tile(np.concatenate(f32(m), axis=0), (1, 3)) for m in stack])
    return (jnp.asarray(stack, BF16), jnp.asarray(np.stack([f32(m) for m in qside])),
            jnp.asarray(np.stack([f32(m) for m in pair])))


def _gla_chunk(q, k, v, g2, s, stack, qside_ref, pair_ref, d):
    c = q.shape[0]
    levels = int(math.log2(c))
    gcat = jnp.concatenate(_split3(g2), axis=0)
    cs = _dot(stack, gcat)
    tot_col = _dot_tn(gcat, jnp.ones((3 * c, LANES), BF16))
    diag = jnp.sum(q * k, axis=-1, keepdims=True)
    yield
    cum = cs[0:c]
    tot = cum[0:1] if d == 1 else cum[c - 1:c]
    zz = []
    for lv in range(1, levels + 1):
        e = jnp.exp2(-jnp.abs(cum - cs[lv * c:(lv + 1) * c]))
        z = (jnp.where(qside_ref[d, lv - 1] > 0.5, q, k) * e).astype(BF16)
        zz.append(_dot_nt(z, z))
    q_in = (q * jnp.exp2(cum)).astype(BF16)
    ks = (k * jnp.exp2(tot - cum)).astype(BF16)
    ds = _dot_tn(ks, v)
    yield
    a = pair_ref[d, 0] * diag
    for lv in range(1, levels + 1):
        a = a + pair_ref[d, lv] * zz[lv - 1]
    lhs = jnp.concatenate([q_in, a.astype(BF16)], axis=1)
    o = _dot(lhs, jnp.concatenate([s.astype(BF16), v], axis=0))
    s_new = s * jnp.exp2(tot_col) + ds
    yield
    return o, s_new


def _run_interleaved(gens):
    results = [None] * len(gens)
    live = list(range(len(gens)))
    while live:
        for i in list(live):
            try:
                next(gens[i])
            except StopIteration as stop:
                results[i] = stop.value
                live.remove(i)
    return results


def _hgrn_kernel(qz_ref, zf_ref, zb_ref, v_ref, gz_ref, lbf_ref, lbb_ref, og_ref, stack_ref, qside_ref, pair_ref,
                 o_ref, q_ref, kf_ref, gf_ref, kb_ref, gb_ref, of_ref, ob_ref, st_ref, *, n_ctx_chunks):
    c = GLA_CHUNK
    t = qz_ref.shape[0]
    n = t // c
    heads = range(HGRN_HEADS_PER_STEP)
    qz = qz_ref[...].astype(F32)
    q_ref[...] = qz * _sigmoid(qz) * (HGRN_K_DIM ** -0.5)
    for z_ref, lb_ref, k_ref, g_ref in ((zf_ref, lbf_ref, kf_ref, gf_ref), (zb_ref, lbb_ref, kb_ref, gb_ref)):
        lb = jnp.concatenate([lb_ref[h] for h in heads], axis=-1)
        f = lb + (1.0 - lb) * _sigmoid(z_ref[...].astype(F32))
        k_ref[...] = 1.0 - f
        g_ref[...] = jnp.log(f) * LOG2E
    st_ref[...] = jnp.zeros_like(st_ref)

    def body(i, carry):
        rf = pl.multiple_of(i * c, c)
        cb = jnp.where(i < n_ctx_chunks, n_ctx_chunks - 1 - i, n - 1 - i + n_ctx_chunks)
        rb = pl.multiple_of(cb * c, c)
        chains = [(h, d, pl.ds(r0, c), k_ref, g_ref, out_ref) for h in heads
                  for d, r0, k_ref, g_ref, out_ref in ((0, rf, kf_ref, gf_ref, of_ref), (1, rb, kb_ref, gb_ref, ob_ref))]
        col = lambda h: slice(h * LANES, (h + 1) * LANES)
        outs = _run_interleaved([
            _gla_chunk(q_ref[rows, col(h)], k_ref[rows, col(h)], v_ref[rows, col(h)], g_ref[rows, col(h)],
                       st_ref[h, d], stack_ref[d], qside_ref, pair_ref, d)
            for (h, d, rows, k_ref, g_ref, _) in chains])
        for (h, d, rows, _, _, out_ref), (o, s_new) in zip(chains, outs):
            out_ref[rows, col(h)] = o
            st_ref[h, d] = s_new
        return carry

    lax.fori_loop(0, n, body, 0)
    gz = gz_ref[...].astype(F32)
    for h in heads:
        cols = slice(h * LANES, (h + 1) * LANES)
        o = _rms(of_ref[:, cols] + ob_ref[:, cols]) * og_ref[...]
        o_ref[:, cols] = (o * (gz[:, cols] * _sigmoid(gz[:, cols]))).astype(o_ref.dtype)


def hgrn_bidirectional(p, lb_fwd, lb_bwd, out_gain, *, n_ctx, blk0):
    b, t, _ = p.shape
    c = GLA_CHUNK
    consts = _gla_constants(c)
    h = HGRN_HEADS
    hps = HGRN_HEADS_PER_STEP
    assert blk0 % hps == 0 and h % hps == 0
    seg = lambda s: pl.BlockSpec((None, t, hps * LANES), lambda bi, hi: (bi, 0, (blk0 + s * h) // hps + hi))
    per_head = pl.BlockSpec((hps, 1, LANES), lambda bi, hi: (hi, 0, 0))
    const = lambda a: pl.BlockSpec(a.shape, lambda bi, hi: (0,) * a.ndim)
    seq = pltpu.VMEM((t, hps * LANES), F32)
    return pl.pallas_call(
        functools.partial(_hgrn_kernel, n_ctx_chunks=n_ctx // c),
        grid=(b, h // hps),
        in_specs=[seg(0), seg(1), seg(2), seg(3), seg(4), per_head, per_head,
                  pl.BlockSpec((1, LANES), lambda bi, hi: (0, 0))] + [const(a) for a in consts],
        out_specs=pl.BlockSpec((None, t, hps * LANES), lambda bi, hi: (bi, 0, hi)),
        out_shape=jax.ShapeDtypeStruct((b, t, h * LANES), BF16),
        scratch_shapes=[seq, seq, seq, seq, seq, seq, seq, pltpu.VMEM((hps, 2, LANES, LANES), F32)],
        compiler_params=_cparams(("arbitrary", "arbitrary")),
        name="hgrn_bidirectional",
    )(p, p, p, p, p, lb_fwd.reshape(h, 1, LANES), lb_bwd.reshape(h, 1, LANES),
      out_gain.astype(F32).reshape(1, LANES), *consts)


def _mla_attn_kernel(q_ref, kv_ref, kr_ref, cq_ref, saq_ref, sbq_ref, ck_ref, sak_ref, sbk_ref,
                     qn_ref, qr_ref, kn_ref, krg_ref, o_ref, kt_ref):
    qi = pl.program_id(2)
    ones_nope = _block_ones(MLA_NOPE)
    ones_rope = _block_ones(MLA_ROPE)
    hw = 2 * LANES

    @pl.when(qi == 0)
    def _():
        kr = _block_rms(kr_ref[...].astype(F32), ones_rope, MLA_ROPE) * krg_ref[...]
        kr = _rope(kr, ck_ref[...], sak_ref[...], sbk_ref[...]).T.astype(BF16)
        for h in range(MLA_HEADS_PER_STEP):
            kn = _block_rms(kv_ref[:, h * hw:h * hw + MLA_NOPE].astype(F32), ones_nope, MLA_NOPE) * kn_ref[...]
            kt_ref[h, 0:LANES, :] = kn.T.astype(BF16)
            kt_ref[h, LANES:hw, :] = kr

    scale = (MLA_NOPE + MLA_ROPE) ** -0.5 * LOG2E
    qs = []
    for h in range(MLA_HEADS_PER_STEP):
        qn = _block_rms(q_ref[:, h * hw:h * hw + MLA_NOPE].astype(F32), ones_nope, MLA_NOPE) * (qn_ref[...] * scale)
        qr = _block_rms(q_ref[:, h * hw + MLA_NOPE:(h + 1) * hw].astype(F32), ones_rope, MLA_ROPE) * qr_ref[...]
        qr = _rope(qr, cq_ref[...], saq_ref[...], sbq_ref[...]) * scale
        qs.append(jnp.concatenate([qn.astype(BF16), qr.astype(BF16)], axis=-1))
    heads = range(MLA_HEADS_PER_STEP)
    outs = _attend_streams(qs, [kt_ref.at[h] for h in heads], [(kv_ref, h * hw + MLA_NOPE) for h in heads],
                           kt_ref.shape[2])
    for h, (o, l) in enumerate(outs):
        o_ref[:, h * MLA_V:(h + 1) * MLA_V] = (o * (1.0 / l)).astype(o_ref.dtype)


def mla_attention(q, kv, p1, kr_blk, rope, qn_gain, qr_gain, kn_gain, kr_gain, *, tq, n_ctx):
    b, n_lat, _ = q.shape
    t = kv.shape[1]
    c, sa, sb = rope
    hps = MLA_HEADS_PER_STEP
    q_tile0 = n_ctx // tq
    row_q = lambda bi, h, qi: (qi + q_tile0, 0)
    full = lambda bi, h, qi: (0, 0)
    pad = lambda g: jnp.concatenate([g.astype(F32), jnp.zeros((LANES - g.shape[0],), F32)]).reshape(1, LANES)
    return pl.pallas_call(
        _mla_attn_kernel,
        grid=(b, MLA_HEADS // hps, n_lat // tq),
        in_specs=[
            pl.BlockSpec((None, tq, hps * 2 * LANES), lambda bi, h, qi: (bi, qi, h)),
            pl.BlockSpec((None, t, hps * 2 * LANES), lambda bi, h, qi: (bi, 0, h)),
            pl.BlockSpec((None, t, LANES), lambda bi, h, qi: (bi, 0, kr_blk)),
            pl.BlockSpec((tq, LANES), row_q), pl.BlockSpec((tq, LANES), row_q), pl.BlockSpec((tq, LANES), row_q),
            pl.BlockSpec((t, LANES), full), pl.BlockSpec((t, LANES), full), pl.BlockSpec((t, LANES), full),
            pl.BlockSpec((1, LANES), full), pl.BlockSpec((1, LANES), full),
            pl.BlockSpec((1, LANES), full), pl.BlockSpec((1, LANES), full),
        ],
        out_specs=pl.BlockSpec((None, tq, hps * MLA_V), lambda bi, h, qi: (bi, qi, h)),
        out_shape=jax.ShapeDtypeStruct((b, n_lat, MLA_HEADS * MLA_V), BF16),
        scratch_shapes=[pltpu.VMEM((hps, 2 * LANES, t), BF16)],
        compiler_params=_cparams(("arbitrary", "arbitrary", "arbitrary")),
        name="mla_attention",
    )(q, kv, p1, c, sa, sb, c, sa, sb, pad(qn_gain), pad(qr_gain), pad(kn_gain), pad(kr_gain))


def _proj_residual_kernel(*refs, n_in):
    a_refs = refs[:n_in]
    w_refs = refs[n_in:2 * n_in]
    x_ref, m_ref, o_ref = refs[2 * n_in:]
    acc = _dot(a_refs[0][...], w_refs[0][...])
    for a_ref, w_ref in zip(a_refs[1:], w_refs[1:]):
        acc = acc + _dot(a_ref[...], w_ref[...])
    o_ref[...] = x_ref[...] + m_ref[G_M:G_M + 1, :] * acc


def proj_residual(acts, weights, xs, mods, *, tm, n_ctx_tiles, row_tile0=0):
    b, r, _ = acts[0].shape
    d = xs.shape[-1]
    in_specs = [pl.BlockSpec((None, tm, a.shape[-1]), lambda bi, ti: (bi, ti, 0)) for a in acts]
    in_specs += [pl.BlockSpec(w.shape, lambda bi, ti: (0, 0)) for w in weights]
    in_specs += [
        pl.BlockSpec((None, tm, d), lambda bi, ti: (bi, ti + row_tile0, 0)),
        pl.BlockSpec((None, None, 6, d), lambda bi, ti: (bi, ((ti + row_tile0) >= n_ctx_tiles).astype(jnp.int32), 0, 0)),
    ]
    return pl.pallas_call(
        functools.partial(_proj_residual_kernel, n_in=len(acts)),
        grid=(b, r // tm),
        in_specs=in_specs,
        out_specs=pl.BlockSpec((None, tm, d), lambda bi, ti: (bi, ti, 0)),
        out_shape=jax.ShapeDtypeStruct((b, r, d), F32),
        compiler_params=_cparams(("arbitrary", "arbitrary")),
        name="proj_residual",
    )(*acts, *weights, xs, mods)


PAIRS = [(i, j) for i in range(EXPERTS_PER_GROUP) for j in range(i + 1, EXPERTS_PER_GROUP)]
N_CLASSES = N_GROUPS * len(PAIRS)
CLASS_LO = np.array([EXPERTS_PER_GROUP * g + i for g in range(N_GROUPS) for (i, j) in PAIRS], np.int32)
CLASS_HI = np.array([EXPERTS_PER_GROUP * g + j for g in range(N_GROUPS) for (i, j) in PAIRS], np.int32)
EXPERT_TILE = 256
ROW_SLABS = 8
DMA_UNROLL = 8


def _route_kernel(x_ref, m_ref, g_ref, rw_ref, rb_ref, hf_ref, rows_ref, info_ref, count_ref, cnt_ref):
    h = _rms(x_ref[...]) * g_ref[...]
    h = h * (1.0 + m_ref[SC_F:SC_F + 1, :]) + m_ref[SH_F:SH_F + 1, :]
    hf_ref[...] = h.astype(hf_ref.dtype)
    for k in range(ROW_SLABS):
        rows_ref[:, k, :] = h[:, k * LANES:(k + 1) * LANES]
    h1, h2, h3 = _split3(h)
    w1, w2, w3 = _split3(rw_ref[...])
    logits = (_dot_nt(w1, h1) + (_dot_nt(w1, h2) + _dot_nt(w2, h1))
              + (_dot_nt(w1, h3) + _dot_nt(w2, h2) + _dot_nt(w3, h1)))
    biased = _sigmoid(logits) + rb_ref[...]
    row = [biased[e:e + 1, :] for e in range(N_EXPERTS)]
    gscore = []
    for g in range(N_GROUPS):
        m = row[4 * g:4 * g + 4]
        gscore.append(functools.reduce(jnp.maximum, [m[i] + m[j] for (i, j) in PAIRS]))
    hits = []
    for g in range(N_GROUPS):
        best = None
        for g2 in range(N_GROUPS):
            if g2 == g:
                continue
            wins = (gscore[g] > gscore[g2]) if g2 < g else (gscore[g] >= gscore[g2])
            best = wins if best is None else jnp.logical_and(best, wins)
        chosen = []
        for i in range(EXPERTS_PER_GROUP):
            rank = None
            for j in range(EXPERTS_PER_GROUP):
                if j == i:
                    continue
                mi, mj = row[4 * g + i], row[4 * g + j]
                ahead = ((mj >= mi) if j < i else (mj > mi)).astype(jnp.int32)
                rank = ahead if rank is None else rank + ahead
            chosen.append(rank < 2)
        for (i, j) in PAIRS:
            hits.append(jnp.where(best & chosen[i] & chosen[j], 1.0, 0.0))
    onehot = jnp.concatenate(hits, axis=0)
    tm = onehot.shape[1]
    upper = (lax.broadcasted_iota(jnp.int32, (tm, tm), 0) <= lax.broadcasted_iota(jnp.int32, (tm, tm), 1))
    prefix = _dot(onehot.astype(BF16), upper.astype(BF16))

    @pl.when((pl.program_id(0) == 0) & (pl.program_id(1) == 0))
    def _():
        cnt_ref[...] = jnp.zeros_like(cnt_ref)

    seen = cnt_ref[...]
    cls_id = lax.broadcasted_iota(jnp.int32, onehot.shape, 0).astype(F32)
    cls = jnp.sum(onehot * cls_id, axis=0, keepdims=True)
    rank = jnp.sum(onehot * (seen[:, 0:1] + prefix - 1.0), axis=0, keepdims=True)
    info_ref[...] = jnp.concatenate([cls, rank, jnp.zeros((6, tm), F32)], axis=0).astype(jnp.int32)
    seen = seen + jnp.sum(onehot, axis=1, keepdims=True)
    cnt_ref[...] = seen
    count_ref[...] = seen


def route(xs, mods, gain, router_w, router_bias, *, tm, n_ctx_tiles):
    b, t, d = xs.shape
    nt = t // tm
    return pl.pallas_call(
        _route_kernel,
        grid=(b, nt),
        in_specs=[
            pl.BlockSpec((None, tm, d), lambda bi, ti: (bi, ti, 0)),
            pl.BlockSpec((None, None, 6, d), lambda bi, ti: (bi, (ti >= n_ctx_tiles).astype(jnp.int32), 0, 0)),
            pl.BlockSpec((1, d), lambda bi, ti: (0, 0)),
            pl.BlockSpec((N_EXPERTS, d), lambda bi, ti: (0, 0)),
            pl.BlockSpec((N_EXPERTS, 1), lambda bi, ti: (0, 0)),
        ],
        out_specs=[
            pl.BlockSpec((None, tm, d), lambda bi, ti: (bi, ti, 0)),
            pl.BlockSpec((tm, ROW_SLABS, LANES), lambda bi, ti: (bi * nt + ti, 0, 0)),
            pl.BlockSpec((None, None, 8, tm), lambda bi, ti: (bi, ti, 0, 0)),
            pl.BlockSpec((N_CLASSES, LANES), lambda bi, ti: (0, 0)),
        ],
        out_shape=[jax.ShapeDtypeStruct((b, t, d), BF16),
                   jax.ShapeDtypeStruct((b * t, ROW_SLABS, LANES), F32),
                   jax.ShapeDtypeStruct((b, nt, 8, tm), jnp.int32),
                   jax.ShapeDtypeStruct((N_CLASSES, LANES), F32)],
        scratch_shapes=[pltpu.VMEM((N_CLASSES, LANES), F32)],
        compiler_params=_cparams(("arbitrary", "arbitrary")),
        name="route",
    )(xs, mods, gain.reshape(1, d).astype(F32), router_w.T.astype(F32),
      router_bias.reshape(N_EXPERTS, 1).astype(F32))


def _for_rows(n, fn):
    def body(k, carry):
        for u in range(DMA_UNROLL):
            fn(k * DMA_UNROLL + u)
        return carry
    lax.fori_loop(0, n // DMA_UNROLL, body, 0)


def _invert_kernel(dest_ref, src_ref):
    def clear(i, carry):
        for u in range(DMA_UNROLL):
            src_ref[i * DMA_UNROLL + u] = 0
        return carry

    def put(i, carry):
        for u in range(DMA_UNROLL):
            t = i * DMA_UNROLL + u
            src_ref[dest_ref[t]] = t
        return carry

    lax.fori_loop(0, src_ref.shape[0] // DMA_UNROLL, clear, 0)
    lax.fori_loop(0, dest_ref.shape[0] // DMA_UNROLL, put, 0)


def invert_permutation(dest, n_rows_out):
    smem = pl.BlockSpec(memory_space=pltpu.SMEM)
    return pl.pallas_call(
        _invert_kernel,
        in_specs=[smem],
        out_specs=smem,
        out_shape=jax.ShapeDtypeStruct((n_rows_out,), jnp.int32),
        name="moe_invert",
    )(dest)


def _expert_kernel(te_ref, tv_ref, src_ref, rows_hbm, rwl_ref, rwh_ref, wgl_ref, wul_ref, wdl_ref, wgh_ref, wuh_ref,
                   wdh_ref, y_ref, xbuf_ref, sem_ref, wgl_c, wul_c, wdl_c, wgh_c, wuh_c, wdh_c):
    i = pl.program_id(0)
    n = pl.num_programs(0)
    tg = y_ref.shape[0]
    slot = i % 2
    valid = tv_ref[i] != 0
    prev = jnp.maximum(i - 1, 0)

    def fetch(tile, s):
        base = tile * tg
        for t in range(tg):
            pltpu.make_async_copy(rows_hbm.at[src_ref[base + t]], xbuf_ref.at[s, t], sem_ref.at[s]).start()

    def wait_rows(s):
        pltpu.make_async_copy(rows_hbm.at[pl.ds(0, tg)], xbuf_ref.at[s], sem_ref.at[s]).wait()

    @pl.when(i == 0)
    def _():
        fetch(0, 0)

    for e_off, f32_refs, bf16_refs in ((0, (wgl_ref, wul_ref, wdl_ref), (wgl_c, wul_c, wdl_c)),
                                       (n, (wgh_ref, wuh_ref, wdh_ref), (wgh_c, wuh_c, wdh_c))):
        @pl.when(valid & ((i == 0) | (te_ref[e_off + i] != te_ref[e_off + prev])))
        def _():
            for src, dst in zip(f32_refs, bf16_refs):
                dst[...] = src[...].astype(BF16)

    @pl.when(valid)
    def _():
        wait_rows(slot)
        fetch(jnp.minimum(i + 1, n - 1), 1 - slot)
        x = jnp.concatenate([xbuf_ref[slot, :, k, :] for k in range(ROW_SLABS)], axis=-1)
        xb = x.astype(BF16)
        s_lo = _sigmoid(jnp.sum(x * rwl_ref[...], axis=-1, keepdims=True))
        s_hi = _sigmoid(jnp.sum(x * rwh_ref[...], axis=-1, keepdims=True))
        inv = 1.0 / (s_lo + s_hi)

        def ffn(wg, wu, wd):
            a = _dot(xb, wg[...])
            u = _dot(xb, wu[...])
            return _dot((a * _sigmoid(a) * u).astype(BF16), wd[...])

        y = (s_lo * inv) * ffn(wgl_c, wul_c, wdl_c) + (s_hi * inv) * ffn(wgh_c, wuh_c, wdh_c)
        for k in range(ROW_SLABS):
            y_ref[:, k, :] = y[:, k * LANES:(k + 1) * LANES]

    @pl.when(jnp.logical_not(valid))
    def _():
        y_ref[...] = jnp.zeros_like(y_ref)

    @pl.when((valid & (i == n - 1)) | (jnp.logical_not(valid) & (tv_ref[prev] != 0)))
    def _():
        wait_rows(jnp.where(valid, 1 - slot, slot))


def experts(tile_expert, tile_valid, src, rows, router_w_t, wg, wu, wd):
    tg = EXPERT_TILE
    n_tiles = src.shape[0] // tg
    _, d, ff = wg.shape
    lo = lambda i, te, tv, src: (te[i], 0, 0)
    hi = lambda i, te, tv, src: (te[n_tiles + i], 0, 0)
    wspecs = [pl.BlockSpec((None,) + w.shape[1:], sel) for sel in (lo, hi) for w in (wg, wu, wd)]
    wcache = [pltpu.VMEM(w.shape[1:], BF16) for _ in range(2) for w in (wg, wu, wd)]
    return pl.pallas_call(
        _expert_kernel,
        grid_spec=pltpu.PrefetchScalarGridSpec(
            num_scalar_prefetch=3,
            grid=(n_tiles,),
            in_specs=[pl.BlockSpec(memory_space=pl.ANY),
                      pl.BlockSpec((None, 1, d), lo), pl.BlockSpec((None, 1, d), hi)] + wspecs,
            out_specs=pl.BlockSpec((tg, ROW_SLABS, LANES), lambda i, te, tv, src: (i, 0, 0)),
            scratch_shapes=[pltpu.VMEM((2, tg, ROW_SLABS, LANES), F32), pltpu.SemaphoreType.DMA((2,))] + wcache,
        ),
        out_shape=jax.ShapeDtypeStruct((n_tiles * tg, ROW_SLABS, LANES), F32),
        compiler_params=_cparams(("arbitrary",)),
        name="moe_experts",
    )(tile_expert, tile_valid, src, rows, router_w_t, router_w_t, wg, wu, wd, wg, wu, wd)


def _combine_kernel(dest_ref, hf_ref, x_ref, m_ref, wg_ref, wu_ref, wd_ref, y_hbm, o_ref, ybuf_ref, sem_ref):
    i = pl.program_id(0)
    n = pl.num_programs(0)
    tm = hf_ref.shape[0]
    slot = i % 2

    def fetch(tile, s):
        base = tile * tm
        _for_rows(tm, lambda t: pltpu.make_async_copy(
            y_hbm.at[dest_ref[base + t]], ybuf_ref.at[s, t], sem_ref.at[s]).start())

    @pl.when(i == 0)
    def _():
        fetch(0, 0)

    @pl.when(i + 1 < n)
    def _():
        fetch(i + 1, 1 - slot)

    h = hf_ref[...]
    a = _dot(h, wg_ref[...])
    u = _dot(h, wu_ref[...])
    shared = _dot((a * _sigmoid(a) * u).astype(BF16), wd_ref[...])
    pltpu.make_async_copy(y_hbm.at[pl.ds(0, tm)], ybuf_ref.at[slot], sem_ref.at[slot]).wait()
    for k in range(ROW_SLABS):
        cols = slice(k * LANES, (k + 1) * LANES)
        o_ref[:, cols] = x_ref[:, cols] + m_ref[G_F:G_F + 1, cols] * (shared[:, cols] + ybuf_ref[slot, :, k, :])


def combine(dest, hf, xs, mods, wg, wu, wd, y_sorted, *, tm, n_ctx_tiles):
    b, r, d = hf.shape
    nt = r // tm
    ff = wg.shape[1]
    row = lambda i, dest: (i // nt, i % nt, 0)
    const = lambda i, dest: (0, 0)
    return pl.pallas_call(
        _combine_kernel,
        grid_spec=pltpu.PrefetchScalarGridSpec(
            num_scalar_prefetch=1,
            grid=(b * nt,),
            in_specs=[
                pl.BlockSpec((None, tm, d), row),
                pl.BlockSpec((None, tm, d), row),
                pl.BlockSpec((None, None, 6, d),
                             lambda i, dest: (i // nt, ((i % nt) >= n_ctx_tiles).astype(jnp.int32), 0, 0)),
                pl.BlockSpec((d, ff), const), pl.BlockSpec((d, ff), const), pl.BlockSpec((ff, d), const),
                pl.BlockSpec(memory_space=pl.ANY),
            ],
            out_specs=pl.BlockSpec((None, tm, d), row),
            scratch_shapes=[pltpu.VMEM((2, tm, ROW_SLABS, LANES), F32), pltpu.SemaphoreType.DMA((2,))],
        ),
        out_shape=jax.ShapeDtypeStruct((b, r, d), F32),
        compiler_params=_cparams(("arbitrary",)),
        name="moe_combine",
    )(dest, hf, xs, mods, wg, wu, wd, y_sorted)


def moe_block(xs, mods, gain, router_w, router_bias, ew_gate, ew_up, ew_down, sw_gate, sw_up, sw_down,
              *, tm, n_ctx_tiles):
    b, r, d = xs.shape
    n = b * r
    tg = EXPERT_TILE
    hf, rows, info, counts = route(xs, mods, gain, router_w, router_bias, tm=tm, n_ctx_tiles=n_ctx_tiles)
    cls = info[:, :, 0, :].reshape(n)
    rank = info[:, :, 1, :].reshape(n)
    padded = ((counts[:, 0].astype(jnp.int32) + tg - 1) // tg) * tg
    ends = jnp.cumsum(padded)
    dest = (ends - padded)[cls] + rank
    n_tiles = n // tg + N_CLASSES
    tile_start = jnp.arange(n_tiles, dtype=jnp.int32) * tg
    tile_valid = tile_start < ends[-1]
    last_start = jnp.maximum(ends[-1] - tg, 0)
    start = jnp.where(tile_valid, tile_start, last_start)
    tile_cls = jnp.sum((ends[None, :] <= start[:, None]).astype(jnp.int32), axis=1)
    tile_cls = jnp.minimum(tile_cls, N_CLASSES - 1)
    tile_expert = jnp.concatenate([jnp.asarray(CLASS_LO)[tile_cls], jnp.asarray(CLASS_HI)[tile_cls]])
    src = invert_permutation(dest, n_tiles * tg)
    y_sorted = experts(tile_expert, tile_valid.astype(jnp.int32), src, rows,
                       router_w.T.astype(F32).reshape(N_EXPERTS, 1, d), ew_gate, ew_up, ew_down)
    return combine(dest, hf, xs, mods, sw_gate.astype(BF16), sw_up.astype(BF16), sw_down.astype(BF16), y_sorted,
                   tm=tm, n_ctx_tiles=n_ctx_tiles)


def kernel(x, c, ctx, c_ctx, mod_w, mod_b, norm_mix, norm_ffn, even_w_in, even_w_out, diff_q_gain, diff_k_gain, diff_lambda, diff_subln, hgrn_lb_logits, hgrn_out_gain, odd_w_in, mla_q_a_gain, mla_kv_a_gain, mla_w_uq, mla_w_ukv, mla_q_nope_gain, mla_q_rope_gain, mla_k_nope_gain, mla_k_rope_gain, odd_w_out, router_w, router_bias, expert_w_gate, expert_w_up, expert_w_down, shared_w_gate, shared_w_up, shared_w_down):
    b, n_lat, d = x.shape
    n_ctx = ctx.shape[1]
    depth = mod_w.shape[0]
    tm = 256 if n_ctx % 256 == 0 else 128
    n_ctx_tiles = n_ctx // tm
    rope = rope_lane_tables(n_ctx, n_lat)

    mod_rows = 16
    c_rows = jnp.concatenate([c, c_ctx[None, :], jnp.zeros((mod_rows - b - 1, d), F32)], axis=0)
    mod_all = modulation(c_rows, mod_w, mod_b).reshape(depth, mod_rows, 6, d)
    lb_all = jnp.cumsum(jax.nn.softmax(hgrn_lb_logits.astype(F32), axis=0), axis=0)

    xs = jnp.concatenate([ctx, x], axis=1)
    for layer in range(depth):
        last = layer == depth - 1
        j = layer // 2
        mods = jnp.stack([jnp.broadcast_to(mod_all[layer, b], (b, 6, d)), mod_all[layer, :b]], axis=1)
        row_tile0 = n_ctx_tiles if last else 0
        if layer % 2 == 0:
            lam_init = 0.8 - 0.6 * math.exp(-0.3 * layer)
            p = norm_matmul(xs, norm_mix[layer], even_w_in[j].astype(BF16), tm=tm, mods=mods, n_ctx_tiles=n_ctx_tiles)
            oa = diff_attention(p, diff_lambda[j], diff_q_gain[j], diff_k_gain[j], diff_subln[j], rope,
                                tq=tm, n_ctx=n_ctx, lam_init=lam_init, q_blk0=0, k_blk0=4, v_blk0=8)
            ob = hgrn_bidirectional(p, lb_all[j, 0], lb_all[j, 1], hgrn_out_gain[j], n_ctx=n_ctx, blk0=12)
            if last:
                oa, ob = oa[:, n_ctx:], ob[:, n_ctx:]
            w_out = even_w_out[j].astype(BF16)
            half = oa.shape[-1]
            xs_mid = proj_residual([oa, ob], [w_out[:half], w_out[half:]], xs, mods, tm=tm,
                                   n_ctx_tiles=n_ctx_tiles, row_tile0=row_tile0)
        else:
            w_in = odd_w_in[j]
            zpad = jnp.zeros((d, LANES - MLA_ROPE), F32)
            w_in_r = jnp.concatenate([w_in[:, MLA_Q_LORA:MLA_Q_LORA + MLA_KV_LORA], w_in[:, MLA_Q_LORA + MLA_KV_LORA:],
                                      zpad, w_in[:, :MLA_Q_LORA]], axis=1).astype(BF16)
            p1 = norm_matmul(xs, norm_mix[layer], w_in_r, tm=tm, mods=mods, n_ctx_tiles=n_ctx_tiles)
            w_uq = mla_w_uq[j].reshape(MLA_Q_LORA, MLA_HEADS, MLA_NOPE + MLA_ROPE)
            w_uq = jnp.concatenate([w_uq, jnp.zeros((MLA_Q_LORA, MLA_HEADS, LANES - MLA_ROPE), F32)], axis=-1)
            w_uq = w_uq.reshape(MLA_Q_LORA, MLA_HEADS * 2 * LANES).astype(BF16)
            q = norm_matmul(p1, mla_q_a_gain[j], w_uq, tm=tm, col_block=1, row_tile0=row_tile0)
            kv = norm_matmul(p1, mla_kv_a_gain[j], mla_w_ukv[j].astype(BF16), tm=tm, col_block=0)
            if not last:
                raise NotImplementedError("context queries for a non-final latent-attention layer")
            o = mla_attention(q, kv, p1, 2, rope, mla_q_nope_gain[j], mla_q_rope_gain[j], mla_k_nope_gain[j],
                              mla_k_rope_gain[j], tq=tm, n_ctx=n_ctx)
            xs_mid = proj_residual([o], [odd_w_out[j].astype(BF16)], xs, mods, tm=tm,
                                   n_ctx_tiles=n_ctx_tiles, row_tile0=row_tile0)
        xs = moe_block(xs_mid, mods, norm_ffn[layer], router_w, router_bias,
                       expert_w_gate[layer], expert_w_up[layer], expert_w_down[layer],
                       shared_w_gate[layer], shared_w_up[layer], shared_w_down[layer],
                       tm=tm, n_ctx_tiles=0 if last else n_ctx_tiles)
    return xs if xs.shape[1] == n_lat else xs[:, n_ctx:]
```

```python
import functools
import math

import numpy as np
import jax
import jax.numpy as jnp
from jax import lax
from jax.experimental import pallas as pl
from jax.experimental.pallas import tpu as pltpu

F32 = jnp.float32
BF16 = jnp.bfloat16

LANES = 128
VMEM_LIMIT = 56 * 1024 * 1024

GRID_W = 64
DIFF_HEADS = 4
DIFF_HEAD_DIM = 64
HGRN_HEADS = 4
HGRN_K_DIM = 128
MLA_HEADS = 8
MLA_NOPE = 128
MLA_ROPE = 64
MLA_V = 128
MLA_Q_LORA = 384
MLA_KV_LORA = 256
N_EXPERTS = 16
N_GROUPS = 4
EXPERTS_PER_GROUP = 4
ROPE_BASE = 10000.0
EPS = 1e-6
LOG2E = 1.4426950408889634
GLA_CHUNK = 64
ATTN_KEY_CHUNK = 1152
MLA_HEADS_PER_STEP = 2
DIFF_HEADS_PER_STEP = 2
HGRN_HEADS_PER_STEP = 2

SH_M, SC_M, G_M, SH_F, SC_F, G_F = range(6)


def _sigmoid(x):
    return 1.0 / (1.0 + jnp.exp(-x))


def _dot(a, b):
    return jnp.dot(a, b, preferred_element_type=F32)


def _dot_nt(a, b):
    return lax.dot_general(a, b, (((1,), (1,)), ((), ())), preferred_element_type=F32)


def _dot_tn(a, b):
    return lax.dot_general(a, b, (((0,), (0,)), ((), ())), preferred_element_type=F32)


def _split3(x):
    hi = x.astype(BF16)
    r = x - hi.astype(F32)
    mid = r.astype(BF16)
    lo = (r - mid.astype(F32)).astype(BF16)
    return hi, mid, lo


def _rms(x, width=None):
    n = x.shape[-1] if width is None else width
    return x * lax.rsqrt(jnp.sum(x * x, axis=-1, keepdims=True) * (1.0 / n) + EPS)


def _cparams(sem):
    return pltpu.CompilerParams(dimension_semantics=sem, vmem_limit_bytes=VMEM_LIMIT)


def _mod_kernel(c_ref, w_ref, b_ref, o_ref):
    c = c_ref[...]
    s = c * _sigmoid(c)
    o_ref[...] = _dot(s.astype(BF16), w_ref[...].astype(BF16)) + b_ref[...]


def modulation(c_rows, mod_w, mod_b):
    n_layers, d, n = mod_w.shape
    rows = c_rows.shape[0]
    tn = 1536
    return pl.pallas_call(
        _mod_kernel,
        grid=(n_layers, n // tn),
        in_specs=[
            pl.BlockSpec((rows, d), lambda l, j: (0, 0)),
            pl.BlockSpec((None, d, tn), lambda l, j: (l, 0, j)),
            pl.BlockSpec((None, 1, tn), lambda l, j: (l, 0, j)),
        ],
        out_specs=pl.BlockSpec((None, rows, tn), lambda l, j: (l, 0, j)),
        out_shape=jax.ShapeDtypeStruct((n_layers, rows, n), F32),
        compiler_params=_cparams(("arbitrary", "arbitrary")),
        name="modulation",
    )(c_rows, mod_w, mod_b.reshape(n_layers, 1, n))


def _norm_matmul_kernel(*refs, has_mod):
    if has_mod:
        x_ref, m_ref, g_ref, w_ref, o_ref = refs
    else:
        x_ref, g_ref, w_ref, o_ref = refs
    h = _rms(x_ref[...].astype(F32)) * g_ref[...]
    if has_mod:
        h = h * (1.0 + m_ref[SC_M:SC_M + 1, :]) + m_ref[SH_M:SH_M + 1, :]
    o_ref[...] = _dot(h.astype(BF16), w_ref[...]).astype(o_ref.dtype)


def norm_matmul(x, gain, w, *, tm, col_block=0, row_tile0=0, n_row_tiles=None, mods=None, n_ctx_tiles=0):
    b, t, _ = x.shape
    k, n = w.shape
    if n_row_tiles is None:
        n_row_tiles = t // tm - row_tile0
    in_specs = [pl.BlockSpec((None, tm, k), lambda bi, ti: (bi, ti + row_tile0, col_block))]
    args = [x]
    if mods is not None:
        in_specs.append(pl.BlockSpec((None, None, 6, k),
                                     lambda bi, ti: (bi, ((ti + row_tile0) >= n_ctx_tiles).astype(jnp.int32), 0, 0)))
        args.append(mods)
    in_specs += [pl.BlockSpec((1, k), lambda bi, ti: (0, 0)), pl.BlockSpec((k, n), lambda bi, ti: (0, 0))]
    args += [gain.reshape(1, k).astype(F32), w]
    return pl.pallas_call(
        functools.partial(_norm_matmul_kernel, has_mod=mods is not None),
        grid=(b, n_row_tiles),
        in_specs=in_specs,
        out_specs=pl.BlockSpec((None, tm, n), lambda bi, ti: (bi, ti, 0)),
        out_shape=jax.ShapeDtypeStruct((b, n_row_tiles * tm, n), BF16),
        compiler_params=_cparams(("arbitrary", "arbitrary")),
        name="norm_matmul",
    )(*args)


def rope_lane_tables(n_ctx, n_lat):
    rows = n_lat // GRID_W
    row = np.repeat(np.arange(rows), GRID_W).astype(np.float32)
    col = np.tile(np.arange(GRID_W), rows).astype(np.float32)
    axis_dim = DIFF_HEAD_DIM // 2
    inv_freq = jnp.asarray(ROPE_BASE, F32) ** (-jnp.arange(0, axis_dim, 2, dtype=F32) / axis_dim)
    ang_r = jnp.asarray(row)[:, None] * inv_freq
    ang_c = jnp.asarray(col)[:, None] * inv_freq
    lane = np.arange(LANES)
    freq_idx = lane % 16
    use_col = (lane % 64) >= 32
    first = (lane % 32) < 16
    ang = jnp.where(use_col[None, :], ang_c[:, freq_idx], ang_r[:, freq_idx])
    cos, sin = jnp.cos(ang), jnp.sin(ang)
    c = jnp.concatenate([jnp.ones((n_ctx, LANES), F32), cos], axis=0)
    sa = jnp.concatenate([jnp.zeros((n_ctx, LANES), F32), jnp.where(first[None, :], -sin, 0.0)], axis=0)
    sb = jnp.concatenate([jnp.zeros((n_ctx, LANES), F32), jnp.where(first[None, :], 0.0, sin)], axis=0)
    return c, sa, sb


def _rope(x, c, sa, sb):
    return x * c + pltpu.roll(x, LANES - 16, 1) * sa + pltpu.roll(x, 16, 1) * sb


def _block_ones(width):
    r = lax.broadcasted_iota(jnp.int32, (LANES, LANES), 0) // width
    c = lax.broadcasted_iota(jnp.int32, (LANES, LANES), 1) // width
    return (r == c).astype(BF16)


def _block_rms(x, ones, width):
    xx = x * x
    hi = xx.astype(BF16)
    lo = (xx - hi.astype(F32)).astype(BF16)
    ms = (_dot(hi, ones) + _dot(lo, ones)) * (1.0 / width)
    return x * lax.rsqrt(ms + EPS)


def _attend_streams(qs, kt_refs, vs, nk):
    n = len(qs)
    chunk = ATTN_KEY_CHUNK if nk % ATTN_KEY_CHUNK == 0 else nk
    nc = nk // chunk
    s = [[None] * nc for _ in range(n)]
    p = [[None] * nc for _ in range(n)]
    m, l, o = [None] * n, [None] * n, [None] * n

    def logits(i, c):
        s[i][c] = _dot(qs[i], kt_refs[i][:, c * chunk:(c + 1) * chunk])
        mc = jnp.max(s[i][c], axis=-1, keepdims=True)
        m[i] = mc if c == 0 else jnp.maximum(m[i], mc)

    def exps(i, c):
        e = jnp.exp2(s[i][c] - m[i])
        lc = jnp.sum(e, axis=-1, keepdims=True)
        l[i] = lc if c == 0 else l[i] + lc
        p[i][c] = e.astype(BF16)

    def values(i, c):
        ref, col0 = vs[i]
        oc = _dot(p[i][c], ref[c * chunk:(c + 1) * chunk, col0:col0 + LANES])
        o[i] = oc if c == 0 else o[i] + oc

    for step in range(n + 2):
        for c in range(nc):
            if step < n:
                logits(step, c)
            if 0 <= step - 1 < n:
                exps(step - 1, c)
            if 0 <= step - 2 < n:
                values(step - 2, c)
    return list(zip(o, l))


def _diff_attn_kernel(lam_ref, q_ref, k_ref, v_ref, cq_ref, saq_ref, sbq_ref, ck_ref, sak_ref, sbk_ref,
                      qg_ref, kg_ref, sub_ref, o_ref, kt_ref, *, n_ctx, n_ctx_tiles, lam_init):
    qi = pl.program_id(2)
    ones = _block_ones(DIFF_HEAD_DIM)
    heads = range(DIFF_HEADS_PER_STEP)

    @pl.when(qi == 0)
    def _():
        for h in heads:
            k = _block_rms(k_ref[:, h * LANES:(h + 1) * LANES].astype(F32), ones, DIFF_HEAD_DIM) * kg_ref[...]
            kt_ref[h] = _rope(k, ck_ref[...], sak_ref[...], sbk_ref[...]).T.astype(BF16)

    qs = []
    for h in heads:
        q = _block_rms(q_ref[:, h * LANES:(h + 1) * LANES].astype(F32), ones, DIFF_HEAD_DIM) * qg_ref[...]
        q = _rope(q, cq_ref[...], saq_ref[...], sbq_ref[...]) * (DIFF_HEAD_DIM ** -0.5 * LOG2E)
        lane = lax.broadcasted_iota(jnp.int32, q.shape, 1)
        qs.append(jnp.where(lane < DIFF_HEAD_DIM, q, 0.0).astype(BF16))
        qs.append(jnp.where(lane >= DIFF_HEAD_DIM, q, 0.0).astype(BF16))
    lv = lam_ref[...]
    lam = (jnp.exp(jnp.sum(lv[0:1] * lv[1:2], axis=-1, keepdims=True))
           - jnp.exp(jnp.sum(lv[2:3] * lv[3:4], axis=-1, keepdims=True)) + lam_init)

    def attend(nk):
        outs = _attend_streams(qs, [kt_ref.at[h] for h in heads for _ in range(2)],
                               [(v_ref, h * LANES) for h in heads for _ in range(2)], nk)
        for h in heads:
            (o1, l1), (o2, l2) = outs[2 * h], outs[2 * h + 1]
            o = o1 * (1.0 / l1) - o2 * (lam / l2)
            o_ref[:, h * LANES:(h + 1) * LANES] = (_rms(o) * sub_ref[...] * (1.0 - lam_init)).astype(o_ref.dtype)

    @pl.when(qi < n_ctx_tiles)
    def _():
        attend(n_ctx)

    @pl.when(qi >= n_ctx_tiles)
    def _():
        attend(k_ref.shape[0])


def diff_attention(p, lam_vecs, q_gain, k_gain, subln, rope, *, tq, n_ctx, lam_init, q_blk0, k_blk0, v_blk0):
    b, t, _ = p.shape
    c, sa, sb = rope
    nq = t // tq
    hps = DIFF_HEADS_PER_STEP
    assert q_blk0 % hps == 0 and k_blk0 % hps == 0 and v_blk0 % hps == 0
    row_q = lambda bi, h, qi: (qi, 0)
    full = lambda bi, h, qi: (0, 0)
    tile2 = lambda g: jnp.tile(g.astype(F32), 2).reshape(1, LANES)
    return pl.pallas_call(
        functools.partial(_diff_attn_kernel, n_ctx=n_ctx, n_ctx_tiles=n_ctx // tq, lam_init=lam_init),
        grid=(b, DIFF_HEADS // hps, nq),
        in_specs=[
            pl.BlockSpec((4, DIFF_HEAD_DIM), full),
            pl.BlockSpec((None, tq, hps * LANES), lambda bi, h, qi: (bi, qi, q_blk0 // hps + h)),
            pl.BlockSpec((None, t, hps * LANES), lambda bi, h, qi: (bi, 0, k_blk0 // hps + h)),
            pl.BlockSpec((None, t, hps * LANES), lambda bi, h, qi: (bi, 0, v_blk0 // hps + h)),
            pl.BlockSpec((tq, LANES), row_q), pl.BlockSpec((tq, LANES), row_q), pl.BlockSpec((tq, LANES), row_q),
            pl.BlockSpec((t, LANES), full), pl.BlockSpec((t, LANES), full), pl.BlockSpec((t, LANES), full),
            pl.BlockSpec((1, LANES), full), pl.BlockSpec((1, LANES), full), pl.BlockSpec((1, LANES), full),
        ],
        out_specs=pl.BlockSpec((None, tq, hps * LANES), lambda bi, h, qi: (bi, qi, h)),
        out_shape=jax.ShapeDtypeStruct((b, t, DIFF_HEADS * LANES), BF16),
        scratch_shapes=[pltpu.VMEM((hps, LANES, t), BF16)],
        compiler_params=_cparams(("arbitrary", "arbitrary", "arbitrary")),
        name="diff_attention",
    )(lam_vecs.astype(F32), p, p, p, c, sa, sb, c, sa, sb, tile2(q_gain), tile2(k_gain),
      subln.astype(F32).reshape(1, LANES))


def _gla_constants(c):
    levels = int(math.log2(c))
    t = np.arange(c)[:, None]
    u = np.arange(c)[None, :]
    stack = [[u <= t], [u >= t]]
    qside = [[], []]
    pair = [[t == u], [t == u]]
    for lv in range(1, levels + 1):
        base = (t >> lv) << lv
        half = 1 << (lv - 1)
        stack[0].append(u <= base + half - 1)
        stack[1].append(u >= base + half)
        up_t = ((t >> (lv - 1)) & 1) == 1
        up_u = ((u >> (lv - 1)) & 1) == 1
        same = (t >> lv) == (u >> lv)
        qside[0].append(np.broadcast_to(up_t, (c, LANES)))
        qside[1].append(np.broadcast_to(~up_t, (c, LANES)))
        pair[0].append(same & up_t & ~up_u)
        pair[1].append(same & ~up_t & up_u)
    f32 = lambda x: np.asarray(x, np.float32)
    stack = np.stack([np.tile(np.concatenate(f32(m), axis=0), (1, 3)) for m in stack])
    return (jnp.asarray(stack, BF16), jnp.asarray(np.stack([f32(m) for m in qside])),
            jnp.asarray(np.stack([f32(m) for m in pair])))


def _gla_chunk(q, k, v, g2, s, stack, qside_ref, pair_ref, d):
    c = q.shape[0]
    levels = int(math.log2(c))
    gcat = jnp.concatenate(_split3(g2), axis=0)
    cs = _dot(stack, gcat)
    tot_col = _dot_tn(gcat, jnp.ones((3 * c, LANES), BF16))
    diag = jnp.sum(q * k, axis=-1, keepdims=True)
    yield
    cum = cs[0:c]
    tot = cum[0:1] if d == 1 else cum[c - 1:c]
    zz = []
    for lv in range(1, levels + 1):
        e = jnp.exp2(-jnp.abs(cum - cs[lv * c:(lv + 1) * c]))
        z = (jnp.where(qside_ref[d, lv - 1] > 0.5, q, k) * e).astype(BF16)
        zz.append(_dot_nt(z, z))
    q_in = (q * jnp.exp2(cum)).astype(BF16)
    ks = (k * jnp.exp2(tot - cum)).astype(BF16)
    ds = _dot_tn(ks, v)
    yield
    a = pair_ref[d, 0] * diag
    for lv in range(1, levels + 1):
        a = a + pair_ref[d, lv] * zz[lv - 1]
    lhs = jnp.concatenate([q_in, a.astype(BF16)], axis=1)
    o = _dot(lhs, jnp.concatenate([s.astype(BF16), v], axis=0))
    s_new = s * jnp.exp2(tot_col) + ds
    yield
    return o, s_new


def _run_interleaved(gens):
    results = [None] * len(gens)
    live = list(range(len(gens)))
    while live:
        for i in list(live):
            try:
                next(gens[i])
            except StopIteration as stop:
                results[i] = stop.value
                live.remove(i)
    return results


def _hgrn_kernel(qz_ref, zf_ref, zb_ref, v_ref, gz_ref, lbf_ref, lbb_ref, og_ref, stack_ref, qside_ref, pair_ref,
                 o_ref, q_ref, kf_ref, gf_ref, kb_ref, gb_ref, of_ref, ob_ref, st_ref, *, n_ctx_chunks):
    c = GLA_CHUNK
    t = qz_ref.shape[0]
    n = t // c
    heads = range(HGRN_HEADS_PER_STEP)
    qz = qz_ref[...].astype(F32)
    q_ref[...] = qz * _sigmoid(qz) * (HGRN_K_DIM ** -0.5)
    for z_ref, lb_ref, k_ref, g_ref in ((zf_ref, lbf_ref, kf_ref, gf_ref), (zb_ref, lbb_ref, kb_ref, gb_ref)):
        lb = jnp.concatenate([lb_ref[h] for h in heads], axis=-1)
        f = lb + (1.0 - lb) * _sigmoid(z_ref[...].astype(F32))
        k_ref[...] = 1.0 - f
        g_ref[...] = jnp.log(f) * LOG2E
    st_ref[...] = jnp.zeros_like(st_ref)

    def body(i, carry):
        rf = pl.multiple_of(i * c, c)
        cb = jnp.where(i < n_ctx_chunks, n_ctx_chunks - 1 - i, n - 1 - i + n_ctx_chunks)
        rb = pl.multiple_of(cb * c, c)
        chains = [(h, d, pl.ds(r0, c), k_ref, g_ref, out_ref) for h in heads
                  for d, r0, k_ref, g_ref, out_ref in ((0, rf, kf_ref, gf_ref, of_ref), (1, rb, kb_ref, gb_ref, ob_ref))]
        col = lambda h: slice(h * LANES, (h + 1) * LANES)
        outs = _run_interleaved([
            _gla_chunk(q_ref[rows, col(h)], k_ref[rows, col(h)], v_ref[rows, col(h)], g_ref[rows, col(h)],
                       st_ref[h, d], stack_ref[d], qside_ref, pair_ref, d)
            for (h, d, rows, k_ref, g_ref, _) in chains])
        for (h, d, rows, _, _, out_ref), (o, s_new) in zip(chains, outs):
            out_ref[rows, col(h)] = o
            st_ref[h, d] = s_new
        return carry

    lax.fori_loop(0, n, body, 0)
    gz = gz_ref[...].astype(F32)
    for h in heads:
        cols = slice(h * LANES, (h + 1) * LANES)
        o = _rms(of_ref[:, cols] + ob_ref[:, cols]) * og_ref[...]
        o_ref[:, cols] = (o * (gz[:, cols] * _sigmoid(gz[:, cols]))).astype(o_ref.dtype)


def hgrn_bidirectional(p, lb_fwd, lb_bwd, out_gain, *, n_ctx, blk0):
    b, t, _ = p.shape
    c = GLA_CHUNK
    consts = _gla_constants(c)
    h = HGRN_HEADS
    hps = HGRN_HEADS_PER_STEP
    assert blk0 % hps == 0 and h % hps == 0
    seg = lambda s: pl.BlockSpec((None, t, hps * LANES), lambda bi, hi: (bi, 0, (blk0 + s * h) // hps + hi))
    per_head = pl.BlockSpec((hps, 1, LANES), lambda bi, hi: (hi, 0, 0))
    const = lambda a: pl.BlockSpec(a.shape, lambda bi, hi: (0,) * a.ndim)
    seq = pltpu.VMEM((t, hps * LANES), F32)
    return pl.pallas_call(
        functools.partial(_hgrn_kernel, n_ctx_chunks=n_ctx // c),
        grid=(b, h // hps),
        in_specs=[seg(0), seg(1), seg(2), seg(3), seg(4), per_head, per_head,
                  pl.BlockSpec((1, LANES), lambda bi, hi: (0, 0))] + [const(a) for a in consts],
        out_specs=pl.BlockSpec((None, t, hps * LANES), lambda bi, hi: (bi, 0, hi)),
        out_shape=jax.ShapeDtypeStruct((b, t, h * LANES), BF16),
        scratch_shapes=[seq, seq, seq, seq, seq, seq, seq, pltpu.VMEM((hps, 2, LANES, LANES), F32)],
        compiler_params=_cparams(("arbitrary", "arbitrary")),
        name="hgrn_bidirectional",
    )(p, p, p, p, p, lb_fwd.reshape(h, 1, LANES), lb_bwd.reshape(h, 1, LANES),
      out_gain.astype(F32).reshape(1, LANES), *consts)


def _mla_attn_kernel(q_ref, kv_ref, kr_ref, cq_ref, saq_ref, sbq_ref, ck_ref, sak_ref, sbk_ref,
                     qn_ref, qr_ref, kn_ref, krg_ref, o_ref, kt_ref):
    qi = pl.program_id(2)
    ones_nope = _block_ones(MLA_NOPE)
    ones_rope = _block_ones(MLA_ROPE)
    hw = 2 * LANES

    @pl.when(qi == 0)
    def _():
        kr = _block_rms(kr_ref[...].astype(F32), ones_rope, MLA_ROPE) * krg_ref[...]
        kr = _rope(kr, ck_ref[...], sak_ref[...], sbk_ref[...]).T.astype(BF16)
        for h in range(MLA_HEADS_PER_STEP):
            kn = _block_rms(kv_ref[:, h * hw:h * hw + MLA_NOPE].astype(F32), ones_nope, MLA_NOPE) * kn_ref[...]
            kt_ref[h, 0:LANES, :] = kn.T.astype(BF16)
            kt_ref[h, LANES:hw, :] = kr

    scale = (MLA_NOPE + MLA_ROPE) ** -0.5 * LOG2E
    qs = []
    for h in range(MLA_HEADS_PER_STEP):
        qn = _block_rms(q_ref[:, h * hw:h * hw + MLA_NOPE].astype(F32), ones_nope, MLA_NOPE) * (qn_ref[...] * scale)
        qr = _block_rms(q_ref[:, h * hw + MLA_NOPE:(h + 1) * hw].astype(F32), ones_rope, MLA_ROPE) * qr_ref[...]
        qr = _rope(qr, cq_ref[...], saq_ref[...], sbq_ref[...]) * scale
        qs.append(jnp.concatenate([qn.astype(BF16), qr.astype(BF16)], axis=-1))
    heads = range(MLA_HEADS_PER_STEP)
    outs = _attend_streams(qs, [kt_ref.at[h] for h in heads], [(kv_ref, h * hw + MLA_NOPE) for h in heads],
                           kt_ref.shape[2])
    for h, (o, l) in enumerate(outs):
        o_ref[:, h * MLA_V:(h + 1) * MLA_V] = (o * (1.0 / l)).astype(o_ref.dtype)


def mla_attention(q, kv, p1, kr_blk, rope, qn_gain, qr_gain, kn_gain, kr_gain, *, tq, n_ctx):
    b, n_lat, _ = q.shape
    t = kv.shape[1]
    c, sa, sb = rope
    hps = MLA_HEADS_PER_STEP
    q_tile0 = n_ctx // tq
    row_q = lambda bi, h, qi: (qi + q_tile0, 0)
    full = lambda bi, h, qi: (0, 0)
    pad = lambda g: jnp.concatenate([g.astype(F32), jnp.zeros((LANES - g.shape[0],), F32)]).reshape(1, LANES)
    return pl.pallas_call(
        _mla_attn_kernel,
        grid=(b, MLA_HEADS // hps, n_lat // tq),
        in_specs=[
            pl.BlockSpec((None, tq, hps * 2 * LANES), lambda bi, h, qi: (bi, qi, h)),
            pl.BlockSpec((None, t, hps * 2 * LANES), lambda bi, h, qi: (bi, 0, h)),
            pl.BlockSpec((None, t, LANES), lambda bi, h, qi: (bi, 0, kr_blk)),
            pl.BlockSpec((tq, LANES), row_q), pl.BlockSpec((tq, LANES), row_q), pl.BlockSpec((tq, LANES), row_q),
            pl.BlockSpec((t, LANES), full), pl.BlockSpec((t, LANES), full), pl.BlockSpec((t, LANES), full),
            pl.BlockSpec((1, LANES), full), pl.BlockSpec((1, LANES), full),
            pl.BlockSpec((1, LANES), full), pl.BlockSpec((1, LANES), full),
        ],
        out_specs=pl.BlockSpec((None, tq, hps * MLA_V), lambda bi, h, qi: (bi, qi, h)),
        out_shape=jax.ShapeDtypeStruct((b, n_lat, MLA_HEADS * MLA_V), BF16),
        scratch_shapes=[pltpu.VMEM((hps, 2 * LANES, t), BF16)],
        compiler_params=_cparams(("arbitrary", "arbitrary", "arbitrary")),
        name="mla_attention",
    )(q, kv, p1, c, sa, sb, c, sa, sb, pad(qn_gain), pad(qr_gain), pad(kn_gain), pad(kr_gain))


def _proj_residual_kernel(*refs, n_in):
    a_refs = refs[:n_in]
    w_refs = refs[n_in:2 * n_in]
    x_ref, m_ref, o_ref = refs[2 * n_in:]
    acc = _dot(a_refs[0][...], w_refs[0][...])
    for a_ref, w_ref in zip(a_refs[1:], w_refs[1:]):
        acc = acc + _dot(a_ref[...], w_ref[...])
    o_ref[...] = x_ref[...] + m_ref[G_M:G_M + 1, :] * acc


def proj_residual(acts, weights, xs, mods, *, tm, n_ctx_tiles, row_tile0=0):
    b, r, _ = acts[0].shape
    d = xs.shape[-1]
    in_specs = [pl.BlockSpec((None, tm, a.shape[-1]), lambda bi, ti: (bi, ti, 0)) for a in acts]
    in_specs += [pl.BlockSpec(w.shape, lambda bi, ti: (0, 0)) for w in weights]
    in_specs += [
        pl.BlockSpec((None, tm, d), lambda bi, ti: (bi, ti + row_tile0, 0)),
        pl.BlockSpec((None, None, 6, d), lambda bi, ti: (bi, ((ti + row_tile0) >= n_ctx_tiles).astype(jnp.int32), 0, 0)),
    ]
    return pl.pallas_call(
        functools.partial(_proj_residual_kernel, n_in=len(acts)),
        grid=(b, r // tm),
        in_specs=in_specs,
        out_specs=pl.BlockSpec((None, tm, d), lambda bi, ti: (bi, ti, 0)),
        out_shape=jax.ShapeDtypeStruct((b, r, d), F32),
        compiler_params=_cparams(("arbitrary", "arbitrary")),
        name="proj_residual",
    )(*acts, *weights, xs, mods)


PAIRS = [(i, j) for i in range(EXPERTS_PER_GROUP) for j in range(i + 1, EXPERTS_PER_GROUP)]
N_CLASSES = N_GROUPS * len(PAIRS)
CLASS_LO = np.array([EXPERTS_PER_GROUP * g + i for g in range(N_GROUPS) for (i, j) in PAIRS], np.int32)
CLASS_HI = np.array([EXPERTS_PER_GROUP * g + j for g in range(N_GROUPS) for (i, j) in PAIRS], np.int32)
EXPERT_TILE = 256
ROW_SLABS = 8
DMA_UNROLL = 8


def _route_kernel(x_ref, m_ref, g_ref, rw_ref, rb_ref, hf_ref, rows_ref, info_ref, count_ref, cnt_ref):
    h = _rms(x_ref[...]) * g_ref[...]
    h = h * (1.0 + m_ref[SC_F:SC_F + 1, :]) + m_ref[SH_F:SH_F + 1, :]
    hf_ref[...] = h.astype(hf_ref.dtype)
    for k in range(ROW_SLABS):
        rows_ref[:, k, :] = h[:, k * LANES:(k + 1) * LANES]
    h1, h2, h3 = _split3(h)
    w1, w2, w3 = _split3(rw_ref[...])
    logits = (_dot_nt(w1, h1) + (_dot_nt(w1, h2) + _dot_nt(w2, h1))
              + (_dot_nt(w1, h3) + _dot_nt(w2, h2) + _dot_nt(w3, h1)))
    biased = _sigmoid(logits) + rb_ref[...]
    row = [biased[e:e + 1, :] for e in range(N_EXPERTS)]
    gscore = []
    for g in range(N_GROUPS):
        m = row[4 * g:4 * g + 4]
        gscore.append(functools.reduce(jnp.maximum, [m[i] + m[j] for (i, j) in PAIRS]))
    hits = []
    for g in range(N_GROUPS):
        best = None
        for g2 in range(N_GROUPS):
            if g2 == g:
                continue
            wins = (gscore[g] > gscore[g2]) if g2 < g else (gscore[g] >= gscore[g2])
            best = wins if best is None else jnp.logical_and(best, wins)
        chosen = []
        for i in range(EXPERTS_PER_GROUP):
            rank = None
            for j in range(EXPERTS_PER_GROUP):
                if j == i:
                    continue
                mi, mj = row[4 * g + i], row[4 * g + j]
                ahead = ((mj >= mi) if j < i else (mj > mi)).astype(jnp.int32)
                rank = ahead if rank is None else rank + ahead
            chosen.append(rank < 2)
        for (i, j) in PAIRS:
            hits.append(jnp.where(best & chosen[i] & chosen[j], 1.0, 0.0))
    onehot = jnp.concatenate(hits, axis=0)
    tm = onehot.shape[1]
    upper = (lax.broadcasted_iota(jnp.int32, (tm, tm), 0) <= lax.broadcasted_iota(jnp.int32, (tm, tm), 1))
    prefix = _dot(onehot.astype(BF16), upper.astype(BF16))

    @pl.when((pl.program_id(0) == 0) & (pl.program_id(1) == 0))
    def _():
        cnt_ref[...] = jnp.zeros_like(cnt_ref)

    seen = cnt_ref[...]
    cls_id = lax.broadcasted_iota(jnp.int32, onehot.shape, 0).astype(F32)
    cls = jnp.sum(onehot * cls_id, axis=0, keepdims=True)
    rank = jnp.sum(onehot * (seen[:, 0:1] + prefix - 1.0), axis=0, keepdims=True)
    info_ref[...] = jnp.concatenate([cls, rank, jnp.zeros((6, tm), F32)], axis=0).astype(jnp.int32)
    seen = seen + jnp.sum(onehot, axis=1, keepdims=True)
    cnt_ref[...] = seen
    count_ref[...] = seen


def route(xs, mods, gain, router_w, router_bias, *, tm, n_ctx_tiles):
    b, t, d = xs.shape
    nt = t // tm
    return pl.pallas_call(
        _route_kernel,
        grid=(b, nt),
        in_specs=[
            pl.BlockSpec((None, tm, d), lambda bi, ti: (bi, ti, 0)),
            pl.BlockSpec((None, None, 6, d), lambda bi, ti: (bi, (ti >= n_ctx_tiles).astype(jnp.int32), 0, 0)),
            pl.BlockSpec((1, d), lambda bi, ti: (0, 0)),
            pl.BlockSpec((N_EXPERTS, d), lambda bi, ti: (0, 0)),
            pl.BlockSpec((N_EXPERTS, 1), lambda bi, ti: (0, 0)),
        ],
        out_specs=[
            pl.BlockSpec((None, tm, d), lambda bi, ti: (bi, ti, 0)),
            pl.BlockSpec((tm, ROW_SLABS, LANES), lambda bi, ti: (bi * nt + ti, 0, 0)),
            pl.BlockSpec((None, None, 8, tm), lambda bi, ti: (bi, ti, 0, 0)),
            pl.BlockSpec((N_CLASSES, LANES), lambda bi, ti: (0, 0)),
        ],
        out_shape=[jax.ShapeDtypeStruct((b, t, d), BF16),
                   jax.ShapeDtypeStruct((b * t, ROW_SLABS, LANES), F32),
                   jax.ShapeDtypeStruct((b, nt, 8, tm), jnp.int32),
                   jax.ShapeDtypeStruct((N_CLASSES, LANES), F32)],
        scratch_shapes=[pltpu.VMEM((N_CLASSES, LANES), F32)],
        compiler_params=_cparams(("arbitrary", "arbitrary")),
        name="route",
    )(xs, mods, gain.reshape(1, d).astype(F32), router_w.T.astype(F32),
      router_bias.reshape(N_EXPERTS, 1).astype(F32))


def _for_rows(n, fn):
    def body(k, carry):
        for u in range(DMA_UNROLL):
            fn(k * DMA_UNROLL + u, u)
        return carry
    lax.fori_loop(0, n // DMA_UNROLL, body, 0)


def _invert_kernel(dest_ref, src_ref):
    def clear(i, carry):
        for u in range(2 * DMA_UNROLL):
            src_ref[i * 2 * DMA_UNROLL + u] = 0
        return carry

    def put(i, carry):
        for u in range(2 * DMA_UNROLL):
            t = i * 2 * DMA_UNROLL + u
            src_ref[dest_ref[t]] = t
        return carry

    lax.fori_loop(0, src_ref.shape[0] // (2 * DMA_UNROLL), clear, 0)
    lax.fori_loop(0, dest_ref.shape[0] // (2 * DMA_UNROLL), put, 0)


def invert_permutation(dest, n_rows_out):
    smem = pl.BlockSpec(memory_space=pltpu.SMEM)
    return pl.pallas_call(
        _invert_kernel,
        in_specs=[smem],
        out_specs=smem,
        out_shape=jax.ShapeDtypeStruct((n_rows_out,), jnp.int32),
        name="moe_invert",
    )(dest)


def _expert_kernel(te_ref, tv_ref, src_ref, rows_hbm, rwl_ref, rwh_ref, wgl_ref, wul_ref, wdl_ref, wgh_ref, wuh_ref,
                   wdh_ref, y_ref, xbuf_ref, sem_ref, wgl_c, wul_c, wdl_c, wgh_c, wuh_c, wdh_c):
    i = pl.program_id(0)
    n = pl.num_programs(0)
    tg = y_ref.shape[0]
    slot = i % 2
    valid = tv_ref[i] != 0
    prev = jnp.maximum(i - 1, 0)

    def fetch(tile, s):
        base = tile * tg
        for t in range(tg):
            pltpu.async_copy(rows_hbm.at[src_ref[base + t]], xbuf_ref.at[s, t], sem_ref.at[s], priority=1)

    def wait_rows(s):
        pltpu.make_async_copy(rows_hbm.at[pl.ds(0, tg)], xbuf_ref.at[s], sem_ref.at[s]).wait()

    @pl.when(i == 0)
    def _():
        fetch(0, 0)

    for e_off, f32_refs, bf16_refs in ((0, (wgl_ref, wul_ref, wdl_ref), (wgl_c, wul_c, wdl_c)),
                                       (n, (wgh_ref, wuh_ref, wdh_ref), (wgh_c, wuh_c, wdh_c))):
        @pl.when(valid & ((i == 0) | (te_ref[e_off + i] != te_ref[e_off + prev])))
        def _():
            for src, dst in zip(f32_refs, bf16_refs):
                dst[...] = src[...].astype(BF16)

    @pl.when(valid)
    def _():
        wait_rows(slot)
        fetch(jnp.minimum(i + 1, n - 1), 1 - slot)
        x = jnp.concatenate([xbuf_ref[slot, :, k, :] for k in range(ROW_SLABS)], axis=-1)
        xb = x.astype(BF16)
        s_lo = _sigmoid(jnp.sum(x * rwl_ref[...], axis=-1, keepdims=True))
        s_hi = _sigmoid(jnp.sum(x * rwh_ref[...], axis=-1, keepdims=True))
        inv = 1.0 / (s_lo + s_hi)

        def ffn(wg, wu, wd):
            a = _dot(xb, wg[...])
            u = _dot(xb, wu[...])
            return _dot((a * _sigmoid(a) * u).astype(BF16), wd[...])

        y = (s_lo * inv) * ffn(wgl_c, wul_c, wdl_c) + (s_hi * inv) * ffn(wgh_c, wuh_c, wdh_c)
        for k in range(ROW_SLABS):
            y_ref[:, k, :] = y[:, k * LANES:(k + 1) * LANES]

    @pl.when(jnp.logical_not(valid))
    def _():
        y_ref[...] = jnp.zeros_like(y_ref)

    @pl.when((valid & (i == n - 1)) | (jnp.logical_not(valid) & (tv_ref[prev] != 0)))
    def _():
        wait_rows(jnp.where(valid, 1 - slot, slot))


def experts(tile_expert, tile_valid, src, rows, router_w_t, wg, wu, wd, layer):
    tg = EXPERT_TILE
    n_tiles = src.shape[0] // tg
    d, ff = wg.shape[2:]
    lo = lambda i, te, tv, src: (te[i], 0, 0)
    hi = lambda i, te, tv, src: (te[n_tiles + i], 0, 0)
    wsel = lambda sel: (lambda i, te, tv, src: (layer,) + sel(i, te, tv, src))
    wspecs = [pl.BlockSpec((None, None) + w.shape[2:], wsel(sel)) for sel in (lo, hi) for w in (wg, wu, wd)]
    wcache = [pltpu.VMEM(w.shape[2:], BF16) for _ in range(2) for w in (wg, wu, wd)]
    return pl.pallas_call(
        _expert_kernel,
        grid_spec=pltpu.PrefetchScalarGridSpec(
            num_scalar_prefetch=3,
            grid=(n_tiles,),
            in_specs=[pl.BlockSpec(memory_space=pl.ANY),
                      pl.BlockSpec((None, 1, d), lo), pl.BlockSpec((None, 1, d), hi)] + wspecs,
            out_specs=pl.BlockSpec((tg, ROW_SLABS, LANES), lambda i, te, tv, src: (i, 0, 0)),
            scratch_shapes=[pltpu.VMEM((2, tg, ROW_SLABS, LANES), F32), pltpu.SemaphoreType.DMA((2,))] + wcache,
        ),
        out_shape=jax.ShapeDtypeStruct((n_tiles * tg, ROW_SLABS, LANES), F32),
        compiler_params=_cparams(("arbitrary",)),
        name="moe_experts",
    )(tile_expert, tile_valid, src, rows, router_w_t, router_w_t, wg, wu, wd, wg, wu, wd)


def _combine_kernel(dest_ref, hf_ref, x_ref, m_ref, wg_ref, wu_ref, wd_ref, y_hbm, o_ref, ybuf_ref, sem_ref):
    i = pl.program_id(0)
    n = pl.num_programs(0)
    tm = hf_ref.shape[0]
    slot = i % 2

    def fetch(tile, s):
        base = tile * tm
        _for_rows(tm, lambda t, u: pltpu.async_copy(
            y_hbm.at[dest_ref[base + t]], ybuf_ref.at[s, t], sem_ref.at[s], priority=u % 2))

    @pl.when(i == 0)
    def _():
        fetch(0, 0)

    @pl.when(i + 1 < n)
    def _():
        fetch(i + 1, 1 - slot)

    h = hf_ref[...]
    a = _dot(h, wg_ref[...])
    u = _dot(h, wu_ref[...])
    shared = _dot((a * _sigmoid(a) * u).astype(BF16), wd_ref[...])
    pltpu.make_async_copy(y_hbm.at[pl.ds(0, tm)], ybuf_ref.at[slot], sem_ref.at[slot]).wait()
    for k in range(ROW_SLABS):
        cols = slice(k * LANES, (k + 1) * LANES)
        o_ref[:, cols] = x_ref[:, cols] + m_ref[G_F:G_F + 1, cols] * (shared[:, cols] + ybuf_ref[slot, :, k, :])


def combine(dest, hf, xs, mods, wg, wu, wd, y_sorted, *, tm, n_ctx_tiles):
    b, r, d = hf.shape
    nt = r // tm
    ff = wg.shape[1]
    row = lambda i, dest: (i // nt, i % nt, 0)
    const = lambda i, dest: (0, 0)
    return pl.pallas_call(
        _combine_kernel,
        grid_spec=pltpu.PrefetchScalarGridSpec(
            num_scalar_prefetch=1,
            grid=(b * nt,),
            in_specs=[
                pl.BlockSpec((None, tm, d), row),
                pl.BlockSpec((None, tm, d), row),
                pl.BlockSpec((None, None, 6, d),
                             lambda i, dest: (i // nt, ((i % nt) >= n_ctx_tiles).astype(jnp.int32), 0, 0)),
                pl.BlockSpec((d, ff), const), pl.BlockSpec((d, ff), const), pl.BlockSpec((ff, d), const),
                pl.BlockSpec(memory_space=pl.ANY),
            ],
            out_specs=pl.BlockSpec((None, tm, d), row),
            scratch_shapes=[pltpu.VMEM((2, tm, ROW_SLABS, LANES), F32), pltpu.SemaphoreType.DMA((2,))],
        ),
        out_shape=jax.ShapeDtypeStruct((b, r, d), F32),
        compiler_params=_cparams(("arbitrary",)),
        name="moe_combine",
    )(dest, hf, xs, mods, wg, wu, wd, y_sorted)


def moe_block(xs, mods, gain, router_w, router_bias, ew_gate, ew_up, ew_down, layer, sw_gate, sw_up, sw_down,
              *, tm, n_ctx_tiles):
    b, r, d = xs.shape
    n = b * r
    tg = EXPERT_TILE
    hf, rows, info, counts = route(xs, mods, gain, router_w, router_bias, tm=tm, n_ctx_tiles=n_ctx_tiles)
    cls = info[:, :, 0, :].reshape(n)
    rank = info[:, :, 1, :].reshape(n)
    padded = ((counts[:, 0].astype(jnp.int32) + tg - 1) // tg) * tg
    ends = jnp.cumsum(padded)
    dest = (ends - padded)[cls] + rank
    n_tiles = n // tg + N_CLASSES
    tile_start = jnp.arange(n_tiles, dtype=jnp.int32) * tg
    tile_valid = tile_start < ends[-1]
    last_start = jnp.maximum(ends[-1] - tg, 0)
    start = jnp.where(tile_valid, tile_start, last_start)
    tile_cls = jnp.sum((ends[None, :] <= start[:, None]).astype(jnp.int32), axis=1)
    tile_cls = jnp.minimum(tile_cls, N_CLASSES - 1)
    tile_expert = jnp.concatenate([jnp.asarray(CLASS_LO)[tile_cls], jnp.asarray(CLASS_HI)[tile_cls]])
    src = invert_permutation(dest, n_tiles * tg)
    y_sorted = experts(tile_expert, tile_valid.astype(jnp.int32), src, rows,
                       router_w.T.astype(F32).reshape(N_EXPERTS, 1, d), ew_gate, ew_up, ew_down, layer)
    return combine(dest, hf, xs, mods, sw_gate.astype(BF16), sw_up.astype(BF16), sw_down.astype(BF16), y_sorted,
                   tm=tm, n_ctx_tiles=n_ctx_tiles)


def kernel(x, c, ctx, c_ctx, mod_w, mod_b, norm_mix, norm_ffn, even_w_in, even_w_out, diff_q_gain, diff_k_gain, diff_lambda, diff_subln, hgrn_lb_logits, hgrn_out_gain, odd_w_in, mla_q_a_gain, mla_kv_a_gain, mla_w_uq, mla_w_ukv, mla_q_nope_gain, mla_q_rope_gain, mla_k_nope_gain, mla_k_rope_gain, odd_w_out, router_w, router_bias, expert_w_gate, expert_w_up, expert_w_down, shared_w_gate, shared_w_up, shared_w_down):
    b, n_lat, d = x.shape
    n_ctx = ctx.shape[1]
    depth = mod_w.shape[0]
    tm = 256 if n_ctx % 256 == 0 else 128
    n_ctx_tiles = n_ctx // tm
    rope = rope_lane_tables(n_ctx, n_lat)

    mod_rows = 16
    c_rows = jnp.concatenate([c, c_ctx[None, :], jnp.zeros((mod_rows - b - 1, d), F32)], axis=0)
    mod_all = modulation(c_rows, mod_w, mod_b).reshape(depth, mod_rows, 6, d)
    lb_all = jnp.cumsum(jax.nn.softmax(hgrn_lb_logits.astype(F32), axis=0), axis=0)

    xs = jnp.concatenate([ctx, x], axis=1)
    for layer in range(depth):
        last = layer == depth - 1
        j = layer // 2
        mods = jnp.stack([jnp.broadcast_to(mod_all[layer, b], (b, 6, d)), mod_all[layer, :b]], axis=1)
        row_tile0 = n_ctx_tiles if last else 0
        if layer % 2 == 0:
            lam_init = 0.8 - 0.6 * math.exp(-0.3 * layer)
            p = norm_matmul(xs, norm_mix[layer], even_w_in[j].astype(BF16), tm=tm, mods=mods, n_ctx_tiles=n_ctx_tiles)
            oa = diff_attention(p, diff_lambda[j], diff_q_gain[j], diff_k_gain[j], diff_subln[j], rope,
                                tq=tm, n_ctx=n_ctx, lam_init=lam_init, q_blk0=0, k_blk0=4, v_blk0=8)
            ob = hgrn_bidirectional(p, lb_all[j, 0], lb_all[j, 1], hgrn_out_gain[j], n_ctx=n_ctx, blk0=12)
            if last:
                oa, ob = oa[:, n_ctx:], ob[:, n_ctx:]
            w_out = even_w_out[j].astype(BF16)
            half = oa.shape[-1]
            xs_mid = proj_residual([oa, ob], [w_out[:half], w_out[half:]], xs, mods, tm=tm,
                                   n_ctx_tiles=n_ctx_tiles, row_tile0=row_tile0)
        else:
            w_in = odd_w_in[j]
            zpad = jnp.zeros((d, LANES - MLA_ROPE), F32)
            w_in_r = jnp.concatenate([w_in[:, MLA_Q_LORA:MLA_Q_LORA + MLA_KV_LORA], w_in[:, MLA_Q_LORA + MLA_KV_LORA:],
                                      zpad, w_in[:, :MLA_Q_LORA]], axis=1).astype(BF16)
            p1 = norm_matmul(xs, norm_mix[layer], w_in_r, tm=tm, mods=mods, n_ctx_tiles=n_ctx_tiles)
            w_uq = mla_w_uq[j].reshape(MLA_Q_LORA, MLA_HEADS, MLA_NOPE + MLA_ROPE)
            w_uq = jnp.concatenate([w_uq, jnp.zeros((MLA_Q_LORA, MLA_HEADS, LANES - MLA_ROPE), F32)], axis=-1)
            w_uq = w_uq.reshape(MLA_Q_LORA, MLA_HEADS * 2 * LANES).astype(BF16)
            q = norm_matmul(p1, mla_q_a_gain[j], w_uq, tm=tm, col_block=1, row_tile0=row_tile0)
            kv = norm_matmul(p1, mla_kv_a_gain[j], mla_w_ukv[j].astype(BF16), tm=tm, col_block=0)
            if not last:
                raise NotImplementedError("context queries for a non-final latent-attention layer")
            o = mla_attention(q, kv, p1, 2, rope, mla_q_nope_gain[j], mla_q_rope_gain[j], mla_k_nope_gain[j],
                              mla_k_rope_gain[j], tq=tm, n_ctx=n_ctx)
            xs_mid = proj_residual([o], [odd_w_out[j].astype(BF16)], xs, mods, tm=tm,
                                   n_ctx_tiles=n_ctx_tiles, row_tile0=row_tile0)
        xs = moe_block(xs_mid, mods, norm_ffn[layer], router_w, router_bias,
                       expert_w_gate, expert_w_up, expert_w_down, layer,
                       shared_w_gate[layer], shared_w_up[layer], shared_w_down[layer],
                       tm=tm, n_ctx_tiles=0 if last else n_ctx_tiles)
    return xs if xs.shape[1] == n_lat else xs[:, n_ctx:]
```

```python
import functools
import math

import numpy as np
import jax
import jax.numpy as jnp
from jax import lax
from jax.experimental import pallas as pl
from jax.experimental.pallas import tpu as pltpu

F32 = jnp.float32
BF16 = jnp.bfloat16

LANES = 128
VMEM_LIMIT = 56 * 1024 * 1024

GRID_W = 64
DIFF_HEADS = 4
DIFF_HEAD_DIM = 64
HGRN_HEADS = 4
HGRN_K_DIM = 128
MLA_HEADS = 8
MLA_NOPE = 128
MLA_ROPE = 64
MLA_V = 128
MLA_Q_LORA = 384
MLA_KV_LORA = 256
N_EXPERTS = 16
N_GROUPS = 4
EXPERTS_PER_GROUP = 4
ROPE_BASE = 10000.0
EPS = 1e-6
LOG2E = 1.4426950408889634
GLA_CHUNK = 64
ATTN_KEY_CHUNK = 1152
MLA_HEADS_PER_STEP = 2
DIFF_HEADS_PER_STEP = 2
HGRN_HEADS_PER_STEP = 2

SH_M, SC_M, G_M, SH_F, SC_F, G_F = range(6)


def _sigmoid(x):
    return 1.0 / (1.0 + jnp.exp(-x))


def _dot(a, b):
    return jnp.dot(a, b, preferred_element_type=F32)


def _dot_nt(a, b):
    return lax.dot_general(a, b, (((1,), (1,)), ((), ())), preferred_element_type=F32)


def _dot_tn(a, b):
    return lax.dot_general(a, b, (((0,), (0,)), ((), ())), preferred_element_type=F32)


def _split3(x):
    hi = x.astype(BF16)
    r = x - hi.astype(F32)
    mid = r.astype(BF16)
    lo = (r - mid.astype(F32)).astype(BF16)
    return hi, mid, lo


def _rms(x, width=None):
    n = x.shape[-1] if width is None else width
    return x * lax.rsqrt(jnp.sum(x * x, axis=-1, keepdims=True) * (1.0 / n) + EPS)


def _cparams(sem):
    return pltpu.CompilerParams(dimension_semantics=sem, vmem_limit_bytes=VMEM_LIMIT)


def _mod_kernel(c_ref, w_ref, b_ref, o_ref):
    c = c_ref[...]
    s = c * _sigmoid(c)
    o_ref[...] = _dot(s.astype(BF16), w_ref[...].astype(BF16)) + b_ref[...]


def modulation(c_rows, mod_w, mod_b):
    n_layers, d, n = mod_w.shape
    rows = c_rows.shape[0]
    tn = 1536
    return pl.pallas_call(
        _mod_kernel,
        grid=(n_layers, n // tn),
        in_specs=[
            pl.BlockSpec((rows, d), lambda l, j: (0, 0)),
            pl.BlockSpec((None, d, tn), lambda l, j: (l, 0, j)),
            pl.BlockSpec((None, 1, tn), lambda l, j: (l, 0, j)),
        ],
        out_specs=pl.BlockSpec((None, rows, tn), lambda l, j: (l, 0, j)),
        out_shape=jax.ShapeDtypeStruct((n_layers, rows, n), F32),
        compiler_params=_cparams(("arbitrary", "arbitrary")),
        name="modulation",
    )(c_rows, mod_w, mod_b.reshape(n_layers, 1, n))


def _norm_matmul_kernel(*refs, has_mod):
    if has_mod:
        x_ref, m_ref, g_ref, w_ref, o_ref = refs
    else:
        x_ref, g_ref, w_ref, o_ref = refs
    h = _rms(x_ref[...].astype(F32)) * g_ref[...]
    if has_mod:
        h = h * (1.0 + m_ref[SC_M:SC_M + 1, :]) + m_ref[SH_M:SH_M + 1, :]
    o_ref[...] = _dot(h.astype(BF16), w_ref[...]).astype(o_ref.dtype)


def norm_matmul(x, gain, w, *, tm, col_block=0, row_tile0=0, n_row_tiles=None, mods=None, n_ctx_tiles=0):
    b, t, _ = x.shape
    k, n = w.shape
    if n_row_tiles is None:
        n_row_tiles = t // tm - row_tile0
    in_specs = [pl.BlockSpec((None, tm, k), lambda bi, ti: (bi, ti + row_tile0, col_block))]
    args = [x]
    if mods is not None:
        in_specs.append(pl.BlockSpec((None, None, 6, k),
                                     lambda bi, ti: (bi, ((ti + row_tile0) >= n_ctx_tiles).astype(jnp.int32), 0, 0)))
        args.append(mods)
    in_specs += [pl.BlockSpec((1, k), lambda bi, ti: (0, 0)), pl.BlockSpec((k, n), lambda bi, ti: (0, 0))]
    args += [gain.reshape(1, k).astype(F32), w]
    return pl.pallas_call(
        functools.partial(_norm_matmul_kernel, has_mod=mods is not None),
        grid=(b, n_row_tiles),
        in_specs=in_specs,
        out_specs=pl.BlockSpec((None, tm, n), lambda bi, ti: (bi, ti, 0)),
        out_shape=jax.ShapeDtypeStruct((b, n_row_tiles * tm, n), BF16),
        compiler_params=_cparams(("arbitrary", "arbitrary")),
        name="norm_matmul",
    )(*args)


def rope_lane_tables(n_ctx, n_lat):
    rows = n_lat // GRID_W
    row = np.repeat(np.arange(rows), GRID_W).astype(np.float32)
    col = np.tile(np.arange(GRID_W), rows).astype(np.float32)
    axis_dim = DIFF_HEAD_DIM // 2
    inv_freq = jnp.asarray(ROPE_BASE, F32) ** (-jnp.arange(0, axis_dim, 2, dtype=F32) / axis_dim)
    ang_r = jnp.asarray(row)[:, None] * inv_freq
    ang_c = jnp.asarray(col)[:, None] * inv_freq
    lane = np.arange(LANES)
    freq_idx = lane % 16
    use_col = (lane % 64) >= 32
    first = (lane % 32) < 16
    ang = jnp.where(use_col[None, :], ang_c[:, freq_idx], ang_r[:, freq_idx])
    cos, sin = jnp.cos(ang), jnp.sin(ang)
    c = jnp.concatenate([jnp.ones((n_ctx, LANES), F32), cos], axis=0)
    sa = jnp.concatenate([jnp.zeros((n_ctx, LANES), F32), jnp.where(first[None, :], -sin, 0.0)], axis=0)
    sb = jnp.concatenate([jnp.zeros((n_ctx, LANES), F32), jnp.where(first[None, :], 0.0, sin)], axis=0)
    return c, sa, sb


def _rope(x, c, sa, sb):
    return x * c + pltpu.roll(x, LANES - 16, 1) * sa + pltpu.roll(x, 16, 1) * sb


def _block_ones(width):
    r = lax.broadcasted_iota(jnp.int32, (LANES, LANES), 0) // width
    c = lax.broadcasted_iota(jnp.int32, (LANES, LANES), 1) // width
    return (r == c).astype(BF16)


def _block_rms(x, ones, width):
    xx = x * x
    hi = xx.astype(BF16)
    lo = (xx - hi.astype(F32)).astype(BF16)
    ms = (_dot(hi, ones) + _dot(lo, ones)) * (1.0 / width)
    return x * lax.rsqrt(ms + EPS)


def _attend_streams(qs, kt_refs, vs, nk):
    n = len(qs)
    chunk = ATTN_KEY_CHUNK if nk % ATTN_KEY_CHUNK == 0 else nk
    nc = nk // chunk
    s = [[None] * nc for _ in range(n)]
    p = [[None] * nc for _ in range(n)]
    m, l, o = [None] * n, [None] * n, [None] * n

    def logits(i, c):
        s[i][c] = _dot(qs[i], kt_refs[i][:, c * chunk:(c + 1) * chunk])
        mc = jnp.max(s[i][c], axis=-1, keepdims=True)
        m[i] = mc if c == 0 else jnp.maximum(m[i], mc)

    def exps(i, c):
        e = jnp.exp2(s[i][c] - m[i])
        lc = jnp.sum(e, axis=-1, keepdims=True)
        l[i] = lc if c == 0 else l[i] + lc
        p[i][c] = e.astype(BF16)

    def values(i, c):
        ref, col0 = vs[i]
        oc = _dot(p[i][c], ref[c * chunk:(c + 1) * chunk, col0:col0 + LANES])
        o[i] = oc if c == 0 else o[i] + oc

    for step in range(n + 2):
        for c in range(nc):
            if step < n:
                logits(step, c)
            if 0 <= step - 1 < n:
                exps(step - 1, c)
            if 0 <= step - 2 < n:
                values(step - 2, c)
    return list(zip(o, l))


def _diff_attn_kernel(lam_ref, q_ref, k_ref, v_ref, cq_ref, saq_ref, sbq_ref, ck_ref, sak_ref, sbk_ref,
                      qg_ref, kg_ref, sub_ref, o_ref, kt_ref, *, n_ctx, n_ctx_tiles, lam_init):
    qi = pl.program_id(2)
    ones = _block_ones(DIFF_HEAD_DIM)
    heads = range(DIFF_HEADS_PER_STEP)

    @pl.when(qi == 0)
    def _():
        for h in heads:
            k = _block_rms(k_ref[:, h * LANES:(h + 1) * LANES].astype(F32), ones, DIFF_HEAD_DIM) * kg_ref[...]
            kt_ref[h] = _rope(k, ck_ref[...], sak_ref[...], sbk_ref[...]).T.astype(BF16)

    qs = []
    for h in heads:
        q = _block_rms(q_ref[:, h * LANES:(h + 1) * LANES].astype(F32), ones, DIFF_HEAD_DIM) * qg_ref[...]
        q = _rope(q, cq_ref[...], saq_ref[...], sbq_ref[...]) * (DIFF_HEAD_DIM ** -0.5 * LOG2E)
        lane = lax.broadcasted_iota(jnp.int32, q.shape, 1)
        qs.append(jnp.where(lane < DIFF_HEAD_DIM, q, 0.0).astype(BF16))
        qs.append(jnp.where(lane >= DIFF_HEAD_DIM, q, 0.0).astype(BF16))
    lv = lam_ref[...]
    lam = (jnp.exp(jnp.sum(lv[0:1] * lv[1:2], axis=-1, keepdims=True))
           - jnp.exp(jnp.sum(lv[2:3] * lv[3:4], axis=-1, keepdims=True)) + lam_init)

    def attend(nk):
        outs = _attend_streams(qs, [kt_ref.at[h] for h in heads for _ in range(2)],
                               [(v_ref, h * LANES) for h in heads for _ in range(2)], nk)
        for h in heads:
            (o1, l1), (o2, l2) = outs[2 * h], outs[2 * h + 1]
            o = o1 * (1.0 / l1) - o2 * (lam / l2)
            o_ref[:, h * LANES:(h + 1) * LANES] = (_rms(o) * sub_ref[...] * (1.0 - lam_init)).astype(o_ref.dtype)

    @pl.when(qi < n_ctx_tiles)
    def _():
        attend(n_ctx)

    @pl.when(qi >= n_ctx_tiles)
    def _():
        attend(k_ref.shape[0])


def diff_attention(p, lam_vecs, q_gain, k_gain, subln, rope, *, tq, n_ctx, lam_init, q_blk0, k_blk0, v_blk0):
    b, t, _ = p.shape
    c, sa, sb = rope
    nq = t // tq
    hps = DIFF_HEADS_PER_STEP
    assert q_blk0 % hps == 0 and k_blk0 % hps == 0 and v_blk0 % hps == 0
    row_q = lambda bi, h, qi: (qi, 0)
    full = lambda bi, h, qi: (0, 0)
    tile2 = lambda g: jnp.tile(g.astype(F32), 2).reshape(1, LANES)
    return pl.pallas_call(
        functools.partial(_diff_attn_kernel, n_ctx=n_ctx, n_ctx_tiles=n_ctx // tq, lam_init=lam_init),
        grid=(b, DIFF_HEADS // hps, nq),
        in_specs=[
            pl.BlockSpec((4, DIFF_HEAD_DIM), full),
            pl.BlockSpec((None, tq, hps * LANES), lambda bi, h, qi: (bi, qi, q_blk0 // hps + h)),
            pl.BlockSpec((None, t, hps * LANES), lambda bi, h, qi: (bi, 0, k_blk0 // hps + h)),
            pl.BlockSpec((None, t, hps * LANES), lambda bi, h, qi: (bi, 0, v_blk0 // hps + h)),
            pl.BlockSpec((tq, LANES), row_q), pl.BlockSpec((tq, LANES), row_q), pl.BlockSpec((tq, LANES), row_q),
            pl.BlockSpec((t, LANES), full), pl.BlockSpec((t, LANES), full), pl.BlockSpec((t, LANES), full),
            pl.BlockSpec((1, LANES), full), pl.BlockSpec((1, LANES), full), pl.BlockSpec((1, LANES), full),
        ],
        out_specs=pl.BlockSpec((None, tq, hps * LANES), lambda bi, h, qi: (bi, qi, h)),
        out_shape=jax.ShapeDtypeStruct((b, t, DIFF_HEADS * LANES), BF16),
        scratch_shapes=[pltpu.VMEM((hps, LANES, t), BF16)],
        compiler_params=_cparams(("arbitrary", "arbitrary", "arbitrary")),
        name="diff_attention",
    )(lam_vecs.astype(F32), p, p, p, c, sa, sb, c, sa, sb, tile2(q_gain), tile2(k_gain),
      subln.astype(F32).reshape(1, LANES))


def _gla_constants(c):
    levels = int(math.log2(c))
    t = np.arange(c)[:, None]
    u = np.arange(c)[None, :]
    stack = [[u <= t], [u >= t]]
    qside = [[], []]
    pair = [[t == u], [t == u]]
    for lv in range(1, levels + 1):
        base = (t >> lv) << lv
        half = 1 << (lv - 1)
        stack[0].append(u <= base + half - 1)
        stack[1].append(u >= base + half)
        up_t = ((t >> (lv - 1)) & 1) == 1
        up_u = ((u >> (lv - 1)) & 1) == 1
        same = (t >> lv) == (u >> lv)
        qside[0].append(np.broadcast_to(up_t, (c, LANES)))
        qside[1].append(np.broadcast_to(~up_t, (c, LANES)))
        pair[0].append(same & up_t & ~up_u)
        pair[1].append(same & ~up_t & up_u)
    f32 = lambda x: np.asarray(x, np.float32)
    stack = np.stack([np.tile(np.concatenate(f32(m), axis=0), (1, 3)) for m in stack])
    return (jnp.asarray(stack, BF16), jnp.asarray(np.stack([f32(m) for m in qside])),
            jnp.asarray(np.stack([f32(m) for m in pair])))


def _gla_chunk(q, k, v, g2, s, stack, qside_ref, pair_ref, d):
    c = q.shape[0]
    levels = int(math.log2(c))
    gcat = jnp.concatenate(_split3(g2), axis=0)
    cs = _dot(stack, gcat)
    tot_col = _dot_tn(gcat, jnp.ones((3 * c, LANES), BF16))
    diag = jnp.sum(q * k, axis=-1, keepdims=True)
    yield
    cum = cs[0:c]
    tot = cum[0:1] if d == 1 else cum[c - 1:c]
    zz = []
    for lv in range(1, levels + 1):
        e = jnp.exp2(-jnp.abs(cum - cs[lv * c:(lv + 1) * c]))
        z = (jnp.where(qside_ref[d, lv - 1] > 0.5, q, k) * e).astype(BF16)
        zz.append(_dot_nt(z, z))
    q_in = (q * jnp.exp2(cum)).astype(BF16)
    ks = (k * jnp.exp2(tot - cum)).astype(BF16)
    ds = _dot_tn(ks, v)
    yield
    a = pair_ref[d, 0] * diag
    for lv in range(1, levels + 1):
        a = a + pair_ref[d, lv] * zz[lv - 1]
    lhs = jnp.concatenate([q_in, a.astype(BF16)], axis=1)
    o = _dot(lhs, jnp.concatenate([s.astype(BF16), v], axis=0))
    s_new = s * jnp.exp2(tot_col) + ds
    yield
    return o, s_new


def _run_interleaved(gens):
    results = [None] * len(gens)
    live = list(range(len(gens)))
    while live:
        for i in list(live):
            try:
                next(gens[i])
            except StopIteration as stop:
                results[i] = stop.value
                live.remove(i)
    return results


def _hgrn_kernel(qz_ref, zf_ref, zb_ref, v_ref, gz_ref, lbf_ref, lbb_ref, og_ref, stack_ref, qside_ref, pair_ref,
                 o_ref, q_ref, kf_ref, gf_ref, kb_ref, gb_ref, of_ref, ob_ref, st_ref, *, n_ctx_chunks):
    c = GLA_CHUNK
    t = qz_ref.shape[0]
    n = t // c
    heads = range(HGRN_HEADS_PER_STEP)
    qz = qz_ref[...].astype(F32)
    q_ref[...] = qz * _sigmoid(qz) * (HGRN_K_DIM ** -0.5)
    for z_ref, lb_ref, k_ref, g_ref in ((zf_ref, lbf_ref, kf_ref, gf_ref), (zb_ref, lbb_ref, kb_ref, gb_ref)):
        lb = jnp.concatenate([lb_ref[h] for h in heads], axis=-1)
        f = lb + (1.0 - lb) * _sigmoid(z_ref[...].astype(F32))
        k_ref[...] = 1.0 - f
        g_ref[...] = jnp.log(f) * LOG2E
    st_ref[...] = jnp.zeros_like(st_ref)

    def body(i, carry):
        rf = pl.multiple_of(i * c, c)
        cb = jnp.where(i < n_ctx_chunks, n_ctx_chunks - 1 - i, n - 1 - i + n_ctx_chunks)
        rb = pl.multiple_of(cb * c, c)
        chains = [(h, d, pl.ds(r0, c), k_ref, g_ref, out_ref) for h in heads
                  for d, r0, k_ref, g_ref, out_ref in ((0, rf, kf_ref, gf_ref, of_ref), (1, rb, kb_ref, gb_ref, ob_ref))]
        col = lambda h: slice(h * LANES, (h + 1) * LANES)
        outs = _run_interleaved([
            _gla_chunk(q_ref[rows, col(h)], k_ref[rows, col(h)], v_ref[rows, col(h)], g_ref[rows, col(h)],
                       st_ref[h, d], stack_ref[d], qside_ref, pair_ref, d)
            for (h, d, rows, k_ref, g_ref, _) in chains])
        for (h, d, rows, _, _, out_ref), (o, s_new) in zip(chains, outs):
            out_ref[rows, col(h)] = o
            st_ref[h, d] = s_new
        return carry

    lax.fori_loop(0, n, body, 0)
    gz = gz_ref[...].astype(F32)
    for h in heads:
        cols = slice(h * LANES, (h + 1) * LANES)
        o = _rms(of_ref[:, cols] + ob_ref[:, cols]) * og_ref[...]
        o_ref[:, cols] = (o * (gz[:, cols] * _sigmoid(gz[:, cols]))).astype(o_ref.dtype)


def hgrn_bidirectional(p, lb_fwd, lb_bwd, out_gain, *, n_ctx, blk0):
    b, t, _ = p.shape
    c = GLA_CHUNK
    consts = _gla_constants(c)
    h = HGRN_HEADS
    hps = HGRN_HEADS_PER_STEP
    assert blk0 % hps == 0 and h % hps == 0
    seg = lambda s: pl.BlockSpec((None, t, hps * LANES), lambda bi, hi: (bi, 0, (blk0 + s * h) // hps + hi))
    per_head = pl.BlockSpec((hps, 1, LANES), lambda bi, hi: (hi, 0, 0))
    const = lambda a: pl.BlockSpec(a.shape, lambda bi, hi: (0,) * a.ndim)
    seq = pltpu.VMEM((t, hps * LANES), F32)
    return pl.pallas_call(
        functools.partial(_hgrn_kernel, n_ctx_chunks=n_ctx // c),
        grid=(b, h // hps),
        in_specs=[seg(0), seg(1), seg(2), seg(3), seg(4), per_head, per_head,
                  pl.BlockSpec((1, LANES), lambda bi, hi: (0, 0))] + [const(a) for a in consts],
        out_specs=pl.BlockSpec((None, t, hps * LANES), lambda bi, hi: (bi, 0, hi)),
        out_shape=jax.ShapeDtypeStruct((b, t, h * LANES), BF16),
        scratch_shapes=[seq, seq, seq, seq, seq, seq, seq, pltpu.VMEM((hps, 2, LANES, LANES), F32)],
        compiler_params=_cparams(("arbitrary", "arbitrary")),
        name="hgrn_bidirectional",
    )(p, p, p, p, p, lb_fwd.reshape(h, 1, LANES), lb_bwd.reshape(h, 1, LANES),
      out_gain.astype(F32).reshape(1, LANES), *consts)


def _mla_attn_kernel(q_ref, kv_ref, kr_ref, cq_ref, saq_ref, sbq_ref, ck_ref, sak_ref, sbk_ref,
                     qn_ref, qr_ref, kn_ref, krg_ref, o_ref, kt_ref):
    qi = pl.program_id(2)
    ones_nope = _block_ones(MLA_NOPE)
    ones_rope = _block_ones(MLA_ROPE)
    hw = 2 * LANES

    @pl.when(qi == 0)
    def _():
        kr = _block_rms(kr_ref[...].astype(F32), ones_rope, MLA_ROPE) * krg_ref[...]
        kr = _rope(kr, ck_ref[...], sak_ref[...], sbk_ref[...]).T.astype(BF16)
        for h in range(MLA_HEADS_PER_STEP):
            kn = _block_rms(kv_ref[:, h * hw:h * hw + MLA_NOPE].astype(F32), ones_nope, MLA_NOPE) * kn_ref[...]
            kt_ref[h, 0:LANES, :] = kn.T.astype(BF16)
            kt_ref[h, LANES:hw, :] = kr

    scale = (MLA_NOPE + MLA_ROPE) ** -0.5 * LOG2E
    qs = []
    for h in range(MLA_HEADS_PER_STEP):
        qn = _block_rms(q_ref[:, h * hw:h * hw + MLA_NOPE].astype(F32), ones_nope, MLA_NOPE) * (qn_ref[...] * scale)
        qr = _block_rms(q_ref[:, h * hw + MLA_NOPE:(h + 1) * hw].astype(F32), ones_rope, MLA_ROPE) * qr_ref[...]
        qr = _rope(qr, cq_ref[...], saq_ref[...], sbq_ref[...]) * scale
        qs.append(jnp.concatenate([qn.astype(BF16), qr.astype(BF16)], axis=-1))
    heads = range(MLA_HEADS_PER_STEP)
    outs = _attend_streams(qs, [kt_ref.at[h] for h in heads], [(kv_ref, h * hw + MLA_NOPE) for h in heads],
                           kt_ref.shape[2])
    for h, (o, l) in enumerate(outs):
        o_ref[:, h * MLA_V:(h + 1) * MLA_V] = (o * (1.0 / l)).astype(o_ref.dtype)


def mla_attention(q, kv, p1, kr_blk, rope, qn_gain, qr_gain, kn_gain, kr_gain, *, tq, n_ctx):
    b, n_lat, _ = q.shape
    t = kv.shape[1]
    c, sa, sb = rope
    hps = MLA_HEADS_PER_STEP
    row_q = lambda bi, h, qi: (qi, 0)
    full = lambda bi, h, qi: (0, 0)
    pad = lambda g: jnp.concatenate([g.astype(F32), jnp.zeros((LANES - g.shape[0],), F32)]).reshape(1, LANES)
    return pl.pallas_call(
        _mla_attn_kernel,
        grid=(b, MLA_HEADS // hps, n_lat // tq),
        in_specs=[
            pl.BlockSpec((None, tq, hps * 2 * LANES), lambda bi, h, qi: (bi, qi, h)),
            pl.BlockSpec((None, t, hps * 2 * LANES), lambda bi, h, qi: (bi, 0, h)),
            pl.BlockSpec((None, t, LANES), lambda bi, h, qi: (bi, 0, kr_blk)),
            pl.BlockSpec((tq, LANES), row_q), pl.BlockSpec((tq, LANES), row_q), pl.BlockSpec((tq, LANES), row_q),
            pl.BlockSpec((t, LANES), full), pl.BlockSpec((t, LANES), full), pl.BlockSpec((t, LANES), full),
            pl.BlockSpec((1, LANES), full), pl.BlockSpec((1, LANES), full),
            pl.BlockSpec((1, LANES), full), pl.BlockSpec((1, LANES), full),
        ],
        out_specs=pl.BlockSpec((None, tq, hps * MLA_V), lambda bi, h, qi: (bi, qi, h)),
        out_shape=jax.ShapeDtypeStruct((b, n_lat, MLA_HEADS * MLA_V), BF16),
        scratch_shapes=[pltpu.VMEM((hps, 2 * LANES, t), BF16)],
        compiler_params=_cparams(("arbitrary", "arbitrary", "arbitrary")),
        name="mla_attention",
    )(q, kv, p1, c[n_ctx:], sa[n_ctx:], sb[n_ctx:], c, sa, sb, pad(qn_gain), pad(qr_gain), pad(kn_gain), pad(kr_gain))


def _proj_residual_kernel(*refs, n_in):
    a_refs = refs[:n_in]
    w_refs = refs[n_in:2 * n_in]
    x_ref, m_ref, o_ref = refs[2 * n_in:]
    acc = _dot(a_refs[0][...], w_refs[0][...])
    for a_ref, w_ref in zip(a_refs[1:], w_refs[1:]):
        acc = acc + _dot(a_ref[...], w_ref[...])
    o_ref[...] = x_ref[...] + m_ref[G_M:G_M + 1, :] * acc


def proj_residual(acts, weights, xs, mods, *, tm, n_ctx_tiles, row_tile0=0):
    b, r, _ = acts[0].shape
    d = xs.shape[-1]
    in_specs = [pl.BlockSpec((None, tm, a.shape[-1]), lambda bi, ti: (bi, ti, 0)) for a in acts]
    in_specs += [pl.BlockSpec(w.shape, lambda bi, ti: (0, 0)) for w in weights]
    in_specs += [
        pl.BlockSpec((None, tm, d), lambda bi, ti: (bi, ti + row_tile0, 0)),
        pl.BlockSpec((None, None, 6, d), lambda bi, ti: (bi, ((ti + row_tile0) >= n_ctx_tiles).astype(jnp.int32), 0, 0)),
    ]
    return pl.pallas_call(
        functools.partial(_proj_residual_kernel, n_in=len(acts)),
        grid=(b, r // tm),
        in_specs=in_specs,
        out_specs=pl.BlockSpec((None, tm, d), lambda bi, ti: (bi, ti, 0)),
        out_shape=jax.ShapeDtypeStruct((b, r, d), F32),
        compiler_params=_cparams(("arbitrary", "arbitrary")),
        name="proj_residual",
    )(*acts, *weights, xs, mods)


PAIRS = [(i, j) for i in range(EXPERTS_PER_GROUP) for j in range(i + 1, EXPERTS_PER_GROUP)]
N_CLASSES = N_GROUPS * len(PAIRS)
CLASS_LO = np.array([EXPERTS_PER_GROUP * g + i for g in range(N_GROUPS) for (i, j) in PAIRS], np.int32)
CLASS_HI = np.array([EXPERTS_PER_GROUP * g + j for g in range(N_GROUPS) for (i, j) in PAIRS], np.int32)
EXPERT_TILE = 256
ROW_SLABS = 8
DMA_UNROLL = 8


def _route_kernel(x_ref, m_ref, g_ref, rw_ref, rb_ref, hf_ref, rows_ref, info_ref, count_ref, cnt_ref):
    h = _rms(x_ref[...]) * g_ref[...]
    h = h * (1.0 + m_ref[SC_F:SC_F + 1, :]) + m_ref[SH_F:SH_F + 1, :]
    hf_ref[...] = h.astype(hf_ref.dtype)
    rows_ref[...] = _rows_to_slabs(h)
    h1, h2, h3 = _split3(h)
    w1, w2, w3 = _split3(rw_ref[...])
    logits = (_dot_nt(w1, h1) + (_dot_nt(w1, h2) + _dot_nt(w2, h1))
              + (_dot_nt(w1, h3) + _dot_nt(w2, h2) + _dot_nt(w3, h1)))
    biased = _sigmoid(logits) + rb_ref[...]
    row = [biased[e:e + 1, :] for e in range(N_EXPERTS)]
    gscore = []
    for g in range(N_GROUPS):
        m = row[4 * g:4 * g + 4]
        gscore.append(functools.reduce(jnp.maximum, [m[i] + m[j] for (i, j) in PAIRS]))
    hits = []
    for g in range(N_GROUPS):
        best = None
        for g2 in range(N_GROUPS):
            if g2 == g:
                continue
            wins = (gscore[g] > gscore[g2]) if g2 < g else (gscore[g] >= gscore[g2])
            best = wins if best is None else jnp.logical_and(best, wins)
        chosen = []
        for i in range(EXPERTS_PER_GROUP):
            rank = None
            for j in range(EXPERTS_PER_GROUP):
                if j == i:
                    continue
                mi, mj = row[4 * g + i], row[4 * g + j]
                ahead = ((mj >= mi) if j < i else (mj > mi)).astype(jnp.int32)
                rank = ahead if rank is None else rank + ahead
            chosen.append(rank < 2)
        for (i, j) in PAIRS:
            hits.append(jnp.where(best & chosen[i] & chosen[j], 1.0, 0.0))
    onehot = jnp.concatenate(hits, axis=0)
    tm = onehot.shape[1]
    upper = (lax.broadcasted_iota(jnp.int32, (tm, tm), 0) <= lax.broadcasted_iota(jnp.int32, (tm, tm), 1))
    prefix = _dot(onehot.astype(BF16), upper.astype(BF16))

    @pl.when((pl.program_id(0) == 0) & (pl.program_id(1) == 0))
    def _():
        cnt_ref[...] = jnp.zeros_like(cnt_ref)

    seen = cnt_ref[...]
    cls_id = lax.broadcasted_iota(jnp.int32, onehot.shape, 0).astype(F32)
    cls = jnp.sum(onehot * cls_id, axis=0, keepdims=True)
    rank = jnp.sum(onehot * (seen[:, 0:1] + prefix - 1.0), axis=0, keepdims=True)
    info_ref[...] = jnp.concatenate([cls, rank, jnp.zeros((6, tm), F32)], axis=0).astype(jnp.int32)
    seen = seen + jnp.sum(onehot, axis=1, keepdims=True)
    cnt_ref[...] = seen
    count_ref[...] = seen


def route(xs, mods, gain, router_w, router_bias, *, tm, n_ctx_tiles):
    b, t, d = xs.shape
    nt = t // tm
    return pl.pallas_call(
        _route_kernel,
        grid=(b, nt),
        in_specs=[
            pl.BlockSpec((None, tm, d), lambda bi, ti: (bi, ti, 0)),
            pl.BlockSpec((None, None, 6, d), lambda bi, ti: (bi, (ti >= n_ctx_tiles).astype(jnp.int32), 0, 0)),
            pl.BlockSpec((1, d), lambda bi, ti: (0, 0)),
            pl.BlockSpec((N_EXPERTS, d), lambda bi, ti: (0, 0)),
            pl.BlockSpec((N_EXPERTS, 1), lambda bi, ti: (0, 0)),
        ],
        out_specs=[
            pl.BlockSpec((None, tm, d), lambda bi, ti: (bi, ti, 0)),
            pl.BlockSpec((tm, ROW_SLABS, LANES), lambda bi, ti: (bi * nt + ti, 0, 0)),
            pl.BlockSpec((None, None, 8, tm), lambda bi, ti: (bi, ti, 0, 0)),
            pl.BlockSpec((N_CLASSES, LANES), lambda bi, ti: (0, 0)),
        ],
        out_shape=[jax.ShapeDtypeStruct((b, t, d), BF16),
                   jax.ShapeDtypeStruct((b * t, ROW_SLABS, LANES), F32),
                   jax.ShapeDtypeStruct((b, nt, 8, tm), jnp.int32),
                   jax.ShapeDtypeStruct((N_CLASSES, LANES), F32)],
        scratch_shapes=[pltpu.VMEM((N_CLASSES, LANES), F32)],
        compiler_params=_cparams(("arbitrary", "arbitrary")),
        name="route",
    )(xs, mods, gain.reshape(1, d).astype(F32), router_w.T.astype(F32),
      router_bias.reshape(N_EXPERTS, 1).astype(F32))


def _for_rows(n, fn):
    def body(k, carry):
        for u in range(DMA_UNROLL):
            fn(k * DMA_UNROLL + u, u)
        return carry
    lax.fori_loop(0, n // DMA_UNROLL, body, 0)


def _transpose8(parts):
    sub = lax.broadcasted_iota(jnp.int32, parts[0].shape, 1)
    for s in (4, 2, 1):
        keep = (sub & s) == 0
        new = list(parts)
        for i in range(8):
            if i & s:
                continue
            a, b = parts[i], parts[i | s]
            new[i] = jnp.where(keep, a, pltpu.roll(b, s, 1))
            new[i | s] = jnp.where(keep, pltpu.roll(a, 8 - s, 1), b)
        parts = new
    return parts


def _slabs_to_rows(slabs):
    g = slabs.shape[0] // 8
    x4 = slabs.reshape(g, 8, ROW_SLABS, LANES)
    parts = _transpose8([x4[:, t] for t in range(8)])
    return jnp.concatenate([p.reshape(g * 8, LANES) for p in parts], axis=-1)


def _rows_to_slabs(x):
    g = x.shape[0] // 8
    parts = _transpose8([x[:, k * LANES:(k + 1) * LANES].reshape(g, 8, LANES) for k in range(ROW_SLABS)])
    return jnp.stack(parts, axis=1).reshape(g * 8, ROW_SLABS, LANES)


def _invert_kernel(dest_ref, src_ref):
    def clear(i, carry):
        for u in range(2 * DMA_UNROLL):
            src_ref[i * 2 * DMA_UNROLL + u] = 0
        return carry

    def put(i, carry):
        for u in range(2 * DMA_UNROLL):
            t = i * 2 * DMA_UNROLL + u
            src_ref[dest_ref[t]] = t
        return carry

    lax.fori_loop(0, src_ref.shape[0] // (2 * DMA_UNROLL), clear, 0)
    lax.fori_loop(0, dest_ref.shape[0] // (2 * DMA_UNROLL), put, 0)


def invert_permutation(dest, n_rows_out):
    smem = pl.BlockSpec(memory_space=pltpu.SMEM)
    return pl.pallas_call(
        _invert_kernel,
        in_specs=[smem],
        out_specs=smem,
        out_shape=jax.ShapeDtypeStruct((n_rows_out,), jnp.int32),
        name="moe_invert",
    )(dest)


def _expert_kernel(te_ref, tv_ref, src_ref, rows_hbm, rwl_ref, rwh_ref, wgl_ref, wul_ref, wdl_ref, wgh_ref, wuh_ref,
                   wdh_ref, y_ref, xbuf_ref, sem_ref, wgl_c, wul_c, wdl_c, wgh_c, wuh_c, wdh_c):
    i = pl.program_id(0)
    n = pl.num_programs(0)
    tg = y_ref.shape[0]
    slot = i % 2
    valid = tv_ref[i] != 0
    prev = jnp.maximum(i - 1, 0)

    def fetch(tile, s):
        base = tile * tg
        for t in range(tg):
            pltpu.async_copy(rows_hbm.at[src_ref[base + t]], xbuf_ref.at[s, t], sem_ref.at[s], priority=1)

    def wait_rows(s):
        pltpu.make_async_copy(rows_hbm.at[pl.ds(0, tg)], xbuf_ref.at[s], sem_ref.at[s]).wait()

    @pl.when(i == 0)
    def _():
        fetch(0, 0)

    for e_off, f32_refs, bf16_refs in ((0, (wgl_ref, wul_ref, wdl_ref), (wgl_c, wul_c, wdl_c)),
                                       (n, (wgh_ref, wuh_ref, wdh_ref), (wgh_c, wuh_c, wdh_c))):
        @pl.when(valid & ((i == 0) | (te_ref[e_off + i] != te_ref[e_off + prev])))
        def _():
            for src, dst in zip(f32_refs, bf16_refs):
                dst[...] = src[...].astype(BF16)

    @pl.when(valid)
    def _():
        wait_rows(slot)
        x = _slabs_to_rows(xbuf_ref[slot])
        fetch(jnp.minimum(i + 1, n - 1), 1 - slot)
        xb = x.astype(BF16)
        s_lo = _sigmoid(jnp.sum(x * rwl_ref[...], axis=-1, keepdims=True))
        s_hi = _sigmoid(jnp.sum(x * rwh_ref[...], axis=-1, keepdims=True))
        inv = 1.0 / (s_lo + s_hi)

        def ffn(wg, wu, wd):
            a = _dot(xb, wg[...])
            u = _dot(xb, wu[...])
            return _dot((a * _sigmoid(a) * u).astype(BF16), wd[...])

        y = (s_lo * inv) * ffn(wgl_c, wul_c, wdl_c) + (s_hi * inv) * ffn(wgh_c, wuh_c, wdh_c)
        y_ref[...] = _rows_to_slabs(y)

    @pl.when(jnp.logical_not(valid))
    def _():
        y_ref[...] = jnp.zeros_like(y_ref)

    @pl.when((valid & (i == n - 1)) | (jnp.logical_not(valid) & (tv_ref[prev] != 0)))
    def _():
        wait_rows(jnp.where(valid, 1 - slot, slot))


def experts(tile_expert, tile_valid, src, rows, router_w_t, wg, wu, wd, layer):
    tg = EXPERT_TILE
    n_tiles = src.shape[0] // tg
    d, ff = wg.shape[2:]
    lo = lambda i, te, tv, src: (te[i], 0, 0)
    hi = lambda i, te, tv, src: (te[n_tiles + i], 0, 0)
    wsel = lambda sel: (lambda i, te, tv, src: (layer,) + sel(i, te, tv, src))
    wspecs = [pl.BlockSpec((None, None) + w.shape[2:], wsel(sel)) for sel in (lo, hi) for w in (wg, wu, wd)]
    wcache = [pltpu.VMEM(w.shape[2:], BF16) for _ in range(2) for w in (wg, wu, wd)]
    return pl.pallas_call(
        _expert_kernel,
        grid_spec=pltpu.PrefetchScalarGridSpec(
            num_scalar_prefetch=3,
            grid=(n_tiles,),
            in_specs=[pl.BlockSpec(memory_space=pl.ANY),
                      pl.BlockSpec((None, 1, d), lo), pl.BlockSpec((None, 1, d), hi)] + wspecs,
            out_specs=pl.BlockSpec((tg, ROW_SLABS, LANES), lambda i, te, tv, src: (i, 0, 0)),
            scratch_shapes=[pltpu.VMEM((2, tg, ROW_SLABS, LANES), F32), pltpu.SemaphoreType.DMA((2,))] + wcache,
        ),
        out_shape=jax.ShapeDtypeStruct((n_tiles * tg, ROW_SLABS, LANES), F32),
        compiler_params=_cparams(("arbitrary",)),
        name="moe_experts",
    )(tile_expert, tile_valid, src, rows, router_w_t, router_w_t, wg, wu, wd, wg, wu, wd)


def _combine_kernel(dest_ref, hf_ref, x_ref, m_ref, wg_ref, wu_ref, wd_ref, y_hbm, o_ref, ybuf_ref, sem_ref):
    i = pl.program_id(0)
    n = pl.num_programs(0)
    tm = hf_ref.shape[0]
    slot = i % 2

    def fetch(tile, s):
        base = tile * tm
        _for_rows(tm, lambda t, u: pltpu.async_copy(
            y_hbm.at[dest_ref[base + t]], ybuf_ref.at[s, t], sem_ref.at[s], priority=u % 2))

    @pl.when(i == 0)
    def _():
        fetch(0, 0)

    @pl.when(i + 1 < n)
    def _():
        fetch(i + 1, 1 - slot)

    h = hf_ref[...]
    a = _dot(h, wg_ref[...])
    u = _dot(h, wu_ref[...])
    shared = _dot((a * _sigmoid(a) * u).astype(BF16), wd_ref[...])
    pltpu.make_async_copy(y_hbm.at[pl.ds(0, tm)], ybuf_ref.at[slot], sem_ref.at[slot]).wait()
    o_ref[...] = x_ref[...] + m_ref[G_F:G_F + 1, :] * (shared + _slabs_to_rows(ybuf_ref[slot]))


def combine(dest, hf, xs, mods, wg, wu, wd, y_sorted, *, tm, n_ctx_tiles):
    b, r, d = hf.shape
    nt = r // tm
    ff = wg.shape[1]
    row = lambda i, dest: (i // nt, i % nt, 0)
    const = lambda i, dest: (0, 0)
    return pl.pallas_call(
        _combine_kernel,
        grid_spec=pltpu.PrefetchScalarGridSpec(
            num_scalar_prefetch=1,
            grid=(b * nt,),
            in_specs=[
                pl.BlockSpec((None, tm, d), row),
                pl.BlockSpec((None, tm, d), row),
                pl.BlockSpec((None, None, 6, d),
                             lambda i, dest: (i // nt, ((i % nt) >= n_ctx_tiles).astype(jnp.int32), 0, 0)),
                pl.BlockSpec((d, ff), const), pl.BlockSpec((d, ff), const), pl.BlockSpec((ff, d), const),
                pl.BlockSpec(memory_space=pl.ANY),
            ],
            out_specs=pl.BlockSpec((None, tm, d), row),
            scratch_shapes=[pltpu.VMEM((2, tm, ROW_SLABS, LANES), F32), pltpu.SemaphoreType.DMA((2,))],
        ),
        out_shape=jax.ShapeDtypeStruct((b, r, d), F32),
        compiler_params=_cparams(("arbitrary",)),
        name="moe_combine",
    )(dest, hf, xs, mods, wg, wu, wd, y_sorted)


def moe_block(xs, mods, gain, router_w, router_bias, ew_gate, ew_up, ew_down, layer, sw_gate, sw_up, sw_down,
              *, tm, n_ctx_tiles):
    b, r, d = xs.shape
    n = b * r
    tg = EXPERT_TILE
    hf, rows, info, counts = route(xs, mods, gain, router_w, router_bias, tm=tm, n_ctx_tiles=n_ctx_tiles)
    cls = info[:, :, 0, :].reshape(n)
    rank = info[:, :, 1, :].reshape(n)
    padded = ((counts[:, 0].astype(jnp.int32) + tg - 1) // tg) * tg
    ends = jnp.cumsum(padded)
    dest = (ends - padded)[cls] + rank
    n_tiles = n // tg + N_CLASSES
    tile_start = jnp.arange(n_tiles, dtype=jnp.int32) * tg
    tile_valid = tile_start < ends[-1]
    last_start = jnp.maximum(ends[-1] - tg, 0)
    start = jnp.where(tile_valid, tile_start, last_start)
    tile_cls = jnp.sum((ends[None, :] <= start[:, None]).astype(jnp.int32), axis=1)
    tile_cls = jnp.minimum(tile_cls, N_CLASSES - 1)
    tile_expert = jnp.concatenate([jnp.asarray(CLASS_LO)[tile_cls], jnp.asarray(CLASS_HI)[tile_cls]])
    src = invert_permutation(dest, n_tiles * tg)
    y_sorted = experts(tile_expert, tile_valid.astype(jnp.int32), src, rows,
                       router_w.T.astype(F32).reshape(N_EXPERTS, 1, d), ew_gate, ew_up, ew_down, layer)
    return combine(dest, hf, xs, mods, sw_gate.astype(BF16), sw_up.astype(BF16), sw_down.astype(BF16), y_sorted,
                   tm=tm, n_ctx_tiles=n_ctx_tiles)


def kernel(x, c, ctx, c_ctx, mod_w, mod_b, norm_mix, norm_ffn, even_w_in, even_w_out, diff_q_gain, diff_k_gain, diff_lambda, diff_subln, hgrn_lb_logits, hgrn_out_gain, odd_w_in, mla_q_a_gain, mla_kv_a_gain, mla_w_uq, mla_w_ukv, mla_q_nope_gain, mla_q_rope_gain, mla_k_nope_gain, mla_k_rope_gain, odd_w_out, router_w, router_bias, expert_w_gate, expert_w_up, expert_w_down, shared_w_gate, shared_w_up, shared_w_down):
    b, n_lat, d = x.shape
    n_ctx = ctx.shape[1]
    depth = mod_w.shape[0]
    tm = 256 if n_ctx % 256 == 0 else 128
    n_ctx_tiles = n_ctx // tm
    rope = rope_lane_tables(n_ctx, n_lat)

    mod_rows = 16
    c_rows = jnp.concatenate([c, c_ctx[None, :], jnp.zeros((mod_rows - b - 1, d), F32)], axis=0)
    mod_all = modulation(c_rows, mod_w, mod_b).reshape(depth, mod_rows, 6, d)
    lb_all = jnp.cumsum(jax.nn.softmax(hgrn_lb_logits.astype(F32), axis=0), axis=0)

    xs = jnp.concatenate([ctx, x], axis=1)
    for layer in range(depth):
        last = layer == depth - 1
        j = layer // 2
        mods = jnp.stack([jnp.broadcast_to(mod_all[layer, b], (b, 6, d)), mod_all[layer, :b]], axis=1)
        row_tile0 = n_ctx_tiles if last else 0
        if layer % 2 == 0:
            lam_init = 0.8 - 0.6 * math.exp(-0.3 * layer)
            p = norm_matmul(xs, norm_mix[layer], even_w_in[j].astype(BF16), tm=tm, mods=mods, n_ctx_tiles=n_ctx_tiles)
            oa = diff_attention(p, diff_lambda[j], diff_q_gain[j], diff_k_gain[j], diff_subln[j], rope,
                                tq=tm, n_ctx=n_ctx, lam_init=lam_init, q_blk0=0, k_blk0=4, v_blk0=8)
            ob = hgrn_bidirectional(p, lb_all[j, 0], lb_all[j, 1], hgrn_out_gain[j], n_ctx=n_ctx, blk0=12)
            if last:
                oa, ob = oa[:, n_ctx:], ob[:, n_ctx:]
            w_out = even_w_out[j].astype(BF16)
            half = oa.shape[-1]
            xs_mid = proj_residual([oa, ob], [w_out[:half], w_out[half:]], xs, mods, tm=tm,
                                   n_ctx_tiles=n_ctx_tiles, row_tile0=row_tile0)
        else:
            w_in = odd_w_in[j]
            zpad = jnp.zeros((d, LANES - MLA_ROPE), F32)
            w_in_r = jnp.concatenate([w_in[:, MLA_Q_LORA:MLA_Q_LORA + MLA_KV_LORA], w_in[:, MLA_Q_LORA + MLA_KV_LORA:],
                                      zpad, w_in[:, :MLA_Q_LORA]], axis=1).astype(BF16)
            p1 = norm_matmul(xs, norm_mix[layer], w_in_r, tm=tm, mods=mods, n_ctx_tiles=n_ctx_tiles)
            w_uq = mla_w_uq[j].reshape(MLA_Q_LORA, MLA_HEADS, MLA_NOPE + MLA_ROPE)
            w_uq = jnp.concatenate([w_uq, jnp.zeros((MLA_Q_LORA, MLA_HEADS, LANES - MLA_ROPE), F32)], axis=-1)
            w_uq = w_uq.reshape(MLA_Q_LORA, MLA_HEADS * 2 * LANES).astype(BF16)
            q = norm_matmul(p1, mla_q_a_gain[j], w_uq, tm=tm, col_block=1, row_tile0=row_tile0)
            kv = norm_matmul(p1, mla_kv_a_gain[j], mla_w_ukv[j].astype(BF16), tm=tm, col_block=0)
            if not last:
                raise NotImplementedError("context queries for a non-final latent-attention layer")
            o = mla_attention(q, kv, p1, 2, rope, mla_q_nope_gain[j], mla_q_rope_gain[j], mla_k_nope_gain[j],
                              mla_k_rope_gain[j], tq=2 * tm, n_ctx=n_ctx)
            xs_mid = proj_residual([o], [odd_w_out[j].astype(BF16)], xs, mods, tm=tm,
                                   n_ctx_tiles=n_ctx_tiles, row_tile0=row_tile0)
        xs = moe_block(xs_mid, mods, norm_ffn[layer], router_w, router_bias,
                       expert_w_gate, expert_w_up, expert_w_down, layer,
                       shared_w_gate[layer], shared_w_up[layer], shared_w_down[layer],
                       tm=tm, n_ctx_tiles=0 if last else n_ctx_tiles)
    return xs if xs.shape[1] == n_lat else xs[:, n_ctx:]
```

```python
import functools
import math

import numpy as np
import jax
import jax.numpy as jnp
from jax import lax
from jax.experimental import pallas as pl
from jax.experimental.pallas import tpu as pltpu

F32 = jnp.float32
BF16 = jnp.bfloat16

LANES = 128
VMEM_LIMIT = 56 * 1024 * 1024

GRID_W = 64
DIFF_HEADS = 4
DIFF_HEAD_DIM = 64
HGRN_HEADS = 4
HGRN_K_DIM = 128
MLA_HEADS = 8
MLA_NOPE = 128
MLA_ROPE = 64
MLA_V = 128
MLA_Q_LORA = 384
MLA_KV_LORA = 256
N_EXPERTS = 16
N_GROUPS = 4
EXPERTS_PER_GROUP = 4
ROPE_BASE = 10000.0
EPS = 1e-6
LOG2E = 1.4426950408889634
GLA_CHUNK = 64
ATTN_KEY_CHUNK = 1152
MLA_HEADS_PER_STEP = 2
DIFF_HEADS_PER_STEP = 2
HGRN_HEADS_PER_STEP = 2

SH_M, SC_M, G_M, SH_F, SC_F, G_F = range(6)


def _sigmoid(x):
    return 1.0 / (1.0 + jnp.exp(-x))


def _dot(a, b):
    return jnp.dot(a, b, preferred_element_type=F32)


def _dot_nt(a, b):
    return lax.dot_general(a, b, (((1,), (1,)), ((), ())), preferred_element_type=F32)


def _dot_tn(a, b):
    return lax.dot_general(a, b, (((0,), (0,)), ((), ())), preferred_element_type=F32)


def _split3(x):
    hi = x.astype(BF16)
    r = x - hi.astype(F32)
    mid = r.astype(BF16)
    lo = (r - mid.astype(F32)).astype(BF16)
    return hi, mid, lo


def _rms(x, width=None):
    n = x.shape[-1] if width is None else width
    return x * lax.rsqrt(jnp.sum(x * x, axis=-1, keepdims=True) * (1.0 / n) + EPS)


def _cparams(sem):
    return pltpu.CompilerParams(dimension_semantics=sem, vmem_limit_bytes=VMEM_LIMIT)


def _mod_kernel(c_ref, w_ref, b_ref, o_ref):
    c = c_ref[...]
    s = c * _sigmoid(c)
    o_ref[...] = _dot(s.astype(BF16), w_ref[...].astype(BF16)) + b_ref[...]


def modulation(c_rows, mod_w, mod_b):
    n_layers, d, n = mod_w.shape
    rows = c_rows.shape[0]
    tn = 1536
    return pl.pallas_call(
        _mod_kernel,
        grid=(n_layers, n // tn),
        in_specs=[
            pl.BlockSpec((rows, d), lambda l, j: (0, 0)),
            pl.BlockSpec((None, d, tn), lambda l, j: (l, 0, j)),
            pl.BlockSpec((None, 1, tn), lambda l, j: (l, 0, j)),
        ],
        out_specs=pl.BlockSpec((None, rows, tn), lambda l, j: (l, 0, j)),
        out_shape=jax.ShapeDtypeStruct((n_layers, rows, n), F32),
        compiler_params=_cparams(("arbitrary", "arbitrary")),
        name="modulation",
    )(c_rows, mod_w, mod_b.reshape(n_layers, 1, n))


def _norm_matmul_kernel(*refs, has_mod):
    if has_mod:
        x_ref, m_ref, g_ref, w_ref, o_ref = refs
    else:
        x_ref, g_ref, w_ref, o_ref = refs
    h = _rms(x_ref[...].astype(F32)) * g_ref[...]
    if has_mod:
        h = h * (1.0 + m_ref[SC_M:SC_M + 1, :]) + m_ref[SH_M:SH_M + 1, :]
    o_ref[...] = _dot(h.astype(BF16), w_ref[...]).astype(o_ref.dtype)


def norm_matmul(x, gain, w, *, tm, col_block=0, row_tile0=0, n_row_tiles=None, mods=None, n_ctx_tiles=0):
    b, t, _ = x.shape
    k, n = w.shape
    if n_row_tiles is None:
        n_row_tiles = t // tm - row_tile0
    in_specs = [pl.BlockSpec((None, tm, k), lambda bi, ti: (bi, ti + row_tile0, col_block))]
    args = [x]
    if mods is not None:
        in_specs.append(pl.BlockSpec((None, None, 6, k),
                                     lambda bi, ti: (bi, ((ti + row_tile0) >= n_ctx_tiles).astype(jnp.int32), 0, 0)))
        args.append(mods)
    in_specs += [pl.BlockSpec((1, k), lambda bi, ti: (0, 0)), pl.BlockSpec((k, n), lambda bi, ti: (0, 0))]
    args += [gain.reshape(1, k).astype(F32), w]
    return pl.pallas_call(
        functools.partial(_norm_matmul_kernel, has_mod=mods is not None),
        grid=(b, n_row_tiles),
        in_specs=in_specs,
        out_specs=pl.BlockSpec((None, tm, n), lambda bi, ti: (bi, ti, 0)),
        out_shape=jax.ShapeDtypeStruct((b, n_row_tiles * tm, n), BF16),
        compiler_params=_cparams(("arbitrary", "arbitrary")),
        name="norm_matmul",
    )(*args)


def rope_lane_tables(n_ctx, n_lat):
    rows = n_lat // GRID_W
    row = np.repeat(np.arange(rows), GRID_W).astype(np.float32)
    col = np.tile(np.arange(GRID_W), rows).astype(np.float32)
    axis_dim = DIFF_HEAD_DIM // 2
    inv_freq = jnp.asarray(ROPE_BASE, F32) ** (-jnp.arange(0, axis_dim, 2, dtype=F32) / axis_dim)
    ang_r = jnp.asarray(row)[:, None] * inv_freq
    ang_c = jnp.asarray(col)[:, None] * inv_freq
    lane = np.arange(LANES)
    freq_idx = lane % 16
    use_col = (lane % 64) >= 32
    first = (lane % 32) < 16
    ang = jnp.where(use_col[None, :], ang_c[:, freq_idx], ang_r[:, freq_idx])
    cos, sin = jnp.cos(ang), jnp.sin(ang)
    c = jnp.concatenate([jnp.ones((n_ctx, LANES), F32), cos], axis=0)
    sa = jnp.concatenate([jnp.zeros((n_ctx, LANES), F32), jnp.where(first[None, :], -sin, 0.0)], axis=0)
    sb = jnp.concatenate([jnp.zeros((n_ctx, LANES), F32), jnp.where(first[None, :], 0.0, sin)], axis=0)
    return c, sa, sb


def _rope(x, c, sa, sb):
    return x * c + pltpu.roll(x, LANES - 16, 1) * sa + pltpu.roll(x, 16, 1) * sb


def _block_ones(width):
    r = lax.broadcasted_iota(jnp.int32, (LANES, LANES), 0) // width
    c = lax.broadcasted_iota(jnp.int32, (LANES, LANES), 1) // width
    return (r == c).astype(BF16)


def _block_rms(x, ones, width):
    xx = x * x
    hi = xx.astype(BF16)
    lo = (xx - hi.astype(F32)).astype(BF16)
    ms = (_dot(hi, ones) + _dot(lo, ones)) * (1.0 / width)
    return x * lax.rsqrt(ms + EPS)


def _attend_streams(qs, kt_refs, vs, nk):
    n = len(qs)
    chunk = ATTN_KEY_CHUNK if nk % ATTN_KEY_CHUNK == 0 else nk
    nc = nk // chunk
    s = [[None] * nc for _ in range(n)]
    p = [[None] * nc for _ in range(n)]
    m, l, o = [None] * n, [None] * n, [None] * n

    def logits(i, c):
        s[i][c] = _dot(qs[i], kt_refs[i][:, c * chunk:(c + 1) * chunk])
        mc = jnp.max(s[i][c], axis=-1, keepdims=True)
        m[i] = mc if c == 0 else jnp.maximum(m[i], mc)

    def exps(i, c):
        e = jnp.exp2(s[i][c] - m[i])
        lc = jnp.sum(e, axis=-1, keepdims=True)
        l[i] = lc if c == 0 else l[i] + lc
        p[i][c] = e.astype(BF16)

    def values(i, c):
        ref, col0 = vs[i]
        oc = _dot(p[i][c], ref[c * chunk:(c + 1) * chunk, col0:col0 + LANES])
        o[i] = oc if c == 0 else o[i] + oc

    for step in range(n + 2):
        for c in range(nc):
            if step < n:
                logits(step, c)
            if 0 <= step - 1 < n:
                exps(step - 1, c)
            if 0 <= step - 2 < n:
                values(step - 2, c)
    return list(zip(o, l))


def _diff_attn_kernel(lam_ref, q_ref, k_ref, v_ref, cq_ref, saq_ref, sbq_ref, ck_ref, sak_ref, sbk_ref,
                      qg_ref, kg_ref, sub_ref, o_ref, kt_ref, *, n_ctx, n_ctx_tiles, lam_init):
    qi = pl.program_id(2)
    ones = _block_ones(DIFF_HEAD_DIM)
    heads = range(DIFF_HEADS_PER_STEP)

    @pl.when(qi == 0)
    def _():
        for h in heads:
            k = _block_rms(k_ref[:, h * LANES:(h + 1) * LANES].astype(F32), ones, DIFF_HEAD_DIM) * kg_ref[...]
            kt_ref[h] = _rope(k, ck_ref[...], sak_ref[...], sbk_ref[...]).T.astype(BF16)

    qs = []
    for h in heads:
        q = _block_rms(q_ref[:, h * LANES:(h + 1) * LANES].astype(F32), ones, DIFF_HEAD_DIM) * qg_ref[...]
        q = _rope(q, cq_ref[...], saq_ref[...], sbq_ref[...]) * (DIFF_HEAD_DIM ** -0.5 * LOG2E)
        lane = lax.broadcasted_iota(jnp.int32, q.shape, 1)
        qs.append(jnp.where(lane < DIFF_HEAD_DIM, q, 0.0).astype(BF16))
        qs.append(jnp.where(lane >= DIFF_HEAD_DIM, q, 0.0).astype(BF16))
    lv = lam_ref[...]
    lam = (jnp.exp(jnp.sum(lv[0:1] * lv[1:2], axis=-1, keepdims=True))
           - jnp.exp(jnp.sum(lv[2:3] * lv[3:4], axis=-1, keepdims=True)) + lam_init)

    def attend(nk):
        outs = _attend_streams(qs, [kt_ref.at[h] for h in heads for _ in range(2)],
                               [(v_ref, h * LANES) for h in heads for _ in range(2)], nk)
        for h in heads:
            (o1, l1), (o2, l2) = outs[2 * h], outs[2 * h + 1]
            o = o1 * (1.0 / l1) - o2 * (lam / l2)
            o_ref[:, h * LANES:(h + 1) * LANES] = (_rms(o) * sub_ref[...] * (1.0 - lam_init)).astype(o_ref.dtype)

    @pl.when(qi < n_ctx_tiles)
    def _():
        attend(n_ctx)

    @pl.when(qi >= n_ctx_tiles)
    def _():
        attend(k_ref.shape[0])


def diff_attention(p, lam_vecs, q_gain, k_gain, subln, rope, *, tq, n_ctx, lam_init, q_blk0, k_blk0, v_blk0):
    b, t, _ = p.shape
    c, sa, sb = rope
    nq = t // tq
    hps = DIFF_HEADS_PER_STEP
    assert q_blk0 % hps == 0 and k_blk0 % hps == 0 and v_blk0 % hps == 0
    row_q = lambda bi, h, qi: (qi, 0)
    full = lambda bi, h, qi: (0, 0)
    tile2 = lambda g: jnp.tile(g.astype(F32), 2).reshape(1, LANES)
    return pl.pallas_call(
        functools.partial(_diff_attn_kernel, n_ctx=n_ctx, n_ctx_tiles=n_ctx // tq, lam_init=lam_init),
        grid=(b, DIFF_HEADS // hps, nq),
        in_specs=[
            pl.BlockSpec((4, DIFF_HEAD_DIM), full),
            pl.BlockSpec((None, tq, hps * LANES), lambda bi, h, qi: (bi, qi, q_blk0 // hps + h)),
            pl.BlockSpec((None, t, hps * LANES), lambda bi, h, qi: (bi, 0, k_blk0 // hps + h)),
            pl.BlockSpec((None, t, hps * LANES), lambda bi, h, qi: (bi, 0, v_blk0 // hps + h)),
            pl.BlockSpec((tq, LANES), row_q), pl.BlockSpec((tq, LANES), row_q), pl.BlockSpec((tq, LANES), row_q),
            pl.BlockSpec((t, LANES), full), pl.BlockSpec((t, LANES), full), pl.BlockSpec((t, LANES), full),
            pl.BlockSpec((1, LANES), full), pl.BlockSpec((1, LANES), full), pl.BlockSpec((1, LANES), full),
        ],
        out_specs=pl.BlockSpec((None, tq, hps * LANES), lambda bi, h, qi: (bi, qi, h)),
        out_shape=jax.ShapeDtypeStruct((b, t, DIFF_HEADS * LANES), BF16),
        scratch_shapes=[pltpu.VMEM((hps, LANES, t), BF16)],
        compiler_params=_cparams(("arbitrary", "arbitrary", "arbitrary")),
        name="diff_attention",
    )(lam_vecs.astype(F32), p, p, p, c, sa, sb, c, sa, sb, tile2(q_gain), tile2(k_gain),
      subln.astype(F32).reshape(1, LANES))


def _gla_constants(c):
    levels = int(math.log2(c))
    t = np.arange(c)[:, None]
    u = np.arange(c)[None, :]
    stack = [[u <= t], [u >= t]]
    qside = [[], []]
    pair = [[t == u], [t == u]]
    for lv in range(1, levels + 1):
        base = (t >> lv) << lv
        half = 1 << (lv - 1)
        stack[0].append(u <= base + half - 1)
        stack[1].append(u >= base + half)
        up_t = ((t >> (lv - 1)) & 1) == 1
        up_u = ((u >> (lv - 1)) & 1) == 1
        same = (t >> lv) == (u >> lv)
        qside[0].append(np.broadcast_to(up_t, (c, LANES)))
        qside[1].append(np.broadcast_to(~up_t, (c, LANES)))
        pair[0].append(same & up_t & ~up_u)
        pair[1].append(same & ~up_t & up_u)
    f32 = lambda x: np.asarray(x, np.float32)
    stack = np.stack([np.tile(np.concatenate(f32(m), axis=0), (1, 3)) for m in stack])
    return (jnp.asarray(stack, BF16), jnp.asarray(np.stack([f32(m) for m in qside])),
            jnp.asarray(np.stack([f32(m) for m in pair])))


def _gla_chunk(q, k, v, g2, s, stack, qside_ref, pair_ref, d):
    c = q.shape[0]
    levels = int(math.log2(c))
    gcat = jnp.concatenate(_split3(g2), axis=0)
    cs = _dot(stack, gcat)
    tot_col = _dot_tn(gcat, jnp.ones((3 * c, LANES), BF16))
    diag = jnp.sum(q * k, axis=-1, keepdims=True)
    yield
    cum = cs[0:c]
    tot = cum[0:1] if d == 1 else cum[c - 1:c]
    zz = []
    for lv in range(1, levels + 1):
        e = jnp.exp2(-jnp.abs(cum - cs[lv * c:(lv + 1) * c]))
        z = (jnp.where(qside_ref[d, lv - 1] > 0.5, q, k) * e).astype(BF16)
        zz.append(_dot_nt(z, z))
    q_in = (q * jnp.exp2(cum)).astype(BF16)
    ks = (k * jnp.exp2(tot - cum)).astype(BF16)
    ds = _dot_tn(ks, v)
    yield
    a = pair_ref[d, 0] * diag
    for lv in range(1, levels + 1):
        a = a + pair_ref[d, lv] * zz[lv - 1]
    lhs = jnp.concatenate([q_in, a.astype(BF16)], axis=1)
    o = _dot(lhs, jnp.concatenate([s.astype(BF16), v], axis=0))
    s_new = s * jnp.exp2(tot_col) + ds
    yield
    return o, s_new


def _run_interleaved(gens):
    results = [None] * len(gens)
    live = list(range(len(gens)))
    while live:
        for i in list(live):
            try:
                next(gens[i])
            except StopIteration as stop:
                results[i] = stop.value
                live.remove(i)
    return results


def _hgrn_kernel(qz_ref, zf_ref, zb_ref, v_ref, gz_ref, lbf_ref, lbb_ref, og_ref, stack_ref, qside_ref, pair_ref,
                 o_ref, q_ref, kf_ref, gf_ref, kb_ref, gb_ref, of_ref, ob_ref, st_ref, *, n_ctx_chunks):
    c = GLA_CHUNK
    t = qz_ref.shape[0]
    n = t // c
    heads = range(HGRN_HEADS_PER_STEP)
    qz = qz_ref[...].astype(F32)
    q_ref[...] = qz * _sigmoid(qz) * (HGRN_K_DIM ** -0.5)
    for z_ref, lb_ref, k_ref, g_ref in ((zf_ref, lbf_ref, kf_ref, gf_ref), (zb_ref, lbb_ref, kb_ref, gb_ref)):
        lb = jnp.concatenate([lb_ref[h] for h in heads], axis=-1)
        f = lb + (1.0 - lb) * _sigmoid(z_ref[...].astype(F32))
        k_ref[...] = 1.0 - f
        g_ref[...] = jnp.log(f) * LOG2E
    st_ref[...] = jnp.zeros_like(st_ref)

    def body(i, carry):
        rf = pl.multiple_of(i * c, c)
        cb = jnp.where(i < n_ctx_chunks, n_ctx_chunks - 1 - i, n - 1 - i + n_ctx_chunks)
        rb = pl.multiple_of(cb * c, c)
        chains = [(h, d, pl.ds(r0, c), k_ref, g_ref, out_ref) for h in heads
                  for d, r0, k_ref, g_ref, out_ref in ((0, rf, kf_ref, gf_ref, of_ref), (1, rb, kb_ref, gb_ref, ob_ref))]
        col = lambda h: slice(h * LANES, (h + 1) * LANES)
        outs = _run_interleaved([
            _gla_chunk(q_ref[rows, col(h)], k_ref[rows, col(h)], v_ref[rows, col(h)], g_ref[rows, col(h)],
                       st_ref[h, d], stack_ref[d], qside_ref, pair_ref, d)
            for (h, d, rows, k_ref, g_ref, _) in chains])
        for (h, d, rows, _, _, out_ref), (o, s_new) in zip(chains, outs):
            out_ref[rows, col(h)] = o
            st_ref[h, d] = s_new
        return carry

    lax.fori_loop(0, n, body, 0)
    gz = gz_ref[...].astype(F32)
    for h in heads:
        cols = slice(h * LANES, (h + 1) * LANES)
        o = _rms(of_ref[:, cols] + ob_ref[:, cols]) * og_ref[...]
        o_ref[:, cols] = (o * (gz[:, cols] * _sigmoid(gz[:, cols]))).astype(o_ref.dtype)


def hgrn_bidirectional(p, lb_fwd, lb_bwd, out_gain, *, n_ctx, blk0):
    b, t, _ = p.shape
    c = GLA_CHUNK
    consts = _gla_constants(c)
    h = HGRN_HEADS
    hps = HGRN_HEADS_PER_STEP
    assert blk0 % hps == 0 and h % hps == 0
    seg = lambda s: pl.BlockSpec((None, t, hps * LANES), lambda bi, hi: (bi, 0, (blk0 + s * h) // hps + hi))
    per_head = pl.BlockSpec((hps, 1, LANES), lambda bi, hi: (hi, 0, 0))
    const = lambda a: pl.BlockSpec(a.shape, lambda bi, hi: (0,) * a.ndim)
    seq = pltpu.VMEM((t, hps * LANES), F32)
    return pl.pallas_call(
        functools.partial(_hgrn_kernel, n_ctx_chunks=n_ctx // c),
        grid=(b, h // hps),
        in_specs=[seg(0), seg(1), seg(2), seg(3), seg(4), per_head, per_head,
                  pl.BlockSpec((1, LANES), lambda bi, hi: (0, 0))] + [const(a) for a in consts],
        out_specs=pl.BlockSpec((None, t, hps * LANES), lambda bi, hi: (bi, 0, hi)),
        out_shape=jax.ShapeDtypeStruct((b, t, h * LANES), BF16),
        scratch_shapes=[seq, seq, seq, seq, seq, seq, seq, pltpu.VMEM((hps, 2, LANES, LANES), F32)],
        compiler_params=_cparams(("arbitrary", "arbitrary")),
        name="hgrn_bidirectional",
    )(p, p, p, p, p, lb_fwd.reshape(h, 1, LANES), lb_bwd.reshape(h, 1, LANES),
      out_gain.astype(F32).reshape(1, LANES), *consts)


def _mla_attn_kernel(q_ref, kv_ref, kr_ref, cq_ref, saq_ref, sbq_ref, ck_ref, sak_ref, sbk_ref,
                     qn_ref, qr_ref, kn_ref, krg_ref, o_ref, kt_ref):
    qi = pl.program_id(2)
    ones_nope = _block_ones(MLA_NOPE)
    ones_rope = _block_ones(MLA_ROPE)
    hw = 2 * LANES

    @pl.when(qi == 0)
    def _():
        kr = _block_rms(kr_ref[...].astype(F32), ones_rope, MLA_ROPE) * krg_ref[...]
        kr = _rope(kr, ck_ref[...], sak_ref[...], sbk_ref[...]).T.astype(BF16)
        for h in range(MLA_HEADS_PER_STEP):
            kn = _block_rms(kv_ref[:, h * hw:h * hw + MLA_NOPE].astype(F32), ones_nope, MLA_NOPE) * kn_ref[...]
            kt_ref[h, 0:LANES, :] = kn.T.astype(BF16)
            kt_ref[h, LANES:hw, :] = kr

    scale = (MLA_NOPE + MLA_ROPE) ** -0.5 * LOG2E
    qs = []
    for h in range(MLA_HEADS_PER_STEP):
        qn = _block_rms(q_ref[:, h * hw:h * hw + MLA_NOPE].astype(F32), ones_nope, MLA_NOPE) * (qn_ref[...] * scale)
        qr = _block_rms(q_ref[:, h * hw + MLA_NOPE:(h + 1) * hw].astype(F32), ones_rope, MLA_ROPE) * qr_ref[...]
        qr = _rope(qr, cq_ref[...], saq_ref[...], sbq_ref[...]) * scale
        qs.append(jnp.concatenate([qn.astype(BF16), qr.astype(BF16)], axis=-1))
    heads = range(MLA_HEADS_PER_STEP)
    outs = _attend_streams(qs, [kt_ref.at[h] for h in heads], [(kv_ref, h * hw + MLA_NOPE) for h in heads],
                           kt_ref.shape[2])
    for h, (o, l) in enumerate(outs):
        o_ref[:, h * MLA_V:(h + 1) * MLA_V] = (o * (1.0 / l)).astype(o_ref.dtype)


def mla_attention(q, kv, p1, kr_blk, rope, qn_gain, qr_gain, kn_gain, kr_gain, *, tq, n_ctx):
    b, n_lat, _ = q.shape
    t = kv.shape[1]
    c, sa, sb = rope
    hps = MLA_HEADS_PER_STEP
    row_q = lambda bi, h, qi: (qi, 0)
    full = lambda bi, h, qi: (0, 0)
    pad = lambda g: jnp.concatenate([g.astype(F32), jnp.zeros((LANES - g.shape[0],), F32)]).reshape(1, LANES)
    return pl.pallas_call(
        _mla_attn_kernel,
        grid=(b, MLA_HEADS // hps, n_lat // tq),
        in_specs=[
            pl.BlockSpec((None, tq, hps * 2 * LANES), lambda bi, h, qi: (bi, qi, h)),
            pl.BlockSpec((None, t, hps * 2 * LANES), lambda bi, h, qi: (bi, 0, h)),
            pl.BlockSpec((None, t, LANES), lambda bi, h, qi: (bi, 0, kr_blk)),
            pl.BlockSpec((tq, LANES), row_q), pl.BlockSpec((tq, LANES), row_q), pl.BlockSpec((tq, LANES), row_q),
            pl.BlockSpec((t, LANES), full), pl.BlockSpec((t, LANES), full), pl.BlockSpec((t, LANES), full),
            pl.BlockSpec((1, LANES), full), pl.BlockSpec((1, LANES), full),
            pl.BlockSpec((1, LANES), full), pl.BlockSpec((1, LANES), full),
        ],
        out_specs=pl.BlockSpec((None, tq, hps * MLA_V), lambda bi, h, qi: (bi, qi, h)),
        out_shape=jax.ShapeDtypeStruct((b, n_lat, MLA_HEADS * MLA_V), BF16),
        scratch_shapes=[pltpu.VMEM((hps, 2 * LANES, t), BF16)],
        compiler_params=_cparams(("arbitrary", "arbitrary", "arbitrary")),
        name="mla_attention",
    )(q, kv, p1, c[n_ctx:], sa[n_ctx:], sb[n_ctx:], c, sa, sb, pad(qn_gain), pad(qr_gain), pad(kn_gain), pad(kr_gain))


def _proj_residual_kernel(*refs, n_in):
    a_refs = refs[:n_in]
    w_refs = refs[n_in:2 * n_in]
    x_ref, m_ref, o_ref = refs[2 * n_in:]
    acc = _dot(a_refs[0][...], w_refs[0][...])
    for a_ref, w_ref in zip(a_refs[1:], w_refs[1:]):
        acc = acc + _dot(a_ref[...], w_ref[...])
    o_ref[...] = x_ref[...] + m_ref[G_M:G_M + 1, :] * acc


def proj_residual(acts, weights, xs, mods, *, tm, n_ctx_tiles, row_tile0=0):
    b, r, _ = acts[0].shape
    d = xs.shape[-1]
    in_specs = [pl.BlockSpec((None, tm, a.shape[-1]), lambda bi, ti: (bi, ti, 0)) for a in acts]
    in_specs += [pl.BlockSpec(w.shape, lambda bi, ti: (0, 0)) for w in weights]
    in_specs += [
        pl.BlockSpec((None, tm, d), lambda bi, ti: (bi, ti + row_tile0, 0)),
        pl.BlockSpec((None, None, 6, d), lambda bi, ti: (bi, ((ti + row_tile0) >= n_ctx_tiles).astype(jnp.int32), 0, 0)),
    ]
    return pl.pallas_call(
        functools.partial(_proj_residual_kernel, n_in=len(acts)),
        grid=(b, r // tm),
        in_specs=in_specs,
        out_specs=pl.BlockSpec((None, tm, d), lambda bi, ti: (bi, ti, 0)),
        out_shape=jax.ShapeDtypeStruct((b, r, d), F32),
        compiler_params=_cparams(("arbitrary", "arbitrary")),
        name="proj_residual",
    )(*acts, *weights, xs, mods)


PAIRS = [(i, j) for i in range(EXPERTS_PER_GROUP) for j in range(i + 1, EXPERTS_PER_GROUP)]
N_CLASSES = N_GROUPS * len(PAIRS)
CLASS_LO = np.array([EXPERTS_PER_GROUP * g + i for g in range(N_GROUPS) for (i, j) in PAIRS], np.int32)
CLASS_HI = np.array([EXPERTS_PER_GROUP * g + j for g in range(N_GROUPS) for (i, j) in PAIRS], np.int32)
EXPERT_TILE = 256
ROW_SLABS = 8
DMA_UNROLL = 8


def _route_kernel(x_ref, m_ref, g_ref, rw_ref, rb_ref, hf_ref, rows_ref, info_ref, count_ref, cnt_ref):
    h = _rms(x_ref[...]) * g_ref[...]
    h = h * (1.0 + m_ref[SC_F:SC_F + 1, :]) + m_ref[SH_F:SH_F + 1, :]
    hf_ref[...] = h.astype(hf_ref.dtype)
    rows_ref[...] = _rows_to_slabs(h)
    h1, h2, h3 = _split3(h)
    w1, w2, w3 = _split3(rw_ref[...])
    logits = (_dot_nt(w1, h1) + (_dot_nt(w1, h2) + _dot_nt(w2, h1))
              + (_dot_nt(w1, h3) + _dot_nt(w2, h2) + _dot_nt(w3, h1)))
    biased = _sigmoid(logits) + rb_ref[...]
    row = [biased[e:e + 1, :] for e in range(N_EXPERTS)]
    gscore = []
    for g in range(N_GROUPS):
        m = row[4 * g:4 * g + 4]
        gscore.append(functools.reduce(jnp.maximum, [m[i] + m[j] for (i, j) in PAIRS]))
    hits = []
    for g in range(N_GROUPS):
        best = None
        for g2 in range(N_GROUPS):
            if g2 == g:
                continue
            wins = (gscore[g] > gscore[g2]) if g2 < g else (gscore[g] >= gscore[g2])
            best = wins if best is None else jnp.logical_and(best, wins)
        chosen = []
        for i in range(EXPERTS_PER_GROUP):
            rank = None
            for j in range(EXPERTS_PER_GROUP):
                if j == i:
                    continue
                mi, mj = row[4 * g + i], row[4 * g + j]
                ahead = ((mj >= mi) if j < i else (mj > mi)).astype(jnp.int32)
                rank = ahead if rank is None else rank + ahead
            chosen.append(rank < 2)
        for (i, j) in PAIRS:
            hits.append(jnp.where(best & chosen[i] & chosen[j], 1.0, 0.0))
    onehot = jnp.concatenate(hits, axis=0)
    tm = onehot.shape[1]
    upper = (lax.broadcasted_iota(jnp.int32, (tm, tm), 0) <= lax.broadcasted_iota(jnp.int32, (tm, tm), 1))
    prefix = _dot(onehot.astype(BF16), upper.astype(BF16))

    @pl.when((pl.program_id(0) == 0) & (pl.program_id(1) == 0))
    def _():
        cnt_ref[...] = jnp.zeros_like(cnt_ref)

    seen = cnt_ref[...]
    cls_id = lax.broadcasted_iota(jnp.int32, onehot.shape, 0).astype(F32)
    cls = jnp.sum(onehot * cls_id, axis=0, keepdims=True)
    rank = jnp.sum(onehot * (seen[:, 0:1] + prefix - 1.0), axis=0, keepdims=True)
    info_ref[...] = jnp.concatenate([cls, rank, jnp.zeros((6, tm), F32)], axis=0).astype(jnp.int32)
    seen = seen + jnp.sum(onehot, axis=1, keepdims=True)
    cnt_ref[...] = seen
    count_ref[...] = seen


def route(xs, mods, gain, router_w, router_bias, *, tm, n_ctx_tiles):
    b, t, d = xs.shape
    nt = t // tm
    return pl.pallas_call(
        _route_kernel,
        grid=(b, nt),
        in_specs=[
            pl.BlockSpec((None, tm, d), lambda bi, ti: (bi, ti, 0)),
            pl.BlockSpec((None, None, 6, d), lambda bi, ti: (bi, (ti >= n_ctx_tiles).astype(jnp.int32), 0, 0)),
            pl.BlockSpec((1, d), lambda bi, ti: (0, 0)),
            pl.BlockSpec((N_EXPERTS, d), lambda bi, ti: (0, 0)),
            pl.BlockSpec((N_EXPERTS, 1), lambda bi, ti: (0, 0)),
        ],
        out_specs=[
            pl.BlockSpec((None, tm, d), lambda bi, ti: (bi, ti, 0)),
            pl.BlockSpec((tm, ROW_SLABS, LANES), lambda bi, ti: (bi * nt + ti, 0, 0)),
            pl.BlockSpec((None, None, 8, tm), lambda bi, ti: (bi, ti, 0, 0)),
            pl.BlockSpec((N_CLASSES, LANES), lambda bi, ti: (0, 0)),
        ],
        out_shape=[jax.ShapeDtypeStruct((b, t, d), BF16),
                   jax.ShapeDtypeStruct((b * t, ROW_SLABS, LANES), F32),
                   jax.ShapeDtypeStruct((b, nt, 8, tm), jnp.int32),
                   jax.ShapeDtypeStruct((N_CLASSES, LANES), F32)],
        scratch_shapes=[pltpu.VMEM((N_CLASSES, LANES), F32)],
        compiler_params=_cparams(("arbitrary", "arbitrary")),
        name="route",
    )(xs, mods, gain.reshape(1, d).astype(F32), router_w.T.astype(F32),
      router_bias.reshape(N_EXPERTS, 1).astype(F32))


def _for_rows(n, fn):
    def body(k, carry):
        for u in range(DMA_UNROLL):
            fn(k * DMA_UNROLL + u, u)
        return carry
    lax.fori_loop(0, n // DMA_UNROLL, body, 0)


def _transpose8(parts):
    sub = lax.broadcasted_iota(jnp.int32, parts[0].shape, 1)
    for s in (4, 2, 1):
        keep = (sub & s) == 0
        new = list(parts)
        for i in range(8):
            if i & s:
                continue
            a, b = parts[i], parts[i | s]
            new[i] = jnp.where(keep, a, pltpu.roll(b, s, 1))
            new[i | s] = jnp.where(keep, pltpu.roll(a, 8 - s, 1), b)
        parts = new
    return parts


def _slabs_to_rows(slabs):
    g = slabs.shape[0] // 8
    x4 = slabs.reshape(g, 8, ROW_SLABS, LANES)
    parts = _transpose8([x4[:, t] for t in range(8)])
    return jnp.concatenate([p.reshape(g * 8, LANES) for p in parts], axis=-1)


def _rows_to_slabs(x):
    g = x.shape[0] // 8
    parts = _transpose8([x[:, k * LANES:(k + 1) * LANES].reshape(g, 8, LANES) for k in range(ROW_SLABS)])
    return jnp.stack(parts, axis=1).reshape(g * 8, ROW_SLABS, LANES)


def _invert_kernel(dest_ref, src_ref):
    def clear(i, carry):
        for u in range(2 * DMA_UNROLL):
            src_ref[i * 2 * DMA_UNROLL + u] = 0
        return carry

    def put(i, carry):
        for u in range(2 * DMA_UNROLL):
            t = i * 2 * DMA_UNROLL + u
            src_ref[dest_ref[t]] = t
        return carry

    lax.fori_loop(0, src_ref.shape[0] // (2 * DMA_UNROLL), clear, 0)
    lax.fori_loop(0, dest_ref.shape[0] // (2 * DMA_UNROLL), put, 0)


def invert_permutation(dest, n_rows_out):
    smem = pl.BlockSpec(memory_space=pltpu.SMEM)
    return pl.pallas_call(
        _invert_kernel,
        in_specs=[smem],
        out_specs=smem,
        out_shape=jax.ShapeDtypeStruct((n_rows_out,), jnp.int32),
        name="moe_invert",
    )(dest)


def _expert_kernel(te_ref, tv_ref, nx_ref, src_ref, rows_hbm, rwl_ref, rwh_ref, wg_hbm, wu_hbm, wd_hbm, y_ref,
                   xbuf_ref, sem_ref, stage_g, stage_u, stage_d, cache_g, cache_u, cache_d, wsem_ref, *, layer):
    i = pl.program_id(0)
    n = pl.num_programs(0)
    tg = y_ref.shape[0]
    slot = i % 2
    valid = tv_ref[i] != 0
    prev = jnp.maximum(i - 1, 0)

    def fetch(tile, s):
        base = tile * tg
        for t in range(tg):
            pltpu.async_copy(rows_hbm.at[src_ref[base + t]], xbuf_ref.at[s, t], sem_ref.at[s], priority=1)

    def wait_rows(s):
        pltpu.make_async_copy(rows_hbm.at[pl.ds(0, tg)], xbuf_ref.at[s], sem_ref.at[s]).wait()

    def weight_copies(s, e):
        return [pltpu.make_async_copy(w_hbm.at[layer, e], stage.at[s], wsem_ref.at[s])
                for w_hbm, stage in ((wg_hbm, stage_g), (wu_hbm, stage_u), (wd_hbm, stage_d))]

    @pl.when(i == 0)
    def _():
        fetch(0, 0)
        for s in range(2):
            for copy in weight_copies(s, te_ref[s * n]):
                copy.start()

    for s in range(2):
        expert = te_ref[s * n + i]

        @pl.when(valid & ((i == 0) | (expert != te_ref[s * n + prev])))
        def _():
            for copy in weight_copies(s, expert):
                copy.wait()
            cache_g[s] = stage_g[s].astype(BF16)
            cache_u[s] = stage_u[s].astype(BF16)
            cache_d[s] = stage_d[s].astype(BF16)
            upcoming = nx_ref[s * n + i]

            @pl.when(upcoming >= 0)
            def _():
                for copy in weight_copies(s, upcoming):
                    copy.start()

    @pl.when(valid)
    def _():
        wait_rows(slot)
        x = _slabs_to_rows(xbuf_ref[slot])
        fetch(jnp.minimum(i + 1, n - 1), 1 - slot)
        xb = x.astype(BF16)
        s_lo = _sigmoid(jnp.sum(x * rwl_ref[...], axis=-1, keepdims=True))
        s_hi = _sigmoid(jnp.sum(x * rwh_ref[...], axis=-1, keepdims=True))
        inv = 1.0 / (s_lo + s_hi)

        def ffn(s):
            a = _dot(xb, cache_g[s])
            u = _dot(xb, cache_u[s])
            return _dot((a * _sigmoid(a) * u).astype(BF16), cache_d[s])

        y_ref[...] = _rows_to_slabs((s_lo * inv) * ffn(0) + (s_hi * inv) * ffn(1))

    @pl.when(jnp.logical_not(valid))
    def _():
        y_ref[...] = jnp.zeros_like(y_ref)

    @pl.when((valid & (i == n - 1)) | (jnp.logical_not(valid) & (tv_ref[prev] != 0)))
    def _():
        wait_rows(jnp.where(valid, 1 - slot, slot))


def experts(tile_expert, tile_valid, next_expert, src, rows, router_w_t, wg, wu, wd, layer):
    tg = EXPERT_TILE
    n_tiles = src.shape[0] // tg
    d, ff = wg.shape[2:]
    lo = lambda i, te, tv, nx, src: (te[i], 0, 0)
    hi = lambda i, te, tv, nx, src: (te[n_tiles + i], 0, 0)
    hbm = pl.BlockSpec(memory_space=pl.ANY)
    return pl.pallas_call(
        functools.partial(_expert_kernel, layer=layer),
        grid_spec=pltpu.PrefetchScalarGridSpec(
            num_scalar_prefetch=4,
            grid=(n_tiles,),
            in_specs=[hbm, pl.BlockSpec((None, 1, d), lo), pl.BlockSpec((None, 1, d), hi), hbm, hbm, hbm],
            out_specs=pl.BlockSpec((tg, ROW_SLABS, LANES), lambda i, te, tv, nx, src: (i, 0, 0)),
            scratch_shapes=[pltpu.VMEM((2, tg, ROW_SLABS, LANES), F32), pltpu.SemaphoreType.DMA((2,)),
                            pltpu.VMEM((2, d, ff), F32), pltpu.VMEM((2, d, ff), F32), pltpu.VMEM((2, ff, d), F32),
                            pltpu.VMEM((2, d, ff), BF16), pltpu.VMEM((2, d, ff), BF16), pltpu.VMEM((2, ff, d), BF16),
                            pltpu.SemaphoreType.DMA((2,))],
        ),
        out_shape=jax.ShapeDtypeStruct((n_tiles * tg, ROW_SLABS, LANES), F32),
        compiler_params=_cparams(("arbitrary",)),
        name="moe_experts",
    )(tile_expert, tile_valid, next_expert, src, rows, router_w_t, router_w_t, wg, wu, wd)


def _combine_kernel(dest_ref, hf_ref, x_ref, m_ref, wg_ref, wu_ref, wd_ref, y_hbm, o_ref, ybuf_ref, sem_ref):
    i = pl.program_id(0)
    n = pl.num_programs(0)
    tm = hf_ref.shape[0]
    slot = i % 2

    def fetch(tile, s):
        base = tile * tm
        _for_rows(tm, lambda t, u: pltpu.async_copy(
            y_hbm.at[dest_ref[base + t]], ybuf_ref.at[s, t], sem_ref.at[s], priority=u % 2))

    @pl.when(i == 0)
    def _():
        fetch(0, 0)

    @pl.when(i + 1 < n)
    def _():
        fetch(i + 1, 1 - slot)

    h = hf_ref[...]
    a = _dot(h, wg_ref[...])
    u = _dot(h, wu_ref[...])
    shared = _dot((a * _sigmoid(a) * u).astype(BF16), wd_ref[...])
    pltpu.make_async_copy(y_hbm.at[pl.ds(0, tm)], ybuf_ref.at[slot], sem_ref.at[slot]).wait()
    o_ref[...] = x_ref[...] + m_ref[G_F:G_F + 1, :] * (shared + _slabs_to_rows(ybuf_ref[slot]))


def combine(dest, hf, xs, mods, wg, wu, wd, y_sorted, *, tm, n_ctx_tiles):
    b, r, d = hf.shape
    nt = r // tm
    ff = wg.shape[1]
    row = lambda i, dest: (i // nt, i % nt, 0)
    const = lambda i, dest: (0, 0)
    return pl.pallas_call(
        _combine_kernel,
        grid_spec=pltpu.PrefetchScalarGridSpec(
            num_scalar_prefetch=1,
            grid=(b * nt,),
            in_specs=[
                pl.BlockSpec((None, tm, d), row),
                pl.BlockSpec((None, tm, d), row),
                pl.BlockSpec((None, None, 6, d),
                             lambda i, dest: (i // nt, ((i % nt) >= n_ctx_tiles).astype(jnp.int32), 0, 0)),
                pl.BlockSpec((d, ff), const), pl.BlockSpec((d, ff), const), pl.BlockSpec((ff, d), const),
                pl.BlockSpec(memory_space=pl.ANY),
            ],
            out_specs=pl.BlockSpec((None, tm, d), row),
            scratch_shapes=[pltpu.VMEM((2, tm, ROW_SLABS, LANES), F32), pltpu.SemaphoreType.DMA((2,))],
        ),
        out_shape=jax.ShapeDtypeStruct((b, r, d), F32),
        compiler_params=_cparams(("arbitrary",)),
        name="moe_combine",
    )(dest, hf, xs, mods, wg, wu, wd, y_sorted)


def moe_block(xs, mods, gain, router_w, router_bias, ew_gate, ew_up, ew_down, layer, sw_gate, sw_up, sw_down,
              *, tm, n_ctx_tiles):
    b, r, d = xs.shape
    n = b * r
    tg = EXPERT_TILE
    hf, rows, info, counts = route(xs, mods, gain, router_w, router_bias, tm=tm, n_ctx_tiles=n_ctx_tiles)
    cls = info[:, :, 0, :].reshape(n)
    rank = info[:, :, 1, :].reshape(n)
    padded = ((counts[:, 0].astype(jnp.int32) + tg - 1) // tg) * tg
    ends = jnp.cumsum(padded)
    dest = (ends - padded)[cls] + rank
    n_tiles = n // tg + N_CLASSES
    tile_start = jnp.arange(n_tiles, dtype=jnp.int32) * tg
    tile_valid = tile_start < ends[-1]
    last_start = jnp.maximum(ends[-1] - tg, 0)
    start = jnp.where(tile_valid, tile_start, last_start)
    tile_cls = jnp.sum((ends[None, :] <= start[:, None]).astype(jnp.int32), axis=1)
    tile_cls = jnp.minimum(tile_cls, N_CLASSES - 1)
    slot_expert = jnp.stack([jnp.asarray(CLASS_LO)[tile_cls], jnp.asarray(CLASS_HI)[tile_cls]])
    tile_expert = slot_expert.reshape(-1)
    later = (tile_start[None, :] > tile_start[:, None]) & tile_valid[None, :]
    differs = slot_expert[:, None, :] != slot_expert[:, :, None]
    first = jnp.min(jnp.where(later[None] & differs, jnp.arange(n_tiles)[None, None, :], n_tiles), axis=-1)
    next_expert = jnp.where(first < n_tiles, jnp.take_along_axis(slot_expert, jnp.minimum(first, n_tiles - 1), axis=1),
                            -1).reshape(-1).astype(jnp.int32)
    src = invert_permutation(dest, n_tiles * tg)
    y_sorted = experts(tile_expert, tile_valid.astype(jnp.int32), next_expert, src, rows,
                       router_w.T.astype(F32).reshape(N_EXPERTS, 1, d), ew_gate, ew_up, ew_down, layer)
    return combine(dest, hf, xs, mods, sw_gate.astype(BF16), sw_up.astype(BF16), sw_down.astype(BF16), y_sorted,
                   tm=tm, n_ctx_tiles=n_ctx_tiles)


def kernel(x, c, ctx, c_ctx, mod_w, mod_b, norm_mix, norm_ffn, even_w_in, even_w_out, diff_q_gain, diff_k_gain, diff_lambda, diff_subln, hgrn_lb_logits, hgrn_out_gain, odd_w_in, mla_q_a_gain, mla_kv_a_gain, mla_w_uq, mla_w_ukv, mla_q_nope_gain, mla_q_rope_gain, mla_k_nope_gain, mla_k_rope_gain, odd_w_out, router_w, router_bias, expert_w_gate, expert_w_up, expert_w_down, shared_w_gate, shared_w_up, shared_w_down):
    b, n_lat, d = x.shape
    n_ctx = ctx.shape[1]
    depth = mod_w.shape[0]
    tm = 256 if n_ctx % 256 == 0 else 128
    n_ctx_tiles = n_ctx // tm
    rope = rope_lane_tables(n_ctx, n_lat)

    mod_rows = 16
    c_rows = jnp.concatenate([c, c_ctx[None, :], jnp.zeros((mod_rows - b - 1, d), F32)], axis=0)
    mod_all = modulation(c_rows, mod_w, mod_b).reshape(depth, mod_rows, 6, d)
    lb_all = jnp.cumsum(jax.nn.softmax(hgrn_lb_logits.astype(F32), axis=0), axis=0)

    xs = jnp.concatenate([ctx, x], axis=1)
    for layer in range(depth):
        last = layer == depth - 1
        j = layer // 2
        mods = jnp.stack([jnp.broadcast_to(mod_all[layer, b], (b, 6, d)), mod_all[layer, :b]], axis=1)
        row_tile0 = n_ctx_tiles if last else 0
        if layer % 2 == 0:
            lam_init = 0.8 - 0.6 * math.exp(-0.3 * layer)
            p = norm_matmul(xs, norm_mix[layer], even_w_in[j].astype(BF16), tm=tm, mods=mods, n_ctx_tiles=n_ctx_tiles)
            oa = diff_attention(p, diff_lambda[j], diff_q_gain[j], diff_k_gain[j], diff_subln[j], rope,
                                tq=tm, n_ctx=n_ctx, lam_init=lam_init, q_blk0=0, k_blk0=4, v_blk0=8)
            ob = hgrn_bidirectional(p, lb_all[j, 0], lb_all[j, 1], hgrn_out_gain[j], n_ctx=n_ctx, blk0=12)
            if last:
                oa, ob = oa[:, n_ctx:], ob[:, n_ctx:]
            w_out = even_w_out[j].astype(BF16)
            half = oa.shape[-1]
            xs_mid = proj_residual([oa, ob], [w_out[:half], w_out[half:]], xs, mods, tm=tm,
                                   n_ctx_tiles=n_ctx_tiles, row_tile0=row_tile0)
        else:
            w_in = odd_w_in[j]
            zpad = jnp.zeros((d, LANES - MLA_ROPE), F32)
            w_in_r = jnp.concatenate([w_in[:, MLA_Q_LORA:MLA_Q_LORA + MLA_KV_LORA], w_in[:, MLA_Q_LORA + MLA_KV_LORA:],
                                      zpad, w_in[:, :MLA_Q_LORA]], axis=1).astype(BF16)
            p1 = norm_matmul(xs, norm_mix[layer], w_in_r, tm=tm, mods=mods, n_ctx_tiles=n_ctx_tiles)
            w_uq = mla_w_uq[j].reshape(MLA_Q_LORA, MLA_HEADS, MLA_NOPE + MLA_ROPE)
            w_uq = jnp.concatenate([w_uq, jnp.zeros((MLA_Q_LORA, MLA_HEADS, LANES - MLA_ROPE), F32)], axis=-1)
            w_uq = w_uq.reshape(MLA_Q_LORA, MLA_HEADS * 2 * LANES).astype(BF16)
            q = norm_matmul(p1, mla_q_a_gain[j], w_uq, tm=tm, col_block=1, row_tile0=row_tile0)
            kv = norm_matmul(p1, mla_kv_a_gain[j], mla_w_ukv[j].astype(BF16), tm=tm, col_block=0)
            if not last:
                raise NotImplementedError("context queries for a non-final latent-attention layer")
            o = mla_attention(q, kv, p1, 2, rope, mla_q_nope_gain[j], mla_q_rope_gain[j], mla_k_nope_gain[j],
                              mla_k_rope_gain[j], tq=2 * tm, n_ctx=n_ctx)
            xs_mid = proj_residual([o], [odd_w_out[j].astype(BF16)], xs, mods, tm=tm,
                                   n_ctx_tiles=n_ctx_tiles, row_tile0=row_tile0)
        xs = moe_block(xs_mid, mods, norm_ffn[layer], router_w, router_bias,
                       expert_w_gate, expert_w_up, expert_w_down, layer,
                       shared_w_gate[layer], shared_w_up[layer], shared_w_down[layer],
                       tm=tm, n_ctx_tiles=0 if last else n_ctx_tiles)
    return xs if xs.shape[1] == n_lat else xs[:, n_ctx:]
```

```python
import functools
import math

import numpy as np
import jax
import jax.numpy as jnp
from jax import lax
from jax.experimental import pallas as pl
from jax.experimental.pallas import tpu as pltpu

F32 = jnp.float32
BF16 = jnp.bfloat16

LANES = 128
VMEM_LIMIT = 56 * 1024 * 1024

GRID_W = 64
DIFF_HEADS = 4
DIFF_HEAD_DIM = 64
HGRN_HEADS = 4
HGRN_K_DIM = 128
MLA_HEADS = 8
MLA_NOPE = 128
MLA_ROPE = 64
MLA_V = 128
MLA_Q_LORA = 384
MLA_KV_LORA = 256
N_EXPERTS = 16
N_GROUPS = 4
EXPERTS_PER_GROUP = 4
ROPE_BASE = 10000.0
EPS = 1e-6
LOG2E = 1.4426950408889634
GLA_CHUNK = 64
ATTN_KEY_CHUNK = 1152
MLA_HEADS_PER_STEP = 2
DIFF_HEADS_PER_STEP = 2
HGRN_HEADS_PER_STEP = 2

SH_M, SC_M, G_M, SH_F, SC_F, G_F = range(6)


def _sigmoid(x):
    return 1.0 / (1.0 + jnp.exp(-x))


def _dot(a, b):
    return jnp.dot(a, b, preferred_element_type=F32)


def _dot_nt(a, b):
    return lax.dot_general(a, b, (((1,), (1,)), ((), ())), preferred_element_type=F32)


def _dot_tn(a, b):
    return lax.dot_general(a, b, (((0,), (0,)), ((), ())), preferred_element_type=F32)


def _split3(x):
    hi = x.astype(BF16)
    r = x - hi.astype(F32)
    mid = r.astype(BF16)
    lo = (r - mid.astype(F32)).astype(BF16)
    return hi, mid, lo


def _rms(x, width=None):
    n = x.shape[-1] if width is None else width
    return x * lax.rsqrt(jnp.sum(x * x, axis=-1, keepdims=True) * (1.0 / n) + EPS)


def _cparams(sem):
    return pltpu.CompilerParams(dimension_semantics=sem, vmem_limit_bytes=VMEM_LIMIT)


def _mod_kernel(c_ref, w_ref, b_ref, o_ref):
    c = c_ref[...]
    s = c * _sigmoid(c)
    o_ref[...] = _dot(s.astype(BF16), w_ref[...].astype(BF16)) + b_ref[...]


def modulation(c_rows, mod_w, mod_b):
    n_layers, d, n = mod_w.shape
    rows = c_rows.shape[0]
    tn = 1536
    return pl.pallas_call(
        _mod_kernel,
        grid=(n_layers, n // tn),
        in_specs=[
            pl.BlockSpec((rows, d), lambda l, j: (0, 0)),
            pl.BlockSpec((None, d, tn), lambda l, j: (l, 0, j)),
            pl.BlockSpec((None, 1, tn), lambda l, j: (l, 0, j)),
        ],
        out_specs=pl.BlockSpec((None, rows, tn), lambda l, j: (l, 0, j)),
        out_shape=jax.ShapeDtypeStruct((n_layers, rows, n), F32),
        compiler_params=_cparams(("arbitrary", "arbitrary")),
        name="modulation",
    )(c_rows, mod_w, mod_b.reshape(n_layers, 1, n))


def _norm_matmul_kernel(*refs, has_mod):
    if has_mod:
        x_ref, m_ref, g_ref, w_ref, o_ref = refs
    else:
        x_ref, g_ref, w_ref, o_ref = refs
    h = _rms(x_ref[...].astype(F32)) * g_ref[...]
    if has_mod:
        h = h * (1.0 + m_ref[SC_M:SC_M + 1, :]) + m_ref[SH_M:SH_M + 1, :]
    o_ref[...] = _dot(h.astype(BF16), w_ref[...]).astype(o_ref.dtype)


def norm_matmul(x, gain, w, *, tm, col_block=0, row_tile0=0, n_row_tiles=None, mods=None, n_ctx_tiles=0):
    b, t, _ = x.shape
    k, n = w.shape
    if n_row_tiles is None:
        n_row_tiles = t // tm - row_tile0
    in_specs = [pl.BlockSpec((None, tm, k), lambda bi, ti: (bi, ti + row_tile0, col_block))]
    args = [x]
    if mods is not None:
        in_specs.append(pl.BlockSpec((None, None, 6, k),
                                     lambda bi, ti: (bi, ((ti + row_tile0) >= n_ctx_tiles).astype(jnp.int32), 0, 0)))
        args.append(mods)
    in_specs += [pl.BlockSpec((1, k), lambda bi, ti: (0, 0)), pl.BlockSpec((k, n), lambda bi, ti: (0, 0))]
    args += [gain.reshape(1, k).astype(F32), w]
    return pl.pallas_call(
        functools.partial(_norm_matmul_kernel, has_mod=mods is not None),
        grid=(b, n_row_tiles),
        in_specs=in_specs,
        out_specs=pl.BlockSpec((None, tm, n), lambda bi, ti: (bi, ti, 0)),
        out_shape=jax.ShapeDtypeStruct((b, n_row_tiles * tm, n), BF16),
        compiler_params=_cparams(("arbitrary", "arbitrary")),
        name="norm_matmul",
    )(*args)


def rope_lane_tables(n_ctx, n_lat):
    rows = n_lat // GRID_W
    row = np.repeat(np.arange(rows), GRID_W).astype(np.float32)
    col = np.tile(np.arange(GRID_W), rows).astype(np.float32)
    axis_dim = DIFF_HEAD_DIM // 2
    inv_freq = jnp.asarray(ROPE_BASE, F32) ** (-jnp.arange(0, axis_dim, 2, dtype=F32) / axis_dim)
    ang_r = jnp.asarray(row)[:, None] * inv_freq
    ang_c = jnp.asarray(col)[:, None] * inv_freq
    lane = np.arange(LANES)
    freq_idx = lane % 16
    use_col = (lane % 64) >= 32
    first = (lane % 32) < 16
    ang = jnp.where(use_col[None, :], ang_c[:, freq_idx], ang_r[:, freq_idx])
    cos, sin = jnp.cos(ang), jnp.sin(ang)
    c = jnp.concatenate([jnp.ones((n_ctx, LANES), F32), cos], axis=0)
    sa = jnp.concatenate([jnp.zeros((n_ctx, LANES), F32), jnp.where(first[None, :], -sin, 0.0)], axis=0)
    sb = jnp.concatenate([jnp.zeros((n_ctx, LANES), F32), jnp.where(first[None, :], 0.0, sin)], axis=0)
    return c, sa, sb


def _rope(x, c, sa, sb):
    return x * c + pltpu.roll(x, LANES - 16, 1) * sa + pltpu.roll(x, 16, 1) * sb


def _block_ones(width):
    r = lax.broadcasted_iota(jnp.int32, (LANES, LANES), 0) // width
    c = lax.broadcasted_iota(jnp.int32, (LANES, LANES), 1) // width
    return (r == c).astype(BF16)


def _block_rms(x, ones, width):
    xx = x * x
    hi = xx.astype(BF16)
    lo = (xx - hi.astype(F32)).astype(BF16)
    ms = (_dot(hi, ones) + _dot(lo, ones)) * (1.0 / width)
    return x * lax.rsqrt(ms + EPS)


def _attend_streams(qs, kt_refs, vs, nk):
    n = len(qs)
    chunk = ATTN_KEY_CHUNK if nk % ATTN_KEY_CHUNK == 0 else nk
    nc = nk // chunk
    s = [[None] * nc for _ in range(n)]
    p = [[None] * nc for _ in range(n)]
    m, l, o = [None] * n, [None] * n, [None] * n

    def logits(i, c):
        s[i][c] = _dot(qs[i], kt_refs[i][:, c * chunk:(c + 1) * chunk])
        mc = jnp.max(s[i][c], axis=-1, keepdims=True)
        m[i] = mc if c == 0 else jnp.maximum(m[i], mc)

    def exps(i, c):
        e = jnp.exp2(s[i][c] - m[i])
        lc = jnp.sum(e, axis=-1, keepdims=True)
        l[i] = lc if c == 0 else l[i] + lc
        p[i][c] = e.astype(BF16)

    def values(i, c):
        ref, col0 = vs[i]
        oc = _dot(p[i][c], ref[c * chunk:(c + 1) * chunk, col0:col0 + LANES])
        o[i] = oc if c == 0 else o[i] + oc

    for step in range(n + 2):
        for c in range(nc):
            if step < n:
                logits(step, c)
            if 0 <= step - 1 < n:
                exps(step - 1, c)
            if 0 <= step - 2 < n:
                values(step - 2, c)
    return list(zip(o, l))


def _diff_attn_kernel(lam_ref, q_ref, k_ref, v_ref, cq_ref, saq_ref, sbq_ref, ck_ref, sak_ref, sbk_ref,
                      qg_ref, kg_ref, sub_ref, o_ref, kt_ref, *, n_ctx, n_ctx_tiles, lam_init):
    qi = pl.program_id(2)
    ones = _block_ones(DIFF_HEAD_DIM)
    heads = range(DIFF_HEADS_PER_STEP)

    @pl.when(qi == 0)
    def _():
        for h in heads:
            k = _block_rms(k_ref[:, h * LANES:(h + 1) * LANES].astype(F32), ones, DIFF_HEAD_DIM) * kg_ref[...]
            kt_ref[h] = _rope(k, ck_ref[...], sak_ref[...], sbk_ref[...]).T.astype(BF16)

    qs = []
    for h in heads:
        q = _block_rms(q_ref[:, h * LANES:(h + 1) * LANES].astype(F32), ones, DIFF_HEAD_DIM) * qg_ref[...]
        q = _rope(q, cq_ref[...], saq_ref[...], sbq_ref[...]) * (DIFF_HEAD_DIM ** -0.5 * LOG2E)
        lane = lax.broadcasted_iota(jnp.int32, q.shape, 1)
        qs.append(jnp.where(lane < DIFF_HEAD_DIM, q, 0.0).astype(BF16))
        qs.append(jnp.where(lane >= DIFF_HEAD_DIM, q, 0.0).astype(BF16))
    lv = lam_ref[...]
    lam = (jnp.exp(jnp.sum(lv[0:1] * lv[1:2], axis=-1, keepdims=True))
           - jnp.exp(jnp.sum(lv[2:3] * lv[3:4], axis=-1, keepdims=True)) + lam_init)

    def attend(nk):
        outs = _attend_streams(qs, [kt_ref.at[h] for h in heads for _ in range(2)],
                               [(v_ref, h * LANES) for h in heads for _ in range(2)], nk)
        for h in heads:
            (o1, l1), (o2, l2) = outs[2 * h], outs[2 * h + 1]
            o = o1 * (1.0 / l1) - o2 * (lam / l2)
            o_ref[:, h * LANES:(h + 1) * LANES] = (_rms(o) * sub_ref[...] * (1.0 - lam_init)).astype(o_ref.dtype)

    @pl.when(qi < n_ctx_tiles)
    def _():
        attend(n_ctx)

    @pl.when(qi >= n_ctx_tiles)
    def _():
        attend(k_ref.shape[0])


def diff_attention(p, lam_vecs, q_gain, k_gain, subln, rope, *, tq, n_ctx, lam_init, q_blk0, k_blk0, v_blk0):
    b, t, _ = p.shape
    c, sa, sb = rope
    nq = t // tq
    hps = DIFF_HEADS_PER_STEP
    assert q_blk0 % hps == 0 and k_blk0 % hps == 0 and v_blk0 % hps == 0
    row_q = lambda bi, h, qi: (qi, 0)
    full = lambda bi, h, qi: (0, 0)
    tile2 = lambda g: jnp.tile(g.astype(F32), 2).reshape(1, LANES)
    return pl.pallas_call(
        functools.partial(_diff_attn_kernel, n_ctx=n_ctx, n_ctx_tiles=n_ctx // tq, lam_init=lam_init),
        grid=(b, DIFF_HEADS // hps, nq),
        in_specs=[
            pl.BlockSpec((4, DIFF_HEAD_DIM), full),
            pl.BlockSpec((None, tq, hps * LANES), lambda bi, h, qi: (bi, qi, q_blk0 // hps + h)),
            pl.BlockSpec((None, t, hps * LANES), lambda bi, h, qi: (bi, 0, k_blk0 // hps + h)),
            pl.BlockSpec((None, t, hps * LANES), lambda bi, h, qi: (bi, 0, v_blk0 // hps + h)),
            pl.BlockSpec((tq, LANES), row_q), pl.BlockSpec((tq, LANES), row_q), pl.BlockSpec((tq, LANES), row_q),
            pl.BlockSpec((t, LANES), full), pl.BlockSpec((t, LANES), full), pl.BlockSpec((t, LANES), full),
            pl.BlockSpec((1, LANES), full), pl.BlockSpec((1, LANES), full), pl.BlockSpec((1, LANES), full),
        ],
        out_specs=pl.BlockSpec((None, tq, hps * LANES), lambda bi, h, qi: (bi, qi, h)),
        out_shape=jax.ShapeDtypeStruct((b, t, DIFF_HEADS * LANES), BF16),
        scratch_shapes=[pltpu.VMEM((hps, LANES, t), BF16)],
        compiler_params=_cparams(("arbitrary", "arbitrary", "arbitrary")),
        name="diff_attention",
    )(lam_vecs.astype(F32), p, p, p, c, sa, sb, c, sa, sb, tile2(q_gain), tile2(k_gain),
      subln.astype(F32).reshape(1, LANES))


def _gla_constants(c):
    levels = int(math.log2(c))
    t = np.arange(c)[:, None]
    u = np.arange(c)[None, :]
    stack = [[u <= t], [u >= t]]
    qside = [[], []]
    pair = [[t == u], [t == u]]
    for lv in range(1, levels + 1):
        base = (t >> lv) << lv
        half = 1 << (lv - 1)
        stack[0].append(u <= base + half - 1)
        stack[1].append(u >= base + half)
        up_t = ((t >> (lv - 1)) & 1) == 1
        up_u = ((u >> (lv - 1)) & 1) == 1
        same = (t >> lv) == (u >> lv)
        qside[0].append(np.broadcast_to(up_t, (c, LANES)))
        qside[1].append(np.broadcast_to(~up_t, (c, LANES)))
        pair[0].append(same & up_t & ~up_u)
        pair[1].append(same & ~up_t & up_u)
    f32 = lambda x: np.asarray(x, np.float32)
    stack = np.stack([np.tile(np.concatenate(f32(m), axis=0), (1, 3)) for m in stack])
    return (jnp.asarray(stack, BF16), jnp.asarray(np.stack([f32(m) for m in qside])),
            jnp.asarray(np.stack([f32(m) for m in pair])))


def _gla_chunk(q, k, v, g2, s, stack, qside_ref, pair_ref, d):
    c = q.shape[0]
    levels = int(math.log2(c))
    gcat = jnp.concatenate(_split3(g2), axis=0)
    cs = _dot(stack, gcat)
    tot_col = _dot_tn(gcat, jnp.ones((3 * c, LANES), BF16))
    diag = jnp.sum(q * k, axis=-1, keepdims=True)
    yield
    cum = cs[0:c]
    tot = cum[0:1] if d == 1 else cum[c - 1:c]
    zz = []
    for lv in range(1, levels + 1):
        e = jnp.exp2(-jnp.abs(cum - cs[lv * c:(lv + 1) * c]))
        z = (jnp.where(qside_ref[d, lv - 1] > 0.5, q, k) * e).astype(BF16)
        zz.append(_dot_nt(z, z))
    q_in = (q * jnp.exp2(cum)).astype(BF16)
    ks = (k * jnp.exp2(tot - cum)).astype(BF16)
    ds = _dot_tn(ks, v)
    yield
    a = pair_ref[d, 0] * diag
    for lv in range(1, levels + 1):
        a = a + pair_ref[d, lv] * zz[lv - 1]
    lhs = jnp.concatenate([q_in, a.astype(BF16)], axis=1)
    o = _dot(lhs, jnp.concatenate([s.astype(BF16), v], axis=0))
    s_new = s * jnp.exp2(tot_col) + ds
    yield
    return o, s_new


def _run_interleaved(gens):
    results = [None] * len(gens)
    live = list(range(len(gens)))
    while live:
        for i in list(live):
            try:
                next(gens[i])
            except StopIteration as stop:
                results[i] = stop.value
                live.remove(i)
    return results


def _hgrn_kernel(qz_ref, zf_ref, zb_ref, v_ref, gz_ref, lbf_ref, lbb_ref, og_ref, stack_ref, qside_ref, pair_ref,
                 o_ref, q_ref, kf_ref, gf_ref, kb_ref, gb_ref, of_ref, ob_ref, st_ref, *, n_ctx_chunks):
    c = GLA_CHUNK
    t = qz_ref.shape[0]
    n = t // c
    heads = range(HGRN_HEADS_PER_STEP)
    qz = qz_ref[...].astype(F32)
    q_ref[...] = qz * _sigmoid(qz) * (HGRN_K_DIM ** -0.5)
    for z_ref, lb_ref, k_ref, g_ref in ((zf_ref, lbf_ref, kf_ref, gf_ref), (zb_ref, lbb_ref, kb_ref, gb_ref)):
        lb = jnp.concatenate([lb_ref[h] for h in heads], axis=-1)
        f = lb + (1.0 - lb) * _sigmoid(z_ref[...].astype(F32))
        k_ref[...] = 1.0 - f
        g_ref[...] = jnp.log(f) * LOG2E
    st_ref[...] = jnp.zeros_like(st_ref)

    def body(i, carry):
        rf = pl.multiple_of(i * c, c)
        cb = jnp.where(i < n_ctx_chunks, n_ctx_chunks - 1 - i, n - 1 - i + n_ctx_chunks)
        rb = pl.multiple_of(cb * c, c)
        chains = [(h, d, pl.ds(r0, c), k_ref, g_ref, out_ref) for h in heads
                  for d, r0, k_ref, g_ref, out_ref in ((0, rf, kf_ref, gf_ref, of_ref), (1, rb, kb_ref, gb_ref, ob_ref))]
        col = lambda h: slice(h * LANES, (h + 1) * LANES)
        outs = _run_interleaved([
            _gla_chunk(q_ref[rows, col(h)], k_ref[rows, col(h)], v_ref[rows, col(h)], g_ref[rows, col(h)],
                       st_ref[h, d], stack_ref[d], qside_ref, pair_ref, d)
            for (h, d, rows, k_ref, g_ref, _) in chains])
        for (h, d, rows, _, _, out_ref), (o, s_new) in zip(chains, outs):
            out_ref[rows, col(h)] = o
            st_ref[h, d] = s_new
        return carry

    lax.fori_loop(0, n, body, 0)
    gz = gz_ref[...].astype(F32)
    for h in heads:
        cols = slice(h * LANES, (h + 1) * LANES)
        o = _rms(of_ref[:, cols] + ob_ref[:, cols]) * og_ref[...]
        o_ref[:, cols] = (o * (gz[:, cols] * _sigmoid(gz[:, cols]))).astype(o_ref.dtype)


def hgrn_bidirectional(p, lb_fwd, lb_bwd, out_gain, *, n_ctx, blk0):
    b, t, _ = p.shape
    c = GLA_CHUNK
    consts = _gla_constants(c)
    h = HGRN_HEADS
    hps = HGRN_HEADS_PER_STEP
    assert blk0 % hps == 0 and h % hps == 0
    seg = lambda s: pl.BlockSpec((None, t, hps * LANES), lambda bi, hi: (bi, 0, (blk0 + s * h) // hps + hi))
    per_head = pl.BlockSpec((hps, 1, LANES), lambda bi, hi: (hi, 0, 0))
    const = lambda a: pl.BlockSpec(a.shape, lambda bi, hi: (0,) * a.ndim)
    seq = pltpu.VMEM((t, hps * LANES), F32)
    return pl.pallas_call(
        functools.partial(_hgrn_kernel, n_ctx_chunks=n_ctx // c),
        grid=(b, h // hps),
        in_specs=[seg(0), seg(1), seg(2), seg(3), seg(4), per_head, per_head,
                  pl.BlockSpec((1, LANES), lambda bi, hi: (0, 0))] + [const(a) for a in consts],
        out_specs=pl.BlockSpec((None, t, hps * LANES), lambda bi, hi: (bi, 0, hi)),
        out_shape=jax.ShapeDtypeStruct((b, t, h * LANES), BF16),
        scratch_shapes=[seq, seq, seq, seq, seq, seq, seq, pltpu.VMEM((hps, 2, LANES, LANES), F32)],
        compiler_params=_cparams(("arbitrary", "arbitrary")),
        name="hgrn_bidirectional",
    )(p, p, p, p, p, lb_fwd.reshape(h, 1, LANES), lb_bwd.reshape(h, 1, LANES),
      out_gain.astype(F32).reshape(1, LANES), *consts)


def _mla_attn_kernel(q_ref, kv_ref, kr_ref, cq_ref, saq_ref, sbq_ref, ck_ref, sak_ref, sbk_ref,
                     qn_ref, qr_ref, kn_ref, krg_ref, o_ref, kt_ref):
    qi = pl.program_id(2)
    ones_nope = _block_ones(MLA_NOPE)
    ones_rope = _block_ones(MLA_ROPE)
    hw = 2 * LANES

    @pl.when(qi == 0)
    def _():
        kr = _block_rms(kr_ref[...].astype(F32), ones_rope, MLA_ROPE) * krg_ref[...]
        kr = _rope(kr, ck_ref[...], sak_ref[...], sbk_ref[...]).T.astype(BF16)
        for h in range(MLA_HEADS_PER_STEP):
            kn = _block_rms(kv_ref[:, h * hw:h * hw + MLA_NOPE].astype(F32), ones_nope, MLA_NOPE) * kn_ref[...]
            kt_ref[h, 0:LANES, :] = kn.T.astype(BF16)
            kt_ref[h, LANES:hw, :] = kr

    scale = (MLA_NOPE + MLA_ROPE) ** -0.5 * LOG2E
    qs = []
    for h in range(MLA_HEADS_PER_STEP):
        qn = _block_rms(q_ref[:, h * hw:h * hw + MLA_NOPE].astype(F32), ones_nope, MLA_NOPE) * (qn_ref[...] * scale)
        qr = _block_rms(q_ref[:, h * hw + MLA_NOPE:(h + 1) * hw].astype(F32), ones_rope, MLA_ROPE) * qr_ref[...]
        qr = _rope(qr, cq_ref[...], saq_ref[...], sbq_ref[...]) * scale
        qs.append(jnp.concatenate([qn.astype(BF16), qr.astype(BF16)], axis=-1))
    heads = range(MLA_HEADS_PER_STEP)
    outs = _attend_streams(qs, [kt_ref.at[h] for h in heads], [(kv_ref, h * hw + MLA_NOPE) for h in heads],
                           kt_ref.shape[2])
    for h, (o, l) in enumerate(outs):
        o_ref[:, h * MLA_V:(h + 1) * MLA_V] = (o * (1.0 / l)).astype(o_ref.dtype)


def mla_attention(q, kv, p1, kr_blk, rope, qn_gain, qr_gain, kn_gain, kr_gain, *, tq, n_ctx):
    b, n_lat, _ = q.shape
    t = kv.shape[1]
    c, sa, sb = rope
    hps = MLA_HEADS_PER_STEP
    row_q = lambda bi, h, qi: (qi, 0)
    full = lambda bi, h, qi: (0, 0)
    pad = lambda g: jnp.concatenate([g.astype(F32), jnp.zeros((LANES - g.shape[0],), F32)]).reshape(1, LANES)
    return pl.pallas_call(
        _mla_attn_kernel,
        grid=(b, MLA_HEADS // hps, n_lat // tq),
        in_specs=[
            pl.BlockSpec((None, tq, hps * 2 * LANES), lambda bi, h, qi: (bi, qi, h)),
            pl.BlockSpec((None, t, hps * 2 * LANES), lambda bi, h, qi: (bi, 0, h)),
            pl.BlockSpec((None, t, LANES), lambda bi, h, qi: (bi, 0, kr_blk)),
            pl.BlockSpec((tq, LANES), row_q), pl.BlockSpec((tq, LANES), row_q), pl.BlockSpec((tq, LANES), row_q),
            pl.BlockSpec((t, LANES), full), pl.BlockSpec((t, LANES), full), pl.BlockSpec((t, LANES), full),
            pl.BlockSpec((1, LANES), full), pl.BlockSpec((1, LANES), full),
            pl.BlockSpec((1, LANES), full), pl.BlockSpec((1, LANES), full),
        ],
        out_specs=pl.BlockSpec((None, tq, hps * MLA_V), lambda bi, h, qi: (bi, qi, h)),
        out_shape=jax.ShapeDtypeStruct((b, n_lat, MLA_HEADS * MLA_V), BF16),
        scratch_shapes=[pltpu.VMEM((hps, 2 * LANES, t), BF16)],
        compiler_params=_cparams(("arbitrary", "arbitrary", "arbitrary")),
        name="mla_attention",
    )(q, kv, p1, c[n_ctx:], sa[n_ctx:], sb[n_ctx:], c, sa, sb, pad(qn_gain), pad(qr_gain), pad(kn_gain), pad(kr_gain))


def _proj_residual_kernel(*refs, n_in):
    a_refs = refs[:n_in]
    w_refs = refs[n_in:2 * n_in]
    x_ref, m_ref, o_ref = refs[2 * n_in:]
    acc = _dot(a_refs[0][...], w_refs[0][...])
    for a_ref, w_ref in zip(a_refs[1:], w_refs[1:]):
        acc = acc + _dot(a_ref[...], w_ref[...])
    o_ref[...] = x_ref[...] + m_ref[G_M:G_M + 1, :] * acc


def proj_residual(acts, weights, xs, mods, *, tm, n_ctx_tiles, row_tile0=0):
    b, r, _ = acts[0].shape
    d = xs.shape[-1]
    in_specs = [pl.BlockSpec((None, tm, a.shape[-1]), lambda bi, ti: (bi, ti, 0)) for a in acts]
    in_specs += [pl.BlockSpec(w.shape, lambda bi, ti: (0, 0)) for w in weights]
    in_specs += [
        pl.BlockSpec((None, tm, d), lambda bi, ti: (bi, ti + row_tile0, 0)),
        pl.BlockSpec((None, None, 6, d), lambda bi, ti: (bi, ((ti + row_tile0) >= n_ctx_tiles).astype(jnp.int32), 0, 0)),
    ]
    return pl.pallas_call(
        functools.partial(_proj_residual_kernel, n_in=len(acts)),
        grid=(b, r // tm),
        in_specs=in_specs,
        out_specs=pl.BlockSpec((None, tm, d), lambda bi, ti: (bi, ti, 0)),
        out_shape=jax.ShapeDtypeStruct((b, r, d), F32),
        compiler_params=_cparams(("arbitrary", "arbitrary")),
        name="proj_residual",
    )(*acts, *weights, xs, mods)


PAIRS = [(i, j) for i in range(EXPERTS_PER_GROUP) for j in range(i + 1, EXPERTS_PER_GROUP)]
N_CLASSES = N_GROUPS * len(PAIRS)
CLASS_LO = np.array([EXPERTS_PER_GROUP * g + i for g in range(N_GROUPS) for (i, j) in PAIRS], np.int32)
CLASS_HI = np.array([EXPERTS_PER_GROUP * g + j for g in range(N_GROUPS) for (i, j) in PAIRS], np.int32)
EXPERT_TILE = 256
ROW_SLABS = 8
DMA_UNROLL = 8


def _route_kernel(x_ref, m_ref, g_ref, rw_ref, rb_ref, hf_ref, rows_ref, info_ref, count_ref, cnt_ref):
    h = _rms(x_ref[...]) * g_ref[...]
    h = h * (1.0 + m_ref[SC_F:SC_F + 1, :]) + m_ref[SH_F:SH_F + 1, :]
    hf_ref[...] = h.astype(hf_ref.dtype)
    rows_ref[...] = _rows_to_slabs(h)
    h1, h2, _ = _split3(h)
    w1, w2, _ = _split3(rw_ref[...])
    logits = _dot_nt(w1, h1) + (_dot_nt(w1, h2) + _dot_nt(w2, h1))
    biased = _sigmoid(logits) + rb_ref[...]
    row = [biased[e:e + 1, :] for e in range(N_EXPERTS)]
    gscore = []
    for g in range(N_GROUPS):
        m = row[4 * g:4 * g + 4]
        gscore.append(functools.reduce(jnp.maximum, [m[i] + m[j] for (i, j) in PAIRS]))
    hits = []
    for g in range(N_GROUPS):
        best = None
        for g2 in range(N_GROUPS):
            if g2 == g:
                continue
            wins = (gscore[g] > gscore[g2]) if g2 < g else (gscore[g] >= gscore[g2])
            best = wins if best is None else jnp.logical_and(best, wins)
        chosen = []
        for i in range(EXPERTS_PER_GROUP):
            rank = None
            for j in range(EXPERTS_PER_GROUP):
                if j == i:
                    continue
                mi, mj = row[4 * g + i], row[4 * g + j]
                ahead = ((mj >= mi) if j < i else (mj > mi)).astype(jnp.int32)
                rank = ahead if rank is None else rank + ahead
            chosen.append(rank < 2)
        for (i, j) in PAIRS:
            hits.append(jnp.where(best & chosen[i] & chosen[j], 1.0, 0.0))
    onehot = jnp.concatenate(hits, axis=0)
    tm = onehot.shape[1]
    upper = (lax.broadcasted_iota(jnp.int32, (tm, tm), 0) <= lax.broadcasted_iota(jnp.int32, (tm, tm), 1))
    prefix = _dot(onehot.astype(BF16), upper.astype(BF16))

    @pl.when((pl.program_id(0) == 0) & (pl.program_id(1) == 0))
    def _():
        cnt_ref[...] = jnp.zeros_like(cnt_ref)

    seen = cnt_ref[...]
    cls_id = lax.broadcasted_iota(jnp.int32, onehot.shape, 0).astype(F32)
    cls = jnp.sum(onehot * cls_id, axis=0, keepdims=True)
    rank = jnp.sum(onehot * (seen[:, 0:1] + prefix - 1.0), axis=0, keepdims=True)
    info_ref[...] = jnp.concatenate([cls, rank, jnp.zeros((6, tm), F32)], axis=0).astype(jnp.int32)
    seen = seen + jnp.sum(onehot, axis=1, keepdims=True)
    cnt_ref[...] = seen
    count_ref[...] = seen


def route(xs, mods, gain, router_w, router_bias, *, tm, n_ctx_tiles):
    b, t, d = xs.shape
    nt = t // tm
    return pl.pallas_call(
        _route_kernel,
        grid=(b, nt),
        in_specs=[
            pl.BlockSpec((None, tm, d), lambda bi, ti: (bi, ti, 0)),
            pl.BlockSpec((None, None, 6, d), lambda bi, ti: (bi, (ti >= n_ctx_tiles).astype(jnp.int32), 0, 0)),
            pl.BlockSpec((1, d), lambda bi, ti: (0, 0)),
            pl.BlockSpec((N_EXPERTS, d), lambda bi, ti: (0, 0)),
            pl.BlockSpec((N_EXPERTS, 1), lambda bi, ti: (0, 0)),
        ],
        out_specs=[
            pl.BlockSpec((None, tm, d), lambda bi, ti: (bi, ti, 0)),
            pl.BlockSpec((tm, ROW_SLABS, LANES), lambda bi, ti: (bi * nt + ti, 0, 0)),
            pl.BlockSpec((None, None, 8, tm), lambda bi, ti: (bi, ti, 0, 0)),
            pl.BlockSpec((N_CLASSES, LANES), lambda bi, ti: (0, 0)),
        ],
        out_shape=[jax.ShapeDtypeStruct((b, t, d), BF16),
                   jax.ShapeDtypeStruct((b * t, ROW_SLABS, LANES), F32),
                   jax.ShapeDtypeStruct((b, nt, 8, tm), jnp.int32),
                   jax.ShapeDtypeStruct((N_CLASSES, LANES), F32)],
        scratch_shapes=[pltpu.VMEM((N_CLASSES, LANES), F32)],
        compiler_params=_cparams(("arbitrary", "arbitrary")),
        name="route",
    )(xs, mods, gain.reshape(1, d).astype(F32), router_w.T.astype(F32),
      router_bias.reshape(N_EXPERTS, 1).astype(F32))


def _transpose8(parts):
    sub = lax.broadcasted_iota(jnp.int32, parts[0].shape, 1)
    for s in (4, 2, 1):
        keep = (sub & s) == 0
        new = list(parts)
        for i in range(8):
            if i & s:
                continue
            a, b = parts[i], parts[i | s]
            new[i] = jnp.where(keep, a, pltpu.roll(b, s, 1))
            new[i | s] = jnp.where(keep, pltpu.roll(a, 8 - s, 1), b)
        parts = new
    return parts


def _slabs_to_rows(slabs):
    g = slabs.shape[0] // 8
    x4 = slabs.reshape(g, 8, ROW_SLABS, LANES)
    parts = _transpose8([x4[:, t] for t in range(8)])
    return jnp.concatenate([p.reshape(g * 8, LANES) for p in parts], axis=-1)


def _rows_to_slabs(x):
    g = x.shape[0] // 8
    parts = _transpose8([x[:, k * LANES:(k + 1) * LANES].reshape(g, 8, LANES) for k in range(ROW_SLABS)])
    return jnp.stack(parts, axis=1).reshape(g * 8, ROW_SLABS, LANES)


def _invert_kernel(dest_ref, src_ref):
    def clear(i, carry):
        for u in range(2 * DMA_UNROLL):
            src_ref[i * 2 * DMA_UNROLL + u] = 0
        return carry

    def put(i, carry):
        for u in range(2 * DMA_UNROLL):
            t = i * 2 * DMA_UNROLL + u
            src_ref[dest_ref[t]] = t
        return carry

    lax.fori_loop(0, src_ref.shape[0] // (2 * DMA_UNROLL), clear, 0)
    lax.fori_loop(0, dest_ref.shape[0] // (2 * DMA_UNROLL), put, 0)


def invert_permutation(dest, n_rows_out):
    smem = pl.BlockSpec(memory_space=pltpu.SMEM)
    return pl.pallas_call(
        _invert_kernel,
        in_specs=[smem],
        out_specs=smem,
        out_shape=jax.ShapeDtypeStruct((n_rows_out,), jnp.int32),
        name="moe_invert",
    )(dest)


def _expert_kernel(te_ref, tv_ref, nx_ref, src_ref, rows_hbm, rwl_ref, rwh_ref, wg_hbm, wu_hbm, wd_hbm, y_ref,
                   xbuf_ref, sem_ref, stage_g, stage_u, stage_d, cache_g, cache_u, cache_d, wsem_ref, *, layer):
    i = pl.program_id(0)
    n = pl.num_programs(0)
    tg = y_ref.shape[0]
    slot = i % 2
    valid = tv_ref[i] != 0
    prev = jnp.maximum(i - 1, 0)

    def fetch(tile, s):
        base = tile * tg
        for t in range(tg):
            pltpu.async_copy(rows_hbm.at[src_ref[base + t]], xbuf_ref.at[s, t], sem_ref.at[s], priority=t % 2)

    def wait_rows(s):
        pltpu.make_async_copy(rows_hbm.at[pl.ds(0, tg)], xbuf_ref.at[s], sem_ref.at[s]).wait()

    def weight_copies(s, e):
        return [pltpu.make_async_copy(w_hbm.at[layer, e], stage.at[s], wsem_ref.at[s])
                for w_hbm, stage in ((wg_hbm, stage_g), (wu_hbm, stage_u), (wd_hbm, stage_d))]

    @pl.when(i == 0)
    def _():
        fetch(0, 0)
        for s in range(2):
            for copy in weight_copies(s, te_ref[s * n]):
                copy.start()

    for s in range(2):
        expert = te_ref[s * n + i]

        @pl.when(valid & ((i == 0) | (expert != te_ref[s * n + prev])))
        def _():
            for copy in weight_copies(s, expert):
                copy.wait()
            cache_g[s] = stage_g[s].astype(BF16)
            cache_u[s] = stage_u[s].astype(BF16)
            cache_d[s] = stage_d[s].astype(BF16)
            upcoming = nx_ref[s * n + i]

            @pl.when(upcoming >= 0)
            def _():
                for copy in weight_copies(s, upcoming):
                    copy.start()

    @pl.when(valid)
    def _():
        wait_rows(slot)
        x = _slabs_to_rows(xbuf_ref[slot])
        fetch(jnp.minimum(i + 1, n - 1), 1 - slot)
        xb = x.astype(BF16)
        s_lo = _sigmoid(jnp.sum(x * rwl_ref[...], axis=-1, keepdims=True))
        s_hi = _sigmoid(jnp.sum(x * rwh_ref[...], axis=-1, keepdims=True))
        inv = 1.0 / (s_lo + s_hi)

        def ffn(s):
            a = _dot(xb, cache_g[s])
            u = _dot(xb, cache_u[s])
            return _dot((a * _sigmoid(a) * u).astype(BF16), cache_d[s])

        y_ref[...] = _rows_to_slabs((s_lo * inv) * ffn(0) + (s_hi * inv) * ffn(1))

    @pl.when(jnp.logical_not(valid))
    def _():
        y_ref[...] = jnp.zeros_like(y_ref)

    @pl.when((valid & (i == n - 1)) | (jnp.logical_not(valid) & (tv_ref[prev] != 0)))
    def _():
        wait_rows(jnp.where(valid, 1 - slot, slot))


def experts(tile_expert, tile_valid, next_expert, src, rows, router_w_t, wg, wu, wd, layer):
    tg = EXPERT_TILE
    n_tiles = src.shape[0] // tg
    d, ff = wg.shape[2:]
    lo = lambda i, te, tv, nx, src: (te[i], 0, 0)
    hi = lambda i, te, tv, nx, src: (te[n_tiles + i], 0, 0)
    hbm = pl.BlockSpec(memory_space=pl.ANY)
    return pl.pallas_call(
        functools.partial(_expert_kernel, layer=layer),
        grid_spec=pltpu.PrefetchScalarGridSpec(
            num_scalar_prefetch=4,
            grid=(n_tiles,),
            in_specs=[hbm, pl.BlockSpec((None, 1, d), lo), pl.BlockSpec((None, 1, d), hi), hbm, hbm, hbm],
            out_specs=pl.BlockSpec((tg, ROW_SLABS, LANES), lambda i, te, tv, nx, src: (i, 0, 0)),
            scratch_shapes=[pltpu.VMEM((2, tg, ROW_SLABS, LANES), F32), pltpu.SemaphoreType.DMA((2,)),
                            pltpu.VMEM((2, d, ff), F32), pltpu.VMEM((2, d, ff), F32), pltpu.VMEM((2, ff, d), F32),
                            pltpu.VMEM((2, d, ff), BF16), pltpu.VMEM((2, d, ff), BF16), pltpu.VMEM((2, ff, d), BF16),
                            pltpu.SemaphoreType.DMA((2,))],
        ),
        out_shape=jax.ShapeDtypeStruct((n_tiles * tg, ROW_SLABS, LANES), F32),
        compiler_params=_cparams(("arbitrary",)),
        name="moe_experts",
    )(tile_expert, tile_valid, next_expert, src, rows, router_w_t, router_w_t, wg, wu, wd)


def _combine_kernel(dest_ref, hf_ref, x_ref, m_ref, wg_ref, wu_ref, wd_ref, y_hbm, o_ref, ybuf_ref, sem_ref):
    i = pl.program_id(0)
    n = pl.num_programs(0)
    tm = hf_ref.shape[0]
    slot = i % 2

    def fetch(tile, s):
        base = tile * tm
        for t in range(tm):
            pltpu.async_copy(y_hbm.at[dest_ref[base + t]], ybuf_ref.at[s, t], sem_ref.at[s], priority=t % 2)

    def wait_rows(s):
        pltpu.make_async_copy(y_hbm.at[pl.ds(0, tm)], ybuf_ref.at[s], sem_ref.at[s]).wait()

    @pl.when(i == 0)
    def _():
        fetch(0, 0)

    fetch(jnp.minimum(i + 1, n - 1), 1 - slot)
    h = hf_ref[...]
    a = _dot(h, wg_ref[...])
    u = _dot(h, wu_ref[...])
    shared = _dot((a * _sigmoid(a) * u).astype(BF16), wd_ref[...])
    wait_rows(slot)
    o_ref[...] = x_ref[...] + m_ref[G_F:G_F + 1, :] * (shared + _slabs_to_rows(ybuf_ref[slot]))

    @pl.when(i == n - 1)
    def _():
        wait_rows(1 - slot)


def combine(dest, hf, xs, mods, wg, wu, wd, y_sorted, *, tm, n_ctx_tiles):
    b, r, d = hf.shape
    nt = r // tm
    ff = wg.shape[1]
    row = lambda i, dest: (i // nt, i % nt, 0)
    const = lambda i, dest: (0, 0)
    return pl.pallas_call(
        _combine_kernel,
        grid_spec=pltpu.PrefetchScalarGridSpec(
            num_scalar_prefetch=1,
            grid=(b * nt,),
            in_specs=[
                pl.BlockSpec((None, tm, d), row),
                pl.BlockSpec((None, tm, d), row),
                pl.BlockSpec((None, None, 6, d),
                             lambda i, dest: (i // nt, ((i % nt) >= n_ctx_tiles).astype(jnp.int32), 0, 0)),
                pl.BlockSpec((d, ff), const), pl.BlockSpec((d, ff), const), pl.BlockSpec((ff, d), const),
                pl.BlockSpec(memory_space=pl.ANY),
            ],
            out_specs=pl.BlockSpec((None, tm, d), row),
            scratch_shapes=[pltpu.VMEM((2, tm, ROW_SLABS, LANES), F32), pltpu.SemaphoreType.DMA((2,))],
        ),
        out_shape=jax.ShapeDtypeStruct((b, r, d), F32),
        compiler_params=_cparams(("arbitrary",)),
        name="moe_combine",
    )(dest, hf, xs, mods, wg, wu, wd, y_sorted)


def moe_block(xs, mods, gain, router_w, router_bias, ew_gate, ew_up, ew_down, layer, sw_gate, sw_up, sw_down,
              *, tm, n_ctx_tiles):
    b, r, d = xs.shape
    n = b * r
    tg = EXPERT_TILE
    hf, rows, info, counts = route(xs, mods, gain, router_w, router_bias, tm=tm, n_ctx_tiles=n_ctx_tiles)
    cls = info[:, :, 0, :].reshape(n)
    rank = info[:, :, 1, :].reshape(n)
    padded = ((counts[:, 0].astype(jnp.int32) + tg - 1) // tg) * tg
    ends = jnp.cumsum(padded)
    dest = (ends - padded)[cls] + rank
    n_tiles = n // tg + N_CLASSES
    tile_start = jnp.arange(n_tiles, dtype=jnp.int32) * tg
    tile_valid = tile_start < ends[-1]
    last_start = jnp.maximum(ends[-1] - tg, 0)
    start = jnp.where(tile_valid, tile_start, last_start)
    tile_cls = jnp.sum((ends[None, :] <= start[:, None]).astype(jnp.int32), axis=1)
    tile_cls = jnp.minimum(tile_cls, N_CLASSES - 1)
    slot_expert = jnp.stack([jnp.asarray(CLASS_LO)[tile_cls], jnp.asarray(CLASS_HI)[tile_cls]])
    tile_expert = slot_expert.reshape(-1)
    later = (tile_start[None, :] > tile_start[:, None]) & tile_valid[None, :]
    differs = slot_expert[:, None, :] != slot_expert[:, :, None]
    first = jnp.min(jnp.where(later[None] & differs, jnp.arange(n_tiles)[None, None, :], n_tiles), axis=-1)
    next_expert = jnp.where(first < n_tiles, jnp.take_along_axis(slot_expert, jnp.minimum(first, n_tiles - 1), axis=1),
                            -1).reshape(-1).astype(jnp.int32)
    src = invert_permutation(dest, n_tiles * tg)
    y_sorted = experts(tile_expert, tile_valid.astype(jnp.int32), next_expert, src, rows,
                       router_w.T.astype(F32).reshape(N_EXPERTS, 1, d), ew_gate, ew_up, ew_down, layer)
    return combine(dest, hf, xs, mods, sw_gate.astype(BF16), sw_up.astype(BF16), sw_down.astype(BF16), y_sorted,
                   tm=tm, n_ctx_tiles=n_ctx_tiles)


def kernel(x, c, ctx, c_ctx, mod_w, mod_b, norm_mix, norm_ffn, even_w_in, even_w_out, diff_q_gain, diff_k_gain, diff_lambda, diff_subln, hgrn_lb_logits, hgrn_out_gain, odd_w_in, mla_q_a_gain, mla_kv_a_gain, mla_w_uq, mla_w_ukv, mla_q_nope_gain, mla_q_rope_gain, mla_k_nope_gain, mla_k_rope_gain, odd_w_out, router_w, router_bias, expert_w_gate, expert_w_up, expert_w_down, shared_w_gate, shared_w_up, shared_w_down):
    b, n_lat, d = x.shape
    n_ctx = ctx.shape[1]
    depth = mod_w.shape[0]
    tm = 256 if n_ctx % 256 == 0 else 128
    n_ctx_tiles = n_ctx // tm
    rope = rope_lane_tables(n_ctx, n_lat)

    mod_rows = 16
    c_rows = jnp.concatenate([c, c_ctx[None, :], jnp.zeros((mod_rows - b - 1, d), F32)], axis=0)
    mod_all = modulation(c_rows, mod_w, mod_b).reshape(depth, mod_rows, 6, d)
    lb_all = jnp.cumsum(jax.nn.softmax(hgrn_lb_logits.astype(F32), axis=0), axis=0)

    xs = jnp.concatenate([ctx, x], axis=1)
    for layer in range(depth):
        last = layer == depth - 1
        j = layer // 2
        mods = jnp.stack([jnp.broadcast_to(mod_all[layer, b], (b, 6, d)), mod_all[layer, :b]], axis=1)
        row_tile0 = n_ctx_tiles if last else 0
        if layer % 2 == 0:
            lam_init = 0.8 - 0.6 * math.exp(-0.3 * layer)
            p = norm_matmul(xs, norm_mix[layer], even_w_in[j].astype(BF16), tm=tm, mods=mods, n_ctx_tiles=n_ctx_tiles)
            oa = diff_attention(p, diff_lambda[j], diff_q_gain[j], diff_k_gain[j], diff_subln[j], rope,
                                tq=tm, n_ctx=n_ctx, lam_init=lam_init, q_blk0=0, k_blk0=4, v_blk0=8)
            ob = hgrn_bidirectional(p, lb_all[j, 0], lb_all[j, 1], hgrn_out_gain[j], n_ctx=n_ctx, blk0=12)
            if last:
                oa, ob = oa[:, n_ctx:], ob[:, n_ctx:]
            w_out = even_w_out[j].astype(BF16)
            half = oa.shape[-1]
            xs_mid = proj_residual([oa, ob], [w_out[:half], w_out[half:]], xs, mods, tm=tm,
                                   n_ctx_tiles=n_ctx_tiles, row_tile0=row_tile0)
        else:
            w_in = odd_w_in[j]
            zpad = jnp.zeros((d, LANES - MLA_ROPE), F32)
            w_in_r = jnp.concatenate([w_in[:, MLA_Q_LORA:MLA_Q_LORA + MLA_KV_LORA], w_in[:, MLA_Q_LORA + MLA_KV_LORA:],
                                      zpad, w_in[:, :MLA_Q_LORA]], axis=1).astype(BF16)
            p1 = norm_matmul(xs, norm_mix[layer], w_in_r, tm=tm, mods=mods, n_ctx_tiles=n_ctx_tiles)
            w_uq = mla_w_uq[j].reshape(MLA_Q_LORA, MLA_HEADS, MLA_NOPE + MLA_ROPE)
            w_uq = jnp.concatenate([w_uq, jnp.zeros((MLA_Q_LORA, MLA_HEADS, LANES - MLA_ROPE), F32)], axis=-1)
            w_uq = w_uq.reshape(MLA_Q_LORA, MLA_HEADS * 2 * LANES).astype(BF16)
            q = norm_matmul(p1, mla_q_a_gain[j], w_uq, tm=tm, col_block=1, row_tile0=row_tile0)
            kv = norm_matmul(p1, mla_kv_a_gain[j], mla_w_ukv[j].astype(BF16), tm=tm, col_block=0)
            if not last:
                raise NotImplementedError("context queries for a non-final latent-attention layer")
            o = mla_attention(q, kv, p1, 2, rope, mla_q_nope_gain[j], mla_q_rope_gain[j], mla_k_nope_gain[j],
                              mla_k_rope_gain[j], tq=2 * tm, n_ctx=n_ctx)
            xs_mid = proj_residual([o], [odd_w_out[j].astype(BF16)], xs, mods, tm=tm,
                                   n_ctx_tiles=n_ctx_tiles, row_tile0=row_tile0)
        xs = moe_block(xs_mid, mods, norm_ffn[layer], router_w, router_bias,
                       expert_w_gate, expert_w_up, expert_w_down, layer,
                       shared_w_gate[layer], shared_w_up[layer], shared_w_down[layer],
                       tm=tm, n_ctx_tiles=0 if last else n_ctx_tiles)
    return xs if xs.shape[1] == n_lat else xs[:, n_ctx:]
```

```python
import functools
import math

import numpy as np
import jax
import jax.numpy as jnp
from jax import lax
from jax.experimental import pallas as pl
from jax.experimental.pallas import tpu as pltpu

F32 = jnp.float32
BF16 = jnp.bfloat16

LANES = 128
VMEM_LIMIT = 56 * 1024 * 1024

GRID_W = 64
DIFF_HEADS = 4
DIFF_HEAD_DIM = 64
HGRN_HEADS = 4
HGRN_K_DIM = 128
MLA_HEADS = 8
MLA_NOPE = 128
MLA_ROPE = 64
MLA_V = 128
MLA_Q_LORA = 384
MLA_KV_LORA = 256
N_EXPERTS = 16
N_GROUPS = 4
EXPERTS_PER_GROUP = 4
ROPE_BASE = 10000.0
EPS = 1e-6
LOG2E = 1.4426950408889634
GLA_CHUNK = 64
ATTN_KEY_CHUNK = 1152
MLA_HEADS_PER_STEP = 2
DIFF_HEADS_PER_STEP = 2
HGRN_HEADS_PER_STEP = 2

SH_M, SC_M, G_M, SH_F, SC_F, G_F = range(6)


def _sigmoid(x):
    return 1.0 / (1.0 + jnp.exp(-x))


def _dot(a, b):
    return jnp.dot(a, b, preferred_element_type=F32)


def _dot_nt(a, b):
    return lax.dot_general(a, b, (((1,), (1,)), ((), ())), preferred_element_type=F32)


def _dot_tn(a, b):
    return lax.dot_general(a, b, (((0,), (0,)), ((), ())), preferred_element_type=F32)


def _split3(x):
    hi = x.astype(BF16)
    r = x - hi.astype(F32)
    mid = r.astype(BF16)
    lo = (r - mid.astype(F32)).astype(BF16)
    return hi, mid, lo


def _rms(x, width=None):
    n = x.shape[-1] if width is None else width
    return x * lax.rsqrt(jnp.sum(x * x, axis=-1, keepdims=True) * (1.0 / n) + EPS)


def _cparams(sem):
    return pltpu.CompilerParams(dimension_semantics=sem, vmem_limit_bytes=VMEM_LIMIT)


def _mod_kernel(c_ref, w_ref, b_ref, o_ref):
    c = c_ref[...]
    s = c * _sigmoid(c)
    o_ref[...] = _dot(s.astype(BF16), w_ref[...].astype(BF16)) + b_ref[...]


def modulation(c_rows, mod_w, mod_b):
    n_layers, d, n = mod_w.shape
    rows = c_rows.shape[0]
    tn = 1536
    return pl.pallas_call(
        _mod_kernel,
        grid=(n_layers, n // tn),
        in_specs=[
            pl.BlockSpec((rows, d), lambda l, j: (0, 0)),
            pl.BlockSpec((None, d, tn), lambda l, j: (l, 0, j)),
            pl.BlockSpec((None, 1, tn), lambda l, j: (l, 0, j)),
        ],
        out_specs=pl.BlockSpec((None, rows, tn), lambda l, j: (l, 0, j)),
        out_shape=jax.ShapeDtypeStruct((n_layers, rows, n), F32),
        compiler_params=_cparams(("arbitrary", "arbitrary")),
        name="modulation",
    )(c_rows, mod_w, mod_b.reshape(n_layers, 1, n))


def _norm_matmul_kernel(x_ref, m_ref, g_ref, w_ref, o_ref):
    h = _rms(x_ref[...]) * g_ref[...]
    h = h * (1.0 + m_ref[SC_M:SC_M + 1, :]) + m_ref[SH_M:SH_M + 1, :]
    o_ref[...] = _dot(h.astype(BF16), w_ref[...]).astype(o_ref.dtype)


def norm_matmul(x, mods, gain, w, *, tm, n_ctx_tiles):
    b, t, k = x.shape
    n = w.shape[1]
    return pl.pallas_call(
        _norm_matmul_kernel,
        grid=(b, t // tm),
        in_specs=[pl.BlockSpec((None, tm, k), lambda bi, ti: (bi, ti, 0)),
                  pl.BlockSpec((None, None, 6, k), lambda bi, ti: (bi, (ti >= n_ctx_tiles).astype(jnp.int32), 0, 0)),
                  pl.BlockSpec((1, k), lambda bi, ti: (0, 0)), pl.BlockSpec((k, n), lambda bi, ti: (0, 0))],
        out_specs=pl.BlockSpec((None, tm, n), lambda bi, ti: (bi, ti, 0)),
        out_shape=jax.ShapeDtypeStruct((b, t, n), BF16),
        compiler_params=_cparams(("arbitrary", "arbitrary")),
        name="norm_matmul",
    )(x, mods, gain.reshape(1, k).astype(F32), w)


def _mla_proj_kernel(x_ref, m_ref, g_ref, win_ref, qg_ref, kvg_ref, wuq_ref, wukv_ref, kr_ref, q_ref, kv_ref,
                     *, n_ctx_tiles):
    h = _rms(x_ref[...]) * g_ref[...]
    h = h * (1.0 + m_ref[SC_M:SC_M + 1, :]) + m_ref[SH_M:SH_M + 1, :]
    p = _dot(h.astype(BF16), win_ref[...])
    kr_ref[...] = p[:, MLA_KV_LORA:MLA_KV_LORA + LANES].astype(kr_ref.dtype)
    ckv = _rms(p[:, 0:MLA_KV_LORA]) * kvg_ref[...]
    kv_ref[...] = _dot(ckv.astype(BF16), wukv_ref[...]).astype(kv_ref.dtype)

    @pl.when(pl.program_id(1) >= n_ctx_tiles)
    def _():
        cq = _rms(p[:, MLA_KV_LORA + LANES:]) * qg_ref[...]
        q_ref[...] = _dot(cq.astype(BF16), wuq_ref[...]).astype(q_ref.dtype)

    @pl.when(pl.program_id(1) < n_ctx_tiles)
    def _():
        q_ref[...] = jnp.zeros_like(q_ref)


def mla_projections(xs, mods, gain, w_in_r, q_a_gain, kv_a_gain, w_uq, w_ukv, *, tm, n_ctx_tiles):
    b, t, d = xs.shape
    nt = t // tm
    n_lat = t - n_ctx_tiles * tm
    const = lambda a: pl.BlockSpec(a.shape, lambda bi, ti: (0, 0))
    row = lambda bi, ti: (bi, ti, 0)
    g2 = lambda g: g.reshape(1, -1).astype(F32)
    args = [g2(gain), w_in_r, g2(q_a_gain), g2(kv_a_gain), w_uq, w_ukv]
    return pl.pallas_call(
        functools.partial(_mla_proj_kernel, n_ctx_tiles=n_ctx_tiles),
        grid=(b, nt),
        in_specs=[pl.BlockSpec((None, tm, d), row),
                  pl.BlockSpec((None, None, 6, d), lambda bi, ti: (bi, (ti >= n_ctx_tiles).astype(jnp.int32), 0, 0))]
                 + [const(a) for a in args],
        out_specs=[pl.BlockSpec((None, tm, LANES), row),
                   pl.BlockSpec((None, tm, w_uq.shape[1]), lambda bi, ti: (bi, jnp.maximum(ti - n_ctx_tiles, 0), 0)),
                   pl.BlockSpec((None, tm, w_ukv.shape[1]), row)],
        out_shape=[jax.ShapeDtypeStruct((b, t, LANES), BF16),
                   jax.ShapeDtypeStruct((b, n_lat, w_uq.shape[1]), BF16),
                   jax.ShapeDtypeStruct((b, t, w_ukv.shape[1]), BF16)],
        compiler_params=_cparams(("arbitrary", "arbitrary")),
        name="mla_projections",
    )(xs, mods, *args)


def rope_lane_tables(n_ctx, n_lat):
    rows = n_lat // GRID_W
    row = np.repeat(np.arange(rows), GRID_W).astype(np.float32)
    col = np.tile(np.arange(GRID_W), rows).astype(np.float32)
    axis_dim = DIFF_HEAD_DIM // 2
    inv_freq = jnp.asarray(ROPE_BASE, F32) ** (-jnp.arange(0, axis_dim, 2, dtype=F32) / axis_dim)
    ang_r = jnp.asarray(row)[:, None] * inv_freq
    ang_c = jnp.asarray(col)[:, None] * inv_freq
    lane = np.arange(LANES)
    freq_idx = lane % 16
    use_col = (lane % 64) >= 32
    first = (lane % 32) < 16
    ang = jnp.where(use_col[None, :], ang_c[:, freq_idx], ang_r[:, freq_idx])
    cos, sin = jnp.cos(ang), jnp.sin(ang)
    c = jnp.concatenate([jnp.ones((n_ctx, LANES), F32), cos], axis=0)
    sa = jnp.concatenate([jnp.zeros((n_ctx, LANES), F32), jnp.where(first[None, :], -sin, 0.0)], axis=0)
    sb = jnp.concatenate([jnp.zeros((n_ctx, LANES), F32), jnp.where(first[None, :], 0.0, sin)], axis=0)
    return c, sa, sb


def _rope(x, c, sa, sb):
    return x * c + pltpu.roll(x, LANES - 16, 1) * sa + pltpu.roll(x, 16, 1) * sb


def _block_ones(width):
    r = lax.broadcasted_iota(jnp.int32, (LANES, LANES), 0) // width
    c = lax.broadcasted_iota(jnp.int32, (LANES, LANES), 1) // width
    return (r == c).astype(BF16)


def _block_rms(x, ones, width):
    xx = x * x
    hi = xx.astype(BF16)
    lo = (xx - hi.astype(F32)).astype(BF16)
    ms = (_dot(hi, ones) + _dot(lo, ones)) * (1.0 / width)
    return x * lax.rsqrt(ms + EPS)


def _attend_streams(qs, kt_refs, vs, nk):
    n = len(qs)
    chunk = ATTN_KEY_CHUNK if nk % ATTN_KEY_CHUNK == 0 else nk
    nc = nk // chunk
    s = [[None] * nc for _ in range(n)]
    p = [[None] * nc for _ in range(n)]
    m, l, o = [None] * n, [None] * n, [None] * n

    def logits(i, c):
        s[i][c] = _dot(qs[i], kt_refs[i][:, c * chunk:(c + 1) * chunk])
        mc = jnp.max(s[i][c], axis=-1, keepdims=True)
        m[i] = mc if c == 0 else jnp.maximum(m[i], mc)

    def exps(i, c):
        e = jnp.exp2(s[i][c] - m[i])
        lc = jnp.sum(e, axis=-1, keepdims=True)
        l[i] = lc if c == 0 else l[i] + lc
        p[i][c] = e.astype(BF16)

    def values(i, c):
        ref, col0 = vs[i]
        oc = _dot(p[i][c], ref[c * chunk:(c + 1) * chunk, col0:col0 + LANES])
        o[i] = oc if c == 0 else o[i] + oc

    for step in range(n + 2):
        for c in range(nc):
            if step < n:
                logits(step, c)
            if 0 <= step - 1 < n:
                exps(step - 1, c)
            if 0 <= step - 2 < n:
                values(step - 2, c)
    return list(zip(o, l))


def _diff_attn_kernel(lam_ref, q_ref, k_ref, v_ref, cq_ref, saq_ref, sbq_ref, ck_ref, sak_ref, sbk_ref,
                      qg_ref, kg_ref, sub_ref, o_ref, kt_ref, *, n_ctx, n_ctx_tiles, lam_init):
    qi = pl.program_id(2)
    ones = _block_ones(DIFF_HEAD_DIM)
    heads = range(DIFF_HEADS_PER_STEP)

    @pl.when(qi == 0)
    def _():
        for h in heads:
            k = _block_rms(k_ref[:, h * LANES:(h + 1) * LANES].astype(F32), ones, DIFF_HEAD_DIM) * kg_ref[...]
            kt_ref[h] = _rope(k, ck_ref[...], sak_ref[...], sbk_ref[...]).T.astype(BF16)

    qs = []
    for h in heads:
        q = _block_rms(q_ref[:, h * LANES:(h + 1) * LANES].astype(F32), ones, DIFF_HEAD_DIM) * qg_ref[...]
        q = _rope(q, cq_ref[...], saq_ref[...], sbq_ref[...]) * (DIFF_HEAD_DIM ** -0.5 * LOG2E)
        lane = lax.broadcasted_iota(jnp.int32, q.shape, 1)
        qs.append(jnp.where(lane < DIFF_HEAD_DIM, q, 0.0).astype(BF16))
        qs.append(jnp.where(lane >= DIFF_HEAD_DIM, q, 0.0).astype(BF16))
    lv = lam_ref[...]
    lam = (jnp.exp(jnp.sum(lv[0:1] * lv[1:2], axis=-1, keepdims=True))
           - jnp.exp(jnp.sum(lv[2:3] * lv[3:4], axis=-1, keepdims=True)) + lam_init)

    def attend(nk):
        outs = _attend_streams(qs, [kt_ref.at[h] for h in heads for _ in range(2)],
                               [(v_ref, h * LANES) for h in heads for _ in range(2)], nk)
        for h in heads:
            (o1, l1), (o2, l2) = outs[2 * h], outs[2 * h + 1]
            o = o1 * (1.0 / l1) - o2 * (lam / l2)
            o_ref[:, h * LANES:(h + 1) * LANES] = (_rms(o) * sub_ref[...] * (1.0 - lam_init)).astype(o_ref.dtype)

    @pl.when(qi < n_ctx_tiles)
    def _():
        attend(n_ctx)

    @pl.when(qi >= n_ctx_tiles)
    def _():
        attend(k_ref.shape[0])


def diff_attention(p, lam_vecs, q_gain, k_gain, subln, rope, *, tq, n_ctx, lam_init, q_blk0, k_blk0, v_blk0):
    b, t, _ = p.shape
    c, sa, sb = rope
    nq = t // tq
    hps = DIFF_HEADS_PER_STEP
    assert q_blk0 % hps == 0 and k_blk0 % hps == 0 and v_blk0 % hps == 0
    row_q = lambda bi, h, qi: (qi, 0)
    full = lambda bi, h, qi: (0, 0)
    tile2 = lambda g: jnp.tile(g.astype(F32), 2).reshape(1, LANES)
    return pl.pallas_call(
        functools.partial(_diff_attn_kernel, n_ctx=n_ctx, n_ctx_tiles=n_ctx // tq, lam_init=lam_init),
        grid=(b, DIFF_HEADS // hps, nq),
        in_specs=[
            pl.BlockSpec((4, DIFF_HEAD_DIM), full),
            pl.BlockSpec((None, tq, hps * LANES), lambda bi, h, qi: (bi, qi, q_blk0 // hps + h)),
            pl.BlockSpec((None, t, hps * LANES), lambda bi, h, qi: (bi, 0, k_blk0 // hps + h)),
            pl.BlockSpec((None, t, hps * LANES), lambda bi, h, qi: (bi, 0, v_blk0 // hps + h)),
            pl.BlockSpec((tq, LANES), row_q), pl.BlockSpec((tq, LANES), row_q), pl.BlockSpec((tq, LANES), row_q),
            pl.BlockSpec((t, LANES), full), pl.BlockSpec((t, LANES), full), pl.BlockSpec((t, LANES), full),
            pl.BlockSpec((1, LANES), full), pl.BlockSpec((1, LANES), full), pl.BlockSpec((1, LANES), full),
        ],
        out_specs=pl.BlockSpec((None, tq, hps * LANES), lambda bi, h, qi: (bi, qi, h)),
        out_shape=jax.ShapeDtypeStruct((b, t, DIFF_HEADS * LANES), BF16),
        scratch_shapes=[pltpu.VMEM((hps, LANES, t), BF16)],
        compiler_params=_cparams(("arbitrary", "arbitrary", "arbitrary")),
        name="diff_attention",
    )(lam_vecs.astype(F32), p, p, p, c, sa, sb, c, sa, sb, tile2(q_gain), tile2(k_gain),
      subln.astype(F32).reshape(1, LANES))


def _gla_constants(c):
    levels = int(math.log2(c))
    t = np.arange(c)[:, None]
    u = np.arange(c)[None, :]
    stack = [[u <= t], [u >= t]]
    qside = [[], []]
    pair = [[t == u], [t == u]]
    for lv in range(1, levels + 1):
        base = (t >> lv) << lv
        half = 1 << (lv - 1)
        stack[0].append(u <= base + half - 1)
        stack[1].append(u >= base + half)
        up_t = ((t >> (lv - 1)) & 1) == 1
        up_u = ((u >> (lv - 1)) & 1) == 1
        same = (t >> lv) == (u >> lv)
        qside[0].append(np.broadcast_to(up_t, (c, LANES)))
        qside[1].append(np.broadcast_to(~up_t, (c, LANES)))
        pair[0].append(same & up_t & ~up_u)
        pair[1].append(same & ~up_t & up_u)
    f32 = lambda x: np.asarray(x, np.float32)
    stack = np.stack([np.tile(np.concatenate(f32(m), axis=0), (1, 3)) for m in stack])
    return (jnp.asarray(stack, BF16), jnp.asarray(np.stack([f32(m) for m in qside])),
            jnp.asarray(np.stack([f32(m) for m in pair])))


def _gla_chunk(q, k, v, g2, s, stack, qside_ref, pair_ref, d):
    c = q.shape[0]
    levels = int(math.log2(c))
    gcat = jnp.concatenate(_split3(g2), axis=0)
    cs = _dot(stack, gcat)
    tot_col = _dot_tn(gcat, jnp.ones((3 * c, LANES), BF16))
    diag = jnp.sum(q * k, axis=-1, keepdims=True)
    yield
    cum = cs[0:c]
    tot = cum[0:1] if d == 1 else cum[c - 1:c]
    zz = []
    for lv in range(1, levels + 1):
        e = jnp.exp2(-jnp.abs(cum - cs[lv * c:(lv + 1) * c]))
        z = (jnp.where(qside_ref[d, lv - 1] > 0.5, q, k) * e).astype(BF16)
        zz.append(_dot_nt(z, z))
    q_in = (q * jnp.exp2(cum)).astype(BF16)
    ks = (k * jnp.exp2(tot - cum)).astype(BF16)
    ds = _dot_tn(ks, v)
    yield
    a = pair_ref[d, 0] * diag
    for lv in range(1, levels + 1):
        a = a + pair_ref[d, lv] * zz[lv - 1]
    lhs = jnp.concatenate([q_in, a.astype(BF16)], axis=1)
    o = _dot(lhs, jnp.concatenate([s.astype(BF16), v], axis=0))
    s_new = s * jnp.exp2(tot_col) + ds
    yield
    return o, s_new


def _run_interleaved(gens):
    results = [None] * len(gens)
    live = list(range(len(gens)))
    while live:
        for i in list(live):
            try:
                next(gens[i])
            except StopIteration as stop:
                results[i] = stop.value
                live.remove(i)
    return results


def _hgrn_kernel(qz_ref, zf_ref, zb_ref, v_ref, gz_ref, lbf_ref, lbb_ref, og_ref, stack_ref, qside_ref, pair_ref,
                 o_ref, q_ref, kf_ref, gf_ref, kb_ref, gb_ref, of_ref, ob_ref, st_ref, *, n_ctx_chunks):
    c = GLA_CHUNK
    t = qz_ref.shape[0]
    n = t // c
    heads = range(HGRN_HEADS_PER_STEP)
    qz = qz_ref[...].astype(F32)
    q_ref[...] = qz * _sigmoid(qz) * (HGRN_K_DIM ** -0.5)
    for z_ref, lb_ref, k_ref, g_ref in ((zf_ref, lbf_ref, kf_ref, gf_ref), (zb_ref, lbb_ref, kb_ref, gb_ref)):
        lb = jnp.concatenate([lb_ref[h] for h in heads], axis=-1)
        f = lb + (1.0 - lb) * _sigmoid(z_ref[...].astype(F32))
        k_ref[...] = 1.0 - f
        g_ref[...] = jnp.log(f) * LOG2E
    st_ref[...] = jnp.zeros_like(st_ref)

    def body(i, carry):
        rf = pl.multiple_of(i * c, c)
        cb = jnp.where(i < n_ctx_chunks, n_ctx_chunks - 1 - i, n - 1 - i + n_ctx_chunks)
        rb = pl.multiple_of(cb * c, c)
        chains = [(h, d, pl.ds(r0, c), k_ref, g_ref, out_ref) for h in heads
                  for d, r0, k_ref, g_ref, out_ref in ((0, rf, kf_ref, gf_ref, of_ref), (1, rb, kb_ref, gb_ref, ob_ref))]
        col = lambda h: slice(h * LANES, (h + 1) * LANES)
        outs = _run_interleaved([
            _gla_chunk(q_ref[rows, col(h)], k_ref[rows, col(h)], v_ref[rows, col(h)], g_ref[rows, col(h)],
                       st_ref[h, d], stack_ref[d], qside_ref, pair_ref, d)
            for (h, d, rows, k_ref, g_ref, _) in chains])
        for (h, d, rows, _, _, out_ref), (o, s_new) in zip(chains, outs):
            out_ref[rows, col(h)] = o
            st_ref[h, d] = s_new
        return carry

    lax.fori_loop(0, n, body, 0)
    gz = gz_ref[...].astype(F32)
    for h in heads:
        cols = slice(h * LANES, (h + 1) * LANES)
        o = _rms(of_ref[:, cols] + ob_ref[:, cols]) * og_ref[...]
        o_ref[:, cols] = (o * (gz[:, cols] * _sigmoid(gz[:, cols]))).astype(o_ref.dtype)


def hgrn_bidirectional(p, lb_fwd, lb_bwd, out_gain, *, n_ctx, blk0):
    b, t, _ = p.shape
    c = GLA_CHUNK
    consts = _gla_constants(c)
    h = HGRN_HEADS
    hps = HGRN_HEADS_PER_STEP
    assert blk0 % hps == 0 and h % hps == 0
    seg = lambda s: pl.BlockSpec((None, t, hps * LANES), lambda bi, hi: (bi, 0, (blk0 + s * h) // hps + hi))
    per_head = pl.BlockSpec((hps, 1, LANES), lambda bi, hi: (hi, 0, 0))
    const = lambda a: pl.BlockSpec(a.shape, lambda bi, hi: (0,) * a.ndim)
    seq = pltpu.VMEM((t, hps * LANES), F32)
    return pl.pallas_call(
        functools.partial(_hgrn_kernel, n_ctx_chunks=n_ctx // c),
        grid=(b, h // hps),
        in_specs=[seg(0), seg(1), seg(2), seg(3), seg(4), per_head, per_head,
                  pl.BlockSpec((1, LANES), lambda bi, hi: (0, 0))] + [const(a) for a in consts],
        out_specs=pl.BlockSpec((None, t, hps * LANES), lambda bi, hi: (bi, 0, hi)),
        out_shape=jax.ShapeDtypeStruct((b, t, h * LANES), BF16),
        scratch_shapes=[seq, seq, seq, seq, seq, seq, seq, pltpu.VMEM((hps, 2, LANES, LANES), F32)],
        compiler_params=_cparams(("arbitrary", "arbitrary")),
        name="hgrn_bidirectional",
    )(p, p, p, p, p, lb_fwd.reshape(h, 1, LANES), lb_bwd.reshape(h, 1, LANES),
      out_gain.astype(F32).reshape(1, LANES), *consts)


def _mla_attn_kernel(q_ref, kv_ref, kr_ref, cq_ref, saq_ref, sbq_ref, ck_ref, sak_ref, sbk_ref,
                     qn_ref, qr_ref, kn_ref, krg_ref, o_ref, kt_ref):
    qi = pl.program_id(2)
    ones_nope = _block_ones(MLA_NOPE)
    ones_rope = _block_ones(MLA_ROPE)
    hw = 2 * LANES

    @pl.when(qi == 0)
    def _():
        kr = _block_rms(kr_ref[...].astype(F32), ones_rope, MLA_ROPE) * krg_ref[...]
        kr = _rope(kr, ck_ref[...], sak_ref[...], sbk_ref[...]).T.astype(BF16)
        for h in range(MLA_HEADS_PER_STEP):
            kn = _block_rms(kv_ref[:, h * hw:h * hw + MLA_NOPE].astype(F32), ones_nope, MLA_NOPE) * kn_ref[...]
            kt_ref[h, 0:LANES, :] = kn.T.astype(BF16)
            kt_ref[h, LANES:hw, :] = kr

    scale = (MLA_NOPE + MLA_ROPE) ** -0.5 * LOG2E
    qs = []
    for h in range(MLA_HEADS_PER_STEP):
        qn = _block_rms(q_ref[:, h * hw:h * hw + MLA_NOPE].astype(F32), ones_nope, MLA_NOPE) * (qn_ref[...] * scale)
        qr = _block_rms(q_ref[:, h * hw + MLA_NOPE:(h + 1) * hw].astype(F32), ones_rope, MLA_ROPE) * qr_ref[...]
        qr = _rope(qr, cq_ref[...], saq_ref[...], sbq_ref[...]) * scale
        qs.append(jnp.concatenate([qn.astype(BF16), qr.astype(BF16)], axis=-1))
    heads = range(MLA_HEADS_PER_STEP)
    outs = _attend_streams(qs, [kt_ref.at[h] for h in heads], [(kv_ref, h * hw + MLA_NOPE) for h in heads],
                           kt_ref.shape[2])
    for h, (o, l) in enumerate(outs):
        o_ref[:, h * MLA_V:(h + 1) * MLA_V] = (o * (1.0 / l)).astype(o_ref.dtype)


def mla_attention(q, kv, p1, kr_blk, rope, qn_gain, qr_gain, kn_gain, kr_gain, *, tq, n_ctx):
    b, n_lat, _ = q.shape
    t = kv.shape[1]
    c, sa, sb = rope
    hps = MLA_HEADS_PER_STEP
    row_q = lambda bi, h, qi: (qi, 0)
    full = lambda bi, h, qi: (0, 0)
    pad = lambda g: jnp.concatenate([g.astype(F32), jnp.zeros((LANES - g.shape[0],), F32)]).reshape(1, LANES)
    return pl.pallas_call(
        _mla_attn_kernel,
        grid=(b, MLA_HEADS // hps, n_lat // tq),
        in_specs=[
            pl.BlockSpec((None, tq, hps * 2 * LANES), lambda bi, h, qi: (bi, qi, h)),
            pl.BlockSpec((None, t, hps * 2 * LANES), lambda bi, h, qi: (bi, 0, h)),
            pl.BlockSpec((None, t, LANES), lambda bi, h, qi: (bi, 0, kr_blk)),
            pl.BlockSpec((tq, LANES), row_q), pl.BlockSpec((tq, LANES), row_q), pl.BlockSpec((tq, LANES), row_q),
            pl.BlockSpec((t, LANES), full), pl.BlockSpec((t, LANES), full), pl.BlockSpec((t, LANES), full),
            pl.BlockSpec((1, LANES), full), pl.BlockSpec((1, LANES), full),
            pl.BlockSpec((1, LANES), full), pl.BlockSpec((1, LANES), full),
        ],
        out_specs=pl.BlockSpec((None, tq, hps * MLA_V), lambda bi, h, qi: (bi, qi, h)),
        out_shape=jax.ShapeDtypeStruct((b, n_lat, MLA_HEADS * MLA_V), BF16),
        scratch_shapes=[pltpu.VMEM((hps, 2 * LANES, t), BF16)],
        compiler_params=_cparams(("arbitrary", "arbitrary", "arbitrary")),
        name="mla_attention",
    )(q, kv, p1, c[n_ctx:], sa[n_ctx:], sb[n_ctx:], c, sa, sb, pad(qn_gain), pad(qr_gain), pad(kn_gain), pad(kr_gain))


def _proj_residual_kernel(*refs, n_in):
    a_refs = refs[:n_in]
    w_refs = refs[n_in:2 * n_in]
    x_ref, m_ref, o_ref = refs[2 * n_in:]
    acc = _dot(a_refs[0][...], w_refs[0][...])
    for a_ref, w_ref in zip(a_refs[1:], w_refs[1:]):
        acc = acc + _dot(a_ref[...], w_ref[...])
    o_ref[...] = x_ref[...] + m_ref[G_M:G_M + 1, :] * acc


def proj_residual(acts, weights, xs, mods, *, tm, n_ctx_tiles, row_tile0=0):
    b, r, _ = acts[0].shape
    d = xs.shape[-1]
    in_specs = [pl.BlockSpec((None, tm, a.shape[-1]), lambda bi, ti: (bi, ti, 0)) for a in acts]
    in_specs += [pl.BlockSpec(w.shape, lambda bi, ti: (0, 0)) for w in weights]
    in_specs += [
        pl.BlockSpec((None, tm, d), lambda bi, ti: (bi, ti + row_tile0, 0)),
        pl.BlockSpec((None, None, 6, d), lambda bi, ti: (bi, ((ti + row_tile0) >= n_ctx_tiles).astype(jnp.int32), 0, 0)),
    ]
    return pl.pallas_call(
        functools.partial(_proj_residual_kernel, n_in=len(acts)),
        grid=(b, r // tm),
        in_specs=in_specs,
        out_specs=pl.BlockSpec((None, tm, d), lambda bi, ti: (bi, ti, 0)),
        out_shape=jax.ShapeDtypeStruct((b, r, d), F32),
        compiler_params=_cparams(("arbitrary", "arbitrary")),
        name="proj_residual",
    )(*acts, *weights, xs, mods)


PAIRS = [(i, j) for i in range(EXPERTS_PER_GROUP) for j in range(i + 1, EXPERTS_PER_GROUP)]
N_CLASSES = N_GROUPS * len(PAIRS)
CLASS_LO = np.array([EXPERTS_PER_GROUP * g + i for g in range(N_GROUPS) for (i, j) in PAIRS], np.int32)
CLASS_HI = np.array([EXPERTS_PER_GROUP * g + j for g in range(N_GROUPS) for (i, j) in PAIRS], np.int32)
EXPERT_TILE = 256
ROW_SLABS = 8
DMA_UNROLL = 8
ROW_SLOTS = 3


def _route_kernel(x_ref, m_ref, g_ref, rw_ref, rb_ref, hf_ref, rows_ref, info_ref, count_ref, cnt_ref):
    h = _rms(x_ref[...]) * g_ref[...]
    h = h * (1.0 + m_ref[SC_F:SC_F + 1, :]) + m_ref[SH_F:SH_F + 1, :]
    hf_ref[...] = h.astype(hf_ref.dtype)
    rows_ref[...] = _rows_to_slabs(h)
    h1, h2, _ = _split3(h)
    w1, w2, _ = _split3(rw_ref[...])
    logits = _dot_nt(w1, h1) + (_dot_nt(w1, h2) + _dot_nt(w2, h1))
    biased = _sigmoid(logits) + rb_ref[...]
    row = [biased[e:e + 1, :] for e in range(N_EXPERTS)]
    gscore = []
    for g in range(N_GROUPS):
        m = row[4 * g:4 * g + 4]
        gscore.append(functools.reduce(jnp.maximum, [m[i] + m[j] for (i, j) in PAIRS]))
    hits = []
    for g in range(N_GROUPS):
        best = None
        for g2 in range(N_GROUPS):
            if g2 == g:
                continue
            wins = (gscore[g] > gscore[g2]) if g2 < g else (gscore[g] >= gscore[g2])
            best = wins if best is None else jnp.logical_and(best, wins)
        chosen = []
        for i in range(EXPERTS_PER_GROUP):
            rank = None
            for j in range(EXPERTS_PER_GROUP):
                if j == i:
                    continue
                mi, mj = row[4 * g + i], row[4 * g + j]
                ahead = ((mj >= mi) if j < i else (mj > mi)).astype(jnp.int32)
                rank = ahead if rank is None else rank + ahead
            chosen.append(rank < 2)
        for (i, j) in PAIRS:
            hits.append(jnp.where(best & chosen[i] & chosen[j], 1.0, 0.0))
    onehot = jnp.concatenate(hits, axis=0)
    tm = onehot.shape[1]
    upper = (lax.broadcasted_iota(jnp.int32, (tm, tm), 0) <= lax.broadcasted_iota(jnp.int32, (tm, tm), 1))
    prefix = _dot(onehot.astype(BF16), upper.astype(BF16))

    @pl.when((pl.program_id(0) == 0) & (pl.program_id(1) == 0))
    def _():
        cnt_ref[...] = jnp.zeros_like(cnt_ref)

    seen = cnt_ref[...]
    cls_id = lax.broadcasted_iota(jnp.int32, onehot.shape, 0).astype(F32)
    cls = jnp.sum(onehot * cls_id, axis=0, keepdims=True)
    rank = jnp.sum(onehot * (seen[:, 0:1] + prefix - 1.0), axis=0, keepdims=True)
    info_ref[...] = jnp.concatenate([cls, rank, jnp.zeros((6, tm), F32)], axis=0).astype(jnp.int32)
    seen = seen + jnp.sum(onehot, axis=1, keepdims=True)
    cnt_ref[...] = seen
    count_ref[...] = seen


def route(xs, mods, gain, router_w, router_bias, *, tm, n_ctx_tiles):
    b, t, d = xs.shape
    nt = t // tm
    return pl.pallas_call(
        _route_kernel,
        grid=(b, nt),
        in_specs=[
            pl.BlockSpec((None, tm, d), lambda bi, ti: (bi, ti, 0)),
            pl.BlockSpec((None, None, 6, d), lambda bi, ti: (bi, (ti >= n_ctx_tiles).astype(jnp.int32), 0, 0)),
            pl.BlockSpec((1, d), lambda bi, ti: (0, 0)),
            pl.BlockSpec((N_EXPERTS, d), lambda bi, ti: (0, 0)),
            pl.BlockSpec((N_EXPERTS, 1), lambda bi, ti: (0, 0)),
        ],
        out_specs=[
            pl.BlockSpec((None, tm, d), lambda bi, ti: (bi, ti, 0)),
            pl.BlockSpec((tm, ROW_SLABS, LANES), lambda bi, ti: (bi * nt + ti, 0, 0)),
            pl.BlockSpec((None, None, 8, tm), lambda bi, ti: (bi, ti, 0, 0)),
            pl.BlockSpec((N_CLASSES, LANES), lambda bi, ti: (0, 0)),
        ],
        out_shape=[jax.ShapeDtypeStruct((b, t, d), BF16),
                   jax.ShapeDtypeStruct((b * t, ROW_SLABS, LANES), F32),
                   jax.ShapeDtypeStruct((b, nt, 8, tm), jnp.int32),
                   jax.ShapeDtypeStruct((N_CLASSES, LANES), F32)],
        scratch_shapes=[pltpu.VMEM((N_CLASSES, LANES), F32)],
        compiler_params=_cparams(("arbitrary", "arbitrary")),
        name="route",
    )(xs, mods, gain.reshape(1, d).astype(F32), router_w.T.astype(F32),
      router_bias.reshape(N_EXPERTS, 1).astype(F32))


def _transpose8(parts):
    sub = lax.broadcasted_iota(jnp.int32, parts[0].shape, 1)
    for s in (4, 2, 1):
        keep = (sub & s) == 0
        new = list(parts)
        for i in range(8):
            if i & s:
                continue
            a, b = parts[i], parts[i | s]
            new[i] = jnp.where(keep, a, pltpu.roll(b, s, 1))
            new[i | s] = jnp.where(keep, pltpu.roll(a, 8 - s, 1), b)
        parts = new
    return parts


def _slabs_to_rows(slabs):
    g = slabs.shape[0] // 8
    x4 = slabs.reshape(g, 8, ROW_SLABS, LANES)
    parts = _transpose8([x4[:, t] for t in range(8)])
    return jnp.concatenate([p.reshape(g * 8, LANES) for p in parts], axis=-1)


def _rows_to_slabs(x):
    g = x.shape[0] // 8
    parts = _transpose8([x[:, k * LANES:(k + 1) * LANES].reshape(g, 8, LANES) for k in range(ROW_SLABS)])
    return jnp.stack(parts, axis=1).reshape(g * 8, ROW_SLABS, LANES)


def _invert_kernel(dest_ref, src_ref):
    def clear(i, carry):
        for u in range(2 * DMA_UNROLL):
            src_ref[i * 2 * DMA_UNROLL + u] = 0
        return carry

    def put(i, carry):
        for u in range(2 * DMA_UNROLL):
            t = i * 2 * DMA_UNROLL + u
            src_ref[dest_ref[t]] = t
        return carry

    lax.fori_loop(0, src_ref.shape[0] // (2 * DMA_UNROLL), clear, 0)
    lax.fori_loop(0, dest_ref.shape[0] // (2 * DMA_UNROLL), put, 0)


def invert_permutation(dest, n_rows_out):
    smem = pl.BlockSpec(memory_space=pltpu.SMEM)
    return pl.pallas_call(
        _invert_kernel,
        in_specs=[smem],
        out_specs=smem,
        out_shape=jax.ShapeDtypeStruct((n_rows_out,), jnp.int32),
        name="moe_invert",
    )(dest)


def _expert_kernel(te_ref, tv_ref, nx_ref, src_ref, rows_hbm, rwl_ref, rwh_ref, wg_hbm, wu_hbm, wd_hbm, y_ref,
                   xbuf_ref, sem_ref, stage_g, stage_u, stage_d, cache_g, cache_u, cache_d, wsem_ref, *, layer):
    i = pl.program_id(0)
    n = pl.num_programs(0)
    tg = y_ref.shape[0]
    slot = i % ROW_SLOTS
    valid = tv_ref[i] != 0
    prev = jnp.maximum(i - 1, 0)

    def fetch(tile, s):
        base = tile * tg
        for t in range(tg):
            pltpu.async_copy(rows_hbm.at[src_ref[base + t]], xbuf_ref.at[s, t], sem_ref.at[s], priority=t % 2)

    def wait_rows(s):
        pltpu.make_async_copy(rows_hbm.at[pl.ds(0, tg)], xbuf_ref.at[s], sem_ref.at[s]).wait()

    def weight_copies(s, e):
        return [pltpu.make_async_copy(w_hbm.at[layer, e], stage.at[s], wsem_ref.at[s])
                for w_hbm, stage in ((wg_hbm, stage_g), (wu_hbm, stage_u), (wd_hbm, stage_d))]

    @pl.when(i == 0)
    def _():
        fetch(0, 0)
        fetch(jnp.minimum(1, n - 1), 1)
        for s in range(2):
            for copy in weight_copies(s, te_ref[s * n]):
                copy.start()

    for s in range(2):
        expert = te_ref[s * n + i]

        @pl.when(valid & ((i == 0) | (expert != te_ref[s * n + prev])))
        def _():
            for copy in weight_copies(s, expert):
                copy.wait()
            cache_g[s] = stage_g[s].astype(BF16)
            cache_u[s] = stage_u[s].astype(BF16)
            cache_d[s] = stage_d[s].astype(BF16)
            upcoming = nx_ref[s * n + i]

            @pl.when(upcoming >= 0)
            def _():
                for copy in weight_copies(s, upcoming):
                    copy.start()

    @pl.when(valid)
    def _():
        wait_rows(slot)
        x = _slabs_to_rows(xbuf_ref[slot])
        fetch(jnp.minimum(i + 2, n - 1), (i + 2) % ROW_SLOTS)
        xb = x.astype(BF16)
        s_lo = _sigmoid(jnp.sum(x * rwl_ref[...], axis=-1, keepdims=True))
        s_hi = _sigmoid(jnp.sum(x * rwh_ref[...], axis=-1, keepdims=True))
        inv = 1.0 / (s_lo + s_hi)

        def ffn(s):
            a = _dot(xb, cache_g[s])
            u = _dot(xb, cache_u[s])
            return _dot((a * _sigmoid(a) * u).astype(BF16), cache_d[s])

        y_ref[...] = _rows_to_slabs((s_lo * inv) * ffn(0) + (s_hi * inv) * ffn(1))

    @pl.when(jnp.logical_not(valid))
    def _():
        y_ref[...] = jnp.zeros_like(y_ref)

    @pl.when((valid & (i == n - 1)) | (jnp.logical_not(valid) & (tv_ref[prev] != 0)))
    def _():
        first = jnp.where(valid, i + 1, i)
        wait_rows(first % ROW_SLOTS)
        wait_rows((first + 1) % ROW_SLOTS)


def experts(tile_expert, tile_valid, next_expert, src, rows, router_w_t, wg, wu, wd, layer):
    tg = EXPERT_TILE
    n_tiles = src.shape[0] // tg
    d, ff = wg.shape[2:]
    lo = lambda i, te, tv, nx, src: (te[i], 0, 0)
    hi = lambda i, te, tv, nx, src: (te[n_tiles + i], 0, 0)
    hbm = pl.BlockSpec(memory_space=pl.ANY)
    return pl.pallas_call(
        functools.partial(_expert_kernel, layer=layer),
        grid_spec=pltpu.PrefetchScalarGridSpec(
            num_scalar_prefetch=4,
            grid=(n_tiles,),
            in_specs=[hbm, pl.BlockSpec((None, 1, d), lo), pl.BlockSpec((None, 1, d), hi), hbm, hbm, hbm],
            out_specs=pl.BlockSpec((tg, ROW_SLABS, LANES), lambda i, te, tv, nx, src: (i, 0, 0)),
            scratch_shapes=[pltpu.VMEM((ROW_SLOTS, tg, ROW_SLABS, LANES), F32), pltpu.SemaphoreType.DMA((ROW_SLOTS,)),
                            pltpu.VMEM((2, d, ff), F32), pltpu.VMEM((2, d, ff), F32), pltpu.VMEM((2, ff, d), F32),
                            pltpu.VMEM((2, d, ff), BF16), pltpu.VMEM((2, d, ff), BF16), pltpu.VMEM((2, ff, d), BF16),
                            pltpu.SemaphoreType.DMA((2,))],
        ),
        out_shape=jax.ShapeDtypeStruct((n_tiles * tg, ROW_SLABS, LANES), F32),
        compiler_params=_cparams(("arbitrary",)),
        name="moe_experts",
    )(tile_expert, tile_valid, next_expert, src, rows, router_w_t, router_w_t, wg, wu, wd)


def _combine_kernel(dest_ref, hf_ref, x_ref, m_ref, wg_ref, wu_ref, wd_ref, y_hbm, o_ref, ybuf_ref, sem_ref):
    i = pl.program_id(0)
    n = pl.num_programs(0)
    tm = hf_ref.shape[0]
    slot = i % 2

    def fetch(tile, s):
        base = tile * tm
        for t in range(tm):
            pltpu.async_copy(y_hbm.at[dest_ref[base + t]], ybuf_ref.at[s, t], sem_ref.at[s], priority=t % 2)

    def wait_rows(s):
        pltpu.make_async_copy(y_hbm.at[pl.ds(0, tm)], ybuf_ref.at[s], sem_ref.at[s]).wait()

    @pl.when(i == 0)
    def _():
        fetch(0, 0)

    fetch(jnp.minimum(i + 1, n - 1), 1 - slot)
    h = hf_ref[...]
    a = _dot(h, wg_ref[...])
    u = _dot(h, wu_ref[...])
    shared = _dot((a * _sigmoid(a) * u).astype(BF16), wd_ref[...])
    wait_rows(slot)
    o_ref[...] = x_ref[...] + m_ref[G_F:G_F + 1, :] * (shared + _slabs_to_rows(ybuf_ref[slot]))

    @pl.when(i == n - 1)
    def _():
        wait_rows(1 - slot)


def combine(dest, hf, xs, mods, wg, wu, wd, y_sorted, *, tm, n_ctx_tiles):
    b, r, d = hf.shape
    nt = r // tm
    ff = wg.shape[1]
    row = lambda i, dest: (i // nt, i % nt, 0)
    const = lambda i, dest: (0, 0)
    return pl.pallas_call(
        _combine_kernel,
        grid_spec=pltpu.PrefetchScalarGridSpec(
            num_scalar_prefetch=1,
            grid=(b * nt,),
            in_specs=[
                pl.BlockSpec((None, tm, d), row),
                pl.BlockSpec((None, tm, d), row),
                pl.BlockSpec((None, None, 6, d),
                             lambda i, dest: (i // nt, ((i % nt) >= n_ctx_tiles).astype(jnp.int32), 0, 0)),
                pl.BlockSpec((d, ff), const), pl.BlockSpec((d, ff), const), pl.BlockSpec((ff, d), const),
                pl.BlockSpec(memory_space=pl.ANY),
            ],
            out_specs=pl.BlockSpec((None, tm, d), row),
            scratch_shapes=[pltpu.VMEM((2, tm, ROW_SLABS, LANES), F32), pltpu.SemaphoreType.DMA((2,))],
        ),
        out_shape=jax.ShapeDtypeStruct((b, r, d), F32),
        compiler_params=_cparams(("arbitrary",)),
        name="moe_combine",
    )(dest, hf, xs, mods, wg, wu, wd, y_sorted)


def moe_block(xs, mods, gain, router_w, router_bias, ew_gate, ew_up, ew_down, layer, sw_gate, sw_up, sw_down,
              *, tm, n_ctx_tiles):
    b, r, d = xs.shape
    n = b * r
    tg = EXPERT_TILE
    hf, rows, info, counts = route(xs, mods, gain, router_w, router_bias, tm=tm, n_ctx_tiles=n_ctx_tiles)
    cls = info[:, :, 0, :].reshape(n)
    rank = info[:, :, 1, :].reshape(n)
    padded = ((counts[:, 0].astype(jnp.int32) + tg - 1) // tg) * tg
    ends = jnp.cumsum(padded)
    dest = (ends - padded)[cls] + rank
    n_tiles = n // tg + N_CLASSES
    tile_start = jnp.arange(n_tiles, dtype=jnp.int32) * tg
    tile_valid = tile_start < ends[-1]
    last_start = jnp.maximum(ends[-1] - tg, 0)
    start = jnp.where(tile_valid, tile_start, last_start)
    tile_cls = jnp.sum((ends[None, :] <= start[:, None]).astype(jnp.int32), axis=1)
    tile_cls = jnp.minimum(tile_cls, N_CLASSES - 1)
    slot_expert = jnp.stack([jnp.asarray(CLASS_LO)[tile_cls], jnp.asarray(CLASS_HI)[tile_cls]])
    tile_expert = slot_expert.reshape(-1)
    later = (tile_start[None, :] > tile_start[:, None]) & tile_valid[None, :]
    differs = slot_expert[:, None, :] != slot_expert[:, :, None]
    first = jnp.min(jnp.where(later[None] & differs, jnp.arange(n_tiles)[None, None, :], n_tiles), axis=-1)
    next_expert = jnp.where(first < n_tiles, jnp.take_along_axis(slot_expert, jnp.minimum(first, n_tiles - 1), axis=1),
                            -1).reshape(-1).astype(jnp.int32)
    src = invert_permutation(dest, n_tiles * tg)
    y_sorted = experts(tile_expert, tile_valid.astype(jnp.int32), next_expert, src, rows,
                       router_w.T.astype(F32).reshape(N_EXPERTS, 1, d), ew_gate, ew_up, ew_down, layer)
    return combine(dest, hf, xs, mods, sw_gate.astype(BF16), sw_up.astype(BF16), sw_down.astype(BF16), y_sorted,
                   tm=tm, n_ctx_tiles=n_ctx_tiles)


def kernel(x, c, ctx, c_ctx, mod_w, mod_b, norm_mix, norm_ffn, even_w_in, even_w_out, diff_q_gain, diff_k_gain, diff_lambda, diff_subln, hgrn_lb_logits, hgrn_out_gain, odd_w_in, mla_q_a_gain, mla_kv_a_gain, mla_w_uq, mla_w_ukv, mla_q_nope_gain, mla_q_rope_gain, mla_k_nope_gain, mla_k_rope_gain, odd_w_out, router_w, router_bias, expert_w_gate, expert_w_up, expert_w_down, shared_w_gate, shared_w_up, shared_w_down):
    b, n_lat, d = x.shape
    n_ctx = ctx.shape[1]
    depth = mod_w.shape[0]
    tm = 256 if n_ctx % 256 == 0 else 128
    n_ctx_tiles = n_ctx // tm
    rope = rope_lane_tables(n_ctx, n_lat)

    mod_rows = 16
    c_rows = jnp.concatenate([c, c_ctx[None, :], jnp.zeros((mod_rows - b - 1, d), F32)], axis=0)
    mod_all = modulation(c_rows, mod_w, mod_b).reshape(depth, mod_rows, 6, d)
    lb_all = jnp.cumsum(jax.nn.softmax(hgrn_lb_logits.astype(F32), axis=0), axis=0)

    xs = jnp.concatenate([ctx, x], axis=1)
    for layer in range(depth):
        last = layer == depth - 1
        j = layer // 2
        mods = jnp.stack([jnp.broadcast_to(mod_all[layer, b], (b, 6, d)), mod_all[layer, :b]], axis=1)
        row_tile0 = n_ctx_tiles if last else 0
        if layer % 2 == 0:
            lam_init = 0.8 - 0.6 * math.exp(-0.3 * layer)
            p = norm_matmul(xs, mods, norm_mix[layer], even_w_in[j].astype(BF16), tm=tm, n_ctx_tiles=n_ctx_tiles)
            oa = diff_attention(p, diff_lambda[j], diff_q_gain[j], diff_k_gain[j], diff_subln[j], rope,
                                tq=tm, n_ctx=n_ctx, lam_init=lam_init, q_blk0=0, k_blk0=4, v_blk0=8)
            ob = hgrn_bidirectional(p, lb_all[j, 0], lb_all[j, 1], hgrn_out_gain[j], n_ctx=n_ctx, blk0=12)
            if last:
                oa, ob = oa[:, n_ctx:], ob[:, n_ctx:]
            w_out = even_w_out[j].astype(BF16)
            half = oa.shape[-1]
            xs_mid = proj_residual([oa, ob], [w_out[:half], w_out[half:]], xs, mods, tm=tm,
                                   n_ctx_tiles=n_ctx_tiles, row_tile0=row_tile0)
        else:
            w_in = odd_w_in[j]
            zpad = jnp.zeros((d, LANES - MLA_ROPE), F32)
            w_in_r = jnp.concatenate([w_in[:, MLA_Q_LORA:MLA_Q_LORA + MLA_KV_LORA], w_in[:, MLA_Q_LORA + MLA_KV_LORA:],
                                      zpad, w_in[:, :MLA_Q_LORA]], axis=1).astype(BF16)
            w_uq = mla_w_uq[j].reshape(MLA_Q_LORA, MLA_HEADS, MLA_NOPE + MLA_ROPE)
            w_uq = jnp.concatenate([w_uq, jnp.zeros((MLA_Q_LORA, MLA_HEADS, LANES - MLA_ROPE), F32)], axis=-1)
            w_uq = w_uq.reshape(MLA_Q_LORA, MLA_HEADS * 2 * LANES).astype(BF16)
            if not last:
                raise NotImplementedError("context queries for a non-final latent-attention layer")
            kr, q, kv = mla_projections(xs, mods, norm_mix[layer], w_in_r, mla_q_a_gain[j], mla_kv_a_gain[j], w_uq,
                                        mla_w_ukv[j].astype(BF16), tm=tm, n_ctx_tiles=n_ctx_tiles)
            o = mla_attention(q, kv, kr, 0, rope, mla_q_nope_gain[j], mla_q_rope_gain[j], mla_k_nope_gain[j],
                              mla_k_rope_gain[j], tq=2 * tm, n_ctx=n_ctx)
            xs_mid = proj_residual([o], [odd_w_out[j].astype(BF16)], xs, mods, tm=tm,
                                   n_ctx_tiles=n_ctx_tiles, row_tile0=row_tile0)
        xs = moe_block(xs_mid, mods, norm_ffn[layer], router_w, router_bias,
                       expert_w_gate, expert_w_up, expert_w_down, layer,
                       shared_w_gate[layer], shared_w_up[layer], shared_w_down[layer],
                       tm=tm, n_ctx_tiles=0 if last else n_ctx_tiles)
    return xs if xs.shape[1] == n_lat else xs[:, n_ctx:]
```

```python
import functools
import math

import numpy as np
import jax
import jax.numpy as jnp
from jax import lax
from jax.experimental import pallas as pl
from jax.experimental.pallas import tpu as pltpu

F32 = jnp.float32
BF16 = jnp.bfloat16

LANES = 128
VMEM_LIMIT = 56 * 1024 * 1024

GRID_W = 64
DIFF_HEADS = 4
DIFF_HEAD_DIM = 64
HGRN_HEADS = 4
HGRN_K_DIM = 128
MLA_HEADS = 8
MLA_NOPE = 128
MLA_ROPE = 64
MLA_V = 128
MLA_Q_LORA = 384
MLA_KV_LORA = 256
N_EXPERTS = 16
N_GROUPS = 4
EXPERTS_PER_GROUP = 4
ROPE_BASE = 10000.0
EPS = 1e-6
LOG2E = 1.4426950408889634
GLA_CHUNK = 64
ATTN_KEY_CHUNK = 1152
MLA_HEADS_PER_STEP = 2
DIFF_HEADS_PER_STEP = 2
HGRN_HEADS_PER_STEP = 2

SH_M, SC_M, G_M, SH_F, SC_F, G_F = range(6)


def _sigmoid(x):
    return 1.0 / (1.0 + jnp.exp(-x))


def _dot(a, b):
    return jnp.dot(a, b, preferred_element_type=F32)


def _dot_nt(a, b):
    return lax.dot_general(a, b, (((1,), (1,)), ((), ())), preferred_element_type=F32)


def _dot_tn(a, b):
    return lax.dot_general(a, b, (((0,), (0,)), ((), ())), preferred_element_type=F32)


def _split3(x):
    hi = x.astype(BF16)
    r = x - hi.astype(F32)
    mid = r.astype(BF16)
    lo = (r - mid.astype(F32)).astype(BF16)
    return hi, mid, lo


def _rms(x, width=None):
    n = x.shape[-1] if width is None else width
    return x * lax.rsqrt(jnp.sum(x * x, axis=-1, keepdims=True) * (1.0 / n) + EPS)


def _cparams(sem):
    return pltpu.CompilerParams(dimension_semantics=sem, vmem_limit_bytes=VMEM_LIMIT)


def _mod_kernel(c_ref, w_ref, b_ref, o_ref):
    c = c_ref[...]
    s = c * _sigmoid(c)
    o_ref[...] = _dot(s.astype(BF16), w_ref[...].astype(BF16)) + b_ref[...]


def modulation(c_rows, mod_w, mod_b):
    n_layers, d, n = mod_w.shape
    rows = c_rows.shape[0]
    tn = 1536
    return pl.pallas_call(
        _mod_kernel,
        grid=(n_layers, n // tn),
        in_specs=[
            pl.BlockSpec((rows, d), lambda l, j: (0, 0)),
            pl.BlockSpec((None, d, tn), lambda l, j: (l, 0, j)),
            pl.BlockSpec((None, 1, tn), lambda l, j: (l, 0, j)),
        ],
        out_specs=pl.BlockSpec((None, rows, tn), lambda l, j: (l, 0, j)),
        out_shape=jax.ShapeDtypeStruct((n_layers, rows, n), F32),
        compiler_params=_cparams(("arbitrary", "arbitrary")),
        name="modulation",
    )(c_rows, mod_w, mod_b.reshape(n_layers, 1, n))


def _stream_specs(xs, tm, n_ctx_tiles, row_tile0=0):
    parts = xs if isinstance(xs, tuple) else (xs,)
    d = parts[0].shape[-1]
    if len(parts) == 1:
        return [pl.BlockSpec((None, tm, d), lambda bi, ti: (bi, ti + row_tile0, 0))], list(parts)
    return [pl.BlockSpec((None, tm, d), lambda bi, ti: (bi, jnp.minimum(ti + row_tile0, n_ctx_tiles - 1), 0)),
            pl.BlockSpec((None, tm, d), lambda bi, ti: (bi, jnp.maximum(ti + row_tile0 - n_ctx_tiles, 0), 0))], list(parts)


def _stream_tile(x_refs, n_ctx_tiles, row_tile0=0):
    if len(x_refs) == 1:
        return x_refs[0][...]
    return jnp.where(pl.program_id(1) + row_tile0 < n_ctx_tiles, x_refs[0][...], x_refs[1][...])


def _norm_matmul_kernel(*refs, n_x, n_ctx_tiles):
    m_ref, g_ref, w_ref, o_ref = refs[n_x:]
    h = _rms(_stream_tile(refs[:n_x], n_ctx_tiles)) * g_ref[...]
    h = h * (1.0 + m_ref[SC_M:SC_M + 1, :]) + m_ref[SH_M:SH_M + 1, :]
    o_ref[...] = _dot(h.astype(BF16), w_ref[...]).astype(o_ref.dtype)


def norm_matmul(xs, mods, gain, w, *, tm, n_ctx_tiles):
    x_specs, x_args = _stream_specs(xs, tm, n_ctx_tiles)
    b = x_args[0].shape[0]
    t = sum(a.shape[1] for a in x_args)
    k, n = w.shape
    return pl.pallas_call(
        functools.partial(_norm_matmul_kernel, n_x=len(x_args), n_ctx_tiles=n_ctx_tiles),
        grid=(b, t // tm),
        in_specs=x_specs + [
            pl.BlockSpec((None, None, 6, k), lambda bi, ti: (bi, (ti >= n_ctx_tiles).astype(jnp.int32), 0, 0)),
            pl.BlockSpec((1, k), lambda bi, ti: (0, 0)), pl.BlockSpec((k, n), lambda bi, ti: (0, 0))],
        out_specs=pl.BlockSpec((None, tm, n), lambda bi, ti: (bi, ti, 0)),
        out_shape=jax.ShapeDtypeStruct((b, t, n), BF16),
        compiler_params=_cparams(("arbitrary", "arbitrary")),
        name="norm_matmul",
    )(*x_args, mods, gain.reshape(1, k).astype(F32), w)


def _mla_proj_kernel(x_ref, m_ref, g_ref, win_ref, qg_ref, kvg_ref, wuq_ref, wukv_ref, kr_ref, q_ref, kv_ref,
                     *, n_ctx_tiles):
    h = _rms(x_ref[...]) * g_ref[...]
    h = h * (1.0 + m_ref[SC_M:SC_M + 1, :]) + m_ref[SH_M:SH_M + 1, :]
    p = _dot(h.astype(BF16), win_ref[...])
    kr_ref[...] = p[:, MLA_KV_LORA:MLA_KV_LORA + LANES].astype(kr_ref.dtype)
    ckv = _rms(p[:, 0:MLA_KV_LORA]) * kvg_ref[...]
    kv_ref[...] = _dot(ckv.astype(BF16), wukv_ref[...]).astype(kv_ref.dtype)

    @pl.when(pl.program_id(1) >= n_ctx_tiles)
    def _():
        cq = _rms(p[:, MLA_KV_LORA + LANES:]) * qg_ref[...]
        q_ref[...] = _dot(cq.astype(BF16), wuq_ref[...]).astype(q_ref.dtype)

    @pl.when(pl.program_id(1) < n_ctx_tiles)
    def _():
        q_ref[...] = jnp.zeros_like(q_ref)


def mla_projections(xs, mods, gain, w_in_r, q_a_gain, kv_a_gain, w_uq, w_ukv, *, tm, n_ctx_tiles):
    b, t, d = xs.shape
    nt = t // tm
    n_lat = t - n_ctx_tiles * tm
    const = lambda a: pl.BlockSpec(a.shape, lambda bi, ti: (0, 0))
    row = lambda bi, ti: (bi, ti, 0)
    g2 = lambda g: g.reshape(1, -1).astype(F32)
    args = [g2(gain), w_in_r, g2(q_a_gain), g2(kv_a_gain), w_uq, w_ukv]
    return pl.pallas_call(
        functools.partial(_mla_proj_kernel, n_ctx_tiles=n_ctx_tiles),
        grid=(b, nt),
        in_specs=[pl.BlockSpec((None, tm, d), row),
                  pl.BlockSpec((None, None, 6, d), lambda bi, ti: (bi, (ti >= n_ctx_tiles).astype(jnp.int32), 0, 0))]
                 + [const(a) for a in args],
        out_specs=[pl.BlockSpec((None, tm, LANES), row),
                   pl.BlockSpec((None, tm, w_uq.shape[1]), lambda bi, ti: (bi, jnp.maximum(ti - n_ctx_tiles, 0), 0)),
                   pl.BlockSpec((None, tm, w_ukv.shape[1]), row)],
        out_shape=[jax.ShapeDtypeStruct((b, t, LANES), BF16),
                   jax.ShapeDtypeStruct((b, n_lat, w_uq.shape[1]), BF16),
                   jax.ShapeDtypeStruct((b, t, w_ukv.shape[1]), BF16)],
        compiler_params=_cparams(("arbitrary", "arbitrary")),
        name="mla_projections",
    )(xs, mods, *args)


def rope_lane_tables(n_ctx, n_lat):
    rows = n_lat // GRID_W
    row = np.repeat(np.arange(rows), GRID_W).astype(np.float32)
    col = np.tile(np.arange(GRID_W), rows).astype(np.float32)
    axis_dim = DIFF_HEAD_DIM // 2
    inv_freq = jnp.asarray(ROPE_BASE, F32) ** (-jnp.arange(0, axis_dim, 2, dtype=F32) / axis_dim)
    ang_r = jnp.asarray(row)[:, None] * inv_freq
    ang_c = jnp.asarray(col)[:, None] * inv_freq
    lane = np.arange(LANES)
    freq_idx = lane % 16
    use_col = (lane % 64) >= 32
    first = (lane % 32) < 16
    ang = jnp.where(use_col[None, :], ang_c[:, freq_idx], ang_r[:, freq_idx])
    cos, sin = jnp.cos(ang), jnp.sin(ang)
    c = jnp.concatenate([jnp.ones((n_ctx, LANES), F32), cos], axis=0)
    sa = jnp.concatenate([jnp.zeros((n_ctx, LANES), F32), jnp.where(first[None, :], -sin, 0.0)], axis=0)
    sb = jnp.concatenate([jnp.zeros((n_ctx, LANES), F32), jnp.where(first[None, :], 0.0, sin)], axis=0)
    return c, sa, sb


def _rope(x, c, sa, sb):
    return x * c + pltpu.roll(x, LANES - 16, 1) * sa + pltpu.roll(x, 16, 1) * sb


def _block_ones(width):
    r = lax.broadcasted_iota(jnp.int32, (LANES, LANES), 0) // width
    c = lax.broadcasted_iota(jnp.int32, (LANES, LANES), 1) // width
    return (r == c).astype(BF16)


def _block_rms(x, ones, width):
    xx = x * x
    hi = xx.astype(BF16)
    lo = (xx - hi.astype(F32)).astype(BF16)
    ms = (_dot(hi, ones) + _dot(lo, ones)) * (1.0 / width)
    return x * lax.rsqrt(ms + EPS)


def _attend_streams(qs, kt_refs, vo_refs, nk):
    n = len(qs)
    chunk = ATTN_KEY_CHUNK if nk % ATTN_KEY_CHUNK == 0 else nk
    nc = nk // chunk
    s = [[None] * nc for _ in range(n)]
    p = [[None] * nc for _ in range(n)]
    m, ol = [None] * n, [None] * n

    def logits(i, c):
        s[i][c] = _dot(qs[i], kt_refs[i][:, c * chunk:(c + 1) * chunk])
        mc = jnp.max(s[i][c], axis=-1, keepdims=True)
        m[i] = mc if c == 0 else jnp.maximum(m[i], mc)

    def exps(i, c):
        p[i][c] = jnp.exp2(s[i][c] - m[i]).astype(BF16)

    def values(i, c):
        oc = _dot(p[i][c], vo_refs[i][c * chunk:(c + 1) * chunk, :])
        ol[i] = oc if c == 0 else ol[i] + oc

    for step in range(n + 2):
        for c in range(nc):
            if step < n:
                logits(step, c)
            if 0 <= step - 1 < n:
                exps(step - 1, c)
            if 0 <= step - 2 < n:
                values(step - 2, c)
    return [(x[:, 0:LANES], x[:, LANES:LANES + 1]) for x in ol]


def _diff_attn_kernel(lam_ref, q_ref, k_ref, v_ref, cq_ref, saq_ref, sbq_ref, ck_ref, sak_ref, sbk_ref,
                      qg_ref, kg_ref, sub_ref, o_ref, kt_ref, vo_ref, *, n_ctx, n_ctx_tiles, lam_init):
    qi = pl.program_id(2)
    ones = _block_ones(DIFF_HEAD_DIM)
    heads = range(DIFF_HEADS_PER_STEP)

    @pl.when(qi == 0)
    def _():
        for h in heads:
            k = _block_rms(k_ref[:, h * LANES:(h + 1) * LANES].astype(F32), ones, DIFF_HEAD_DIM) * kg_ref[...]
            kt_ref[h] = _rope(k, ck_ref[...], sak_ref[...], sbk_ref[...]).T.astype(BF16)
            vo_ref[h, :, 0:LANES] = v_ref[:, h * LANES:(h + 1) * LANES]
            vo_ref[h, :, LANES:2 * LANES] = jnp.ones((v_ref.shape[0], LANES), BF16)

    qs = []
    for h in heads:
        q = _block_rms(q_ref[:, h * LANES:(h + 1) * LANES].astype(F32), ones, DIFF_HEAD_DIM) * qg_ref[...]
        q = _rope(q, cq_ref[...], saq_ref[...], sbq_ref[...]) * (DIFF_HEAD_DIM ** -0.5 * LOG2E)
        lane = lax.broadcasted_iota(jnp.int32, q.shape, 1)
        qs.append(jnp.where(lane < DIFF_HEAD_DIM, q, 0.0).astype(BF16))
        qs.append(jnp.where(lane >= DIFF_HEAD_DIM, q, 0.0).astype(BF16))
    lv = lam_ref[...]
    lam = (jnp.exp(jnp.sum(lv[0:1] * lv[1:2], axis=-1, keepdims=True))
           - jnp.exp(jnp.sum(lv[2:3] * lv[3:4], axis=-1, keepdims=True)) + lam_init)

    def attend(nk):
        outs = _attend_streams(qs, [kt_ref.at[h] for h in heads for _ in range(2)],
                               [vo_ref.at[h] for h in heads for _ in range(2)], nk)
        for h in heads:
            (o1, l1), (o2, l2) = outs[2 * h], outs[2 * h + 1]
            o = o1 * (1.0 / l1) - o2 * (lam / l2)
            o_ref[:, h * LANES:(h + 1) * LANES] = (_rms(o) * sub_ref[...] * (1.0 - lam_init)).astype(o_ref.dtype)

    @pl.when(qi < n_ctx_tiles)
    def _():
        attend(n_ctx)

    @pl.when(qi >= n_ctx_tiles)
    def _():
        attend(k_ref.shape[0])


def diff_attention(p, lam_vecs, q_gain, k_gain, subln, rope, *, tq, n_ctx, lam_init, q_blk0, k_blk0, v_blk0):
    b, t, _ = p.shape
    c, sa, sb = rope
    nq = t // tq
    hps = DIFF_HEADS_PER_STEP
    assert q_blk0 % hps == 0 and k_blk0 % hps == 0 and v_blk0 % hps == 0
    row_q = lambda bi, h, qi: (qi, 0)
    full = lambda bi, h, qi: (0, 0)
    tile2 = lambda g: jnp.tile(g.astype(F32), 2).reshape(1, LANES)
    return pl.pallas_call(
        functools.partial(_diff_attn_kernel, n_ctx=n_ctx, n_ctx_tiles=n_ctx // tq, lam_init=lam_init),
        grid=(b, DIFF_HEADS // hps, nq),
        in_specs=[
            pl.BlockSpec((4, DIFF_HEAD_DIM), full),
            pl.BlockSpec((None, tq, hps * LANES), lambda bi, h, qi: (bi, qi, q_blk0 // hps + h)),
            pl.BlockSpec((None, t, hps * LANES), lambda bi, h, qi: (bi, 0, k_blk0 // hps + h)),
            pl.BlockSpec((None, t, hps * LANES), lambda bi, h, qi: (bi, 0, v_blk0 // hps + h)),
            pl.BlockSpec((tq, LANES), row_q), pl.BlockSpec((tq, LANES), row_q), pl.BlockSpec((tq, LANES), row_q),
            pl.BlockSpec((t, LANES), full), pl.BlockSpec((t, LANES), full), pl.BlockSpec((t, LANES), full),
            pl.BlockSpec((1, LANES), full), pl.BlockSpec((1, LANES), full), pl.BlockSpec((1, LANES), full),
        ],
        out_specs=pl.BlockSpec((None, tq, hps * LANES), lambda bi, h, qi: (bi, qi, h)),
        out_shape=jax.ShapeDtypeStruct((b, t, DIFF_HEADS * LANES), BF16),
        scratch_shapes=[pltpu.VMEM((hps, LANES, t), BF16), pltpu.VMEM((hps, t, 2 * LANES), BF16)],
        compiler_params=_cparams(("arbitrary", "arbitrary", "arbitrary")),
        name="diff_attention",
    )(lam_vecs.astype(F32), p, p, p, c, sa, sb, c, sa, sb, tile2(q_gain), tile2(k_gain),
      subln.astype(F32).reshape(1, LANES))


def _gla_constants(c):
    levels = int(math.log2(c))
    t = np.arange(c)[:, None]
    u = np.arange(c)[None, :]
    stack = [[u <= t], [u >= t]]
    qside = [[], []]
    pair = [[t == u], [t == u]]
    for lv in range(1, levels + 1):
        base = (t >> lv) << lv
        half = 1 << (lv - 1)
        stack[0].append(u <= base + half - 1)
        stack[1].append(u >= base + half)
        up_t = ((t >> (lv - 1)) & 1) == 1
        up_u = ((u >> (lv - 1)) & 1) == 1
        same = (t >> lv) == (u >> lv)
        qside[0].append(np.broadcast_to(up_t, (c, LANES)))
        qside[1].append(np.broadcast_to(~up_t, (c, LANES)))
        pair[0].append(same & up_t & ~up_u)
        pair[1].append(same & ~up_t & up_u)
    f32 = lambda x: np.asarray(x, np.float32)
    stack = np.stack([np.tile(np.concatenate(f32(m), axis=0), (1, 3)) for m in stack])
    return (jnp.asarray(stack, BF16), jnp.asarray(np.stack([f32(m) for m in qside])),
            jnp.asarray(np.stack([f32(m) for m in pair])))


def _gla_chunk(q, k, v, g2, s, stack, qside_ref, pair_ref, d):
    c = q.shape[0]
    levels = int(math.log2(c))
    gcat = jnp.concatenate(_split3(g2), axis=0)
    cs = _dot(stack, gcat)
    tot_col = _dot_tn(gcat, jnp.ones((3 * c, LANES), BF16))
    diag = jnp.sum(q * k, axis=-1, keepdims=True)
    yield
    cum = cs[0:c]
    tot = cum[0:1] if d == 1 else cum[c - 1:c]
    zz = []
    for lv in range(1, levels + 1):
        e = jnp.exp2(-jnp.abs(cum - cs[lv * c:(lv + 1) * c]))
        z = (jnp.where(qside_ref[d, lv - 1] > 0.5, q, k) * e).astype(BF16)
        zz.append(_dot_nt(z, z))
    q_in = (q * jnp.exp2(cum)).astype(BF16)
    ks = (k * jnp.exp2(tot - cum)).astype(BF16)
    ds = _dot_tn(ks, v)
    yield
    a = pair_ref[d, 0] * diag
    for lv in range(1, levels + 1):
        a = a + pair_ref[d, lv] * zz[lv - 1]
    lhs = jnp.concatenate([q_in, a.astype(BF16)], axis=1)
    o = _dot(lhs, jnp.concatenate([s.astype(BF16), v], axis=0))
    s_new = s * jnp.exp2(tot_col) + ds
    yield
    return o, s_new


def _run_interleaved(gens):
    results = [None] * len(gens)
    live = list(range(len(gens)))
    while live:
        for i in list(live):
            try:
                next(gens[i])
            except StopIteration as stop:
                results[i] = stop.value
                live.remove(i)
    return results


def _hgrn_kernel(qz_ref, zf_ref, zb_ref, v_ref, gz_ref, lbf_ref, lbb_ref, og_ref, stack_ref, qside_ref, pair_ref,
                 o_ref, q_ref, kf_ref, gf_ref, kb_ref, gb_ref, of_ref, ob_ref, st_ref, *, n_ctx_chunks):
    c = GLA_CHUNK
    t = qz_ref.shape[0]
    n = t // c
    heads = range(HGRN_HEADS_PER_STEP)
    qz = qz_ref[...].astype(F32)
    q_ref[...] = qz * _sigmoid(qz) * (HGRN_K_DIM ** -0.5)
    for z_ref, lb_ref, k_ref, g_ref in ((zf_ref, lbf_ref, kf_ref, gf_ref), (zb_ref, lbb_ref, kb_ref, gb_ref)):
        lb = jnp.concatenate([lb_ref[h] for h in heads], axis=-1)
        f = lb + (1.0 - lb) * _sigmoid(z_ref[...].astype(F32))
        k_ref[...] = 1.0 - f
        g_ref[...] = jnp.log(f) * LOG2E
    st_ref[...] = jnp.zeros_like(st_ref)

    def body(i, carry):
        rf = pl.multiple_of(i * c, c)
        cb = jnp.where(i < n_ctx_chunks, n_ctx_chunks - 1 - i, n - 1 - i + n_ctx_chunks)
        rb = pl.multiple_of(cb * c, c)
        chains = [(h, d, pl.ds(r0, c), k_ref, g_ref, out_ref) for h in heads
                  for d, r0, k_ref, g_ref, out_ref in ((0, rf, kf_ref, gf_ref, of_ref), (1, rb, kb_ref, gb_ref, ob_ref))]
        col = lambda h: slice(h * LANES, (h + 1) * LANES)
        outs = _run_interleaved([
            _gla_chunk(q_ref[rows, col(h)], k_ref[rows, col(h)], v_ref[rows, col(h)], g_ref[rows, col(h)],
                       st_ref[h, d], stack_ref[d], qside_ref, pair_ref, d)
            for (h, d, rows, k_ref, g_ref, _) in chains])
        for (h, d, rows, _, _, out_ref), (o, s_new) in zip(chains, outs):
            out_ref[rows, col(h)] = o
            st_ref[h, d] = s_new
        return carry

    lax.fori_loop(0, n, body, 0)
    gz = gz_ref[...].astype(F32)
    for h in heads:
        cols = slice(h * LANES, (h + 1) * LANES)
        o = _rms(of_ref[:, cols] + ob_ref[:, cols]) * og_ref[...]
        o_ref[:, cols] = (o * (gz[:, cols] * _sigmoid(gz[:, cols]))).astype(o_ref.dtype)


def hgrn_bidirectional(p, lb_fwd, lb_bwd, out_gain, *, n_ctx, blk0):
    b, t, _ = p.shape
    c = GLA_CHUNK
    consts = _gla_constants(c)
    h = HGRN_HEADS
    hps = HGRN_HEADS_PER_STEP
    assert blk0 % hps == 0 and h % hps == 0
    seg = lambda s: pl.BlockSpec((None, t, hps * LANES), lambda bi, hi: (bi, 0, (blk0 + s * h) // hps + hi))
    per_head = pl.BlockSpec((hps, 1, LANES), lambda bi, hi: (hi, 0, 0))
    const = lambda a: pl.BlockSpec(a.shape, lambda bi, hi: (0,) * a.ndim)
    seq = pltpu.VMEM((t, hps * LANES), F32)
    return pl.pallas_call(
        functools.partial(_hgrn_kernel, n_ctx_chunks=n_ctx // c),
        grid=(b, h // hps),
        in_specs=[seg(0), seg(1), seg(2), seg(3), seg(4), per_head, per_head,
                  pl.BlockSpec((1, LANES), lambda bi, hi: (0, 0))] + [const(a) for a in consts],
        out_specs=pl.BlockSpec((None, t, hps * LANES), lambda bi, hi: (bi, 0, hi)),
        out_shape=jax.ShapeDtypeStruct((b, t, h * LANES), BF16),
        scratch_shapes=[seq, seq, seq, seq, seq, seq, seq, pltpu.VMEM((hps, 2, LANES, LANES), F32)],
        compiler_params=_cparams(("arbitrary", "arbitrary")),
        name="hgrn_bidirectional",
    )(p, p, p, p, p, lb_fwd.reshape(h, 1, LANES), lb_bwd.reshape(h, 1, LANES),
      out_gain.astype(F32).reshape(1, LANES), *consts)


def _mla_attn_kernel(q_ref, kv_ref, kr_ref, cq_ref, saq_ref, sbq_ref, ck_ref, sak_ref, sbk_ref,
                     qn_ref, qr_ref, kn_ref, krg_ref, o_ref, kt_ref, vo_ref):
    qi = pl.program_id(2)
    ones_nope = _block_ones(MLA_NOPE)
    ones_rope = _block_ones(MLA_ROPE)
    hw = 2 * LANES

    @pl.when(qi == 0)
    def _():
        kr = _block_rms(kr_ref[...].astype(F32), ones_rope, MLA_ROPE) * krg_ref[...]
        kr = _rope(kr, ck_ref[...], sak_ref[...], sbk_ref[...]).T.astype(BF16)
        for h in range(MLA_HEADS_PER_STEP):
            kn = _block_rms(kv_ref[:, h * hw:h * hw + MLA_NOPE].astype(F32), ones_nope, MLA_NOPE) * kn_ref[...]
            kt_ref[h, 0:LANES, :] = kn.T.astype(BF16)
            kt_ref[h, LANES:hw, :] = kr
            vo_ref[h, :, 0:LANES] = kv_ref[:, h * hw + MLA_NOPE:(h + 1) * hw]
            vo_ref[h, :, LANES:2 * LANES] = jnp.ones((kv_ref.shape[0], LANES), BF16)

    scale = (MLA_NOPE + MLA_ROPE) ** -0.5 * LOG2E
    qs = []
    for h in range(MLA_HEADS_PER_STEP):
        qn = _block_rms(q_ref[:, h * hw:h * hw + MLA_NOPE].astype(F32), ones_nope, MLA_NOPE) * (qn_ref[...] * scale)
        qr = _block_rms(q_ref[:, h * hw + MLA_NOPE:(h + 1) * hw].astype(F32), ones_rope, MLA_ROPE) * qr_ref[...]
        qr = _rope(qr, cq_ref[...], saq_ref[...], sbq_ref[...]) * scale
        qs.append(jnp.concatenate([qn.astype(BF16), qr.astype(BF16)], axis=-1))
    heads = range(MLA_HEADS_PER_STEP)
    outs = _attend_streams(qs, [kt_ref.at[h] for h in heads], [vo_ref.at[h] for h in heads], kt_ref.shape[2])
    for h, (o, l) in enumerate(outs):
        o_ref[:, h * MLA_V:(h + 1) * MLA_V] = (o * (1.0 / l)).astype(o_ref.dtype)


def mla_attention(q, kv, p1, kr_blk, rope, qn_gain, qr_gain, kn_gain, kr_gain, *, tq, n_ctx):
    b, n_lat, _ = q.shape
    t = kv.shape[1]
    c, sa, sb = rope
    hps = MLA_HEADS_PER_STEP
    row_q = lambda bi, h, qi: (qi, 0)
    full = lambda bi, h, qi: (0, 0)
    pad = lambda g: jnp.concatenate([g.astype(F32), jnp.zeros((LANES - g.shape[0],), F32)]).reshape(1, LANES)
    return pl.pallas_call(
        _mla_attn_kernel,
        grid=(b, MLA_HEADS // hps, n_lat // tq),
        in_specs=[
            pl.BlockSpec((None, tq, hps * 2 * LANES), lambda bi, h, qi: (bi, qi, h)),
            pl.BlockSpec((None, t, hps * 2 * LANES), lambda bi, h, qi: (bi, 0, h)),
            pl.BlockSpec((None, t, LANES), lambda bi, h, qi: (bi, 0, kr_blk)),
            pl.BlockSpec((tq, LANES), row_q), pl.BlockSpec((tq, LANES), row_q), pl.BlockSpec((tq, LANES), row_q),
            pl.BlockSpec((t, LANES), full), pl.BlockSpec((t, LANES), full), pl.BlockSpec((t, LANES), full),
            pl.BlockSpec((1, LANES), full), pl.BlockSpec((1, LANES), full),
            pl.BlockSpec((1, LANES), full), pl.BlockSpec((1, LANES), full),
        ],
        out_specs=pl.BlockSpec((None, tq, hps * MLA_V), lambda bi, h, qi: (bi, qi, h)),
        out_shape=jax.ShapeDtypeStruct((b, n_lat, MLA_HEADS * MLA_V), BF16),
        scratch_shapes=[pltpu.VMEM((hps, 2 * LANES, t), BF16), pltpu.VMEM((hps, t, 2 * LANES), BF16)],
        compiler_params=_cparams(("arbitrary", "arbitrary", "arbitrary")),
        name="mla_attention",
    )(q, kv, p1, c[n_ctx:], sa[n_ctx:], sb[n_ctx:], c, sa, sb, pad(qn_gain), pad(qr_gain), pad(kn_gain), pad(kr_gain))


def _proj_residual_kernel(*refs, n_in, n_x, n_ctx_tiles, row_tile0):
    a_refs = refs[:n_in]
    w_refs = refs[n_in:2 * n_in]
    x_refs = refs[2 * n_in:2 * n_in + n_x]
    m_ref, o_ref = refs[2 * n_in + n_x:]
    acc = _dot(a_refs[0][...], w_refs[0][...])
    for a_ref, w_ref in zip(a_refs[1:], w_refs[1:]):
        acc = acc + _dot(a_ref[...], w_ref[...])
    o_ref[...] = _stream_tile(x_refs, n_ctx_tiles, row_tile0) + m_ref[G_M:G_M + 1, :] * acc


def proj_residual(acts, weights, xs, mods, *, tm, n_ctx_tiles, row_tile0=0):
    b, r, _ = acts[0].shape
    x_specs, x_args = _stream_specs(xs, tm, n_ctx_tiles, row_tile0)
    d = x_args[0].shape[-1]
    in_specs = [pl.BlockSpec((None, tm, a.shape[-1]), lambda bi, ti: (bi, ti, 0)) for a in acts]
    in_specs += [pl.BlockSpec(w.shape, lambda bi, ti: (0, 0)) for w in weights]
    in_specs += x_specs + [
        pl.BlockSpec((None, None, 6, d), lambda bi, ti: (bi, ((ti + row_tile0) >= n_ctx_tiles).astype(jnp.int32), 0, 0)),
    ]
    return pl.pallas_call(
        functools.partial(_proj_residual_kernel, n_in=len(acts), n_x=len(x_args), n_ctx_tiles=n_ctx_tiles,
                          row_tile0=row_tile0),
        grid=(b, r // tm),
        in_specs=in_specs,
        out_specs=pl.BlockSpec((None, tm, d), lambda bi, ti: (bi, ti, 0)),
        out_shape=jax.ShapeDtypeStruct((b, r, d), F32),
        compiler_params=_cparams(("arbitrary", "arbitrary")),
        name="proj_residual",
    )(*acts, *weights, *x_args, mods)


PAIRS = [(i, j) for i in range(EXPERTS_PER_GROUP) for j in range(i + 1, EXPERTS_PER_GROUP)]
N_CLASSES = N_GROUPS * len(PAIRS)
CLASS_LO = np.array([EXPERTS_PER_GROUP * g + i for g in range(N_GROUPS) for (i, j) in PAIRS], np.int32)
CLASS_HI = np.array([EXPERTS_PER_GROUP * g + j for g in range(N_GROUPS) for (i, j) in PAIRS], np.int32)
EXPERT_TILE = 256
ROW_SLABS = 8
DMA_UNROLL = 8
ROW_SLOTS = 4


def _route_kernel(x_ref, m_ref, g_ref, rw_ref, rb_ref, hf_ref, rows_ref, info_ref, count_ref, cnt_ref):
    h = _rms(x_ref[...]) * g_ref[...]
    h = h * (1.0 + m_ref[SC_F:SC_F + 1, :]) + m_ref[SH_F:SH_F + 1, :]
    hf_ref[...] = h.astype(hf_ref.dtype)
    rows_ref[...] = _rows_to_slabs(h)
    h1, h2, _ = _split3(h)
    w1, w2, _ = _split3(rw_ref[...])
    logits = _dot_nt(w1, h1) + (_dot_nt(w1, h2) + _dot_nt(w2, h1))
    biased = _sigmoid(logits) + rb_ref[...]
    row = [biased[e:e + 1, :] for e in range(N_EXPERTS)]
    gscore = []
    for g in range(N_GROUPS):
        m = row[4 * g:4 * g + 4]
        gscore.append(functools.reduce(jnp.maximum, [m[i] + m[j] for (i, j) in PAIRS]))
    hits = []
    for g in range(N_GROUPS):
        best = None
        for g2 in range(N_GROUPS):
            if g2 == g:
                continue
            wins = (gscore[g] > gscore[g2]) if g2 < g else (gscore[g] >= gscore[g2])
            best = wins if best is None else jnp.logical_and(best, wins)
        chosen = []
        for i in range(EXPERTS_PER_GROUP):
            rank = None
            for j in range(EXPERTS_PER_GROUP):
                if j == i:
                    continue
                mi, mj = row[4 * g + i], row[4 * g + j]
                ahead = ((mj >= mi) if j < i else (mj > mi)).astype(jnp.int32)
                rank = ahead if rank is None else rank + ahead
            chosen.append(rank < 2)
        for (i, j) in PAIRS:
            hits.append(jnp.where(best & chosen[i] & chosen[j], 1.0, 0.0))
    onehot = jnp.concatenate(hits, axis=0)
    tm = onehot.shape[1]
    upper = (lax.broadcasted_iota(jnp.int32, (tm, tm), 0) <= lax.broadcasted_iota(jnp.int32, (tm, tm), 1))
    prefix = _dot(onehot.astype(BF16), upper.astype(BF16))

    @pl.when((pl.program_id(0) == 0) & (pl.program_id(1) == 0))
    def _():
        cnt_ref[...] = jnp.zeros_like(cnt_ref)

    seen = cnt_ref[...]
    cls_id = lax.broadcasted_iota(jnp.int32, onehot.shape, 0).astype(F32)
    cls = jnp.sum(onehot * cls_id, axis=0, keepdims=True)
    rank = jnp.sum(onehot * (seen[:, 0:1] + prefix - 1.0), axis=0, keepdims=True)
    info_ref[...] = jnp.concatenate([cls, rank, jnp.zeros((6, tm), F32)], axis=0).astype(jnp.int32)
    seen = seen + jnp.sum(onehot, axis=1, keepdims=True)
    cnt_ref[...] = seen
    count_ref[...] = seen


def route(xs, mods, gain, router_w, router_bias, *, tm, n_ctx_tiles):
    b, t, d = xs.shape
    nt = t // tm
    return pl.pallas_call(
        _route_kernel,
        grid=(b, nt),
        in_specs=[
            pl.BlockSpec((None, tm, d), lambda bi, ti: (bi, ti, 0)),
            pl.BlockSpec((None, None, 6, d), lambda bi, ti: (bi, (ti >= n_ctx_tiles).astype(jnp.int32), 0, 0)),
            pl.BlockSpec((1, d), lambda bi, ti: (0, 0)),
            pl.BlockSpec((N_EXPERTS, d), lambda bi, ti: (0, 0)),
            pl.BlockSpec((N_EXPERTS, 1), lambda bi, ti: (0, 0)),
        ],
        out_specs=[
            pl.BlockSpec((None, tm, d), lambda bi, ti: (bi, ti, 0)),
            pl.BlockSpec((tm, ROW_SLABS, LANES), lambda bi, ti: (bi * nt + ti, 0, 0)),
            pl.BlockSpec((None, None, 8, tm), lambda bi, ti: (bi, ti, 0, 0)),
            pl.BlockSpec((N_CLASSES, LANES), lambda bi, ti: (0, 0)),
        ],
        out_shape=[jax.ShapeDtypeStruct((b, t, d), BF16),
                   jax.ShapeDtypeStruct((b * t, ROW_SLABS, LANES), F32),
                   jax.ShapeDtypeStruct((b, nt, 8, tm), jnp.int32),
                   jax.ShapeDtypeStruct((N_CLASSES, LANES), F32)],
        scratch_shapes=[pltpu.VMEM((N_CLASSES, LANES), F32)],
        compiler_params=_cparams(("arbitrary", "arbitrary")),
        name="route",
    )(xs, mods, gain.reshape(1, d).astype(F32), router_w.T.astype(F32),
      router_bias.reshape(N_EXPERTS, 1).astype(F32))


def _transpose8(parts):
    sub = lax.broadcasted_iota(jnp.int32, parts[0].shape, 1)
    for s in (4, 2, 1):
        keep = (sub & s) == 0
        new = list(parts)
        for i in range(8):
            if i & s:
                continue
            a, b = parts[i], parts[i | s]
            new[i] = jnp.where(keep, a, pltpu.roll(b, s, 1))
            new[i | s] = jnp.where(keep, pltpu.roll(a, 8 - s, 1), b)
        parts = new
    return parts


def _slabs_to_rows(slabs):
    g = slabs.shape[0] // 8
    x4 = slabs.reshape(g, 8, ROW_SLABS, LANES)
    parts = _transpose8([x4[:, t] for t in range(8)])
    return jnp.concatenate([p.reshape(g * 8, LANES) for p in parts], axis=-1)


def _rows_to_slabs(x):
    g = x.shape[0] // 8
    parts = _transpose8([x[:, k * LANES:(k + 1) * LANES].reshape(g, 8, LANES) for k in range(ROW_SLABS)])
    return jnp.stack(parts, axis=1).reshape(g * 8, ROW_SLABS, LANES)


def _invert_kernel(dest_ref, src_ref):
    def clear(i, carry):
        for u in range(2 * DMA_UNROLL):
            src_ref[i * 2 * DMA_UNROLL + u] = 0
        return carry

    def put(i, carry):
        for u in range(2 * DMA_UNROLL):
            t = i * 2 * DMA_UNROLL + u
            src_ref[dest_ref[t]] = t
        return carry

    lax.fori_loop(0, src_ref.shape[0] // (2 * DMA_UNROLL), clear, 0)
    lax.fori_loop(0, dest_ref.shape[0] // (2 * DMA_UNROLL), put, 0)


def invert_permutation(dest, n_rows_out):
    smem = pl.BlockSpec(memory_space=pltpu.SMEM)
    return pl.pallas_call(
        _invert_kernel,
        in_specs=[smem],
        out_specs=smem,
        out_shape=jax.ShapeDtypeStruct((n_rows_out,), jnp.int32),
        name="moe_invert",
    )(dest)


def _expert_kernel(te_ref, tv_ref, nx_ref, src_ref, rows_hbm, rwl_ref, rwh_ref, wg_hbm, wu_hbm, wd_hbm, y_ref,
                   xbuf_ref, sem_ref, stage_g, stage_u, stage_d, cache_g, cache_u, cache_d, wsem_ref, *, layer):
    i = pl.program_id(0)
    n = pl.num_programs(0)
    tg = y_ref.shape[0]
    slot = i % ROW_SLOTS
    valid = tv_ref[i] != 0
    prev = jnp.maximum(i - 1, 0)

    def fetch(tile, s):
        base = tile * tg
        for t in range(tg):
            pltpu.async_copy(rows_hbm.at[src_ref[base + t]], xbuf_ref.at[s, t], sem_ref.at[s], priority=t % 2)

    def wait_rows(s):
        pltpu.make_async_copy(rows_hbm.at[pl.ds(0, tg)], xbuf_ref.at[s], sem_ref.at[s]).wait()

    def weight_copies(s, e):
        return [pltpu.make_async_copy(w_hbm.at[layer, e], stage.at[s], wsem_ref.at[s])
                for w_hbm, stage in ((wg_hbm, stage_g), (wu_hbm, stage_u), (wd_hbm, stage_d))]

    @pl.when(i == 0)
    def _():
        for k in range(ROW_SLOTS - 1):
            fetch(jnp.minimum(k, n - 1), k)
        for s in range(2):
            for copy in weight_copies(s, te_ref[s * n]):
                copy.start()

    for s in range(2):
        expert = te_ref[s * n + i]

        @pl.when(valid & ((i == 0) | (expert != te_ref[s * n + prev])))
        def _():
            for copy in weight_copies(s, expert):
                copy.wait()
            cache_g[s] = stage_g[s].astype(BF16)
            cache_u[s] = stage_u[s].astype(BF16)
            cache_d[s] = stage_d[s].astype(BF16)
            upcoming = nx_ref[s * n + i]

            @pl.when(upcoming >= 0)
            def _():
                for copy in weight_copies(s, upcoming):
                    copy.start()

    @pl.when(valid)
    def _():
        wait_rows(slot)
        x = _slabs_to_rows(xbuf_ref[slot])
        fetch(jnp.minimum(i + ROW_SLOTS - 1, n - 1), (i + ROW_SLOTS - 1) % ROW_SLOTS)
        xb = x.astype(BF16)
        s_lo = _sigmoid(jnp.sum(x * rwl_ref[...], axis=-1, keepdims=True))
        s_hi = _sigmoid(jnp.sum(x * rwh_ref[...], axis=-1, keepdims=True))
        inv = 1.0 / (s_lo + s_hi)

        def ffn(s):
            a = _dot(xb, cache_g[s])
            u = _dot(xb, cache_u[s])
            return _dot((a * _sigmoid(a) * u).astype(BF16), cache_d[s])

        y_ref[...] = _rows_to_slabs((s_lo * inv) * ffn(0) + (s_hi * inv) * ffn(1))

    @pl.when(jnp.logical_not(valid))
    def _():
        y_ref[...] = jnp.zeros_like(y_ref)

    @pl.when((valid & (i == n - 1)) | (jnp.logical_not(valid) & (tv_ref[prev] != 0)))
    def _():
        first = jnp.where(valid, i + 1, i)
        for k in range(ROW_SLOTS - 1):
            wait_rows((first + k) % ROW_SLOTS)


def experts(tile_expert, tile_valid, next_expert, src, rows, router_w_t, wg, wu, wd, layer):
    tg = EXPERT_TILE
    n_tiles = src.shape[0] // tg
    d, ff = wg.shape[2:]
    lo = lambda i, te, tv, nx, src: (te[i], 0, 0)
    hi = lambda i, te, tv, nx, src: (te[n_tiles + i], 0, 0)
    hbm = pl.BlockSpec(memory_space=pl.ANY)
    return pl.pallas_call(
        functools.partial(_expert_kernel, layer=layer),
        grid_spec=pltpu.PrefetchScalarGridSpec(
            num_scalar_prefetch=4,
            grid=(n_tiles,),
            in_specs=[hbm, pl.BlockSpec((None, 1, d), lo), pl.BlockSpec((None, 1, d), hi), hbm, hbm, hbm],
            out_specs=pl.BlockSpec((tg, ROW_SLABS, LANES), lambda i, te, tv, nx, src: (i, 0, 0)),
            scratch_shapes=[pltpu.VMEM((ROW_SLOTS, tg, ROW_SLABS, LANES), F32), pltpu.SemaphoreType.DMA((ROW_SLOTS,)),
                            pltpu.VMEM((2, d, ff), F32), pltpu.VMEM((2, d, ff), F32), pltpu.VMEM((2, ff, d), F32),
                            pltpu.VMEM((2, d, ff), BF16), pltpu.VMEM((2, d, ff), BF16), pltpu.VMEM((2, ff, d), BF16),
                            pltpu.SemaphoreType.DMA((2,))],
        ),
        out_shape=jax.ShapeDtypeStruct((n_tiles * tg, ROW_SLABS, LANES), F32),
        compiler_params=_cparams(("arbitrary",)),
        name="moe_experts",
    )(tile_expert, tile_valid, next_expert, src, rows, router_w_t, router_w_t, wg, wu, wd)


def _combine_kernel(dest_ref, hf_ref, x_ref, m_ref, wg_ref, wu_ref, wd_ref, y_hbm, o_ref, ybuf_ref, sem_ref):
    i = pl.program_id(0)
    n = pl.num_programs(0)
    tm = hf_ref.shape[0]
    slot = i % 2

    def fetch(tile, s):
        base = tile * tm
        for t in range(tm):
            pltpu.async_copy(y_hbm.at[dest_ref[base + t]], ybuf_ref.at[s, t], sem_ref.at[s], priority=t % 2)

    def wait_rows(s):
        pltpu.make_async_copy(y_hbm.at[pl.ds(0, tm)], ybuf_ref.at[s], sem_ref.at[s]).wait()

    @pl.when(i == 0)
    def _():
        fetch(0, 0)

    fetch(jnp.minimum(i + 1, n - 1), 1 - slot)
    h = hf_ref[...]
    a = _dot(h, wg_ref[...])
    u = _dot(h, wu_ref[...])
    shared = _dot((a * _sigmoid(a) * u).astype(BF16), wd_ref[...])
    wait_rows(slot)
    o_ref[...] = x_ref[...] + m_ref[G_F:G_F + 1, :] * (shared + _slabs_to_rows(ybuf_ref[slot]))

    @pl.when(i == n - 1)
    def _():
        wait_rows(1 - slot)


def combine(dest, hf, xs, mods, wg, wu, wd, y_sorted, *, tm, n_ctx_tiles):
    b, r, d = hf.shape
    nt = r // tm
    ff = wg.shape[1]
    row = lambda i, dest: (i // nt, i % nt, 0)
    const = lambda i, dest: (0, 0)
    return pl.pallas_call(
        _combine_kernel,
        grid_spec=pltpu.PrefetchScalarGridSpec(
            num_scalar_prefetch=1,
            grid=(b * nt,),
            in_specs=[
                pl.BlockSpec((None, tm, d), row),
                pl.BlockSpec((None, tm, d), row),
                pl.BlockSpec((None, None, 6, d),
                             lambda i, dest: (i // nt, ((i % nt) >= n_ctx_tiles).astype(jnp.int32), 0, 0)),
                pl.BlockSpec((d, ff), const), pl.BlockSpec((d, ff), const), pl.BlockSpec((ff, d), const),
                pl.BlockSpec(memory_space=pl.ANY),
            ],
            out_specs=pl.BlockSpec((None, tm, d), row),
            scratch_shapes=[pltpu.VMEM((2, tm, ROW_SLABS, LANES), F32), pltpu.SemaphoreType.DMA((2,))],
        ),
        out_shape=jax.ShapeDtypeStruct((b, r, d), F32),
        compiler_params=_cparams(("arbitrary",)),
        name="moe_combine",
    )(dest, hf, xs, mods, wg, wu, wd, y_sorted)


def moe_block(xs, mods, gain, router_w, router_bias, ew_gate, ew_up, ew_down, layer, sw_gate, sw_up, sw_down,
              *, tm, n_ctx_tiles):
    b, r, d = xs.shape
    n = b * r
    tg = EXPERT_TILE
    hf, rows, info, counts = route(xs, mods, gain, router_w, router_bias, tm=tm, n_ctx_tiles=n_ctx_tiles)
    cls = info[:, :, 0, :].reshape(n)
    rank = info[:, :, 1, :].reshape(n)
    padded = ((counts[:, 0].astype(jnp.int32) + tg - 1) // tg) * tg
    ends = jnp.cumsum(padded)
    dest = (ends - padded)[cls] + rank
    n_tiles = n // tg + N_CLASSES
    tile_start = jnp.arange(n_tiles, dtype=jnp.int32) * tg
    tile_valid = tile_start < ends[-1]
    last_start = jnp.maximum(ends[-1] - tg, 0)
    start = jnp.where(tile_valid, tile_start, last_start)
    tile_cls = jnp.sum((ends[None, :] <= start[:, None]).astype(jnp.int32), axis=1)
    tile_cls = jnp.minimum(tile_cls, N_CLASSES - 1)
    slot_expert = jnp.stack([jnp.asarray(CLASS_LO)[tile_cls], jnp.asarray(CLASS_HI)[tile_cls]])
    tile_expert = slot_expert.reshape(-1)
    later = (tile_start[None, :] > tile_start[:, None]) & tile_valid[None, :]
    differs = slot_expert[:, None, :] != slot_expert[:, :, None]
    first = jnp.min(jnp.where(later[None] & differs, jnp.arange(n_tiles)[None, None, :], n_tiles), axis=-1)
    next_expert = jnp.where(first < n_tiles, jnp.take_along_axis(slot_expert, jnp.minimum(first, n_tiles - 1), axis=1),
                            -1).reshape(-1).astype(jnp.int32)
    src = invert_permutation(dest, n_tiles * tg)
    y_sorted = experts(tile_expert, tile_valid.astype(jnp.int32), next_expert, src, rows,
                       router_w.T.astype(F32).reshape(N_EXPERTS, 1, d), ew_gate, ew_up, ew_down, layer)
    return combine(dest, hf, xs, mods, sw_gate.astype(BF16), sw_up.astype(BF16), sw_down.astype(BF16), y_sorted,
                   tm=tm, n_ctx_tiles=n_ctx_tiles)


def kernel(x, c, ctx, c_ctx, mod_w, mod_b, norm_mix, norm_ffn, even_w_in, even_w_out, diff_q_gain, diff_k_gain, diff_lambda, diff_subln, hgrn_lb_logits, hgrn_out_gain, odd_w_in, mla_q_a_gain, mla_kv_a_gain, mla_w_uq, mla_w_ukv, mla_q_nope_gain, mla_q_rope_gain, mla_k_nope_gain, mla_k_rope_gain, odd_w_out, router_w, router_bias, expert_w_gate, expert_w_up, expert_w_down, shared_w_gate, shared_w_up, shared_w_down):
    b, n_lat, d = x.shape
    n_ctx = ctx.shape[1]
    depth = mod_w.shape[0]
    tm = 256 if n_ctx % 256 == 0 else 128
    n_ctx_tiles = n_ctx // tm
    rope = rope_lane_tables(n_ctx, n_lat)

    mod_rows = 16
    c_rows = jnp.concatenate([c, c_ctx[None, :], jnp.zeros((mod_rows - b - 1, d), F32)], axis=0)
    mod_all = modulation(c_rows, mod_w, mod_b).reshape(depth, mod_rows, 6, d)
    lb_all = jnp.cumsum(jax.nn.softmax(hgrn_lb_logits.astype(F32), axis=0), axis=0)

    xs = (ctx, x)
    for layer in range(depth):
        last = layer == depth - 1
        j = layer // 2
        mods = jnp.stack([jnp.broadcast_to(mod_all[layer, b], (b, 6, d)), mod_all[layer, :b]], axis=1)
        row_tile0 = n_ctx_tiles if last else 0
        if layer % 2 == 0:
            lam_init = 0.8 - 0.6 * math.exp(-0.3 * layer)
            p = norm_matmul(xs, mods, norm_mix[layer], even_w_in[j].astype(BF16), tm=tm, n_ctx_tiles=n_ctx_tiles)
            oa = diff_attention(p, diff_lambda[j], diff_q_gain[j], diff_k_gain[j], diff_subln[j], rope,
                                tq=tm, n_ctx=n_ctx, lam_init=lam_init, q_blk0=0, k_blk0=4, v_blk0=8)
            ob = hgrn_bidirectional(p, lb_all[j, 0], lb_all[j, 1], hgrn_out_gain[j], n_ctx=n_ctx, blk0=12)
            if last:
                oa, ob = oa[:, n_ctx:], ob[:, n_ctx:]
            w_out = even_w_out[j].astype(BF16)
            half = oa.shape[-1]
            xs_mid = proj_residual([oa, ob], [w_out[:half], w_out[half:]], xs, mods, tm=tm,
                                   n_ctx_tiles=n_ctx_tiles, row_tile0=row_tile0)
        else:
            w_in = odd_w_in[j]
            zpad = jnp.zeros((d, LANES - MLA_ROPE), F32)
            w_in_r = jnp.concatenate([w_in[:, MLA_Q_LORA:MLA_Q_LORA + MLA_KV_LORA], w_in[:, MLA_Q_LORA + MLA_KV_LORA:],
                                      zpad, w_in[:, :MLA_Q_LORA]], axis=1).astype(BF16)
            w_uq = mla_w_uq[j].reshape(MLA_Q_LORA, MLA_HEADS, MLA_NOPE + MLA_ROPE)
            w_uq = jnp.concatenate([w_uq, jnp.zeros((MLA_Q_LORA, MLA_HEADS, LANES - MLA_ROPE), F32)], axis=-1)
            w_uq = w_uq.reshape(MLA_Q_LORA, MLA_HEADS * 2 * LANES).astype(BF16)
            if not last:
                raise NotImplementedError("context queries for a non-final latent-attention layer")
            kr, q, kv = mla_projections(xs, mods, norm_mix[layer], w_in_r, mla_q_a_gain[j], mla_kv_a_gain[j], w_uq,
                                        mla_w_ukv[j].astype(BF16), tm=tm, n_ctx_tiles=n_ctx_tiles)
            o = mla_attention(q, kv, kr, 0, rope, mla_q_nope_gain[j], mla_q_rope_gain[j], mla_k_nope_gain[j],
                              mla_k_rope_gain[j], tq=2 * tm, n_ctx=n_ctx)
            xs_mid = proj_residual([o], [odd_w_out[j].astype(BF16)], xs, mods, tm=tm,
                                   n_ctx_tiles=n_ctx_tiles, row_tile0=row_tile0)
        xs = moe_block(xs_mid, mods, norm_ffn[layer], router_w, router_bias,
                       expert_w_gate, expert_w_up, expert_w_down, layer,
                       shared_w_gate[layer], shared_w_up[layer], shared_w_down[layer],
                       tm=tm, n_ctx_tiles=0 if last else n_ctx_tiles)
    return xs if xs.shape[1] == n_lat else xs[:, n_ctx:]
```

```python
import functools
import math

import numpy as np
import jax
import jax.numpy as jnp
from jax import lax
from jax.experimental import pallas as pl
from jax.experimental.pallas import tpu as pltpu

F32 = jnp.float32
BF16 = jnp.bfloat16

LANES = 128
VMEM_LIMIT = 56 * 1024 * 1024

GRID_W = 64
DIFF_HEADS = 4
DIFF_HEAD_DIM = 64
HGRN_HEADS = 4
HGRN_K_DIM = 128
MLA_HEADS = 8
MLA_NOPE = 128
MLA_ROPE = 64
MLA_V = 128
MLA_Q_LORA = 384
MLA_KV_LORA = 256
N_EXPERTS = 16
N_GROUPS = 4
EXPERTS_PER_GROUP = 4
ROPE_BASE = 10000.0
EPS = 1e-6
LOG2E = 1.4426950408889634
GLA_CHUNK = 64
ATTN_KEY_CHUNK = 1152
MLA_HEADS_PER_STEP = 2
DIFF_HEADS_PER_STEP = 2
HGRN_HEADS_PER_STEP = 2

SH_M, SC_M, G_M, SH_F, SC_F, G_F = range(6)


def _sigmoid(x):
    return 0.5 * jnp.tanh(0.5 * x) + 0.5


def _dot(a, b):
    return jnp.dot(a, b, preferred_element_type=F32)


def _dot_nt(a, b):
    return lax.dot_general(a, b, (((1,), (1,)), ((), ())), preferred_element_type=F32)


def _dot_tn(a, b):
    return lax.dot_general(a, b, (((0,), (0,)), ((), ())), preferred_element_type=F32)


def _split3(x):
    hi = x.astype(BF16)
    r = x - hi.astype(F32)
    mid = r.astype(BF16)
    lo = (r - mid.astype(F32)).astype(BF16)
    return hi, mid, lo


def _rms(x, width=None):
    n = x.shape[-1] if width is None else width
    return x * lax.rsqrt(jnp.sum(x * x, axis=-1, keepdims=True) * (1.0 / n) + EPS)


def _cparams(sem):
    return pltpu.CompilerParams(dimension_semantics=sem, vmem_limit_bytes=VMEM_LIMIT)


def _mod_kernel(c_ref, w_ref, b_ref, o_ref):
    c = c_ref[...]
    s = c * _sigmoid(c)
    o_ref[...] = _dot(s.astype(BF16), w_ref[...].astype(BF16)) + b_ref[...]


def modulation(c_rows, mod_w, mod_b):
    n_layers, d, n = mod_w.shape
    rows = c_rows.shape[0]
    tn = 1536
    return pl.pallas_call(
        _mod_kernel,
        grid=(n_layers, n // tn),
        in_specs=[
            pl.BlockSpec((rows, d), lambda l, j: (0, 0)),
            pl.BlockSpec((None, d, tn), lambda l, j: (l, 0, j)),
            pl.BlockSpec((None, 1, tn), lambda l, j: (l, 0, j)),
        ],
        out_specs=pl.BlockSpec((None, rows, tn), lambda l, j: (l, 0, j)),
        out_shape=jax.ShapeDtypeStruct((n_layers, rows, n), F32),
        compiler_params=_cparams(("arbitrary", "arbitrary")),
        name="modulation",
    )(c_rows, mod_w, mod_b.reshape(n_layers, 1, n))


def _stream_specs(xs, tm, n_ctx_tiles, row_tile0=0):
    parts = xs if isinstance(xs, tuple) else (xs,)
    d = parts[0].shape[-1]
    if len(parts) == 1:
        return [pl.BlockSpec((None, tm, d), lambda bi, ti: (bi, ti + row_tile0, 0))], list(parts)
    return [pl.BlockSpec((None, tm, d), lambda bi, ti: (bi, jnp.minimum(ti + row_tile0, n_ctx_tiles - 1), 0)),
            pl.BlockSpec((None, tm, d), lambda bi, ti: (bi, jnp.maximum(ti + row_tile0 - n_ctx_tiles, 0), 0))], list(parts)


def _stream_tile(x_refs, n_ctx_tiles, row_tile0=0):
    if len(x_refs) == 1:
        return x_refs[0][...]
    return jnp.where(pl.program_id(1) + row_tile0 < n_ctx_tiles, x_refs[0][...], x_refs[1][...])


def _norm_matmul_kernel(*refs, n_x, n_ctx_tiles):
    m_ref, g_ref, w_ref, o_ref = refs[n_x:]
    h = _rms(_stream_tile(refs[:n_x], n_ctx_tiles)) * g_ref[...]
    h = h * (1.0 + m_ref[SC_M:SC_M + 1, :]) + m_ref[SH_M:SH_M + 1, :]
    o_ref[...] = _dot(h.astype(BF16), w_ref[...]).astype(o_ref.dtype)


def norm_matmul(xs, mods, gain, w, *, tm, n_ctx_tiles):
    x_specs, x_args = _stream_specs(xs, tm, n_ctx_tiles)
    b = x_args[0].shape[0]
    t = sum(a.shape[1] for a in x_args)
    k, n = w.shape
    return pl.pallas_call(
        functools.partial(_norm_matmul_kernel, n_x=len(x_args), n_ctx_tiles=n_ctx_tiles),
        grid=(b, t // tm),
        in_specs=x_specs + [
            pl.BlockSpec((None, None, 6, k), lambda bi, ti: (bi, (ti >= n_ctx_tiles).astype(jnp.int32), 0, 0)),
            pl.BlockSpec((1, k), lambda bi, ti: (0, 0)), pl.BlockSpec((k, n), lambda bi, ti: (0, 0))],
        out_specs=pl.BlockSpec((None, tm, n), lambda bi, ti: (bi, ti, 0)),
        out_shape=jax.ShapeDtypeStruct((b, t, n), BF16),
        compiler_params=_cparams(("arbitrary", "arbitrary")),
        name="norm_matmul",
    )(*x_args, mods, gain.reshape(1, k).astype(F32), w)


def _mla_proj_kernel(x_ref, m_ref, g_ref, win_ref, qg_ref, kvg_ref, wuq_ref, wukv_ref, kr_ref, q_ref, kv_ref,
                     *, n_ctx_tiles):
    h = _rms(x_ref[...]) * g_ref[...]
    h = h * (1.0 + m_ref[SC_M:SC_M + 1, :]) + m_ref[SH_M:SH_M + 1, :]
    p = _dot(h.astype(BF16), win_ref[...])
    kr_ref[...] = p[:, MLA_KV_LORA:MLA_KV_LORA + LANES].astype(kr_ref.dtype)
    ckv = _rms(p[:, 0:MLA_KV_LORA]) * kvg_ref[...]
    kv_ref[...] = _dot(ckv.astype(BF16), wukv_ref[...]).astype(kv_ref.dtype)

    @pl.when(pl.program_id(1) >= n_ctx_tiles)
    def _():
        cq = _rms(p[:, MLA_KV_LORA + LANES:]) * qg_ref[...]
        q_ref[...] = _dot(cq.astype(BF16), wuq_ref[...]).astype(q_ref.dtype)

    @pl.when(pl.program_id(1) < n_ctx_tiles)
    def _():
        q_ref[...] = jnp.zeros_like(q_ref)


def mla_projections(xs, mods, gain, w_in_r, q_a_gain, kv_a_gain, w_uq, w_ukv, *, tm, n_ctx_tiles):
    b, t, d = xs.shape
    nt = t // tm
    n_lat = t - n_ctx_tiles * tm
    const = lambda a: pl.BlockSpec(a.shape, lambda bi, ti: (0, 0))
    row = lambda bi, ti: (bi, ti, 0)
    g2 = lambda g: g.reshape(1, -1).astype(F32)
    args = [g2(gain), w_in_r, g2(q_a_gain), g2(kv_a_gain), w_uq, w_ukv]
    return pl.pallas_call(
        functools.partial(_mla_proj_kernel, n_ctx_tiles=n_ctx_tiles),
        grid=(b, nt),
        in_specs=[pl.BlockSpec((None, tm, d), row),
                  pl.BlockSpec((None, None, 6, d), lambda bi, ti: (bi, (ti >= n_ctx_tiles).astype(jnp.int32), 0, 0))]
                 + [const(a) for a in args],
        out_specs=[pl.BlockSpec((None, tm, LANES), row),
                   pl.BlockSpec((None, tm, w_uq.shape[1]), lambda bi, ti: (bi, jnp.maximum(ti - n_ctx_tiles, 0), 0)),
                   pl.BlockSpec((None, tm, w_ukv.shape[1]), row)],
        out_shape=[jax.ShapeDtypeStruct((b, t, LANES), BF16),
                   jax.ShapeDtypeStruct((b, n_lat, w_uq.shape[1]), BF16),
                   jax.ShapeDtypeStruct((b, t, w_ukv.shape[1]), BF16)],
        compiler_params=_cparams(("arbitrary", "arbitrary")),
        name="mla_projections",
    )(xs, mods, *args)


def rope_lane_tables(n_ctx, n_lat):
    rows = n_lat // GRID_W
    row = np.repeat(np.arange(rows), GRID_W).astype(np.float32)
    col = np.tile(np.arange(GRID_W), rows).astype(np.float32)
    axis_dim = DIFF_HEAD_DIM // 2
    inv_freq = jnp.asarray(ROPE_BASE, F32) ** (-jnp.arange(0, axis_dim, 2, dtype=F32) / axis_dim)
    ang_r = jnp.asarray(row)[:, None] * inv_freq
    ang_c = jnp.asarray(col)[:, None] * inv_freq
    lane = np.arange(LANES)
    freq_idx = lane % 16
    use_col = (lane % 64) >= 32
    first = (lane % 32) < 16
    ang = jnp.where(use_col[None, :], ang_c[:, freq_idx], ang_r[:, freq_idx])
    cos, sin = jnp.cos(ang), jnp.sin(ang)
    c = jnp.concatenate([jnp.ones((n_ctx, LANES), F32), cos], axis=0)
    sa = jnp.concatenate([jnp.zeros((n_ctx, LANES), F32), jnp.where(first[None, :], -sin, 0.0)], axis=0)
    sb = jnp.concatenate([jnp.zeros((n_ctx, LANES), F32), jnp.where(first[None, :], 0.0, sin)], axis=0)
    return c, sa, sb


def _rope(x, c, sa, sb):
    return x * c + pltpu.roll(x, LANES - 16, 1) * sa + pltpu.roll(x, 16, 1) * sb


def _block_ones(width):
    r = lax.broadcasted_iota(jnp.int32, (LANES, LANES), 0) // width
    c = lax.broadcasted_iota(jnp.int32, (LANES, LANES), 1) // width
    return (r == c).astype(BF16)


def _block_rms(x, ones, width):
    xx = x * x
    hi = xx.astype(BF16)
    lo = (xx - hi.astype(F32)).astype(BF16)
    ms = (_dot(hi, ones) + _dot(lo, ones)) * (1.0 / width)
    return x * lax.rsqrt(ms + EPS)


def _attend_streams(qs, kt_refs, vo_refs, nk):
    n = len(qs)
    chunk = ATTN_KEY_CHUNK if nk % ATTN_KEY_CHUNK == 0 else nk
    nc = nk // chunk
    s = [[None] * nc for _ in range(n)]
    p = [[None] * nc for _ in range(n)]
    m, ol = [None] * n, [None] * n

    def logits(i, c):
        s[i][c] = _dot(qs[i], kt_refs[i][:, c * chunk:(c + 1) * chunk])
        mc = jnp.max(s[i][c], axis=-1, keepdims=True)
        m[i] = mc if c == 0 else jnp.maximum(m[i], mc)

    def exps(i, c):
        p[i][c] = jnp.exp2(s[i][c] - m[i]).astype(BF16)

    def values(i, c):
        oc = _dot(p[i][c], vo_refs[i][c * chunk:(c + 1) * chunk, :])
        ol[i] = oc if c == 0 else ol[i] + oc

    for step in range(n + 2):
        for c in range(nc):
            if step < n:
                logits(step, c)
            if 0 <= step - 1 < n:
                exps(step - 1, c)
            if 0 <= step - 2 < n:
                values(step - 2, c)
    return [(x[:, 0:LANES], x[:, LANES:LANES + 1]) for x in ol]


def _diff_attn_kernel(lam_ref, q_ref, k_ref, v_ref, cq_ref, saq_ref, sbq_ref, ck_ref, sak_ref, sbk_ref,
                      qg_ref, kg_ref, sub_ref, o_ref, kt_ref, vo_ref, *, n_ctx, n_ctx_tiles, lam_init):
    qi = pl.program_id(2)
    ones = _block_ones(DIFF_HEAD_DIM)
    heads = range(DIFF_HEADS_PER_STEP)

    @pl.when(qi == 0)
    def _():
        for h in heads:
            k = _block_rms(k_ref[:, h * LANES:(h + 1) * LANES].astype(F32), ones, DIFF_HEAD_DIM) * kg_ref[...]
            kt_ref[h] = _rope(k, ck_ref[...], sak_ref[...], sbk_ref[...]).T.astype(BF16)
            vo_ref[h, :, 0:LANES] = v_ref[:, h * LANES:(h + 1) * LANES]
            vo_ref[h, :, LANES:2 * LANES] = jnp.ones((v_ref.shape[0], LANES), BF16)

    qs = []
    for h in heads:
        q = _block_rms(q_ref[:, h * LANES:(h + 1) * LANES].astype(F32), ones, DIFF_HEAD_DIM) * qg_ref[...]
        q = _rope(q, cq_ref[...], saq_ref[...], sbq_ref[...]) * (DIFF_HEAD_DIM ** -0.5 * LOG2E)
        lane = lax.broadcasted_iota(jnp.int32, q.shape, 1)
        qs.append(jnp.where(lane < DIFF_HEAD_DIM, q, 0.0).astype(BF16))
        qs.append(jnp.where(lane >= DIFF_HEAD_DIM, q, 0.0).astype(BF16))
    lv = lam_ref[...]
    lam = (jnp.exp(jnp.sum(lv[0:1] * lv[1:2], axis=-1, keepdims=True))
           - jnp.exp(jnp.sum(lv[2:3] * lv[3:4], axis=-1, keepdims=True)) + lam_init)

    def attend(nk):
        outs = _attend_streams(qs, [kt_ref.at[h] for h in heads for _ in range(2)],
                               [vo_ref.at[h] for h in heads for _ in range(2)], nk)
        for h in heads:
            (o1, l1), (o2, l2) = outs[2 * h], outs[2 * h + 1]
            o = o1 * (1.0 / l1) - o2 * (lam / l2)
            o_ref[:, h * LANES:(h + 1) * LANES] = (_rms(o) * sub_ref[...] * (1.0 - lam_init)).astype(o_ref.dtype)

    @pl.when(qi < n_ctx_tiles)
    def _():
        attend(n_ctx)

    @pl.when(qi >= n_ctx_tiles)
    def _():
        attend(k_ref.shape[0])


def diff_attention(p, lam_vecs, q_gain, k_gain, subln, rope, *, tq, n_ctx, lam_init, q_blk0, k_blk0, v_blk0):
    b, t, _ = p.shape
    c, sa, sb = rope
    nq = t // tq
    hps = DIFF_HEADS_PER_STEP
    assert q_blk0 % hps == 0 and k_blk0 % hps == 0 and v_blk0 % hps == 0
    row_q = lambda bi, h, qi: (qi, 0)
    full = lambda bi, h, qi: (0, 0)
    tile2 = lambda g: jnp.tile(g.astype(F32), 2).reshape(1, LANES)
    return pl.pallas_call(
        functools.partial(_diff_attn_kernel, n_ctx=n_ctx, n_ctx_tiles=n_ctx // tq, lam_init=lam_init),
        grid=(b, DIFF_HEADS // hps, nq),
        in_specs=[
            pl.BlockSpec((4, DIFF_HEAD_DIM), full),
            pl.BlockSpec((None, tq, hps * LANES), lambda bi, h, qi: (bi, qi, q_blk0 // hps + h)),
            pl.BlockSpec((None, t, hps * LANES), lambda bi, h, qi: (bi, 0, k_blk0 // hps + h)),
            pl.BlockSpec((None, t, hps * LANES), lambda bi, h, qi: (bi, 0, v_blk0 // hps + h)),
            pl.BlockSpec((tq, LANES), row_q), pl.BlockSpec((tq, LANES), row_q), pl.BlockSpec((tq, LANES), row_q),
            pl.BlockSpec((t, LANES), full), pl.BlockSpec((t, LANES), full), pl.BlockSpec((t, LANES), full),
            pl.BlockSpec((1, LANES), full), pl.BlockSpec((1, LANES), full), pl.BlockSpec((1, LANES), full),
        ],
        out_specs=pl.BlockSpec((None, tq, hps * LANES), lambda bi, h, qi: (bi, qi, h)),
        out_shape=jax.ShapeDtypeStruct((b, t, DIFF_HEADS * LANES), BF16),
        scratch_shapes=[pltpu.VMEM((hps, LANES, t), BF16), pltpu.VMEM((hps, t, 2 * LANES), BF16)],
        compiler_params=_cparams(("arbitrary", "arbitrary", "arbitrary")),
        name="diff_attention",
    )(lam_vecs.astype(F32), p, p, p, c, sa, sb, c, sa, sb, tile2(q_gain), tile2(k_gain),
      subln.astype(F32).reshape(1, LANES))


def _gla_constants(c):
    levels = int(math.log2(c))
    t = np.arange(c)[:, None]
    u = np.arange(c)[None, :]
    stack = [[u <= t], [u >= t]]
    qside = [[], []]
    pair = [[t == u], [t == u]]
    for lv in range(1, levels + 1):
        up_t = ((t >> (lv - 1)) & 1) == 1
        up_u = ((u >> (lv - 1)) & 1) == 1
        same = (t >> lv) == (u >> lv)
        qside[0].append(np.broadcast_to(up_t, (c, LANES)))
        qside[1].append(np.broadcast_to(~up_t, (c, LANES)))
        pair[0].append(same & up_t & ~up_u)
        pair[1].append(same & ~up_t & up_u)
    f32 = lambda x: np.asarray(x, np.float32)
    stack = np.stack([np.tile(f32(m[0]), (1, 3)) for m in stack])
    return (jnp.asarray(stack, BF16), jnp.asarray(np.stack([f32(m) for m in qside])),
            jnp.asarray(np.stack([f32(m) for m in pair])))


def _split_values(cum, d):
    c, w = cum.shape
    levels = int(math.log2(c))
    g = c // 8
    cum3 = cum.reshape(g, 8, w)
    sub = lax.broadcasted_iota(jnp.int32, (g, 8, w), 1)
    row_of_group = lambda off: jnp.broadcast_to(cum3[:, off:off + 1, :], (g, 8, w))
    if d == 0:
        pair_level = jnp.where((sub & 1) == 1, pltpu.roll(cum3, 1, 1), cum3)
        quad_level = jnp.where(sub < 4, row_of_group(1), row_of_group(5))
        oct_level = row_of_group(3)
    else:
        pair_level = jnp.where((sub & 1) == 0, pltpu.roll(cum3, 7, 1), cum3)
        quad_level = jnp.where(sub < 4, row_of_group(2), row_of_group(6))
        oct_level = row_of_group(4)
    out = [x.reshape(c, w) for x in (pair_level, quad_level, oct_level)]
    for lv in range(4, levels + 1):
        size = 1 << lv
        at = size // 2 - 1 if d == 0 else size // 2
        out.append(jnp.concatenate([jnp.broadcast_to(cum[b0 + at:b0 + at + 1, :], (size, w))
                                    for b0 in range(0, c, size)], axis=0))
    return out


def _gla_chunk(q, k, v, g2, s, stack, qside_ref, pair_ref, d):
    c = q.shape[0]
    levels = int(math.log2(c))
    gcat = jnp.concatenate(_split3(g2), axis=0)
    cum = _dot(stack, gcat)
    tot_col = _dot_tn(gcat, jnp.ones((3 * c, LANES), BF16))
    diag = jnp.sum(q * k, axis=-1, keepdims=True)
    yield
    tot = cum[0:1] if d == 1 else cum[c - 1:c]
    split = _split_values(cum, d)
    zz = []
    for lv in range(1, levels + 1):
        e = jnp.exp2(-jnp.abs(cum - split[lv - 1]))
        z = (jnp.where(qside_ref[d, lv - 1] > 0.5, q, k) * e).astype(BF16)
        zz.append(_dot_nt(z, z))
    q_in = (q * jnp.exp2(cum)).astype(BF16)
    ks = (k * jnp.exp2(tot - cum)).astype(BF16)
    ds = _dot_tn(ks, v)
    yield
    a = pair_ref[d, 0] * diag
    for lv in range(1, levels + 1):
        a = a + pair_ref[d, lv] * zz[lv - 1]
    lhs = jnp.concatenate([q_in, a.astype(BF16)], axis=1)
    o = _dot(lhs, jnp.concatenate([s.astype(BF16), v], axis=0))
    s_new = s * jnp.exp2(tot_col) + ds
    yield
    return o, s_new


def _run_interleaved(gens):
    results = [None] * len(gens)
    live = list(range(len(gens)))
    while live:
        for i in list(live):
            try:
                next(gens[i])
            except StopIteration as stop:
                results[i] = stop.value
                live.remove(i)
    return results


def _hgrn_kernel(qz_ref, zf_ref, zb_ref, v_ref, gz_ref, lbf_ref, lbb_ref, og_ref, stack_ref, qside_ref, pair_ref,
                 o_ref, q_ref, kf_ref, gf_ref, kb_ref, gb_ref, of_ref, ob_ref, st_ref, *, n_ctx_chunks):
    c = GLA_CHUNK
    t = qz_ref.shape[0]
    n = t // c
    heads = range(HGRN_HEADS_PER_STEP)
    qz = qz_ref[...].astype(F32)
    q_ref[...] = qz * _sigmoid(qz) * (HGRN_K_DIM ** -0.5)
    for z_ref, lb_ref, k_ref, g_ref in ((zf_ref, lbf_ref, kf_ref, gf_ref), (zb_ref, lbb_ref, kb_ref, gb_ref)):
        lb = jnp.concatenate([lb_ref[h] for h in heads], axis=-1)
        f = lb + (1.0 - lb) * _sigmoid(z_ref[...].astype(F32))
        k_ref[...] = 1.0 - f
        g_ref[...] = jnp.log(f) * LOG2E
    st_ref[...] = jnp.zeros_like(st_ref)

    def body(i, carry):
        rf = pl.multiple_of(i * c, c)
        cb = jnp.where(i < n_ctx_chunks, n_ctx_chunks - 1 - i, n - 1 - i + n_ctx_chunks)
        rb = pl.multiple_of(cb * c, c)
        chains = [(h, d, pl.ds(r0, c), k_ref, g_ref, out_ref) for h in heads
                  for d, r0, k_ref, g_ref, out_ref in ((0, rf, kf_ref, gf_ref, of_ref), (1, rb, kb_ref, gb_ref, ob_ref))]
        col = lambda h: slice(h * LANES, (h + 1) * LANES)
        outs = _run_interleaved([
            _gla_chunk(q_ref[rows, col(h)], k_ref[rows, col(h)], v_ref[rows, col(h)], g_ref[rows, col(h)],
                       st_ref[h, d], stack_ref[d], qside_ref, pair_ref, d)
            for (h, d, rows, k_ref, g_ref, _) in chains])
        for (h, d, rows, _, _, out_ref), (o, s_new) in zip(chains, outs):
            out_ref[rows, col(h)] = o
            st_ref[h, d] = s_new
        return carry

    lax.fori_loop(0, n, body, 0)
    gz = gz_ref[...].astype(F32)
    for h in heads:
        cols = slice(h * LANES, (h + 1) * LANES)
        o = _rms(of_ref[:, cols] + ob_ref[:, cols]) * og_ref[...]
        o_ref[:, cols] = (o * (gz[:, cols] * _sigmoid(gz[:, cols]))).astype(o_ref.dtype)


def hgrn_bidirectional(p, lb_fwd, lb_bwd, out_gain, *, n_ctx, blk0):
    b, t, _ = p.shape
    c = GLA_CHUNK
    consts = _gla_constants(c)
    h = HGRN_HEADS
    hps = HGRN_HEADS_PER_STEP
    assert blk0 % hps == 0 and h % hps == 0
    seg = lambda s: pl.BlockSpec((None, t, hps * LANES), lambda bi, hi: (bi, 0, (blk0 + s * h) // hps + hi))
    per_head = pl.BlockSpec((hps, 1, LANES), lambda bi, hi: (hi, 0, 0))
    const = lambda a: pl.BlockSpec(a.shape, lambda bi, hi: (0,) * a.ndim)
    seq = pltpu.VMEM((t, hps * LANES), F32)
    return pl.pallas_call(
        functools.partial(_hgrn_kernel, n_ctx_chunks=n_ctx // c),
        grid=(b, h // hps),
        in_specs=[seg(0), seg(1), seg(2), seg(3), seg(4), per_head, per_head,
                  pl.BlockSpec((1, LANES), lambda bi, hi: (0, 0))] + [const(a) for a in consts],
        out_specs=pl.BlockSpec((None, t, hps * LANES), lambda bi, hi: (bi, 0, hi)),
        out_shape=jax.ShapeDtypeStruct((b, t, h * LANES), BF16),
        scratch_shapes=[seq, seq, seq, seq, seq, seq, seq, pltpu.VMEM((hps, 2, LANES, LANES), F32)],
        compiler_params=_cparams(("arbitrary", "arbitrary")),
        name="hgrn_bidirectional",
    )(p, p, p, p, p, lb_fwd.reshape(h, 1, LANES), lb_bwd.reshape(h, 1, LANES),
      out_gain.astype(F32).reshape(1, LANES), *consts)


def _mla_attn_kernel(q_ref, kv_ref, kr_ref, cq_ref, saq_ref, sbq_ref, ck_ref, sak_ref, sbk_ref,
                     qn_ref, qr_ref, kn_ref, krg_ref, o_ref, kt_ref, vo_ref):
    qi = pl.program_id(2)
    ones_nope = _block_ones(MLA_NOPE)
    ones_rope = _block_ones(MLA_ROPE)
    hw = 2 * LANES

    @pl.when(qi == 0)
    def _():
        kr = _block_rms(kr_ref[...].astype(F32), ones_rope, MLA_ROPE) * krg_ref[...]
        kr = _rope(kr, ck_ref[...], sak_ref[...], sbk_ref[...]).T.astype(BF16)
        for h in range(MLA_HEADS_PER_STEP):
            kn = _block_rms(kv_ref[:, h * hw:h * hw + MLA_NOPE].astype(F32), ones_nope, MLA_NOPE) * kn_ref[...]
            kt_ref[h, 0:LANES, :] = kn.T.astype(BF16)
            kt_ref[h, LANES:hw, :] = kr
            vo_ref[h, :, 0:LANES] = kv_ref[:, h * hw + MLA_NOPE:(h + 1) * hw]
            vo_ref[h, :, LANES:2 * LANES] = jnp.ones((kv_ref.shape[0], LANES), BF16)

    scale = (MLA_NOPE + MLA_ROPE) ** -0.5 * LOG2E
    qs = []
    for h in range(MLA_HEADS_PER_STEP):
        qn = _block_rms(q_ref[:, h * hw:h * hw + MLA_NOPE].astype(F32), ones_nope, MLA_NOPE) * (qn_ref[...] * scale)
        qr = _block_rms(q_ref[:, h * hw + MLA_NOPE:(h + 1) * hw].astype(F32), ones_rope, MLA_ROPE) * qr_ref[...]
        qr = _rope(qr, cq_ref[...], saq_ref[...], sbq_ref[...]) * scale
        qs.append(jnp.concatenate([qn.astype(BF16), qr.astype(BF16)], axis=-1))
    heads = range(MLA_HEADS_PER_STEP)
    outs = _attend_streams(qs, [kt_ref.at[h] for h in heads], [vo_ref.at[h] for h in heads], kt_ref.shape[2])
    for h, (o, l) in enumerate(outs):
        o_ref[:, h * MLA_V:(h + 1) * MLA_V] = (o * (1.0 / l)).astype(o_ref.dtype)


def mla_attention(q, kv, p1, kr_blk, rope, qn_gain, qr_gain, kn_gain, kr_gain, *, tq, n_ctx):
    b, n_lat, _ = q.shape
    t = kv.shape[1]
    c, sa, sb = rope
    hps = MLA_HEADS_PER_STEP
    row_q = lambda bi, h, qi: (qi, 0)
    full = lambda bi, h, qi: (0, 0)
    pad = lambda g: jnp.concatenate([g.astype(F32), jnp.zeros((LANES - g.shape[0],), F32)]).reshape(1, LANES)
    return pl.pallas_call(
        _mla_attn_kernel,
        grid=(b, MLA_HEADS // hps, n_lat // tq),
        in_specs=[
            pl.BlockSpec((None, tq, hps * 2 * LANES), lambda bi, h, qi: (bi, qi, h)),
            pl.BlockSpec((None, t, hps * 2 * LANES), lambda bi, h, qi: (bi, 0, h)),
            pl.BlockSpec((None, t, LANES), lambda bi, h, qi: (bi, 0, kr_blk)),
            pl.BlockSpec((tq, LANES), row_q), pl.BlockSpec((tq, LANES), row_q), pl.BlockSpec((tq, LANES), row_q),
            pl.BlockSpec((t, LANES), full), pl.BlockSpec((t, LANES), full), pl.BlockSpec((t, LANES), full),
            pl.BlockSpec((1, LANES), full), pl.BlockSpec((1, LANES), full),
            pl.BlockSpec((1, LANES), full), pl.BlockSpec((1, LANES), full),
        ],
        out_specs=pl.BlockSpec((None, tq, hps * MLA_V), lambda bi, h, qi: (bi, qi, h)),
        out_shape=jax.ShapeDtypeStruct((b, n_lat, MLA_HEADS * MLA_V), BF16),
        scratch_shapes=[pltpu.VMEM((hps, 2 * LANES, t), BF16), pltpu.VMEM((hps, t, 2 * LANES), BF16)],
        compiler_params=_cparams(("arbitrary", "arbitrary", "arbitrary")),
        name="mla_attention",
    )(q, kv, p1, c[n_ctx:], sa[n_ctx:], sb[n_ctx:], c, sa, sb, pad(qn_gain), pad(qr_gain), pad(kn_gain), pad(kr_gain))


def _proj_residual_kernel(*refs, n_in, n_x, n_ctx_tiles, row_tile0):
    a_refs = refs[:n_in]
    w_refs = refs[n_in:2 * n_in]
    x_refs = refs[2 * n_in:2 * n_in + n_x]
    m_ref, o_ref = refs[2 * n_in + n_x:]
    acc = _dot(a_refs[0][...], w_refs[0][...])
    for a_ref, w_ref in zip(a_refs[1:], w_refs[1:]):
        acc = acc + _dot(a_ref[...], w_ref[...])
    o_ref[...] = _stream_tile(x_refs, n_ctx_tiles, row_tile0) + m_ref[G_M:G_M + 1, :] * acc


def proj_residual(acts, weights, xs, mods, *, tm, n_ctx_tiles, row_tile0=0):
    b, r, _ = acts[0].shape
    x_specs, x_args = _stream_specs(xs, tm, n_ctx_tiles, row_tile0)
    d = x_args[0].shape[-1]
    in_specs = [pl.BlockSpec((None, tm, a.shape[-1]), lambda bi, ti: (bi, ti, 0)) for a in acts]
    in_specs += [pl.BlockSpec(w.shape, lambda bi, ti: (0, 0)) for w in weights]
    in_specs += x_specs + [
        pl.BlockSpec((None, None, 6, d), lambda bi, ti: (bi, ((ti + row_tile0) >= n_ctx_tiles).astype(jnp.int32), 0, 0)),
    ]
    return pl.pallas_call(
        functools.partial(_proj_residual_kernel, n_in=len(acts), n_x=len(x_args), n_ctx_tiles=n_ctx_tiles,
                          row_tile0=row_tile0),
        grid=(b, r // tm),
        in_specs=in_specs,
        out_specs=pl.BlockSpec((None, tm, d), lambda bi, ti: (bi, ti, 0)),
        out_shape=jax.ShapeDtypeStruct((b, r, d), F32),
        compiler_params=_cparams(("arbitrary", "arbitrary")),
        name="proj_residual",
    )(*acts, *weights, *x_args, mods)


PAIRS = [(i, j) for i in range(EXPERTS_PER_GROUP) for j in range(i + 1, EXPERTS_PER_GROUP)]
N_CLASSES = N_GROUPS * len(PAIRS)
CLASS_LO = np.array([EXPERTS_PER_GROUP * g + i for g in range(N_GROUPS) for (i, j) in PAIRS], np.int32)
CLASS_HI = np.array([EXPERTS_PER_GROUP * g + j for g in range(N_GROUPS) for (i, j) in PAIRS], np.int32)
EXPERT_TILE = 256
ROW_SLABS = 8
DMA_UNROLL = 8
ROW_SLOTS = 4


def _route_kernel(x_ref, m_ref, g_ref, rw_ref, rb_ref, hf_ref, rows_ref, info_ref, count_ref, cnt_ref):
    h = _rms(x_ref[...]) * g_ref[...]
    h = h * (1.0 + m_ref[SC_F:SC_F + 1, :]) + m_ref[SH_F:SH_F + 1, :]
    hf_ref[...] = h.astype(hf_ref.dtype)
    rows_ref[...] = _rows_to_slabs(h)
    h1, h2, _ = _split3(h)
    w1, w2, _ = _split3(rw_ref[...])
    logits = _dot_nt(w1, h1) + (_dot_nt(w1, h2) + _dot_nt(w2, h1))
    biased = _sigmoid(logits) + rb_ref[...]
    row = [biased[e:e + 1, :] for e in range(N_EXPERTS)]
    gscore = []
    for g in range(N_GROUPS):
        m = row[4 * g:4 * g + 4]
        gscore.append(functools.reduce(jnp.maximum, [m[i] + m[j] for (i, j) in PAIRS]))
    hits = []
    for g in range(N_GROUPS):
        best = None
        for g2 in range(N_GROUPS):
            if g2 == g:
                continue
            wins = (gscore[g] > gscore[g2]) if g2 < g else (gscore[g] >= gscore[g2])
            best = wins if best is None else jnp.logical_and(best, wins)
        chosen = []
        for i in range(EXPERTS_PER_GROUP):
            rank = None
            for j in range(EXPERTS_PER_GROUP):
                if j == i:
                    continue
                mi, mj = row[4 * g + i], row[4 * g + j]
                ahead = ((mj >= mi) if j < i else (mj > mi)).astype(jnp.int32)
                rank = ahead if rank is None else rank + ahead
            chosen.append(rank < 2)
        for (i, j) in PAIRS:
            hits.append(jnp.where(best & chosen[i] & chosen[j], 1.0, 0.0))
    onehot = jnp.concatenate(hits, axis=0)
    tm = onehot.shape[1]
    upper = (lax.broadcasted_iota(jnp.int32, (tm, tm), 0) <= lax.broadcasted_iota(jnp.int32, (tm, tm), 1))
    prefix = _dot(onehot.astype(BF16), upper.astype(BF16))

    @pl.when((pl.program_id(0) == 0) & (pl.program_id(1) == 0))
    def _():
        cnt_ref[...] = jnp.zeros_like(cnt_ref)

    seen = cnt_ref[...]
    cls_id = lax.broadcasted_iota(jnp.int32, onehot.shape, 0).astype(F32)
    cls = jnp.sum(onehot * cls_id, axis=0, keepdims=True)
    rank = jnp.sum(onehot * (seen[:, 0:1] + prefix - 1.0), axis=0, keepdims=True)
    info_ref[...] = jnp.concatenate([cls, rank, jnp.zeros((6, tm), F32)], axis=0).astype(jnp.int32)
    seen = seen + jnp.sum(onehot, axis=1, keepdims=True)
    cnt_ref[...] = seen
    count_ref[...] = seen


def route(xs, mods, gain, router_w, router_bias, *, tm, n_ctx_tiles):
    b, t, d = xs.shape
    nt = t // tm
    return pl.pallas_call(
        _route_kernel,
        grid=(b, nt),
        in_specs=[
            pl.BlockSpec((None, tm, d), lambda bi, ti: (bi, ti, 0)),
            pl.BlockSpec((None, None, 6, d), lambda bi, ti: (bi, (ti >= n_ctx_tiles).astype(jnp.int32), 0, 0)),
            pl.BlockSpec((1, d), lambda bi, ti: (0, 0)),
            pl.BlockSpec((N_EXPERTS, d), lambda bi, ti: (0, 0)),
            pl.BlockSpec((N_EXPERTS, 1), lambda bi, ti: (0, 0)),
        ],
        out_specs=[
            pl.BlockSpec((None, tm, d), lambda bi, ti: (bi, ti, 0)),
            pl.BlockSpec((tm, ROW_SLABS, LANES), lambda bi, ti: (bi * nt + ti, 0, 0)),
            pl.BlockSpec((None, None, 8, tm), lambda bi, ti: (bi, ti, 0, 0)),
            pl.BlockSpec((N_CLASSES, LANES), lambda bi, ti: (0, 0)),
        ],
        out_shape=[jax.ShapeDtypeStruct((b, t, d), BF16),
                   jax.ShapeDtypeStruct((b * t, ROW_SLABS, LANES), F32),
                   jax.ShapeDtypeStruct((b, nt, 8, tm), jnp.int32),
                   jax.ShapeDtypeStruct((N_CLASSES, LANES), F32)],
        scratch_shapes=[pltpu.VMEM((N_CLASSES, LANES), F32)],
        compiler_params=_cparams(("arbitrary", "arbitrary")),
        name="route",
    )(xs, mods, gain.reshape(1, d).astype(F32), router_w.T.astype(F32),
      router_bias.reshape(N_EXPERTS, 1).astype(F32))


def _transpose8(parts):
    sub = lax.broadcasted_iota(jnp.int32, parts[0].shape, 1)
    for s in (4, 2, 1):
        keep = (sub & s) == 0
        new = list(parts)
        for i in range(8):
            if i & s:
                continue
            a, b = parts[i], parts[i | s]
            new[i] = jnp.where(keep, a, pltpu.roll(b, s, 1))
            new[i | s] = jnp.where(keep, pltpu.roll(a, 8 - s, 1), b)
        parts = new
    return parts


def _slabs_to_rows(slabs):
    g = slabs.shape[0] // 8
    x4 = slabs.reshape(g, 8, ROW_SLABS, LANES)
    parts = _transpose8([x4[:, t] for t in range(8)])
    return jnp.concatenate([p.reshape(g * 8, LANES) for p in parts], axis=-1)


def _rows_to_slabs(x):
    g = x.shape[0] // 8
    parts = _transpose8([x[:, k * LANES:(k + 1) * LANES].reshape(g, 8, LANES) for k in range(ROW_SLABS)])
    return jnp.stack(parts, axis=1).reshape(g * 8, ROW_SLABS, LANES)


def _invert_kernel(dest_ref, src_ref):
    def clear(i, carry):
        for u in range(2 * DMA_UNROLL):
            src_ref[i * 2 * DMA_UNROLL + u] = 0
        return carry

    def put(i, carry):
        for u in range(2 * DMA_UNROLL):
            t = i * 2 * DMA_UNROLL + u
            src_ref[dest_ref[t]] = t
        return carry

    lax.fori_loop(0, src_ref.shape[0] // (2 * DMA_UNROLL), clear, 0)
    lax.fori_loop(0, dest_ref.shape[0] // (2 * DMA_UNROLL), put, 0)


def invert_permutation(dest, n_rows_out):
    smem = pl.BlockSpec(memory_space=pltpu.SMEM)
    return pl.pallas_call(
        _invert_kernel,
        in_specs=[smem],
        out_specs=smem,
        out_shape=jax.ShapeDtypeStruct((n_rows_out,), jnp.int32),
        name="moe_invert",
    )(dest)


def _expert_kernel(te_ref, tv_ref, nx_ref, src_ref, rows_hbm, rwl_ref, rwh_ref, wg_hbm, wu_hbm, wd_hbm, y_ref,
                   xbuf_ref, sem_ref, stage_g, stage_u, stage_d, cache_g, cache_u, cache_d, wsem_ref, *, layer):
    i = pl.program_id(0)
    n = pl.num_programs(0)
    tg = y_ref.shape[0]
    slot = i % ROW_SLOTS
    valid = tv_ref[i] != 0
    prev = jnp.maximum(i - 1, 0)

    def fetch(tile, s):
        base = tile * tg
        for t in range(tg):
            pltpu.async_copy(rows_hbm.at[src_ref[base + t]], xbuf_ref.at[s, t], sem_ref.at[s], priority=t % 2)

    def wait_rows(s):
        pltpu.make_async_copy(rows_hbm.at[pl.ds(0, tg)], xbuf_ref.at[s], sem_ref.at[s]).wait()

    def weight_copies(s, e):
        return [pltpu.make_async_copy(w_hbm.at[layer, e], stage.at[s], wsem_ref.at[s])
                for w_hbm, stage in ((wg_hbm, stage_g), (wu_hbm, stage_u), (wd_hbm, stage_d))]

    @pl.when(i == 0)
    def _():
        for k in range(ROW_SLOTS - 1):
            fetch(jnp.minimum(k, n - 1), k)
        for s in range(2):
            for copy in weight_copies(s, te_ref[s * n]):
                copy.start()

    for s in range(2):
        expert = te_ref[s * n + i]

        @pl.when(valid & ((i == 0) | (expert != te_ref[s * n + prev])))
        def _():
            for copy in weight_copies(s, expert):
                copy.wait()
            cache_g[s] = stage_g[s].astype(BF16)
            cache_u[s] = stage_u[s].astype(BF16)
            cache_d[s] = stage_d[s].astype(BF16)
            upcoming = nx_ref[s * n + i]

            @pl.when(upcoming >= 0)
            def _():
                for copy in weight_copies(s, upcoming):
                    copy.start()

    @pl.when(valid)
    def _():
        wait_rows(slot)
        x = _slabs_to_rows(xbuf_ref[slot])
        fetch(jnp.minimum(i + ROW_SLOTS - 1, n - 1), (i + ROW_SLOTS - 1) % ROW_SLOTS)
        xb = x.astype(BF16)
        s_lo = _sigmoid(jnp.sum(x * rwl_ref[...], axis=-1, keepdims=True))
        s_hi = _sigmoid(jnp.sum(x * rwh_ref[...], axis=-1, keepdims=True))
        inv = 1.0 / (s_lo + s_hi)

        def ffn(s):
            a = _dot(xb, cache_g[s])
            u = _dot(xb, cache_u[s])
            return _dot((a * _sigmoid(a) * u).astype(BF16), cache_d[s])

        y_ref[...] = _rows_to_slabs((s_lo * inv) * ffn(0) + (s_hi * inv) * ffn(1))

    @pl.when(jnp.logical_not(valid))
    def _():
        y_ref[...] = jnp.zeros_like(y_ref)

    @pl.when((valid & (i == n - 1)) | (jnp.logical_not(valid) & (tv_ref[prev] != 0)))
    def _():
        first = jnp.where(valid, i + 1, i)
        for k in range(ROW_SLOTS - 1):
            wait_rows((first + k) % ROW_SLOTS)


def experts(tile_expert, tile_valid, next_expert, src, rows, router_w_t, wg, wu, wd, layer):
    tg = EXPERT_TILE
    n_tiles = src.shape[0] // tg
    d, ff = wg.shape[2:]
    lo = lambda i, te, tv, nx, src: (te[i], 0, 0)
    hi = lambda i, te, tv, nx, src: (te[n_tiles + i], 0, 0)
    hbm = pl.BlockSpec(memory_space=pl.ANY)
    return pl.pallas_call(
        functools.partial(_expert_kernel, layer=layer),
        grid_spec=pltpu.PrefetchScalarGridSpec(
            num_scalar_prefetch=4,
            grid=(n_tiles,),
            in_specs=[hbm, pl.BlockSpec((None, 1, d), lo), pl.BlockSpec((None, 1, d), hi), hbm, hbm, hbm],
            out_specs=pl.BlockSpec((tg, ROW_SLABS, LANES), lambda i, te, tv, nx, src: (i, 0, 0)),
            scratch_shapes=[pltpu.VMEM((ROW_SLOTS, tg, ROW_SLABS, LANES), F32), pltpu.SemaphoreType.DMA((ROW_SLOTS,)),
                            pltpu.VMEM((2, d, ff), F32), pltpu.VMEM((2, d, ff), F32), pltpu.VMEM((2, ff, d), F32),
                            pltpu.VMEM((2, d, ff), BF16), pltpu.VMEM((2, d, ff), BF16), pltpu.VMEM((2, ff, d), BF16),
                            pltpu.SemaphoreType.DMA((2,))],
        ),
        out_shape=jax.ShapeDtypeStruct((n_tiles * tg, ROW_SLABS, LANES), F32),
        compiler_params=_cparams(("arbitrary",)),
        name="moe_experts",
    )(tile_expert, tile_valid, next_expert, src, rows, router_w_t, router_w_t, wg, wu, wd)


def _combine_kernel(dest_ref, hf_ref, x_ref, m_ref, wg_ref, wu_ref, wd_ref, y_hbm, o_ref, ybuf_ref, sem_ref):
    i = pl.program_id(0)
    n = pl.num_programs(0)
    tm = hf_ref.shape[0]
    slot = i % 2

    def fetch(tile, s):
        base = tile * tm
        for t in range(tm):
            pltpu.async_copy(y_hbm.at[dest_ref[base + t]], ybuf_ref.at[s, t], sem_ref.at[s], priority=t % 2)

    def wait_rows(s):
        pltpu.make_async_copy(y_hbm.at[pl.ds(0, tm)], ybuf_ref.at[s], sem_ref.at[s]).wait()

    @pl.when(i == 0)
    def _():
        fetch(0, 0)

    fetch(jnp.minimum(i + 1, n - 1), 1 - slot)
    h = hf_ref[...]
    a = _dot(h, wg_ref[...])
    u = _dot(h, wu_ref[...])
    shared = _dot((a * _sigmoid(a) * u).astype(BF16), wd_ref[...])
    wait_rows(slot)
    o_ref[...] = x_ref[...] + m_ref[G_F:G_F + 1, :] * (shared + _slabs_to_rows(ybuf_ref[slot]))

    @pl.when(i == n - 1)
    def _():
        wait_rows(1 - slot)


def combine(dest, hf, xs, mods, wg, wu, wd, y_sorted, *, tm, n_ctx_tiles):
    b, r, d = hf.shape
    nt = r // tm
    ff = wg.shape[1]
    row = lambda i, dest: (i // nt, i % nt, 0)
    const = lambda i, dest: (0, 0)
    return pl.pallas_call(
        _combine_kernel,
        grid_spec=pltpu.PrefetchScalarGridSpec(
            num_scalar_prefetch=1,
            grid=(b * nt,),
            in_specs=[
                pl.BlockSpec((None, tm, d), row),
                pl.BlockSpec((None, tm, d), row),
                pl.BlockSpec((None, None, 6, d),
                             lambda i, dest: (i // nt, ((i % nt) >= n_ctx_tiles).astype(jnp.int32), 0, 0)),
                pl.BlockSpec((d, ff), const), pl.BlockSpec((d, ff), const), pl.BlockSpec((ff, d), const),
                pl.BlockSpec(memory_space=pl.ANY),
            ],
            out_specs=pl.BlockSpec((None, tm, d), row),
            scratch_shapes=[pltpu.VMEM((2, tm, ROW_SLABS, LANES), F32), pltpu.SemaphoreType.DMA((2,))],
        ),
        out_shape=jax.ShapeDtypeStruct((b, r, d), F32),
        compiler_params=_cparams(("arbitrary",)),
        name="moe_combine",
    )(dest, hf, xs, mods, wg, wu, wd, y_sorted)


def moe_block(xs, mods, gain, router_w, router_bias, ew_gate, ew_up, ew_down, layer, sw_gate, sw_up, sw_down,
              *, tm, n_ctx_tiles):
    b, r, d = xs.shape
    n = b * r
    tg = EXPERT_TILE
    hf, rows, info, counts = route(xs, mods, gain, router_w, router_bias, tm=tm, n_ctx_tiles=n_ctx_tiles)
    cls = info[:, :, 0, :].reshape(n)
    rank = info[:, :, 1, :].reshape(n)
    padded = ((counts[:, 0].astype(jnp.int32) + tg - 1) // tg) * tg
    ends = jnp.cumsum(padded)
    dest = (ends - padded)[cls] + rank
    n_tiles = n // tg + N_CLASSES
    tile_start = jnp.arange(n_tiles, dtype=jnp.int32) * tg
    tile_valid = tile_start < ends[-1]
    last_start = jnp.maximum(ends[-1] - tg, 0)
    start = jnp.where(tile_valid, tile_start, last_start)
    tile_cls = jnp.sum((ends[None, :] <= start[:, None]).astype(jnp.int32), axis=1)
    tile_cls = jnp.minimum(tile_cls, N_CLASSES - 1)
    slot_expert = jnp.stack([jnp.asarray(CLASS_LO)[tile_cls], jnp.asarray(CLASS_HI)[tile_cls]])
    tile_expert = slot_expert.reshape(-1)
    later = (tile_start[None, :] > tile_start[:, None]) & tile_valid[None, :]
    differs = slot_expert[:, None, :] != slot_expert[:, :, None]
    first = jnp.min(jnp.where(later[None] & differs, jnp.arange(n_tiles)[None, None, :], n_tiles), axis=-1)
    next_expert = jnp.where(first < n_tiles, jnp.take_along_axis(slot_expert, jnp.minimum(first, n_tiles - 1), axis=1),
                            -1).reshape(-1).astype(jnp.int32)
    src = invert_permutation(dest, n_tiles * tg)
    y_sorted = experts(tile_expert, tile_valid.astype(jnp.int32), next_expert, src, rows,
                       router_w.T.astype(F32).reshape(N_EXPERTS, 1, d), ew_gate, ew_up, ew_down, layer)
    return combine(dest, hf, xs, mods, sw_gate.astype(BF16), sw_up.astype(BF16), sw_down.astype(BF16), y_sorted,
                   tm=tm, n_ctx_tiles=n_ctx_tiles)


def kernel(x, c, ctx, c_ctx, mod_w, mod_b, norm_mix, norm_ffn, even_w_in, even_w_out, diff_q_gain, diff_k_gain, diff_lambda, diff_subln, hgrn_lb_logits, hgrn_out_gain, odd_w_in, mla_q_a_gain, mla_kv_a_gain, mla_w_uq, mla_w_ukv, mla_q_nope_gain, mla_q_rope_gain, mla_k_nope_gain, mla_k_rope_gain, odd_w_out, router_w, router_bias, expert_w_gate, expert_w_up, expert_w_down, shared_w_gate, shared_w_up, shared_w_down):
    b, n_lat, d = x.shape
    n_ctx = ctx.shape[1]
    depth = mod_w.shape[0]
    tm = 256 if n_ctx % 256 == 0 else 128
    n_ctx_tiles = n_ctx // tm
    rope = rope_lane_tables(n_ctx, n_lat)

    mod_rows = 16
    c_rows = jnp.concatenate([c, c_ctx[None, :], jnp.zeros((mod_rows - b - 1, d), F32)], axis=0)
    mod_all = modulation(c_rows, mod_w, mod_b).reshape(depth, mod_rows, 6, d)
    lb_all = jnp.cumsum(jax.nn.softmax(hgrn_lb_logits.astype(F32), axis=0), axis=0)

    xs = (ctx, x)
    for layer in range(depth):
        last = layer == depth - 1
        j = layer // 2
        mods = jnp.stack([jnp.broadcast_to(mod_all[layer, b], (b, 6, d)), mod_all[layer, :b]], axis=1)
        row_tile0 = n_ctx_tiles if last else 0
        if layer % 2 == 0:
            lam_init = 0.8 - 0.6 * math.exp(-0.3 * layer)
            p = norm_matmul(xs, mods, norm_mix[layer], even_w_in[j].astype(BF16), tm=tm, n_ctx_tiles=n_ctx_tiles)
            oa = diff_attention(p, diff_lambda[j], diff_q_gain[j], diff_k_gain[j], diff_subln[j], rope,
                                tq=tm, n_ctx=n_ctx, lam_init=lam_init, q_blk0=0, k_blk0=4, v_blk0=8)
            ob = hgrn_bidirectional(p, lb_all[j, 0], lb_all[j, 1], hgrn_out_gain[j], n_ctx=n_ctx, blk0=12)
            if last:
                oa, ob = oa[:, n_ctx:], ob[:, n_ctx:]
            w_out = even_w_out[j].astype(BF16)
            half = oa.shape[-1]
            xs_mid = proj_residual([oa, ob], [w_out[:half], w_out[half:]], xs, mods, tm=tm,
                                   n_ctx_tiles=n_ctx_tiles, row_tile0=row_tile0)
        else:
            w_in = odd_w_in[j]
            zpad = jnp.zeros((d, LANES - MLA_ROPE), F32)
            w_in_r = jnp.concatenate([w_in[:, MLA_Q_LORA:MLA_Q_LORA + MLA_KV_LORA], w_in[:, MLA_Q_LORA + MLA_KV_LORA:],
                                      zpad, w_in[:, :MLA_Q_LORA]], axis=1).astype(BF16)
            w_uq = mla_w_uq[j].reshape(MLA_Q_LORA, MLA_HEADS, MLA_NOPE + MLA_ROPE)
            w_uq = jnp.concatenate([w_uq, jnp.zeros((MLA_Q_LORA, MLA_HEADS, LANES - MLA_ROPE), F32)], axis=-1)
            w_uq = w_uq.reshape(MLA_Q_LORA, MLA_HEADS * 2 * LANES).astype(BF16)
            if not last:
                raise NotImplementedError("context queries for a non-final latent-attention layer")
            kr, q, kv = mla_projections(xs, mods, norm_mix[layer], w_in_r, mla_q_a_gain[j], mla_kv_a_gain[j], w_uq,
                                        mla_w_ukv[j].astype(BF16), tm=tm, n_ctx_tiles=n_ctx_tiles)
            o = mla_attention(q, kv, kr, 0, rope, mla_q_nope_gain[j], mla_q_rope_gain[j], mla_k_nope_gain[j],
                              mla_k_rope_gain[j], tq=2 * tm, n_ctx=n_ctx)
            xs_mid = proj_residual([o], [odd_w_out[j].astype(BF16)], xs, mods, tm=tm,
                                   n_ctx_tiles=n_ctx_tiles, row_tile0=row_tile0)
        xs = moe_block(xs_mid, mods, norm_ffn[layer], router_w, router_bias,
                       expert_w_gate, expert_w_up, expert_w_down, layer,
                       shared_w_gate[layer], shared_w_up[layer], shared_w_down[layer],
                       tm=tm, n_ctx_tiles=0 if last else n_ctx_tiles)
    return xs if xs.shape[1] == n_lat else xs[:, n_ctx:]
```

```python
import functools
import math

import numpy as np
import jax
import jax.numpy as jnp
from jax import lax
from jax.experimental import pallas as pl
from jax.experimental.pallas import tpu as pltpu

F32 = jnp.float32
BF16 = jnp.bfloat16

LANES = 128
VMEM_LIMIT = 56 * 1024 * 1024

GRID_W = 64
DIFF_HEADS = 4
DIFF_HEAD_DIM = 64
HGRN_HEADS = 4
HGRN_K_DIM = 128
MLA_HEADS = 8
MLA_NOPE = 128
MLA_ROPE = 64
MLA_V = 128
MLA_Q_LORA = 384
MLA_KV_LORA = 256
N_EXPERTS = 16
N_GROUPS = 4
EXPERTS_PER_GROUP = 4
ROPE_BASE = 10000.0
EPS = 1e-6
LOG2E = 1.4426950408889634
GLA_CHUNK = 64
ATTN_KEY_CHUNK = 1152
MLA_HEADS_PER_STEP = 2
DIFF_HEADS_PER_STEP = 2
HGRN_HEADS_PER_STEP = 2

SH_M, SC_M, G_M, SH_F, SC_F, G_F = range(6)


def _sigmoid(x):
    return 0.5 * jnp.tanh(0.5 * x) + 0.5


def _dot(a, b):
    return jnp.dot(a, b, preferred_element_type=F32)


def _dot_nt(a, b):
    return lax.dot_general(a, b, (((1,), (1,)), ((), ())), preferred_element_type=F32)


def _dot_tn(a, b):
    return lax.dot_general(a, b, (((0,), (0,)), ((), ())), preferred_element_type=F32)


def _split3(x):
    hi = x.astype(BF16)
    r = x - hi.astype(F32)
    mid = r.astype(BF16)
    lo = (r - mid.astype(F32)).astype(BF16)
    return hi, mid, lo


def _rms(x, width=None):
    n = x.shape[-1] if width is None else width
    return x * lax.rsqrt(jnp.sum(x * x, axis=-1, keepdims=True) * (1.0 / n) + EPS)


def _cparams(sem):
    return pltpu.CompilerParams(dimension_semantics=sem, vmem_limit_bytes=VMEM_LIMIT)


def _mod_kernel(c_ref, w_ref, b_ref, o_ref):
    c = c_ref[...]
    s = c * _sigmoid(c)
    o_ref[...] = _dot(s.astype(BF16), w_ref[...].astype(BF16)) + b_ref[...]


def modulation(c_rows, mod_w, mod_b):
    n_layers, d, n = mod_w.shape
    rows = c_rows.shape[0]
    tn = 1536
    return pl.pallas_call(
        _mod_kernel,
        grid=(n_layers, n // tn),
        in_specs=[
            pl.BlockSpec((rows, d), lambda l, j: (0, 0)),
            pl.BlockSpec((None, d, tn), lambda l, j: (l, 0, j)),
            pl.BlockSpec((None, 1, tn), lambda l, j: (l, 0, j)),
        ],
        out_specs=pl.BlockSpec((None, rows, tn), lambda l, j: (l, 0, j)),
        out_shape=jax.ShapeDtypeStruct((n_layers, rows, n), F32),
        compiler_params=_cparams(("arbitrary", "arbitrary")),
        name="modulation",
    )(c_rows, mod_w, mod_b.reshape(n_layers, 1, n))


def _stream_specs(xs, tm, n_ctx_tiles, row_tile0=0):
    parts = xs if isinstance(xs, tuple) else (xs,)
    d = parts[0].shape[-1]
    if len(parts) == 1:
        return [pl.BlockSpec((None, tm, d), lambda bi, ti: (bi, ti + row_tile0, 0))], list(parts)
    return [pl.BlockSpec((None, tm, d), lambda bi, ti: (bi, jnp.minimum(ti + row_tile0, n_ctx_tiles - 1), 0)),
            pl.BlockSpec((None, tm, d), lambda bi, ti: (bi, jnp.maximum(ti + row_tile0 - n_ctx_tiles, 0), 0))], list(parts)


def _stream_tile(x_refs, n_ctx_tiles, row_tile0=0):
    if len(x_refs) == 1:
        return x_refs[0][...]
    return jnp.where(pl.program_id(1) + row_tile0 < n_ctx_tiles, x_refs[0][...], x_refs[1][...])


def _norm_matmul_kernel(*refs, n_x, n_ctx_tiles):
    m_ref, g_ref, w_ref, o_ref = refs[n_x:]
    h = _rms(_stream_tile(refs[:n_x], n_ctx_tiles)) * g_ref[...]
    h = h * (1.0 + m_ref[SC_M:SC_M + 1, :]) + m_ref[SH_M:SH_M + 1, :]
    o_ref[...] = _dot(h.astype(BF16), w_ref[...]).astype(o_ref.dtype)


def norm_matmul(xs, mods, gain, w, *, tm, n_ctx_tiles):
    x_specs, x_args = _stream_specs(xs, tm, n_ctx_tiles)
    b = x_args[0].shape[0]
    t = sum(a.shape[1] for a in x_args)
    k, n = w.shape
    return pl.pallas_call(
        functools.partial(_norm_matmul_kernel, n_x=len(x_args), n_ctx_tiles=n_ctx_tiles),
        grid=(b, t // tm),
        in_specs=x_specs + [
            pl.BlockSpec((None, None, 6, k), lambda bi, ti: (bi, (ti >= n_ctx_tiles).astype(jnp.int32), 0, 0)),
            pl.BlockSpec((1, k), lambda bi, ti: (0, 0)), pl.BlockSpec((k, n), lambda bi, ti: (0, 0))],
        out_specs=pl.BlockSpec((None, tm, n), lambda bi, ti: (bi, ti, 0)),
        out_shape=jax.ShapeDtypeStruct((b, t, n), BF16),
        compiler_params=_cparams(("arbitrary", "arbitrary")),
        name="norm_matmul",
    )(*x_args, mods, gain.reshape(1, k).astype(F32), w)


def _mla_proj_kernel(x_ref, m_ref, g_ref, win_ref, qg_ref, kvg_ref, wuq_ref, wukv_ref, kr_ref, q_ref, kv_ref,
                     *, n_ctx_tiles):
    h = _rms(x_ref[...]) * g_ref[...]
    h = h * (1.0 + m_ref[SC_M:SC_M + 1, :]) + m_ref[SH_M:SH_M + 1, :]
    p = _dot(h.astype(BF16), win_ref[...])
    kr_ref[...] = p[:, MLA_KV_LORA:MLA_KV_LORA + LANES].astype(kr_ref.dtype)
    ckv = _rms(p[:, 0:MLA_KV_LORA]) * kvg_ref[...]
    kv_ref[...] = _dot(ckv.astype(BF16), wukv_ref[...]).astype(kv_ref.dtype)

    @pl.when(pl.program_id(1) >= n_ctx_tiles)
    def _():
        cq = _rms(p[:, MLA_KV_LORA + LANES:]) * qg_ref[...]
        q_ref[...] = _dot(cq.astype(BF16), wuq_ref[...]).astype(q_ref.dtype)

    @pl.when(pl.program_id(1) < n_ctx_tiles)
    def _():
        q_ref[...] = jnp.zeros_like(q_ref)


def mla_projections(xs, mods, gain, w_in_r, q_a_gain, kv_a_gain, w_uq, w_ukv, *, tm, n_ctx_tiles):
    b, t, d = xs.shape
    nt = t // tm
    n_lat = t - n_ctx_tiles * tm
    const = lambda a: pl.BlockSpec(a.shape, lambda bi, ti: (0, 0))
    row = lambda bi, ti: (bi, ti, 0)
    g2 = lambda g: g.reshape(1, -1).astype(F32)
    args = [g2(gain), w_in_r, g2(q_a_gain), g2(kv_a_gain), w_uq, w_ukv]
    return pl.pallas_call(
        functools.partial(_mla_proj_kernel, n_ctx_tiles=n_ctx_tiles),
        grid=(b, nt),
        in_specs=[pl.BlockSpec((None, tm, d), row),
                  pl.BlockSpec((None, None, 6, d), lambda bi, ti: (bi, (ti >= n_ctx_tiles).astype(jnp.int32), 0, 0))]
                 + [const(a) for a in args],
        out_specs=[pl.BlockSpec((None, tm, LANES), row),
                   pl.BlockSpec((None, tm, w_uq.shape[1]), lambda bi, ti: (bi, jnp.maximum(ti - n_ctx_tiles, 0), 0)),
                   pl.BlockSpec((None, tm, w_ukv.shape[1]), row)],
        out_shape=[jax.ShapeDtypeStruct((b, t, LANES), BF16),
                   jax.ShapeDtypeStruct((b, n_lat, w_uq.shape[1]), BF16),
                   jax.ShapeDtypeStruct((b, t, w_ukv.shape[1]), BF16)],
        compiler_params=_cparams(("arbitrary", "arbitrary")),
        name="mla_projections",
    )(xs, mods, *args)


def rope_lane_tables(n_ctx, n_lat):
    rows = n_lat // GRID_W
    row = np.repeat(np.arange(rows), GRID_W).astype(np.float32)
    col = np.tile(np.arange(GRID_W), rows).astype(np.float32)
    axis_dim = DIFF_HEAD_DIM // 2
    inv_freq = jnp.asarray(ROPE_BASE, F32) ** (-jnp.arange(0, axis_dim, 2, dtype=F32) / axis_dim)
    ang_r = jnp.asarray(row)[:, None] * inv_freq
    ang_c = jnp.asarray(col)[:, None] * inv_freq
    lane = np.arange(LANES)
    freq_idx = lane % 16
    use_col = (lane % 64) >= 32
    first = (lane % 32) < 16
    ang = jnp.where(use_col[None, :], ang_c[:, freq_idx], ang_r[:, freq_idx])
    cos, sin = jnp.cos(ang), jnp.sin(ang)
    c = jnp.concatenate([jnp.ones((n_ctx, LANES), F32), cos], axis=0)
    sa = jnp.concatenate([jnp.zeros((n_ctx, LANES), F32), jnp.where(first[None, :], -sin, 0.0)], axis=0)
    sb = jnp.concatenate([jnp.zeros((n_ctx, LANES), F32), jnp.where(first[None, :], 0.0, sin)], axis=0)
    return c, sa, sb


def _rope(x, c, sa, sb):
    return x * c + pltpu.roll(x, LANES - 16, 1) * sa + pltpu.roll(x, 16, 1) * sb


def _block_ones(width):
    r = lax.broadcasted_iota(jnp.int32, (LANES, LANES), 0) // width
    c = lax.broadcasted_iota(jnp.int32, (LANES, LANES), 1) // width
    return (r == c).astype(BF16)


def _block_rms(x, ones, width):
    xx = x * x
    hi = xx.astype(BF16)
    lo = (xx - hi.astype(F32)).astype(BF16)
    ms = (_dot(hi, ones) + _dot(lo, ones)) * (1.0 / width)
    return x * lax.rsqrt(ms + EPS)


def _attend_streams(qs, kt_refs, vo_refs, nk):
    n = len(qs)
    chunk = ATTN_KEY_CHUNK if nk % ATTN_KEY_CHUNK == 0 else nk
    nc = nk // chunk
    s = [[None] * nc for _ in range(n)]
    p = [[None] * nc for _ in range(n)]
    m, ol = [None] * n, [None] * n

    def logits(i, c):
        s[i][c] = _dot(qs[i], kt_refs[i][:, c * chunk:(c + 1) * chunk])
        mc = jnp.max(s[i][c], axis=-1, keepdims=True)
        m[i] = mc if c == 0 else jnp.maximum(m[i], mc)

    def exps(i, c):
        p[i][c] = jnp.exp2(s[i][c] - m[i]).astype(BF16)

    def values(i, c):
        oc = _dot(p[i][c], vo_refs[i][c * chunk:(c + 1) * chunk, :])
        ol[i] = oc if c == 0 else ol[i] + oc

    for step in range(n + 2):
        for c in range(nc):
            if step < n:
                logits(step, c)
            if 0 <= step - 1 < n:
                exps(step - 1, c)
            if 0 <= step - 2 < n:
                values(step - 2, c)
    return [(x[:, 0:LANES], x[:, LANES:LANES + 1]) for x in ol]


def _diff_attn_kernel(lam_ref, q_ref, k_ref, v_ref, cq_ref, saq_ref, sbq_ref, ck_ref, sak_ref, sbk_ref,
                      qg_ref, kg_ref, sub_ref, o_ref, kt_ref, vo_ref, *, n_ctx, n_ctx_tiles, lam_init):
    qi = pl.program_id(2)
    ones = _block_ones(DIFF_HEAD_DIM)
    heads = range(DIFF_HEADS_PER_STEP)

    @pl.when(qi == 0)
    def _():
        for h in heads:
            k = _block_rms(k_ref[:, h * LANES:(h + 1) * LANES].astype(F32), ones, DIFF_HEAD_DIM) * kg_ref[...]
            kt_ref[h] = _rope(k, ck_ref[...], sak_ref[...], sbk_ref[...]).T.astype(BF16)
            vo_ref[h, :, 0:LANES] = v_ref[:, h * LANES:(h + 1) * LANES]
            vo_ref[h, :, LANES:2 * LANES] = jnp.ones((v_ref.shape[0], LANES), BF16)

    qs = []
    for h in heads:
        q = _block_rms(q_ref[:, h * LANES:(h + 1) * LANES].astype(F32), ones, DIFF_HEAD_DIM) * qg_ref[...]
        q = _rope(q, cq_ref[...], saq_ref[...], sbq_ref[...]) * (DIFF_HEAD_DIM ** -0.5 * LOG2E)
        lane = lax.broadcasted_iota(jnp.int32, q.shape, 1)
        qs.append(jnp.where(lane < DIFF_HEAD_DIM, q, 0.0).astype(BF16))
        qs.append(jnp.where(lane >= DIFF_HEAD_DIM, q, 0.0).astype(BF16))
    lv = lam_ref[...]
    lam = (jnp.exp(jnp.sum(lv[0:1] * lv[1:2], axis=-1, keepdims=True))
           - jnp.exp(jnp.sum(lv[2:3] * lv[3:4], axis=-1, keepdims=True)) + lam_init)

    def attend(nk):
        outs = _attend_streams(qs, [kt_ref.at[h] for h in heads for _ in range(2)],
                               [vo_ref.at[h] for h in heads for _ in range(2)], nk)
        for h in heads:
            (o1, l1), (o2, l2) = outs[2 * h], outs[2 * h + 1]
            o = o1 * (1.0 / l1) - o2 * (lam / l2)
            o_ref[:, h * LANES:(h + 1) * LANES] = (_rms(o) * sub_ref[...] * (1.0 - lam_init)).astype(o_ref.dtype)

    @pl.when(qi < n_ctx_tiles)
    def _():
        attend(n_ctx)

    @pl.when(qi >= n_ctx_tiles)
    def _():
        attend(k_ref.shape[0])


def diff_attention(p, lam_vecs, q_gain, k_gain, subln, rope, *, tq, n_ctx, lam_init, q_blk0, k_blk0, v_blk0):
    b, t, _ = p.shape
    c, sa, sb = rope
    nq = t // tq
    hps = DIFF_HEADS_PER_STEP
    assert q_blk0 % hps == 0 and k_blk0 % hps == 0 and v_blk0 % hps == 0
    row_q = lambda bi, h, qi: (qi, 0)
    full = lambda bi, h, qi: (0, 0)
    tile2 = lambda g: jnp.tile(g.astype(F32), 2).reshape(1, LANES)
    return pl.pallas_call(
        functools.partial(_diff_attn_kernel, n_ctx=n_ctx, n_ctx_tiles=n_ctx // tq, lam_init=lam_init),
        grid=(b, DIFF_HEADS // hps, nq),
        in_specs=[
            pl.BlockSpec((4, DIFF_HEAD_DIM), full),
            pl.BlockSpec((None, tq, hps * LANES), lambda bi, h, qi: (bi, qi, q_blk0 // hps + h)),
            pl.BlockSpec((None, t, hps * LANES), lambda bi, h, qi: (bi, 0, k_blk0 // hps + h)),
            pl.BlockSpec((None, t, hps * LANES), lambda bi, h, qi: (bi, 0, v_blk0 // hps + h)),
            pl.BlockSpec((tq, LANES), row_q), pl.BlockSpec((tq, LANES), row_q), pl.BlockSpec((tq, LANES), row_q),
            pl.BlockSpec((t, LANES), full), pl.BlockSpec((t, LANES), full), pl.BlockSpec((t, LANES), full),
            pl.BlockSpec((1, LANES), full), pl.BlockSpec((1, LANES), full), pl.BlockSpec((1, LANES), full),
        ],
        out_specs=pl.BlockSpec((None, tq, hps * LANES), lambda bi, h, qi: (bi, qi, h)),
        out_shape=jax.ShapeDtypeStruct((b, t, DIFF_HEADS * LANES), BF16),
        scratch_shapes=[pltpu.VMEM((hps, LANES, t), BF16), pltpu.VMEM((hps, t, 2 * LANES), BF16)],
        compiler_params=_cparams(("arbitrary", "arbitrary", "arbitrary")),
        name="diff_attention",
    )(lam_vecs.astype(F32), p, p, p, c, sa, sb, c, sa, sb, tile2(q_gain), tile2(k_gain),
      subln.astype(F32).reshape(1, LANES))


def _gla_constants(c):
    levels = int(math.log2(c))
    t = np.arange(c)[:, None]
    u = np.arange(c)[None, :]
    stack = [[u <= t], [u >= t]]
    qside = [[], []]
    pair = [[t == u], [t == u]]
    for lv in range(1, levels + 1):
        up_t = ((t >> (lv - 1)) & 1) == 1
        up_u = ((u >> (lv - 1)) & 1) == 1
        same = (t >> lv) == (u >> lv)
        qside[0].append(np.broadcast_to(up_t, (c, LANES)))
        qside[1].append(np.broadcast_to(~up_t, (c, LANES)))
        pair[0].append(same & up_t & ~up_u)
        pair[1].append(same & ~up_t & up_u)
    f32 = lambda x: np.asarray(x, np.float32)
    stack = np.stack([np.tile(f32(m[0]), (1, 3)) for m in stack])
    return (jnp.asarray(stack, BF16), jnp.asarray(np.stack([f32(m) for m in qside])),
            jnp.asarray(np.stack([f32(m) for m in pair])))


def _split_values(cum, d):
    c, w = cum.shape
    levels = int(math.log2(c))
    g = c // 8
    cum3 = cum.reshape(g, 8, w)
    sub = lax.broadcasted_iota(jnp.int32, (g, 8, w), 1)
    row_of_group = lambda off: jnp.broadcast_to(cum3[:, off:off + 1, :], (g, 8, w))
    if d == 0:
        pair_level = jnp.where((sub & 1) == 1, pltpu.roll(cum3, 1, 1), cum3)
        quad_level = jnp.where(sub < 4, row_of_group(1), row_of_group(5))
        oct_level = row_of_group(3)
    else:
        pair_level = jnp.where((sub & 1) == 0, pltpu.roll(cum3, 7, 1), cum3)
        quad_level = jnp.where(sub < 4, row_of_group(2), row_of_group(6))
        oct_level = row_of_group(4)
    out = [x.reshape(c, w) for x in (pair_level, quad_level, oct_level)]
    for lv in range(4, levels + 1):
        size = 1 << lv
        at = size // 2 - 1 if d == 0 else size // 2
        out.append(jnp.concatenate([jnp.broadcast_to(cum[b0 + at:b0 + at + 1, :], (size, w))
                                    for b0 in range(0, c, size)], axis=0))
    return out


def _gla_chunk(q, k, v, g2, s, stack, qside_ref, pair_ref, d):
    c = q.shape[0]
    levels = int(math.log2(c))
    gcat = jnp.concatenate(_split3(g2), axis=0)
    cum = _dot(stack, gcat)
    tot_col = _dot_tn(gcat, jnp.ones((3 * c, LANES), BF16))
    diag = jnp.sum(q * k, axis=-1, keepdims=True)
    yield
    tot = cum[0:1] if d == 1 else cum[c - 1:c]
    split = _split_values(cum, d)
    zz = []
    for lv in range(1, levels + 1):
        e = jnp.exp2(-jnp.abs(cum - split[lv - 1]))
        z = (jnp.where(qside_ref[d, lv - 1] > 0.5, q, k) * e).astype(BF16)
        zz.append(_dot_nt(z, z))
    q_in = (q * jnp.exp2(cum)).astype(BF16)
    ks = (k * jnp.exp2(tot - cum)).astype(BF16)
    ds = _dot_tn(ks, v)
    yield
    a = pair_ref[d, 0] * diag
    for lv in range(1, levels + 1):
        a = a + pair_ref[d, lv] * zz[lv - 1]
    lhs = jnp.concatenate([q_in, a.astype(BF16)], axis=1)
    o = _dot(lhs, jnp.concatenate([s.astype(BF16), v], axis=0))
    s_new = s * jnp.exp2(tot_col) + ds
    yield
    return o, s_new


def _run_interleaved(gens):
    results = [None] * len(gens)
    live = list(range(len(gens)))
    while live:
        for i in list(live):
            try:
                next(gens[i])
            except StopIteration as stop:
                results[i] = stop.value
                live.remove(i)
    return results


def _hgrn_kernel(qz_ref, zf_ref, zb_ref, v_ref, gz_ref, lbf_ref, lbb_ref, og_ref, stack_ref, qside_ref, pair_ref,
                 o_ref, q_ref, kf_ref, gf_ref, kb_ref, gb_ref, of_ref, ob_ref, st_ref, *, n_ctx_chunks):
    c = GLA_CHUNK
    t = qz_ref.shape[0]
    n = t // c
    heads = range(HGRN_HEADS_PER_STEP)
    qz = qz_ref[...].astype(F32)
    q_ref[...] = qz * _sigmoid(qz) * (HGRN_K_DIM ** -0.5)
    for z_ref, lb_ref, k_ref, g_ref in ((zf_ref, lbf_ref, kf_ref, gf_ref), (zb_ref, lbb_ref, kb_ref, gb_ref)):
        lb = jnp.concatenate([lb_ref[h] for h in heads], axis=-1)
        f = lb + (1.0 - lb) * _sigmoid(z_ref[...].astype(F32))
        k_ref[...] = 1.0 - f
        g_ref[...] = jnp.log(f) * LOG2E
    st_ref[...] = jnp.zeros_like(st_ref)

    def body(i, carry):
        rf = pl.multiple_of(i * c, c)
        cb = jnp.where(i < n_ctx_chunks, n_ctx_chunks - 1 - i, n - 1 - i + n_ctx_chunks)
        rb = pl.multiple_of(cb * c, c)
        chains = [(h, d, pl.ds(r0, c), k_ref, g_ref, out_ref) for h in heads
                  for d, r0, k_ref, g_ref, out_ref in ((0, rf, kf_ref, gf_ref, of_ref), (1, rb, kb_ref, gb_ref, ob_ref))]
        col = lambda h: slice(h * LANES, (h + 1) * LANES)
        outs = _run_interleaved([
            _gla_chunk(q_ref[rows, col(h)], k_ref[rows, col(h)], v_ref[rows, col(h)], g_ref[rows, col(h)],
                       st_ref[h, d], stack_ref[d], qside_ref, pair_ref, d)
            for (h, d, rows, k_ref, g_ref, _) in chains])
        for (h, d, rows, _, _, out_ref), (o, s_new) in zip(chains, outs):
            out_ref[rows, col(h)] = o
            st_ref[h, d] = s_new
        return carry

    lax.fori_loop(0, n, body, 0)
    gz = gz_ref[...].astype(F32)
    for h in heads:
        cols = slice(h * LANES, (h + 1) * LANES)
        o = _rms(of_ref[:, cols] + ob_ref[:, cols]) * og_ref[...]
        o_ref[:, cols] = (o * (gz[:, cols] * _sigmoid(gz[:, cols]))).astype(o_ref.dtype)


def hgrn_bidirectional(p, lb_fwd, lb_bwd, out_gain, *, n_ctx, blk0):
    b, t, _ = p.shape
    c = GLA_CHUNK
    consts = _gla_constants(c)
    h = HGRN_HEADS
    hps = HGRN_HEADS_PER_STEP
    assert blk0 % hps == 0 and h % hps == 0
    seg = lambda s: pl.BlockSpec((None, t, hps * LANES), lambda bi, hi: (bi, 0, (blk0 + s * h) // hps + hi))
    per_head = pl.BlockSpec((hps, 1, LANES), lambda bi, hi: (hi, 0, 0))
    const = lambda a: pl.BlockSpec(a.shape, lambda bi, hi: (0,) * a.ndim)
    seq = pltpu.VMEM((t, hps * LANES), F32)
    return pl.pallas_call(
        functools.partial(_hgrn_kernel, n_ctx_chunks=n_ctx // c),
        grid=(b, h // hps),
        in_specs=[seg(0), seg(1), seg(2), seg(3), seg(4), per_head, per_head,
                  pl.BlockSpec((1, LANES), lambda bi, hi: (0, 0))] + [const(a) for a in consts],
        out_specs=pl.BlockSpec((None, t, hps * LANES), lambda bi, hi: (bi, 0, hi)),
        out_shape=jax.ShapeDtypeStruct((b, t, h * LANES), BF16),
        scratch_shapes=[seq, seq, seq, seq, seq, seq, seq, pltpu.VMEM((hps, 2, LANES, LANES), F32)],
        compiler_params=_cparams(("arbitrary", "arbitrary")),
        name="hgrn_bidirectional",
    )(p, p, p, p, p, lb_fwd.reshape(h, 1, LANES), lb_bwd.reshape(h, 1, LANES),
      out_gain.astype(F32).reshape(1, LANES), *consts)


def _mla_attn_kernel(q_ref, kv_ref, kr_ref, cq_ref, saq_ref, sbq_ref, ck_ref, sak_ref, sbk_ref,
                     qn_ref, qr_ref, kn_ref, krg_ref, o_ref, kt_ref, vo_ref):
    qi = pl.program_id(2)
    ones_nope = _block_ones(MLA_NOPE)
    ones_rope = _block_ones(MLA_ROPE)
    hw = 2 * LANES

    @pl.when(qi == 0)
    def _():
        kr = _block_rms(kr_ref[...].astype(F32), ones_rope, MLA_ROPE) * krg_ref[...]
        kr = _rope(kr, ck_ref[...], sak_ref[...], sbk_ref[...]).T.astype(BF16)
        for h in range(MLA_HEADS_PER_STEP):
            kn = _block_rms(kv_ref[:, h * hw:h * hw + MLA_NOPE].astype(F32), ones_nope, MLA_NOPE) * kn_ref[...]
            kt_ref[h, 0:LANES, :] = kn.T.astype(BF16)
            kt_ref[h, LANES:hw, :] = kr
            vo_ref[h, :, 0:LANES] = kv_ref[:, h * hw + MLA_NOPE:(h + 1) * hw]
            vo_ref[h, :, LANES:2 * LANES] = jnp.ones((kv_ref.shape[0], LANES), BF16)

    scale = (MLA_NOPE + MLA_ROPE) ** -0.5 * LOG2E
    qs = []
    for h in range(MLA_HEADS_PER_STEP):
        qn = _block_rms(q_ref[:, h * hw:h * hw + MLA_NOPE].astype(F32), ones_nope, MLA_NOPE) * (qn_ref[...] * scale)
        qr = _block_rms(q_ref[:, h * hw + MLA_NOPE:(h + 1) * hw].astype(F32), ones_rope, MLA_ROPE) * qr_ref[...]
        qr = _rope(qr, cq_ref[...], saq_ref[...], sbq_ref[...]) * scale
        qs.append(jnp.concatenate([qn.astype(BF16), qr.astype(BF16)], axis=-1))
    heads = range(MLA_HEADS_PER_STEP)
    outs = _attend_streams(qs, [kt_ref.at[h] for h in heads], [vo_ref.at[h] for h in heads], kt_ref.shape[2])
    for h, (o, l) in enumerate(outs):
        o_ref[:, h * MLA_V:(h + 1) * MLA_V] = (o * (1.0 / l)).astype(o_ref.dtype)


def mla_attention(q, kv, p1, kr_blk, rope, qn_gain, qr_gain, kn_gain, kr_gain, *, tq, n_ctx):
    b, n_lat, _ = q.shape
    t = kv.shape[1]
    c, sa, sb = rope
    hps = MLA_HEADS_PER_STEP
    row_q = lambda bi, h, qi: (qi, 0)
    full = lambda bi, h, qi: (0, 0)
    pad = lambda g: jnp.concatenate([g.astype(F32), jnp.zeros((LANES - g.shape[0],), F32)]).reshape(1, LANES)
    return pl.pallas_call(
        _mla_attn_kernel,
        grid=(b, MLA_HEADS // hps, n_lat // tq),
        in_specs=[
            pl.BlockSpec((None, tq, hps * 2 * LANES), lambda bi, h, qi: (bi, qi, h)),
            pl.BlockSpec((None, t, hps * 2 * LANES), lambda bi, h, qi: (bi, 0, h)),
            pl.BlockSpec((None, t, LANES), lambda bi, h, qi: (bi, 0, kr_blk)),
            pl.BlockSpec((tq, LANES), row_q), pl.BlockSpec((tq, LANES), row_q), pl.BlockSpec((tq, LANES), row_q),
            pl.BlockSpec((t, LANES), full), pl.BlockSpec((t, LANES), full), pl.BlockSpec((t, LANES), full),
            pl.BlockSpec((1, LANES), full), pl.BlockSpec((1, LANES), full),
            pl.BlockSpec((1, LANES), full), pl.BlockSpec((1, LANES), full),
        ],
        out_specs=pl.BlockSpec((None, tq, hps * MLA_V), lambda bi, h, qi: (bi, qi, h)),
        out_shape=jax.ShapeDtypeStruct((b, n_lat, MLA_HEADS * MLA_V), BF16),
        scratch_shapes=[pltpu.VMEM((hps, 2 * LANES, t), BF16), pltpu.VMEM((hps, t, 2 * LANES), BF16)],
        compiler_params=_cparams(("arbitrary", "arbitrary", "arbitrary")),
        name="mla_attention",
    )(q, kv, p1, c[n_ctx:], sa[n_ctx:], sb[n_ctx:], c, sa, sb, pad(qn_gain), pad(qr_gain), pad(kn_gain), pad(kr_gain))


PAIRS = [(i, j) for i in range(EXPERTS_PER_GROUP) for j in range(i + 1, EXPERTS_PER_GROUP)]
N_CLASSES = N_GROUPS * len(PAIRS)
CLASS_LO = np.array([EXPERTS_PER_GROUP * g + i for g in range(N_GROUPS) for (i, j) in PAIRS], np.int32)
CLASS_HI = np.array([EXPERTS_PER_GROUP * g + j for g in range(N_GROUPS) for (i, j) in PAIRS], np.int32)
EXPERT_TILE = 256
ROW_SLABS = 8
DMA_UNROLL = 8
ROW_SLOTS = 4


def _route_kernel(*refs, n_in, n_x, n_ctx_tiles, row_tile0):
    a_refs = refs[:n_in]
    w_refs = refs[n_in:2 * n_in]
    x_refs = refs[2 * n_in:2 * n_in + n_x]
    m_ref, g_ref, rw_ref, rb_ref, xmid_ref, hf_ref, rows_ref, info_ref, count_ref, cnt_ref = refs[2 * n_in + n_x:]
    acc = _dot(a_refs[0][...], w_refs[0][...])
    for a_ref, w_ref in zip(a_refs[1:], w_refs[1:]):
        acc = acc + _dot(a_ref[...], w_ref[...])
    x_mid = _stream_tile(x_refs, n_ctx_tiles, row_tile0) + m_ref[G_M:G_M + 1, :] * acc
    xmid_ref[...] = x_mid
    h = _rms(x_mid) * g_ref[...]
    h = h * (1.0 + m_ref[SC_F:SC_F + 1, :]) + m_ref[SH_F:SH_F + 1, :]
    hf_ref[...] = h.astype(hf_ref.dtype)
    rows_ref[...] = _rows_to_slabs(h)
    h1, h2, _ = _split3(h)
    w1, w2, _ = _split3(rw_ref[...])
    logits = _dot_nt(w1, h1) + (_dot_nt(w1, h2) + _dot_nt(w2, h1))
    biased = _sigmoid(logits) + rb_ref[...]
    row = [biased[e:e + 1, :] for e in range(N_EXPERTS)]
    gscore = []
    for g in range(N_GROUPS):
        m = row[4 * g:4 * g + 4]
        gscore.append(functools.reduce(jnp.maximum, [m[i] + m[j] for (i, j) in PAIRS]))
    hits = []
    for g in range(N_GROUPS):
        best = None
        for g2 in range(N_GROUPS):
            if g2 == g:
                continue
            wins = (gscore[g] > gscore[g2]) if g2 < g else (gscore[g] >= gscore[g2])
            best = wins if best is None else jnp.logical_and(best, wins)
        chosen = []
        for i in range(EXPERTS_PER_GROUP):
            rank = None
            for j in range(EXPERTS_PER_GROUP):
                if j == i:
                    continue
                mi, mj = row[4 * g + i], row[4 * g + j]
                ahead = ((mj >= mi) if j < i else (mj > mi)).astype(jnp.int32)
                rank = ahead if rank is None else rank + ahead
            chosen.append(rank < 2)
        for (i, j) in PAIRS:
            hits.append(jnp.where(best & chosen[i] & chosen[j], 1.0, 0.0))
    onehot = jnp.concatenate(hits, axis=0)
    tm = onehot.shape[1]
    upper = (lax.broadcasted_iota(jnp.int32, (tm, tm), 0) <= lax.broadcasted_iota(jnp.int32, (tm, tm), 1))
    prefix = _dot(onehot.astype(BF16), upper.astype(BF16))

    @pl.when((pl.program_id(0) == 0) & (pl.program_id(1) == 0))
    def _():
        cnt_ref[...] = jnp.zeros_like(cnt_ref)

    seen = cnt_ref[...]
    cls_id = lax.broadcasted_iota(jnp.int32, onehot.shape, 0).astype(F32)
    cls = jnp.sum(onehot * cls_id, axis=0, keepdims=True)
    rank = jnp.sum(onehot * (seen[:, 0:1] + prefix - 1.0), axis=0, keepdims=True)
    info_ref[...] = jnp.concatenate([cls, rank, jnp.zeros((6, tm), F32)], axis=0).astype(jnp.int32)
    seen = seen + jnp.sum(onehot, axis=1, keepdims=True)
    cnt_ref[...] = seen
    count_ref[...] = seen


def route(acts, weights, xs, mods, gain, router_w, router_bias, *, tm, n_ctx_tiles, row_tile0):
    b, r, _ = acts[0].shape
    nt = r // tm
    x_specs, x_args = _stream_specs(xs, tm, n_ctx_tiles, row_tile0)
    d = x_args[0].shape[-1]
    row = lambda bi, ti: (bi, ti, 0)
    const = lambda bi, ti: (0, 0)
    in_specs = [pl.BlockSpec((None, tm, a.shape[-1]), row) for a in acts]
    in_specs += [pl.BlockSpec(w.shape, const) for w in weights]
    in_specs += x_specs + [
        pl.BlockSpec((None, None, 6, d), lambda bi, ti: (bi, ((ti + row_tile0) >= n_ctx_tiles).astype(jnp.int32), 0, 0)),
        pl.BlockSpec((1, d), const), pl.BlockSpec((N_EXPERTS, d), const), pl.BlockSpec((N_EXPERTS, 1), const)]
    return pl.pallas_call(
        functools.partial(_route_kernel, n_in=len(acts), n_x=len(x_args), n_ctx_tiles=n_ctx_tiles,
                          row_tile0=row_tile0),
        grid=(b, nt),
        in_specs=in_specs,
        out_specs=[
            pl.BlockSpec((None, tm, d), row),
            pl.BlockSpec((None, tm, d), row),
            pl.BlockSpec((tm, ROW_SLABS, LANES), lambda bi, ti: (bi * nt + ti, 0, 0)),
            pl.BlockSpec((None, None, 8, tm), lambda bi, ti: (bi, ti, 0, 0)),
            pl.BlockSpec((N_CLASSES, LANES), const),
        ],
        out_shape=[jax.ShapeDtypeStruct((b, r, d), F32),
                   jax.ShapeDtypeStruct((b, r, d), BF16),
                   jax.ShapeDtypeStruct((b * r, ROW_SLABS, LANES), F32),
                   jax.ShapeDtypeStruct((b, nt, 8, tm), jnp.int32),
                   jax.ShapeDtypeStruct((N_CLASSES, LANES), F32)],
        scratch_shapes=[pltpu.VMEM((N_CLASSES, LANES), F32)],
        compiler_params=_cparams(("arbitrary", "arbitrary")),
        name="route",
    )(*acts, *weights, *x_args, mods, gain.reshape(1, d).astype(F32), router_w.T.astype(F32),
      router_bias.reshape(N_EXPERTS, 1).astype(F32))


def _transpose8(parts):
    sub = lax.broadcasted_iota(jnp.int32, parts[0].shape, 1)
    for s in (4, 2, 1):
        keep = (sub & s) == 0
        new = list(parts)
        for i in range(8):
            if i & s:
                continue
            a, b = parts[i], parts[i | s]
            new[i] = jnp.where(keep, a, pltpu.roll(b, s, 1))
            new[i | s] = jnp.where(keep, pltpu.roll(a, 8 - s, 1), b)
        parts = new
    return parts


def _slabs_to_rows(slabs):
    g = slabs.shape[0] // 8
    x4 = slabs.reshape(g, 8, ROW_SLABS, LANES)
    parts = _transpose8([x4[:, t] for t in range(8)])
    return jnp.concatenate([p.reshape(g * 8, LANES) for p in parts], axis=-1)


def _rows_to_slabs(x):
    g = x.shape[0] // 8
    parts = _transpose8([x[:, k * LANES:(k + 1) * LANES].reshape(g, 8, LANES) for k in range(ROW_SLABS)])
    return jnp.stack(parts, axis=1).reshape(g * 8, ROW_SLABS, LANES)


def _invert_kernel(dest_ref, src_ref):
    def clear(i, carry):
        for u in range(2 * DMA_UNROLL):
            src_ref[i * 2 * DMA_UNROLL + u] = 0
        return carry

    def put(i, carry):
        for u in range(2 * DMA_UNROLL):
            t = i * 2 * DMA_UNROLL + u
            src_ref[dest_ref[t]] = t
        return carry

    lax.fori_loop(0, src_ref.shape[0] // (2 * DMA_UNROLL), clear, 0)
    lax.fori_loop(0, dest_ref.shape[0] // (2 * DMA_UNROLL), put, 0)


def invert_permutation(dest, n_rows_out):
    smem = pl.BlockSpec(memory_space=pltpu.SMEM)
    return pl.pallas_call(
        _invert_kernel,
        in_specs=[smem],
        out_specs=smem,
        out_shape=jax.ShapeDtypeStruct((n_rows_out,), jnp.int32),
        name="moe_invert",
    )(dest)


def _expert_kernel(te_ref, tv_ref, nx_ref, src_ref, rows_hbm, rwl_ref, rwh_ref, wg_hbm, wu_hbm, wd_hbm, y_ref,
                   xbuf_ref, sem_ref, stage_g, stage_u, stage_d, cache_g, cache_u, cache_d, wsem_ref, *, layer):
    i = pl.program_id(0)
    n = pl.num_programs(0)
    tg = y_ref.shape[0]
    slot = i % ROW_SLOTS
    valid = tv_ref[i] != 0
    prev = jnp.maximum(i - 1, 0)

    def fetch(tile, s):
        base = tile * tg
        for t in range(tg):
            pltpu.async_copy(rows_hbm.at[src_ref[base + t]], xbuf_ref.at[s, t], sem_ref.at[s], priority=t % 2)

    def wait_rows(s):
        pltpu.make_async_copy(rows_hbm.at[pl.ds(0, tg)], xbuf_ref.at[s], sem_ref.at[s]).wait()

    def weight_copies(s, e):
        return [pltpu.make_async_copy(w_hbm.at[layer, e], stage.at[s], wsem_ref.at[s])
                for w_hbm, stage in ((wg_hbm, stage_g), (wu_hbm, stage_u), (wd_hbm, stage_d))]

    @pl.when(i == 0)
    def _():
        for k in range(ROW_SLOTS - 1):
            fetch(jnp.minimum(k, n - 1), k)
        for s in range(2):
            for copy in weight_copies(s, te_ref[s * n]):
                copy.start()

    for s in range(2):
        expert = te_ref[s * n + i]

        @pl.when(valid & ((i == 0) | (expert != te_ref[s * n + prev])))
        def _():
            for copy in weight_copies(s, expert):
                copy.wait()
            cache_g[s] = stage_g[s].astype(BF16)
            cache_u[s] = stage_u[s].astype(BF16)
            cache_d[s] = stage_d[s].astype(BF16)
            upcoming = nx_ref[s * n + i]

            @pl.when(upcoming >= 0)
            def _():
                for copy in weight_copies(s, upcoming):
                    copy.start()

    @pl.when(valid)
    def _():
        wait_rows(slot)
        x = _slabs_to_rows(xbuf_ref[slot])
        fetch(jnp.minimum(i + ROW_SLOTS - 1, n - 1), (i + ROW_SLOTS - 1) % ROW_SLOTS)
        xb = x.astype(BF16)
        s_lo = _sigmoid(jnp.sum(x * rwl_ref[...], axis=-1, keepdims=True))
        s_hi = _sigmoid(jnp.sum(x * rwh_ref[...], axis=-1, keepdims=True))
        inv = 1.0 / (s_lo + s_hi)

        def ffn(s):
            a = _dot(xb, cache_g[s])
            u = _dot(xb, cache_u[s])
            return _dot((a * _sigmoid(a) * u).astype(BF16), cache_d[s])

        y_ref[...] = _rows_to_slabs((s_lo * inv) * ffn(0) + (s_hi * inv) * ffn(1))

    @pl.when(jnp.logical_not(valid))
    def _():
        y_ref[...] = jnp.zeros_like(y_ref)

    @pl.when((valid & (i == n - 1)) | (jnp.logical_not(valid) & (tv_ref[prev] != 0)))
    def _():
        first = jnp.where(valid, i + 1, i)
        for k in range(ROW_SLOTS - 1):
            wait_rows((first + k) % ROW_SLOTS)


def experts(tile_expert, tile_valid, next_expert, src, rows, router_w_t, wg, wu, wd, layer):
    tg = EXPERT_TILE
    n_tiles = src.shape[0] // tg
    d, ff = wg.shape[2:]
    lo = lambda i, te, tv, nx, src: (te[i], 0, 0)
    hi = lambda i, te, tv, nx, src: (te[n_tiles + i], 0, 0)
    hbm = pl.BlockSpec(memory_space=pl.ANY)
    return pl.pallas_call(
        functools.partial(_expert_kernel, layer=layer),
        grid_spec=pltpu.PrefetchScalarGridSpec(
            num_scalar_prefetch=4,
            grid=(n_tiles,),
            in_specs=[hbm, pl.BlockSpec((None, 1, d), lo), pl.BlockSpec((None, 1, d), hi), hbm, hbm, hbm],
            out_specs=pl.BlockSpec((tg, ROW_SLABS, LANES), lambda i, te, tv, nx, src: (i, 0, 0)),
            scratch_shapes=[pltpu.VMEM((ROW_SLOTS, tg, ROW_SLABS, LANES), F32), pltpu.SemaphoreType.DMA((ROW_SLOTS,)),
                            pltpu.VMEM((2, d, ff), F32), pltpu.VMEM((2, d, ff), F32), pltpu.VMEM((2, ff, d), F32),
                            pltpu.VMEM((2, d, ff), BF16), pltpu.VMEM((2, d, ff), BF16), pltpu.VMEM((2, ff, d), BF16),
                            pltpu.SemaphoreType.DMA((2,))],
        ),
        out_shape=jax.ShapeDtypeStruct((n_tiles * tg, ROW_SLABS, LANES), F32),
        compiler_params=_cparams(("arbitrary",)),
        name="moe_experts",
    )(tile_expert, tile_valid, next_expert, src, rows, router_w_t, router_w_t, wg, wu, wd)


def _combine_kernel(dest_ref, hf_ref, x_ref, m_ref, wg_ref, wu_ref, wd_ref, y_hbm, o_ref, ybuf_ref, sem_ref):
    i = pl.program_id(0)
    n = pl.num_programs(0)
    tm = hf_ref.shape[0]
    slot = i % 2

    def fetch(tile, s):
        base = tile * tm
        for t in range(tm):
            pltpu.async_copy(y_hbm.at[dest_ref[base + t]], ybuf_ref.at[s, t], sem_ref.at[s], priority=t % 2)

    def wait_rows(s):
        pltpu.make_async_copy(y_hbm.at[pl.ds(0, tm)], ybuf_ref.at[s], sem_ref.at[s]).wait()

    @pl.when(i == 0)
    def _():
        fetch(0, 0)

    fetch(jnp.minimum(i + 1, n - 1), 1 - slot)
    h = hf_ref[...]
    a = _dot(h, wg_ref[...])
    u = _dot(h, wu_ref[...])
    shared = _dot((a * _sigmoid(a) * u).astype(BF16), wd_ref[...])
    wait_rows(slot)
    o_ref[...] = x_ref[...] + m_ref[G_F:G_F + 1, :] * (shared + _slabs_to_rows(ybuf_ref[slot]))

    @pl.when(i == n - 1)
    def _():
        wait_rows(1 - slot)


def combine(dest, hf, xs, mods, wg, wu, wd, y_sorted, *, tm, n_ctx_tiles):
    b, r, d = hf.shape
    nt = r // tm
    ff = wg.shape[1]
    row = lambda i, dest: (i // nt, i % nt, 0)
    const = lambda i, dest: (0, 0)
    return pl.pallas_call(
        _combine_kernel,
        grid_spec=pltpu.PrefetchScalarGridSpec(
            num_scalar_prefetch=1,
            grid=(b * nt,),
            in_specs=[
                pl.BlockSpec((None, tm, d), row),
                pl.BlockSpec((None, tm, d), row),
                pl.BlockSpec((None, None, 6, d),
                             lambda i, dest: (i // nt, ((i % nt) >= n_ctx_tiles).astype(jnp.int32), 0, 0)),
                pl.BlockSpec((d, ff), const), pl.BlockSpec((d, ff), const), pl.BlockSpec((ff, d), const),
                pl.BlockSpec(memory_space=pl.ANY),
            ],
            out_specs=pl.BlockSpec((None, tm, d), row),
            scratch_shapes=[pltpu.VMEM((2, tm, ROW_SLABS, LANES), F32), pltpu.SemaphoreType.DMA((2,))],
        ),
        out_shape=jax.ShapeDtypeStruct((b, r, d), F32),
        compiler_params=_cparams(("arbitrary",)),
        name="moe_combine",
    )(dest, hf, xs, mods, wg, wu, wd, y_sorted)


def moe_block(acts, weights, xs, mods, gain, router_w, router_bias, ew_gate, ew_up, ew_down, layer, sw_gate, sw_up,
              sw_down, *, tm, n_ctx_tiles, row_tile0):
    xs_mid, hf, rows, info, counts = route(acts, weights, xs, mods, gain, router_w, router_bias, tm=tm,
                                           n_ctx_tiles=n_ctx_tiles, row_tile0=row_tile0)
    b, r, d = xs_mid.shape
    n = b * r
    tg = EXPERT_TILE
    cls = info[:, :, 0, :].reshape(n)
    rank = info[:, :, 1, :].reshape(n)
    padded = ((counts[:, 0].astype(jnp.int32) + tg - 1) // tg) * tg
    ends = jnp.cumsum(padded)
    dest = (ends - padded)[cls] + rank
    n_tiles = n // tg + N_CLASSES
    tile_start = jnp.arange(n_tiles, dtype=jnp.int32) * tg
    tile_valid = tile_start < ends[-1]
    last_start = jnp.maximum(ends[-1] - tg, 0)
    start = jnp.where(tile_valid, tile_start, last_start)
    tile_cls = jnp.sum((ends[None, :] <= start[:, None]).astype(jnp.int32), axis=1)
    tile_cls = jnp.minimum(tile_cls, N_CLASSES - 1)
    slot_expert = jnp.stack([jnp.asarray(CLASS_LO)[tile_cls], jnp.asarray(CLASS_HI)[tile_cls]])
    tile_expert = slot_expert.reshape(-1)
    later = (tile_start[None, :] > tile_start[:, None]) & tile_valid[None, :]
    differs = slot_expert[:, None, :] != slot_expert[:, :, None]
    first = jnp.min(jnp.where(later[None] & differs, jnp.arange(n_tiles)[None, None, :], n_tiles), axis=-1)
    next_expert = jnp.where(first < n_tiles, jnp.take_along_axis(slot_expert, jnp.minimum(first, n_tiles - 1), axis=1),
                            -1).reshape(-1).astype(jnp.int32)
    src = invert_permutation(dest, n_tiles * tg)
    y_sorted = experts(tile_expert, tile_valid.astype(jnp.int32), next_expert, src, rows,
                       router_w.T.astype(F32).reshape(N_EXPERTS, 1, d), ew_gate, ew_up, ew_down, layer)
    return combine(dest, hf, xs_mid, mods, sw_gate.astype(BF16), sw_up.astype(BF16), sw_down.astype(BF16), y_sorted,
                   tm=tm, n_ctx_tiles=max(n_ctx_tiles - row_tile0, 0))


def kernel(x, c, ctx, c_ctx, mod_w, mod_b, norm_mix, norm_ffn, even_w_in, even_w_out, diff_q_gain, diff_k_gain, diff_lambda, diff_subln, hgrn_lb_logits, hgrn_out_gain, odd_w_in, mla_q_a_gain, mla_kv_a_gain, mla_w_uq, mla_w_ukv, mla_q_nope_gain, mla_q_rope_gain, mla_k_nope_gain, mla_k_rope_gain, odd_w_out, router_w, router_bias, expert_w_gate, expert_w_up, expert_w_down, shared_w_gate, shared_w_up, shared_w_down):
    b, n_lat, d = x.shape
    n_ctx = ctx.shape[1]
    depth = mod_w.shape[0]
    tm = 256 if n_ctx % 256 == 0 else 128
    n_ctx_tiles = n_ctx // tm
    rope = rope_lane_tables(n_ctx, n_lat)

    mod_rows = 16
    c_rows = jnp.concatenate([c, c_ctx[None, :], jnp.zeros((mod_rows - b - 1, d), F32)], axis=0)
    mod_all = modulation(c_rows, mod_w, mod_b).reshape(depth, mod_rows, 6, d)
    lb_all = jnp.cumsum(jax.nn.softmax(hgrn_lb_logits.astype(F32), axis=0), axis=0)

    xs = (ctx, x)
    for layer in range(depth):
        last = layer == depth - 1
        j = layer // 2
        mods = jnp.stack([jnp.broadcast_to(mod_all[layer, b], (b, 6, d)), mod_all[layer, :b]], axis=1)
        row_tile0 = n_ctx_tiles if last else 0
        if layer % 2 == 0:
            lam_init = 0.8 - 0.6 * math.exp(-0.3 * layer)
            p = norm_matmul(xs, mods, norm_mix[layer], even_w_in[j].astype(BF16), tm=tm, n_ctx_tiles=n_ctx_tiles)
            oa = diff_attention(p, diff_lambda[j], diff_q_gain[j], diff_k_gain[j], diff_subln[j], rope,
                                tq=tm, n_ctx=n_ctx, lam_init=lam_init, q_blk0=0, k_blk0=4, v_blk0=8)
            ob = hgrn_bidirectional(p, lb_all[j, 0], lb_all[j, 1], hgrn_out_gain[j], n_ctx=n_ctx, blk0=12)
            if last:
                oa, ob = oa[:, n_ctx:], ob[:, n_ctx:]
            w_out = even_w_out[j].astype(BF16)
            half = oa.shape[-1]
            acts, weights = [oa, ob], [w_out[:half], w_out[half:]]
        else:
            w_in = odd_w_in[j]
            zpad = jnp.zeros((d, LANES - MLA_ROPE), F32)
            w_in_r = jnp.concatenate([w_in[:, MLA_Q_LORA:MLA_Q_LORA + MLA_KV_LORA], w_in[:, MLA_Q_LORA + MLA_KV_LORA:],
                                      zpad, w_in[:, :MLA_Q_LORA]], axis=1).astype(BF16)
            w_uq = mla_w_uq[j].reshape(MLA_Q_LORA, MLA_HEADS, MLA_NOPE + MLA_ROPE)
            w_uq = jnp.concatenate([w_uq, jnp.zeros((MLA_Q_LORA, MLA_HEADS, LANES - MLA_ROPE), F32)], axis=-1)
            w_uq = w_uq.reshape(MLA_Q_LORA, MLA_HEADS * 2 * LANES).astype(BF16)
            if not last:
                raise NotImplementedError("context queries for a non-final latent-attention layer")
            kr, q, kv = mla_projections(xs, mods, norm_mix[layer], w_in_r, mla_q_a_gain[j], mla_kv_a_gain[j], w_uq,
                                        mla_w_ukv[j].astype(BF16), tm=tm, n_ctx_tiles=n_ctx_tiles)
            o = mla_attention(q, kv, kr, 0, rope, mla_q_nope_gain[j], mla_q_rope_gain[j], mla_k_nope_gain[j],
                              mla_k_rope_gain[j], tq=2 * tm, n_ctx=n_ctx)
            acts, weights = [o], [odd_w_out[j].astype(BF16)]
        xs = moe_block(acts, weights, xs, mods, norm_ffn[layer], router_w, router_bias,
                       expert_w_gate, expert_w_up, expert_w_down, layer,
                       shared_w_gate[layer], shared_w_up[layer], shared_w_down[layer],
                       tm=tm, n_ctx_tiles=n_ctx_tiles, row_tile0=row_tile0)
    return xs if xs.shape[1] == n_lat else xs[:, n_ctx:]
```

```python
import functools
import math

import numpy as np
import jax
import jax.numpy as jnp
from jax import lax
from jax.experimental import pallas as pl
from jax.experimental.pallas import tpu as pltpu

F32 = jnp.float32
BF16 = jnp.bfloat16

LANES = 128
VMEM_LIMIT = 56 * 1024 * 1024

GRID_W = 64
DIFF_HEADS = 4
DIFF_HEAD_DIM = 64
HGRN_HEADS = 4
HGRN_K_DIM = 128
MLA_HEADS = 8
MLA_NOPE = 128
MLA_ROPE = 64
MLA_V = 128
MLA_Q_LORA = 384
MLA_KV_LORA = 256
N_EXPERTS = 16
N_GROUPS = 4
EXPERTS_PER_GROUP = 4
ROPE_BASE = 10000.0
EPS = 1e-6
LOG2E = 1.4426950408889634
GLA_CHUNK = 64
ATTN_KEY_CHUNK = 1152
MLA_HEADS_PER_STEP = 2
DIFF_HEADS_PER_STEP = 2
HGRN_HEADS_PER_STEP = 2

SH_M, SC_M, G_M, SH_F, SC_F, G_F = range(6)


def _sigmoid(x):
    return 0.5 * jnp.tanh(0.5 * x) + 0.5


def _dot(a, b):
    return jnp.dot(a, b, preferred_element_type=F32)


def _dot_nt(a, b):
    return lax.dot_general(a, b, (((1,), (1,)), ((), ())), preferred_element_type=F32)


def _dot_tn(a, b):
    return lax.dot_general(a, b, (((0,), (0,)), ((), ())), preferred_element_type=F32)


def _split3(x):
    hi = x.astype(BF16)
    r = x - hi.astype(F32)
    mid = r.astype(BF16)
    lo = (r - mid.astype(F32)).astype(BF16)
    return hi, mid, lo


def _rms(x, width=None):
    n = x.shape[-1] if width is None else width
    return x * lax.rsqrt(jnp.sum(x * x, axis=-1, keepdims=True) * (1.0 / n) + EPS)


def _cparams(sem):
    return pltpu.CompilerParams(dimension_semantics=sem, vmem_limit_bytes=VMEM_LIMIT)


def _mod_kernel(c_ref, w_ref, b_ref, o_ref):
    c = c_ref[...]
    s = c * _sigmoid(c)
    o_ref[...] = _dot(s.astype(BF16), w_ref[...].astype(BF16)) + b_ref[...]


def modulation(c_rows, mod_w, mod_b):
    n_layers, d, n = mod_w.shape
    rows = c_rows.shape[0]
    tn = 1536
    return pl.pallas_call(
        _mod_kernel,
        grid=(n_layers, n // tn),
        in_specs=[
            pl.BlockSpec((rows, d), lambda l, j: (0, 0)),
            pl.BlockSpec((None, d, tn), lambda l, j: (l, 0, j)),
            pl.BlockSpec((None, 1, tn), lambda l, j: (l, 0, j)),
        ],
        out_specs=pl.BlockSpec((None, rows, tn), lambda l, j: (l, 0, j)),
        out_shape=jax.ShapeDtypeStruct((n_layers, rows, n), F32),
        compiler_params=_cparams(("arbitrary", "arbitrary")),
        name="modulation",
    )(c_rows, mod_w, mod_b.reshape(n_layers, 1, n))


def _stream_specs(xs, tm, n_ctx_tiles, row_tile0=0):
    parts = xs if isinstance(xs, tuple) else (xs,)
    d = parts[0].shape[-1]
    if len(parts) == 1:
        return [pl.BlockSpec((None, tm, d), lambda bi, ti: (bi, ti + row_tile0, 0))], list(parts)
    return [pl.BlockSpec((None, tm, d), lambda bi, ti: (bi, jnp.minimum(ti + row_tile0, n_ctx_tiles - 1), 0)),
            pl.BlockSpec((None, tm, d), lambda bi, ti: (bi, jnp.maximum(ti + row_tile0 - n_ctx_tiles, 0), 0))], list(parts)


def _stream_tile(x_refs, n_ctx_tiles, row_tile0=0):
    if len(x_refs) == 1:
        return x_refs[0][...]
    return jnp.where(pl.program_id(1) + row_tile0 < n_ctx_tiles, x_refs[0][...], x_refs[1][...])


def _norm_matmul_kernel(*refs, n_x, n_ctx_tiles):
    m_ref, g_ref, w_ref, o_ref = refs[n_x:]
    h = _rms(_stream_tile(refs[:n_x], n_ctx_tiles)) * g_ref[...]
    h = h * (1.0 + m_ref[SC_M:SC_M + 1, :]) + m_ref[SH_M:SH_M + 1, :]
    o_ref[...] = _dot(h.astype(BF16), w_ref[...]).astype(o_ref.dtype)


def norm_matmul(xs, mods, gain, w, *, tm, n_ctx_tiles):
    x_specs, x_args = _stream_specs(xs, tm, n_ctx_tiles)
    b = x_args[0].shape[0]
    t = sum(a.shape[1] for a in x_args)
    k, n = w.shape
    return pl.pallas_call(
        functools.partial(_norm_matmul_kernel, n_x=len(x_args), n_ctx_tiles=n_ctx_tiles),
        grid=(b, t // tm),
        in_specs=x_specs + [
            pl.BlockSpec((None, None, 6, k), lambda bi, ti: (bi, (ti >= n_ctx_tiles).astype(jnp.int32), 0, 0)),
            pl.BlockSpec((1, k), lambda bi, ti: (0, 0)), pl.BlockSpec((k, n), lambda bi, ti: (0, 0))],
        out_specs=pl.BlockSpec((None, tm, n), lambda bi, ti: (bi, ti, 0)),
        out_shape=jax.ShapeDtypeStruct((b, t, n), BF16),
        compiler_params=_cparams(("arbitrary", "arbitrary")),
        name="norm_matmul",
    )(*x_args, mods, gain.reshape(1, k).astype(F32), w)


def _mla_proj_kernel(x_ref, m_ref, g_ref, win_ref, qg_ref, kvg_ref, wuq_ref, wukv_ref, kr_ref, q_ref, kv_ref,
                     *, n_ctx_tiles):
    h = _rms(x_ref[...]) * g_ref[...]
    h = h * (1.0 + m_ref[SC_M:SC_M + 1, :]) + m_ref[SH_M:SH_M + 1, :]
    p = _dot(h.astype(BF16), win_ref[...])
    kr_ref[...] = p[:, MLA_KV_LORA:MLA_KV_LORA + LANES].astype(kr_ref.dtype)
    ckv = _rms(p[:, 0:MLA_KV_LORA]) * kvg_ref[...]
    kv_ref[...] = _dot(ckv.astype(BF16), wukv_ref[...]).astype(kv_ref.dtype)

    @pl.when(pl.program_id(1) >= n_ctx_tiles)
    def _():
        cq = _rms(p[:, MLA_KV_LORA + LANES:]) * qg_ref[...]
        q_ref[...] = _dot(cq.astype(BF16), wuq_ref[...]).astype(q_ref.dtype)

    @pl.when(pl.program_id(1) < n_ctx_tiles)
    def _():
        q_ref[...] = jnp.zeros_like(q_ref)


def mla_projections(xs, mods, gain, w_in_r, q_a_gain, kv_a_gain, w_uq, w_ukv, *, tm, n_ctx_tiles):
    b, t, d = xs.shape
    nt = t // tm
    n_lat = t - n_ctx_tiles * tm
    const = lambda a: pl.BlockSpec(a.shape, lambda bi, ti: (0, 0))
    row = lambda bi, ti: (bi, ti, 0)
    g2 = lambda g: g.reshape(1, -1).astype(F32)
    args = [g2(gain), w_in_r, g2(q_a_gain), g2(kv_a_gain), w_uq, w_ukv]
    return pl.pallas_call(
        functools.partial(_mla_proj_kernel, n_ctx_tiles=n_ctx_tiles),
        grid=(b, nt),
        in_specs=[pl.BlockSpec((None, tm, d), row),
                  pl.BlockSpec((None, None, 6, d), lambda bi, ti: (bi, (ti >= n_ctx_tiles).astype(jnp.int32), 0, 0))]
                 + [const(a) for a in args],
        out_specs=[pl.BlockSpec((None, tm, LANES), row),
                   pl.BlockSpec((None, tm, w_uq.shape[1]), lambda bi, ti: (bi, jnp.maximum(ti - n_ctx_tiles, 0), 0)),
                   pl.BlockSpec((None, tm, w_ukv.shape[1]), row)],
        out_shape=[jax.ShapeDtypeStruct((b, t, LANES), BF16),
                   jax.ShapeDtypeStruct((b, n_lat, w_uq.shape[1]), BF16),
                   jax.ShapeDtypeStruct((b, t, w_ukv.shape[1]), BF16)],
        compiler_params=_cparams(("arbitrary", "arbitrary")),
        name="mla_projections",
    )(xs, mods, *args)


def rope_lane_tables(n_ctx, n_lat):
    rows = n_lat // GRID_W
    row = np.repeat(np.arange(rows), GRID_W).astype(np.float32)
    col = np.tile(np.arange(GRID_W), rows).astype(np.float32)
    axis_dim = DIFF_HEAD_DIM // 2
    inv_freq = jnp.asarray(ROPE_BASE, F32) ** (-jnp.arange(0, axis_dim, 2, dtype=F32) / axis_dim)
    ang_r = jnp.asarray(row)[:, None] * inv_freq
    ang_c = jnp.asarray(col)[:, None] * inv_freq
    lane = np.arange(LANES)
    freq_idx = lane % 16
    use_col = (lane % 64) >= 32
    first = (lane % 32) < 16
    ang = jnp.where(use_col[None, :], ang_c[:, freq_idx], ang_r[:, freq_idx])
    cos, sin = jnp.cos(ang), jnp.sin(ang)
    c = jnp.concatenate([jnp.ones((n_ctx, LANES), F32), cos], axis=0)
    sa = jnp.concatenate([jnp.zeros((n_ctx, LANES), F32), jnp.where(first[None, :], -sin, 0.0)], axis=0)
    sb = jnp.concatenate([jnp.zeros((n_ctx, LANES), F32), jnp.where(first[None, :], 0.0, sin)], axis=0)
    return c, sa, sb


def _rope(x, c, sa, sb):
    return x * c + pltpu.roll(x, LANES - 16, 1) * sa + pltpu.roll(x, 16, 1) * sb


def _block_ones(width):
    r = lax.broadcasted_iota(jnp.int32, (LANES, LANES), 0) // width
    c = lax.broadcasted_iota(jnp.int32, (LANES, LANES), 1) // width
    return (r == c).astype(BF16)


def _block_rms(x, ones, width):
    xx = x * x
    hi = xx.astype(BF16)
    lo = (xx - hi.astype(F32)).astype(BF16)
    ms = (_dot(hi, ones) + _dot(lo, ones)) * (1.0 / width)
    return x * lax.rsqrt(ms + EPS)


def _attend_streams(qs, kt_refs, vo_refs, nk):
    n = len(qs)
    chunk = ATTN_KEY_CHUNK if nk % ATTN_KEY_CHUNK == 0 else nk
    nc = nk // chunk
    s = [[None] * nc for _ in range(n)]
    p = [[None] * nc for _ in range(n)]
    m, ol = [None] * n, [None] * n

    def logits(i, c):
        s[i][c] = _dot(qs[i], kt_refs[i][:, c * chunk:(c + 1) * chunk])
        mc = jnp.max(s[i][c], axis=-1, keepdims=True)
        m[i] = mc if c == 0 else jnp.maximum(m[i], mc)

    def exps(i, c):
        p[i][c] = jnp.exp2(s[i][c] - m[i]).astype(BF16)

    def values(i, c):
        oc = _dot(p[i][c], vo_refs[i][c * chunk:(c + 1) * chunk, :])
        ol[i] = oc if c == 0 else ol[i] + oc

    for step in range(n + 2):
        for c in range(nc):
            if step < n:
                logits(step, c)
            if 0 <= step - 1 < n:
                exps(step - 1, c)
            if 0 <= step - 2 < n:
                values(step - 2, c)
    return [(x[:, 0:LANES], x[:, LANES:LANES + 1]) for x in ol]


def _diff_attn_kernel(lam_ref, q_ref, k_ref, v_ref, cq_ref, saq_ref, sbq_ref, ck_ref, sak_ref, sbk_ref,
                      qg_ref, kg_ref, sub_ref, o_ref, kt_ref, vo_ref, *, n_ctx, n_ctx_tiles, lam_init):
    qi = pl.program_id(2)
    ones = _block_ones(DIFF_HEAD_DIM)
    heads = range(DIFF_HEADS_PER_STEP)

    @pl.when(qi == 0)
    def _():
        for h in heads:
            k = _block_rms(k_ref[:, h * LANES:(h + 1) * LANES].astype(F32), ones, DIFF_HEAD_DIM) * kg_ref[...]
            kt_ref[h] = _rope(k, ck_ref[...], sak_ref[...], sbk_ref[...]).T.astype(BF16)
            vo_ref[h, :, 0:LANES] = v_ref[:, h * LANES:(h + 1) * LANES]
            vo_ref[h, :, LANES:2 * LANES] = jnp.ones((v_ref.shape[0], LANES), BF16)

    qs = []
    for h in heads:
        q = _block_rms(q_ref[:, h * LANES:(h + 1) * LANES].astype(F32), ones, DIFF_HEAD_DIM) * qg_ref[...]
        q = _rope(q, cq_ref[...], saq_ref[...], sbq_ref[...]) * (DIFF_HEAD_DIM ** -0.5 * LOG2E)
        lane = lax.broadcasted_iota(jnp.int32, q.shape, 1)
        qs.append(jnp.where(lane < DIFF_HEAD_DIM, q, 0.0).astype(BF16))
        qs.append(jnp.where(lane >= DIFF_HEAD_DIM, q, 0.0).astype(BF16))
    lv = lam_ref[...]
    lam = (jnp.exp(jnp.sum(lv[0:1] * lv[1:2], axis=-1, keepdims=True))
           - jnp.exp(jnp.sum(lv[2:3] * lv[3:4], axis=-1, keepdims=True)) + lam_init)

    def attend(nk):
        outs = _attend_streams(qs, [kt_ref.at[h] for h in heads for _ in range(2)],
                               [vo_ref.at[h] for h in heads for _ in range(2)], nk)
        for h in heads:
            (o1, l1), (o2, l2) = outs[2 * h], outs[2 * h + 1]
            o = o1 * (1.0 / l1) - o2 * (lam / l2)
            o_ref[:, h * LANES:(h + 1) * LANES] = (_rms(o) * sub_ref[...] * (1.0 - lam_init)).astype(o_ref.dtype)

    @pl.when(qi < n_ctx_tiles)
    def _():
        attend(n_ctx)

    @pl.when(qi >= n_ctx_tiles)
    def _():
        attend(k_ref.shape[0])


def diff_attention(p, lam_vecs, q_gain, k_gain, subln, rope, *, tq, n_ctx, lam_init, q_blk0, k_blk0, v_blk0):
    b, t, _ = p.shape
    c, sa, sb = rope
    nq = t // tq
    hps = DIFF_HEADS_PER_STEP
    assert q_blk0 % hps == 0 and k_blk0 % hps == 0 and v_blk0 % hps == 0
    row_q = lambda bi, h, qi: (qi, 0)
    full = lambda bi, h, qi: (0, 0)
    tile2 = lambda g: jnp.tile(g.astype(F32), 2).reshape(1, LANES)
    return pl.pallas_call(
        functools.partial(_diff_attn_kernel, n_ctx=n_ctx, n_ctx_tiles=n_ctx // tq, lam_init=lam_init),
        grid=(b, DIFF_HEADS // hps, nq),
        in_specs=[
            pl.BlockSpec((4, DIFF_HEAD_DIM), full),
            pl.BlockSpec((None, tq, hps * LANES), lambda bi, h, qi: (bi, qi, q_blk0 // hps + h)),
            pl.BlockSpec((None, t, hps * LANES), lambda bi, h, qi: (bi, 0, k_blk0 // hps + h)),
            pl.BlockSpec((None, t, hps * LANES), lambda bi, h, qi: (bi, 0, v_blk0 // hps + h)),
            pl.BlockSpec((tq, LANES), row_q), pl.BlockSpec((tq, LANES), row_q), pl.BlockSpec((tq, LANES), row_q),
            pl.BlockSpec((t, LANES), full), pl.BlockSpec((t, LANES), full), pl.BlockSpec((t, LANES), full),
            pl.BlockSpec((1, LANES), full), pl.BlockSpec((1, LANES), full), pl.BlockSpec((1, LANES), full),
        ],
        out_specs=pl.BlockSpec((None, tq, hps * LANES), lambda bi, h, qi: (bi, qi, h)),
        out_shape=jax.ShapeDtypeStruct((b, t, DIFF_HEADS * LANES), BF16),
        scratch_shapes=[pltpu.VMEM((hps, LANES, t), BF16), pltpu.VMEM((hps, t, 2 * LANES), BF16)],
        compiler_params=_cparams(("arbitrary", "arbitrary", "arbitrary")),
        name="diff_attention",
    )(lam_vecs.astype(F32), p, p, p, c, sa, sb, c, sa, sb, tile2(q_gain), tile2(k_gain),
      subln.astype(F32).reshape(1, LANES))


def _gla_constants(c):
    levels = int(math.log2(c))
    t = np.arange(c)[:, None]
    u = np.arange(c)[None, :]
    stack = [[u <= t], [u >= t]]
    qside = [[], []]
    pair = [[t == u], [t == u]]
    for lv in range(1, levels + 1):
        up_t = ((t >> (lv - 1)) & 1) == 1
        up_u = ((u >> (lv - 1)) & 1) == 1
        same = (t >> lv) == (u >> lv)
        qside[0].append(np.broadcast_to(up_t, (c, LANES)))
        qside[1].append(np.broadcast_to(~up_t, (c, LANES)))
        pair[0].append(same & up_t & ~up_u)
        pair[1].append(same & ~up_t & up_u)
    f32 = lambda x: np.asarray(x, np.float32)
    stack = np.stack([np.tile(f32(m[0]), (1, 3)) for m in stack])
    return (jnp.asarray(stack, BF16), jnp.asarray(np.stack([f32(m) for m in qside])),
            jnp.asarray(np.stack([f32(m) for m in pair])))


def _split_values(cum, d):
    c, w = cum.shape
    levels = int(math.log2(c))
    g = c // 8
    cum3 = cum.reshape(g, 8, w)
    sub = lax.broadcasted_iota(jnp.int32, (g, 8, w), 1)
    row_of_group = lambda off: jnp.broadcast_to(cum3[:, off:off + 1, :], (g, 8, w))
    if d == 0:
        pair_level = jnp.where((sub & 1) == 1, pltpu.roll(cum3, 1, 1), cum3)
        quad_level = jnp.where(sub < 4, row_of_group(1), row_of_group(5))
        oct_level = row_of_group(3)
    else:
        pair_level = jnp.where((sub & 1) == 0, pltpu.roll(cum3, 7, 1), cum3)
        quad_level = jnp.where(sub < 4, row_of_group(2), row_of_group(6))
        oct_level = row_of_group(4)
    out = [x.reshape(c, w) for x in (pair_level, quad_level, oct_level)]
    for lv in range(4, levels + 1):
        size = 1 << lv
        at = size // 2 - 1 if d == 0 else size // 2
        out.append(jnp.concatenate([jnp.broadcast_to(cum[b0 + at:b0 + at + 1, :], (size, w))
                                    for b0 in range(0, c, size)], axis=0))
    return out


def _gla_chunk(q, k, v, g2, s, stack, qside_ref, pair_ref, d):
    c = q.shape[0]
    levels = int(math.log2(c))
    gcat = jnp.concatenate(_split3(g2), axis=0)
    cum = _dot(stack, gcat)
    tot_col = _dot_tn(gcat, jnp.ones((3 * c, LANES), BF16))
    diag = jnp.sum(q * k, axis=-1, keepdims=True)
    yield
    tot = cum[0:1] if d == 1 else cum[c - 1:c]
    split = _split_values(cum, d)
    zz = []
    for lv in range(1, levels + 1):
        e = jnp.exp2(-jnp.abs(cum - split[lv - 1]))
        z = (jnp.where(qside_ref[d, lv - 1] > 0.5, q, k) * e).astype(BF16)
        zz.append(_dot_nt(z, z))
    q_in = (q * jnp.exp2(cum)).astype(BF16)
    ks = (k * jnp.exp2(tot - cum)).astype(BF16)
    ds = _dot_tn(ks, v)
    yield
    a = pair_ref[d, 0] * diag
    for lv in range(1, levels + 1):
        a = a + pair_ref[d, lv] * zz[lv - 1]
    lhs = jnp.concatenate([q_in, a.astype(BF16)], axis=1)
    o = _dot(lhs, jnp.concatenate([s.astype(BF16), v], axis=0))
    s_new = s * jnp.exp2(tot_col) + ds
    yield
    return o, s_new


def _run_interleaved(gens):
    results = [None] * len(gens)
    live = list(range(len(gens)))
    while live:
        for i in list(live):
            try:
                next(gens[i])
            except StopIteration as stop:
                results[i] = stop.value
                live.remove(i)
    return results


def _hgrn_kernel(qz_ref, zf_ref, zb_ref, v_ref, gz_ref, lbf_ref, lbb_ref, og_ref, stack_ref, qside_ref, pair_ref,
                 o_ref, q_ref, kf_ref, gf_ref, kb_ref, gb_ref, of_ref, ob_ref, st_ref, *, n_ctx_chunks):
    c = GLA_CHUNK
    t = qz_ref.shape[0]
    n = t // c
    heads = range(HGRN_HEADS_PER_STEP)
    qz = qz_ref[...].astype(F32)
    q_ref[...] = qz * _sigmoid(qz) * (HGRN_K_DIM ** -0.5)
    for z_ref, lb_ref, k_ref, g_ref in ((zf_ref, lbf_ref, kf_ref, gf_ref), (zb_ref, lbb_ref, kb_ref, gb_ref)):
        lb = jnp.concatenate([lb_ref[h] for h in heads], axis=-1)
        f = lb + (1.0 - lb) * _sigmoid(z_ref[...].astype(F32))
        k_ref[...] = 1.0 - f
        g_ref[...] = jnp.log(f) * LOG2E
    st_ref[...] = jnp.zeros_like(st_ref)

    def body(i, carry):
        rf = pl.multiple_of(i * c, c)
        cb = jnp.where(i < n_ctx_chunks, n_ctx_chunks - 1 - i, n - 1 - i + n_ctx_chunks)
        rb = pl.multiple_of(cb * c, c)
        chains = [(h, d, pl.ds(r0, c), k_ref, g_ref, out_ref) for h in heads
                  for d, r0, k_ref, g_ref, out_ref in ((0, rf, kf_ref, gf_ref, of_ref), (1, rb, kb_ref, gb_ref, ob_ref))]
        col = lambda h: slice(h * LANES, (h + 1) * LANES)
        outs = _run_interleaved([
            _gla_chunk(q_ref[rows, col(h)], k_ref[rows, col(h)], v_ref[rows, col(h)], g_ref[rows, col(h)],
                       st_ref[h, d], stack_ref[d], qside_ref, pair_ref, d)
            for (h, d, rows, k_ref, g_ref, _) in chains])
        for (h, d, rows, _, _, out_ref), (o, s_new) in zip(chains, outs):
            out_ref[rows, col(h)] = o
            st_ref[h, d] = s_new
        return carry

    lax.fori_loop(0, n, body, 0)
    gz = gz_ref[...].astype(F32)
    for h in heads:
        cols = slice(h * LANES, (h + 1) * LANES)
        o = _rms(of_ref[:, cols] + ob_ref[:, cols]) * og_ref[...]
        o_ref[:, cols] = (o * (gz[:, cols] * _sigmoid(gz[:, cols]))).astype(o_ref.dtype)


def hgrn_bidirectional(p, lb_fwd, lb_bwd, out_gain, *, n_ctx, blk0):
    b, t, _ = p.shape
    c = GLA_CHUNK
    consts = _gla_constants(c)
    h = HGRN_HEADS
    hps = HGRN_HEADS_PER_STEP
    assert blk0 % hps == 0 and h % hps == 0
    seg = lambda s: pl.BlockSpec((None, t, hps * LANES), lambda bi, hi: (bi, 0, (blk0 + s * h) // hps + hi))
    per_head = pl.BlockSpec((hps, 1, LANES), lambda bi, hi: (hi, 0, 0))
    const = lambda a: pl.BlockSpec(a.shape, lambda bi, hi: (0,) * a.ndim)
    seq = pltpu.VMEM((t, hps * LANES), F32)
    return pl.pallas_call(
        functools.partial(_hgrn_kernel, n_ctx_chunks=n_ctx // c),
        grid=(b, h // hps),
        in_specs=[seg(0), seg(1), seg(2), seg(3), seg(4), per_head, per_head,
                  pl.BlockSpec((1, LANES), lambda bi, hi: (0, 0))] + [const(a) for a in consts],
        out_specs=pl.BlockSpec((None, t, hps * LANES), lambda bi, hi: (bi, 0, hi)),
        out_shape=jax.ShapeDtypeStruct((b, t, h * LANES), BF16),
        scratch_shapes=[seq, seq, seq, seq, seq, seq, seq, pltpu.VMEM((hps, 2, LANES, LANES), F32)],
        compiler_params=_cparams(("arbitrary", "arbitrary")),
        name="hgrn_bidirectional",
    )(p, p, p, p, p, lb_fwd.reshape(h, 1, LANES), lb_bwd.reshape(h, 1, LANES),
      out_gain.astype(F32).reshape(1, LANES), *consts)


def _mla_attn_kernel(q_ref, kv_ref, kr_ref, cq_ref, saq_ref, sbq_ref, ck_ref, sak_ref, sbk_ref,
                     qn_ref, qr_ref, kn_ref, krg_ref, o_ref, kt_ref, vo_ref):
    qi = pl.program_id(2)
    ones_nope = _block_ones(MLA_NOPE)
    ones_rope = _block_ones(MLA_ROPE)
    hw = 2 * LANES

    @pl.when(qi == 0)
    def _():
        kr = _block_rms(kr_ref[...].astype(F32), ones_rope, MLA_ROPE) * krg_ref[...]
        kr = _rope(kr, ck_ref[...], sak_ref[...], sbk_ref[...]).T.astype(BF16)
        for h in range(MLA_HEADS_PER_STEP):
            kn = _block_rms(kv_ref[:, h * hw:h * hw + MLA_NOPE].astype(F32), ones_nope, MLA_NOPE) * kn_ref[...]
            kt_ref[h, 0:LANES, :] = kn.T.astype(BF16)
            kt_ref[h, LANES:hw, :] = kr
            vo_ref[h, :, 0:LANES] = kv_ref[:, h * hw + MLA_NOPE:(h + 1) * hw]
            vo_ref[h, :, LANES:2 * LANES] = jnp.ones((kv_ref.shape[0], LANES), BF16)

    scale = (MLA_NOPE + MLA_ROPE) ** -0.5 * LOG2E
    qs = []
    for h in range(MLA_HEADS_PER_STEP):
        qn = _block_rms(q_ref[:, h * hw:h * hw + MLA_NOPE].astype(F32), ones_nope, MLA_NOPE) * (qn_ref[...] * scale)
        qr = _block_rms(q_ref[:, h * hw + MLA_NOPE:(h + 1) * hw].astype(F32), ones_rope, MLA_ROPE) * qr_ref[...]
        qr = _rope(qr, cq_ref[...], saq_ref[...], sbq_ref[...]) * scale
        qs.append(jnp.concatenate([qn.astype(BF16), qr.astype(BF16)], axis=-1))
    heads = range(MLA_HEADS_PER_STEP)
    outs = _attend_streams(qs, [kt_ref.at[h] for h in heads], [vo_ref.at[h] for h in heads], kt_ref.shape[2])
    for h, (o, l) in enumerate(outs):
        o_ref[:, h * MLA_V:(h + 1) * MLA_V] = (o * (1.0 / l)).astype(o_ref.dtype)


def mla_attention(q, kv, p1, kr_blk, rope, qn_gain, qr_gain, kn_gain, kr_gain, *, tq, n_ctx):
    b, n_lat, _ = q.shape
    t = kv.shape[1]
    c, sa, sb = rope
    hps = MLA_HEADS_PER_STEP
    row_q = lambda bi, h, qi: (qi, 0)
    full = lambda bi, h, qi: (0, 0)
    pad = lambda g: jnp.concatenate([g.astype(F32), jnp.zeros((LANES - g.shape[0],), F32)]).reshape(1, LANES)
    return pl.pallas_call(
        _mla_attn_kernel,
        grid=(b, MLA_HEADS // hps, n_lat // tq),
        in_specs=[
            pl.BlockSpec((None, tq, hps * 2 * LANES), lambda bi, h, qi: (bi, qi, h)),
            pl.BlockSpec((None, t, hps * 2 * LANES), lambda bi, h, qi: (bi, 0, h)),
            pl.BlockSpec((None, t, LANES), lambda bi, h, qi: (bi, 0, kr_blk)),
            pl.BlockSpec((tq, LANES), row_q), pl.BlockSpec((tq, LANES), row_q), pl.BlockSpec((tq, LANES), row_q),
            pl.BlockSpec((t, LANES), full), pl.BlockSpec((t, LANES), full), pl.BlockSpec((t, LANES), full),
            pl.BlockSpec((1, LANES), full), pl.BlockSpec((1, LANES), full),
            pl.BlockSpec((1, LANES), full), pl.BlockSpec((1, LANES), full),
        ],
        out_specs=pl.BlockSpec((None, tq, hps * MLA_V), lambda bi, h, qi: (bi, qi, h)),
        out_shape=jax.ShapeDtypeStruct((b, n_lat, MLA_HEADS * MLA_V), BF16),
        scratch_shapes=[pltpu.VMEM((hps, 2 * LANES, t), BF16), pltpu.VMEM((hps, t, 2 * LANES), BF16)],
        compiler_params=_cparams(("arbitrary", "arbitrary", "arbitrary")),
        name="mla_attention",
    )(q, kv, p1, c[n_ctx:], sa[n_ctx:], sb[n_ctx:], c, sa, sb, pad(qn_gain), pad(qr_gain), pad(kn_gain), pad(kr_gain))


PAIRS = [(i, j) for i in range(EXPERTS_PER_GROUP) for j in range(i + 1, EXPERTS_PER_GROUP)]
N_CLASSES = N_GROUPS * len(PAIRS)
CLASS_LO = np.array([EXPERTS_PER_GROUP * g + i for g in range(N_GROUPS) for (i, j) in PAIRS], np.int32)
CLASS_HI = np.array([EXPERTS_PER_GROUP * g + j for g in range(N_GROUPS) for (i, j) in PAIRS], np.int32)
EXPERT_TILE = 256
ROW_SLABS = 8
DMA_UNROLL = 8
ROUTE_BATCH = 2
ROW_SLOTS = 4


def _route_kernel(*refs, n_in, n_x, n_ctx_tiles, row_tile0):
    a_refs = refs[:n_in]
    w_refs = refs[n_in:2 * n_in]
    x_refs = refs[2 * n_in:2 * n_in + n_x]
    m_ref, g_ref, rw_ref, rb_ref, xmid_ref, hf_ref, rows_ref, info_ref, count_ref, cnt_ref = refs[2 * n_in + n_x:]
    w1, w2, _ = _split3(rw_ref[...])
    is_ctx = pl.program_id(1) + row_tile0 < n_ctx_tiles
    logits = []
    for e in range(ROUTE_BATCH):
        acc = _dot(a_refs[0][e], w_refs[0][...])
        for a_ref, w_ref in zip(a_refs[1:], w_refs[1:]):
            acc = acc + _dot(a_ref[e], w_ref[...])
        x_in = x_refs[0][e] if n_x == 1 else jnp.where(is_ctx, x_refs[0][e], x_refs[1][e])
        x_mid = x_in + m_ref[e, G_M:G_M + 1, :] * acc
        xmid_ref[e] = x_mid
        h = _rms(x_mid) * g_ref[...]
        h = h * (1.0 + m_ref[e, SC_F:SC_F + 1, :]) + m_ref[e, SH_F:SH_F + 1, :]
        hf_ref[e] = h.astype(hf_ref.dtype)
        rows_ref[e] = _rows_to_slabs(h)
        h1, h2, _ = _split3(h)
        logits.append(_dot_nt(w1, h1) + (_dot_nt(w1, h2) + _dot_nt(w2, h1)))
    biased = _sigmoid(jnp.concatenate(logits, axis=1)) + rb_ref[...]
    row = [biased[e:e + 1, :] for e in range(N_EXPERTS)]
    gscore = []
    for g in range(N_GROUPS):
        m = row[4 * g:4 * g + 4]
        gscore.append(functools.reduce(jnp.maximum, [m[i] + m[j] for (i, j) in PAIRS]))
    hits = []
    for g in range(N_GROUPS):
        best = None
        for g2 in range(N_GROUPS):
            if g2 == g:
                continue
            wins = (gscore[g] > gscore[g2]) if g2 < g else (gscore[g] >= gscore[g2])
            best = wins if best is None else jnp.logical_and(best, wins)
        chosen = []
        for i in range(EXPERTS_PER_GROUP):
            rank = None
            for j in range(EXPERTS_PER_GROUP):
                if j == i:
                    continue
                mi, mj = row[4 * g + i], row[4 * g + j]
                ahead = ((mj >= mi) if j < i else (mj > mi)).astype(jnp.int32)
                rank = ahead if rank is None else rank + ahead
            chosen.append(rank < 2)
        for (i, j) in PAIRS:
            hits.append(jnp.where(best & chosen[i] & chosen[j], 1.0, 0.0))
    onehot = jnp.concatenate(hits, axis=0)
    nt = onehot.shape[1]
    upper = (lax.broadcasted_iota(jnp.int32, (nt, nt), 0) <= lax.broadcasted_iota(jnp.int32, (nt, nt), 1))
    prefix = _dot(onehot.astype(BF16), upper.astype(BF16))

    @pl.when((pl.program_id(0) == 0) & (pl.program_id(1) == 0))
    def _():
        cnt_ref[...] = jnp.zeros_like(cnt_ref)

    seen = cnt_ref[...]
    cls_id = lax.broadcasted_iota(jnp.int32, onehot.shape, 0).astype(F32)
    cls = jnp.sum(onehot * cls_id, axis=0, keepdims=True)
    rank = jnp.sum(onehot * (seen[:, 0:1] + prefix - 1.0), axis=0, keepdims=True)
    info = jnp.concatenate([cls, rank, jnp.zeros((6, nt), F32)], axis=0).astype(jnp.int32)
    tm = nt // ROUTE_BATCH
    for e in range(ROUTE_BATCH):
        info_ref[e] = info[:, e * tm:(e + 1) * tm]
    seen = seen + jnp.sum(onehot, axis=1, keepdims=True)
    cnt_ref[...] = seen
    count_ref[...] = seen


def route(acts, weights, xs, mods, gain, router_w, router_bias, *, tm, n_ctx_tiles, row_tile0):
    b, r, _ = acts[0].shape
    nt = r // tm
    rb = ROUTE_BATCH
    assert b % rb == 0
    parts = xs if isinstance(xs, tuple) else (xs,)
    d = parts[0].shape[-1]
    if len(parts) == 1:
        x_specs = [pl.BlockSpec((rb, tm, d), lambda bi, ti: (bi, ti + row_tile0, 0))]
    else:
        x_specs = [pl.BlockSpec((rb, tm, d), lambda bi, ti: (bi, jnp.minimum(ti + row_tile0, n_ctx_tiles - 1), 0)),
                   pl.BlockSpec((rb, tm, d), lambda bi, ti: (bi, jnp.maximum(ti + row_tile0 - n_ctx_tiles, 0), 0))]
    row = lambda bi, ti: (bi, ti, 0)
    const = lambda bi, ti: (0, 0)
    in_specs = [pl.BlockSpec((rb, tm, a.shape[-1]), row) for a in acts]
    in_specs += [pl.BlockSpec(w.shape, const) for w in weights]
    in_specs += x_specs + [
        pl.BlockSpec((rb, None, 6, d), lambda bi, ti: (bi, ((ti + row_tile0) >= n_ctx_tiles).astype(jnp.int32), 0, 0)),
        pl.BlockSpec((1, d), const), pl.BlockSpec((N_EXPERTS, d), const), pl.BlockSpec((N_EXPERTS, 1), const)]
    x_mid, hf, rows, info, counts = pl.pallas_call(
        functools.partial(_route_kernel, n_in=len(acts), n_x=len(parts), n_ctx_tiles=n_ctx_tiles,
                          row_tile0=row_tile0),
        grid=(b // rb, nt),
        in_specs=in_specs,
        out_specs=[
            pl.BlockSpec((rb, tm, d), row),
            pl.BlockSpec((rb, tm, d), row),
            pl.BlockSpec((rb, tm, ROW_SLABS, LANES), lambda bi, ti: (bi, ti, 0, 0)),
            pl.BlockSpec((rb, None, 8, tm), lambda bi, ti: (bi, ti, 0, 0)),
            pl.BlockSpec((N_CLASSES, LANES), const),
        ],
        out_shape=[jax.ShapeDtypeStruct((b, r, d), F32),
                   jax.ShapeDtypeStruct((b, r, d), BF16),
                   jax.ShapeDtypeStruct((b, r, ROW_SLABS, LANES), F32),
                   jax.ShapeDtypeStruct((b, nt, 8, tm), jnp.int32),
                   jax.ShapeDtypeStruct((N_CLASSES, LANES), F32)],
        scratch_shapes=[pltpu.VMEM((N_CLASSES, LANES), F32)],
        compiler_params=_cparams(("arbitrary", "arbitrary")),
        name="route",
    )(*acts, *weights, *parts, mods, gain.reshape(1, d).astype(F32), router_w.T.astype(F32),
      router_bias.reshape(N_EXPERTS, 1).astype(F32))
    return x_mid, hf, rows.reshape(b * r, ROW_SLABS, LANES), info, counts


def _transpose8(parts):
    sub = lax.broadcasted_iota(jnp.int32, parts[0].shape, 1)
    for s in (4, 2, 1):
        keep = (sub & s) == 0
        new = list(parts)
        for i in range(8):
            if i & s:
                continue
            a, b = parts[i], parts[i | s]
            new[i] = jnp.where(keep, a, pltpu.roll(b, s, 1))
            new[i | s] = jnp.where(keep, pltpu.roll(a, 8 - s, 1), b)
        parts = new
    return parts


def _slabs_to_rows(slabs):
    g = slabs.shape[0] // 8
    x4 = slabs.reshape(g, 8, ROW_SLABS, LANES)
    parts = _transpose8([x4[:, t] for t in range(8)])
    return jnp.concatenate([p.reshape(g * 8, LANES) for p in parts], axis=-1)


def _rows_to_slabs(x):
    g = x.shape[0] // 8
    parts = _transpose8([x[:, k * LANES:(k + 1) * LANES].reshape(g, 8, LANES) for k in range(ROW_SLABS)])
    return jnp.stack(parts, axis=1).reshape(g * 8, ROW_SLABS, LANES)


def _invert_kernel(dest_ref, src_ref):
    def clear(i, carry):
        for u in range(2 * DMA_UNROLL):
            src_ref[i * 2 * DMA_UNROLL + u] = 0
        return carry

    def put(i, carry):
        for u in range(2 * DMA_UNROLL):
            t = i * 2 * DMA_UNROLL + u
            src_ref[dest_ref[t]] = t
        return carry

    lax.fori_loop(0, src_ref.shape[0] // (2 * DMA_UNROLL), clear, 0)
    lax.fori_loop(0, dest_ref.shape[0] // (2 * DMA_UNROLL), put, 0)


def invert_permutation(dest, n_rows_out):
    smem = pl.BlockSpec(memory_space=pltpu.SMEM)
    return pl.pallas_call(
        _invert_kernel,
        in_specs=[smem],
        out_specs=smem,
        out_shape=jax.ShapeDtypeStruct((n_rows_out,), jnp.int32),
        name="moe_invert",
    )(dest)


def _expert_kernel(te_ref, tv_ref, nx_ref, src_ref, rows_hbm, rwl_ref, rwh_ref, wg_hbm, wu_hbm, wd_hbm, y_ref,
                   xbuf_ref, sem_ref, stage_g, stage_u, stage_d, cache_g, cache_u, cache_d, wsem_ref, *, layer):
    i = pl.program_id(0)
    n = pl.num_programs(0)
    tg = y_ref.shape[0]
    slot = i % ROW_SLOTS
    valid = tv_ref[i] != 0
    prev = jnp.maximum(i - 1, 0)

    def fetch(tile, s):
        base = tile * tg
        for t in range(tg):
            pltpu.async_copy(rows_hbm.at[src_ref[base + t]], xbuf_ref.at[s, t], sem_ref.at[s], priority=t % 2)

    def wait_rows(s):
        pltpu.make_async_copy(rows_hbm.at[pl.ds(0, tg)], xbuf_ref.at[s], sem_ref.at[s]).wait()

    def weight_copies(s, e):
        return [pltpu.make_async_copy(w_hbm.at[layer, e], stage.at[s], wsem_ref.at[s])
                for w_hbm, stage in ((wg_hbm, stage_g), (wu_hbm, stage_u), (wd_hbm, stage_d))]

    @pl.when(i == 0)
    def _():
        for k in range(ROW_SLOTS - 1):
            fetch(jnp.minimum(k, n - 1), k)
        for s in range(2):
            for copy in weight_copies(s, te_ref[s * n]):
                copy.start()

    for s in range(2):
        expert = te_ref[s * n + i]

        @pl.when(valid & ((i == 0) | (expert != te_ref[s * n + prev])))
        def _():
            for copy in weight_copies(s, expert):
                copy.wait()
            cache_g[s] = stage_g[s].astype(BF16)
            cache_u[s] = stage_u[s].astype(BF16)
            cache_d[s] = stage_d[s].astype(BF16)
            upcoming = nx_ref[s * n + i]

            @pl.when(upcoming >= 0)
            def _():
                for copy in weight_copies(s, upcoming):
                    copy.start()

    @pl.when(valid)
    def _():
        wait_rows(slot)
        x = _slabs_to_rows(xbuf_ref[slot])
        fetch(jnp.minimum(i + ROW_SLOTS - 1, n - 1), (i + ROW_SLOTS - 1) % ROW_SLOTS)
        xb = x.astype(BF16)
        s_lo = _sigmoid(jnp.sum(x * rwl_ref[...], axis=-1, keepdims=True))
        s_hi = _sigmoid(jnp.sum(x * rwh_ref[...], axis=-1, keepdims=True))
        inv = 1.0 / (s_lo + s_hi)

        def ffn(s):
            a = _dot(xb, cache_g[s])
            u = _dot(xb, cache_u[s])
            return _dot((a * _sigmoid(a) * u).astype(BF16), cache_d[s])

        y_ref[...] = _rows_to_slabs((s_lo * inv) * ffn(0) + (s_hi * inv) * ffn(1))

    @pl.when(jnp.logical_not(valid))
    def _():
        y_ref[...] = jnp.zeros_like(y_ref)

    @pl.when((valid & (i == n - 1)) | (jnp.logical_not(valid) & (tv_ref[prev] != 0)))
    def _():
        first = jnp.where(valid, i + 1, i)
        for k in range(ROW_SLOTS - 1):
            wait_rows((first + k) % ROW_SLOTS)


def experts(tile_expert, tile_valid, next_expert, src, rows, router_w_t, wg, wu, wd, layer):
    tg = EXPERT_TILE
    n_tiles = src.shape[0] // tg
    d, ff = wg.shape[2:]
    lo = lambda i, te, tv, nx, src: (te[i], 0, 0)
    hi = lambda i, te, tv, nx, src: (te[n_tiles + i], 0, 0)
    hbm = pl.BlockSpec(memory_space=pl.ANY)
    return pl.pallas_call(
        functools.partial(_expert_kernel, layer=layer),
        grid_spec=pltpu.PrefetchScalarGridSpec(
            num_scalar_prefetch=4,
            grid=(n_tiles,),
            in_specs=[hbm, pl.BlockSpec((None, 1, d), lo), pl.BlockSpec((None, 1, d), hi), hbm, hbm, hbm],
            out_specs=pl.BlockSpec((tg, ROW_SLABS, LANES), lambda i, te, tv, nx, src: (i, 0, 0)),
            scratch_shapes=[pltpu.VMEM((ROW_SLOTS, tg, ROW_SLABS, LANES), F32), pltpu.SemaphoreType.DMA((ROW_SLOTS,)),
                            pltpu.VMEM((2, d, ff), F32), pltpu.VMEM((2, d, ff), F32), pltpu.VMEM((2, ff, d), F32),
                            pltpu.VMEM((2, d, ff), BF16), pltpu.VMEM((2, d, ff), BF16), pltpu.VMEM((2, ff, d), BF16),
                            pltpu.SemaphoreType.DMA((2,))],
        ),
        out_shape=jax.ShapeDtypeStruct((n_tiles * tg, ROW_SLABS, LANES), F32),
        compiler_params=_cparams(("arbitrary",)),
        name="moe_experts",
    )(tile_expert, tile_valid, next_expert, src, rows, router_w_t, router_w_t, wg, wu, wd)


def _combine_kernel(dest_ref, hf_ref, x_ref, m_ref, wg_ref, wu_ref, wd_ref, y_hbm, o_ref, ybuf_ref, sem_ref):
    i = pl.program_id(0)
    n = pl.num_programs(0)
    tm = hf_ref.shape[0]
    slot = i % 2

    def fetch(tile, s):
        base = tile * tm
        for t in range(tm):
            pltpu.async_copy(y_hbm.at[dest_ref[base + t]], ybuf_ref.at[s, t], sem_ref.at[s], priority=t % 2)

    def wait_rows(s):
        pltpu.make_async_copy(y_hbm.at[pl.ds(0, tm)], ybuf_ref.at[s], sem_ref.at[s]).wait()

    @pl.when(i == 0)
    def _():
        fetch(0, 0)

    fetch(jnp.minimum(i + 1, n - 1), 1 - slot)
    h = hf_ref[...]
    a = _dot(h, wg_ref[...])
    u = _dot(h, wu_ref[...])
    shared = _dot((a * _sigmoid(a) * u).astype(BF16), wd_ref[...])
    wait_rows(slot)
    o_ref[...] = x_ref[...] + m_ref[G_F:G_F + 1, :] * (shared + _slabs_to_rows(ybuf_ref[slot]))

    @pl.when(i == n - 1)
    def _():
        wait_rows(1 - slot)


def combine(dest, hf, xs, mods, wg, wu, wd, y_sorted, *, tm, n_ctx_tiles):
    b, r, d = hf.shape
    nt = r // tm
    ff = wg.shape[1]
    row = lambda i, dest: (i // nt, i % nt, 0)
    const = lambda i, dest: (0, 0)
    return pl.pallas_call(
        _combine_kernel,
        grid_spec=pltpu.PrefetchScalarGridSpec(
            num_scalar_prefetch=1,
            grid=(b * nt,),
            in_specs=[
                pl.BlockSpec((None, tm, d), row),
                pl.BlockSpec((None, tm, d), row),
                pl.BlockSpec((None, None, 6, d),
                             lambda i, dest: (i // nt, ((i % nt) >= n_ctx_tiles).astype(jnp.int32), 0, 0)),
                pl.BlockSpec((d, ff), const), pl.BlockSpec((d, ff), const), pl.BlockSpec((ff, d), const),
                pl.BlockSpec(memory_space=pl.ANY),
            ],
            out_specs=pl.BlockSpec((None, tm, d), row),
            scratch_shapes=[pltpu.VMEM((2, tm, ROW_SLABS, LANES), F32), pltpu.SemaphoreType.DMA((2,))],
        ),
        out_shape=jax.ShapeDtypeStruct((b, r, d), F32),
        compiler_params=_cparams(("arbitrary",)),
        name="moe_combine",
    )(dest, hf, xs, mods, wg, wu, wd, y_sorted)


def moe_block(acts, weights, xs, mods, gain, router_w, router_bias, ew_gate, ew_up, ew_down, layer, sw_gate, sw_up,
              sw_down, *, tm, n_ctx_tiles, row_tile0):
    xs_mid, hf, rows, info, counts = route(acts, weights, xs, mods, gain, router_w, router_bias, tm=tm,
                                           n_ctx_tiles=n_ctx_tiles, row_tile0=row_tile0)
    b, r, d = xs_mid.shape
    n = b * r
    tg = EXPERT_TILE
    cls = info[:, :, 0, :].reshape(n)
    rank = info[:, :, 1, :].reshape(n)
    padded = ((counts[:, 0].astype(jnp.int32) + tg - 1) // tg) * tg
    ends = jnp.cumsum(padded)
    dest = (ends - padded)[cls] + rank
    n_tiles = n // tg + N_CLASSES
    tile_start = jnp.arange(n_tiles, dtype=jnp.int32) * tg
    tile_valid = tile_start < ends[-1]
    last_start = jnp.maximum(ends[-1] - tg, 0)
    start = jnp.where(tile_valid, tile_start, last_start)
    tile_cls = jnp.sum((ends[None, :] <= start[:, None]).astype(jnp.int32), axis=1)
    tile_cls = jnp.minimum(tile_cls, N_CLASSES - 1)
    slot_expert = jnp.stack([jnp.asarray(CLASS_LO)[tile_cls], jnp.asarray(CLASS_HI)[tile_cls]])
    tile_expert = slot_expert.reshape(-1)
    later = (tile_start[None, :] > tile_start[:, None]) & tile_valid[None, :]
    differs = slot_expert[:, None, :] != slot_expert[:, :, None]
    first = jnp.min(jnp.where(later[None] & differs, jnp.arange(n_tiles)[None, None, :], n_tiles), axis=-1)
    next_expert = jnp.where(first < n_tiles, jnp.take_along_axis(slot_expert, jnp.minimum(first, n_tiles - 1), axis=1),
                            -1).reshape(-1).astype(jnp.int32)
    src = invert_permutation(dest, n_tiles * tg)
    y_sorted = experts(tile_expert, tile_valid.astype(jnp.int32), next_expert, src, rows,
                       router_w.T.astype(F32).reshape(N_EXPERTS, 1, d), ew_gate, ew_up, ew_down, layer)
    return combine(dest, hf, xs_mid, mods, sw_gate.astype(BF16), sw_up.astype(BF16), sw_down.astype(BF16), y_sorted,
                   tm=tm, n_ctx_tiles=max(n_ctx_tiles - row_tile0, 0))


def kernel(x, c, ctx, c_ctx, mod_w, mod_b, norm_mix, norm_ffn, even_w_in, even_w_out, diff_q_gain, diff_k_gain, diff_lambda, diff_subln, hgrn_lb_logits, hgrn_out_gain, odd_w_in, mla_q_a_gain, mla_kv_a_gain, mla_w_uq, mla_w_ukv, mla_q_nope_gain, mla_q_rope_gain, mla_k_nope_gain, mla_k_rope_gain, odd_w_out, router_w, router_bias, expert_w_gate, expert_w_up, expert_w_down, shared_w_gate, shared_w_up, shared_w_down):
    b, n_lat, d = x.shape
    n_ctx = ctx.shape[1]
    depth = mod_w.shape[0]
    tm = 256 if n_ctx % 256 == 0 else 128
    n_ctx_tiles = n_ctx // tm
    rope = rope_lane_tables(n_ctx, n_lat)

    mod_rows = 16
    c_rows = jnp.concatenate([c, c_ctx[None, :], jnp.zeros((mod_rows - b - 1, d), F32)], axis=0)
    mod_all = modulation(c_rows, mod_w, mod_b).reshape(depth, mod_rows, 6, d)
    lb_all = jnp.cumsum(jax.nn.softmax(hgrn_lb_logits.astype(F32), axis=0), axis=0)

    xs = (ctx, x)
    for layer in range(depth):
        last = layer == depth - 1
        j = layer // 2
        mods = jnp.stack([jnp.broadcast_to(mod_all[layer, b], (b, 6, d)), mod_all[layer, :b]], axis=1)
        row_tile0 = n_ctx_tiles if last else 0
        if layer % 2 == 0:
            lam_init = 0.8 - 0.6 * math.exp(-0.3 * layer)
            p = norm_matmul(xs, mods, norm_mix[layer], even_w_in[j].astype(BF16), tm=tm, n_ctx_tiles=n_ctx_tiles)
            oa = diff_attention(p, diff_lambda[j], diff_q_gain[j], diff_k_gain[j], diff_subln[j], rope,
                                tq=tm, n_ctx=n_ctx, lam_init=lam_init, q_blk0=0, k_blk0=4, v_blk0=8)
            ob = hgrn_bidirectional(p, lb_all[j, 0], lb_all[j, 1], hgrn_out_gain[j], n_ctx=n_ctx, blk0=12)
            if last:
                oa, ob = oa[:, n_ctx:], ob[:, n_ctx:]
            w_out = even_w_out[j].astype(BF16)
            half = oa.shape[-1]
            acts, weights = [oa, ob], [w_out[:half], w_out[half:]]
        else:
            w_in = odd_w_in[j]
            zpad = jnp.zeros((d, LANES - MLA_ROPE), F32)
            w_in_r = jnp.concatenate([w_in[:, MLA_Q_LORA:MLA_Q_LORA + MLA_KV_LORA], w_in[:, MLA_Q_LORA + MLA_KV_LORA:],
                                      zpad, w_in[:, :MLA_Q_LORA]], axis=1).astype(BF16)
            w_uq = mla_w_uq[j].reshape(MLA_Q_LORA, MLA_HEADS, MLA_NOPE + MLA_ROPE)
            w_uq = jnp.concatenate([w_uq, jnp.zeros((MLA_Q_LORA, MLA_HEADS, LANES - MLA_ROPE), F32)], axis=-1)
            w_uq = w_uq.reshape(MLA_Q_LORA, MLA_HEADS * 2 * LANES).astype(BF16)
            if not last:
                raise NotImplementedError("context queries for a non-final latent-attention layer")
            kr, q, kv = mla_projections(xs, mods, norm_mix[layer], w_in_r, mla_q_a_gain[j], mla_kv_a_gain[j], w_uq,
                                        mla_w_ukv[j].astype(BF16), tm=tm, n_ctx_tiles=n_ctx_tiles)
            o = mla_attention(q, kv, kr, 0, rope, mla_q_nope_gain[j], mla_q_rope_gain[j], mla_k_nope_gain[j],
                              mla_k_rope_gain[j], tq=2 * tm, n_ctx=n_ctx)
            acts, weights = [o], [odd_w_out[j].astype(BF16)]
        xs = moe_block(acts, weights, xs, mods, norm_ffn[layer], router_w, router_bias,
                       expert_w_gate, expert_w_up, expert_w_down, layer,
                       shared_w_gate[layer], shared_w_up[layer], shared_w_down[layer],
                       tm=tm, n_ctx_tiles=n_ctx_tiles, row_tile0=row_tile0)
    return xs if xs.shape[1] == n_lat else xs[:, n_ctx:]
```

```python
import functools
import math

import numpy as np
import jax
import jax.numpy as jnp
from jax import lax
from jax.experimental import pallas as pl
from jax.experimental.pallas import tpu as pltpu

F32 = jnp.float32
BF16 = jnp.bfloat16

LANES = 128
VMEM_LIMIT = 56 * 1024 * 1024

GRID_W = 64
DIFF_HEADS = 4
DIFF_HEAD_DIM = 64
HGRN_HEADS = 4
HGRN_K_DIM = 128
MLA_HEADS = 8
MLA_NOPE = 128
MLA_ROPE = 64
MLA_V = 128
MLA_Q_LORA = 384
MLA_KV_LORA = 256
N_EXPERTS = 16
N_GROUPS = 4
EXPERTS_PER_GROUP = 4
ROPE_BASE = 10000.0
EPS = 1e-6
LOG2E = 1.4426950408889634
GLA_CHUNK = 64
ATTN_KEY_CHUNK = 1152
MLA_HEADS_PER_STEP = 2
DIFF_HEADS_PER_STEP = 2
HGRN_HEADS_PER_STEP = 2

SH_M, SC_M, G_M, SH_F, SC_F, G_F = range(6)


def _sigmoid(x):
    return 0.5 * jnp.tanh(0.5 * x) + 0.5


def _dot(a, b):
    return jnp.dot(a, b, preferred_element_type=F32)


def _dot_nt(a, b):
    return lax.dot_general(a, b, (((1,), (1,)), ((), ())), preferred_element_type=F32)


def _dot_tn(a, b):
    return lax.dot_general(a, b, (((0,), (0,)), ((), ())), preferred_element_type=F32)


def _split3(x):
    hi = x.astype(BF16)
    r = x - hi.astype(F32)
    mid = r.astype(BF16)
    lo = (r - mid.astype(F32)).astype(BF16)
    return hi, mid, lo


def _rms(x, width=None):
    n = x.shape[-1] if width is None else width
    return x * lax.rsqrt(jnp.sum(x * x, axis=-1, keepdims=True) * (1.0 / n) + EPS)


def _cparams(sem):
    return pltpu.CompilerParams(dimension_semantics=sem, vmem_limit_bytes=VMEM_LIMIT)


def _mod_kernel(c_ref, w_ref, b_ref, o_ref):
    c = c_ref[...]
    s = c * _sigmoid(c)
    o_ref[...] = _dot(s.astype(BF16), w_ref[...].astype(BF16)) + b_ref[...]


def modulation(c_rows, mod_w, mod_b):
    n_layers, d, n = mod_w.shape
    rows = c_rows.shape[0]
    tn = 1536
    return pl.pallas_call(
        _mod_kernel,
        grid=(n_layers, n // tn),
        in_specs=[
            pl.BlockSpec((rows, d), lambda l, j: (0, 0)),
            pl.BlockSpec((None, d, tn), lambda l, j: (l, 0, j)),
            pl.BlockSpec((None, 1, tn), lambda l, j: (l, 0, j)),
        ],
        out_specs=pl.BlockSpec((None, rows, tn), lambda l, j: (l, 0, j)),
        out_shape=jax.ShapeDtypeStruct((n_layers, rows, n), F32),
        compiler_params=_cparams(("arbitrary", "arbitrary")),
        name="modulation",
    )(c_rows, mod_w, mod_b.reshape(n_layers, 1, n))


def _stream_specs(xs, tm, n_ctx_tiles, row_tile0=0):
    parts = xs if isinstance(xs, tuple) else (xs,)
    d = parts[0].shape[-1]
    if len(parts) == 1:
        return [pl.BlockSpec((None, tm, d), lambda bi, ti: (bi, ti + row_tile0, 0))], list(parts)
    return [pl.BlockSpec((None, tm, d), lambda bi, ti: (bi, jnp.minimum(ti + row_tile0, n_ctx_tiles - 1), 0)),
            pl.BlockSpec((None, tm, d), lambda bi, ti: (bi, jnp.maximum(ti + row_tile0 - n_ctx_tiles, 0), 0))], list(parts)


def _stream_tile(x_refs, n_ctx_tiles, row_tile0=0):
    if len(x_refs) == 1:
        return x_refs[0][...]
    return jnp.where(pl.program_id(1) + row_tile0 < n_ctx_tiles, x_refs[0][...], x_refs[1][...])


def _norm_matmul_kernel(*refs, n_x, n_ctx_tiles):
    m_ref, g_ref, w_ref, o_ref = refs[n_x:]
    h = _rms(_stream_tile(refs[:n_x], n_ctx_tiles)) * g_ref[...]
    h = h * (1.0 + m_ref[SC_M:SC_M + 1, :]) + m_ref[SH_M:SH_M + 1, :]
    o_ref[...] = _dot(h.astype(BF16), w_ref[...]).astype(o_ref.dtype)


def norm_matmul(xs, mods, gain, w, *, tm, n_ctx_tiles):
    x_specs, x_args = _stream_specs(xs, tm, n_ctx_tiles)
    b = x_args[0].shape[0]
    t = sum(a.shape[1] for a in x_args)
    k, n = w.shape
    return pl.pallas_call(
        functools.partial(_norm_matmul_kernel, n_x=len(x_args), n_ctx_tiles=n_ctx_tiles),
        grid=(b, t // tm),
        in_specs=x_specs + [
            pl.BlockSpec((None, None, 6, k), lambda bi, ti: (bi, (ti >= n_ctx_tiles).astype(jnp.int32), 0, 0)),
            pl.BlockSpec((1, k), lambda bi, ti: (0, 0)), pl.BlockSpec((k, n), lambda bi, ti: (0, 0))],
        out_specs=pl.BlockSpec((None, tm, n), lambda bi, ti: (bi, ti, 0)),
        out_shape=jax.ShapeDtypeStruct((b, t, n), BF16),
        compiler_params=_cparams(("arbitrary", "arbitrary")),
        name="norm_matmul",
    )(*x_args, mods, gain.reshape(1, k).astype(F32), w)


def _mla_proj_kernel(x_ref, m_ref, g_ref, win_ref, qg_ref, kvg_ref, wuq_ref, wukv_ref, kr_ref, q_ref, kv_ref,
                     *, n_ctx_tiles):
    h = _rms(x_ref[...]) * g_ref[...]
    h = h * (1.0 + m_ref[SC_M:SC_M + 1, :]) + m_ref[SH_M:SH_M + 1, :]
    p = _dot(h.astype(BF16), win_ref[...])
    kr_ref[...] = p[:, MLA_KV_LORA:MLA_KV_LORA + LANES].astype(kr_ref.dtype)
    ckv = _rms(p[:, 0:MLA_KV_LORA]) * kvg_ref[...]
    kv_ref[...] = _dot(ckv.astype(BF16), wukv_ref[...]).astype(kv_ref.dtype)

    @pl.when(pl.program_id(1) >= n_ctx_tiles)
    def _():
        cq = _rms(p[:, MLA_KV_LORA + LANES:]) * qg_ref[...]
        q_ref[...] = _dot(cq.astype(BF16), wuq_ref[...]).astype(q_ref.dtype)

    @pl.when(pl.program_id(1) < n_ctx_tiles)
    def _():
        q_ref[...] = jnp.zeros_like(q_ref)


def mla_projections(xs, mods, gain, w_in_r, q_a_gain, kv_a_gain, w_uq, w_ukv, *, tm, n_ctx_tiles):
    b, t, d = xs.shape
    nt = t // tm
    n_lat = t - n_ctx_tiles * tm
    const = lambda a: pl.BlockSpec(a.shape, lambda bi, ti: (0, 0))
    row = lambda bi, ti: (bi, ti, 0)
    g2 = lambda g: g.reshape(1, -1).astype(F32)
    args = [g2(gain), w_in_r, g2(q_a_gain), g2(kv_a_gain), w_uq, w_ukv]
    return pl.pallas_call(
        functools.partial(_mla_proj_kernel, n_ctx_tiles=n_ctx_tiles),
        grid=(b, nt),
        in_specs=[pl.BlockSpec((None, tm, d), row),
                  pl.BlockSpec((None, None, 6, d), lambda bi, ti: (bi, (ti >= n_ctx_tiles).astype(jnp.int32), 0, 0))]
                 + [const(a) for a in args],
        out_specs=[pl.BlockSpec((None, tm, LANES), row),
                   pl.BlockSpec((None, tm, w_uq.shape[1]), lambda bi, ti: (bi, jnp.maximum(ti - n_ctx_tiles, 0), 0)),
                   pl.BlockSpec((None, tm, w_ukv.shape[1]), row)],
        out_shape=[jax.ShapeDtypeStruct((b, t, LANES), BF16),
                   jax.ShapeDtypeStruct((b, n_lat, w_uq.shape[1]), BF16),
                   jax.ShapeDtypeStruct((b, t, w_ukv.shape[1]), BF16)],
        compiler_params=_cparams(("arbitrary", "arbitrary")),
        name="mla_projections",
    )(xs, mods, *args)


def rope_lane_tables(n_ctx, n_lat):
    rows = n_lat // GRID_W
    row = np.repeat(np.arange(rows), GRID_W).astype(np.float32)
    col = np.tile(np.arange(GRID_W), rows).astype(np.float32)
    axis_dim = DIFF_HEAD_DIM // 2
    inv_freq = jnp.asarray(ROPE_BASE, F32) ** (-jnp.arange(0, axis_dim, 2, dtype=F32) / axis_dim)
    ang_r = jnp.asarray(row)[:, None] * inv_freq
    ang_c = jnp.asarray(col)[:, None] * inv_freq
    lane = np.arange(LANES)
    freq_idx = lane % 16
    use_col = (lane % 64) >= 32
    first = (lane % 32) < 16
    ang = jnp.where(use_col[None, :], ang_c[:, freq_idx], ang_r[:, freq_idx])
    cos, sin = jnp.cos(ang), jnp.sin(ang)
    c = jnp.concatenate([jnp.ones((n_ctx, LANES), F32), cos], axis=0)
    sa = jnp.concatenate([jnp.zeros((n_ctx, LANES), F32), jnp.where(first[None, :], -sin, 0.0)], axis=0)
    sb = jnp.concatenate([jnp.zeros((n_ctx, LANES), F32), jnp.where(first[None, :], 0.0, sin)], axis=0)
    return c, sa, sb


def _rope(x, c, sa, sb):
    return x * c + pltpu.roll(x, LANES - 16, 1) * sa + pltpu.roll(x, 16, 1) * sb


def _block_ones(width):
    r = lax.broadcasted_iota(jnp.int32, (LANES, LANES), 0) // width
    c = lax.broadcasted_iota(jnp.int32, (LANES, LANES), 1) // width
    return (r == c).astype(BF16)


def _block_rms(x, ones, width):
    xx = x * x
    hi = xx.astype(BF16)
    lo = (xx - hi.astype(F32)).astype(BF16)
    ms = (_dot(hi, ones) + _dot(lo, ones)) * (1.0 / width)
    return x * lax.rsqrt(ms + EPS)


def _attend_streams(qs, kt_refs, vo_refs, nk):
    n = len(qs)
    chunk = ATTN_KEY_CHUNK if nk % ATTN_KEY_CHUNK == 0 else nk
    nc = nk // chunk
    s = [[None] * nc for _ in range(n)]
    p = [[None] * nc for _ in range(n)]
    m, ol = [None] * n, [None] * n

    def logits(i, c):
        s[i][c] = _dot(qs[i], kt_refs[i][:, c * chunk:(c + 1) * chunk])
        mc = jnp.max(s[i][c], axis=-1, keepdims=True)
        m[i] = mc if c == 0 else jnp.maximum(m[i], mc)

    def exps(i, c):
        p[i][c] = jnp.exp2(s[i][c] - m[i]).astype(BF16)

    def values(i, c):
        oc = _dot(p[i][c], vo_refs[i][c * chunk:(c + 1) * chunk, :])
        ol[i] = oc if c == 0 else ol[i] + oc

    for step in range(n + 2):
        for c in range(nc):
            if step < n:
                logits(step, c)
            if 0 <= step - 1 < n:
                exps(step - 1, c)
            if 0 <= step - 2 < n:
                values(step - 2, c)
    return [(x[:, 0:LANES], x[:, LANES:LANES + 1]) for x in ol]


def _diff_attn_kernel(lam_ref, q_ref, k_ref, v_ref, cq_ref, saq_ref, sbq_ref, ck_ref, sak_ref, sbk_ref,
                      qg_ref, kg_ref, sub_ref, o_ref, kt_ref, vo_ref, *, n_ctx, n_ctx_tiles, lam_init):
    qi = pl.program_id(2)
    ones = _block_ones(DIFF_HEAD_DIM)
    heads = range(DIFF_HEADS_PER_STEP)

    @pl.when(qi == 0)
    def _():
        for h in heads:
            k = _block_rms(k_ref[:, h * LANES:(h + 1) * LANES].astype(F32), ones, DIFF_HEAD_DIM) * kg_ref[...]
            kt_ref[h] = _rope(k, ck_ref[...], sak_ref[...], sbk_ref[...]).T.astype(BF16)
            vo_ref[h, :, 0:LANES] = v_ref[:, h * LANES:(h + 1) * LANES]
            vo_ref[h, :, LANES:2 * LANES] = jnp.ones((v_ref.shape[0], LANES), BF16)

    qs = []
    for h in heads:
        q = _block_rms(q_ref[:, h * LANES:(h + 1) * LANES].astype(F32), ones, DIFF_HEAD_DIM) * qg_ref[...]
        q = _rope(q, cq_ref[...], saq_ref[...], sbq_ref[...]) * (DIFF_HEAD_DIM ** -0.5 * LOG2E)
        lane = lax.broadcasted_iota(jnp.int32, q.shape, 1)
        qs.append(jnp.where(lane < DIFF_HEAD_DIM, q, 0.0).astype(BF16))
        qs.append(jnp.where(lane >= DIFF_HEAD_DIM, q, 0.0).astype(BF16))
    lv = lam_ref[...]
    lam = (jnp.exp(jnp.sum(lv[0:1] * lv[1:2], axis=-1, keepdims=True))
           - jnp.exp(jnp.sum(lv[2:3] * lv[3:4], axis=-1, keepdims=True)) + lam_init)

    def attend(nk):
        outs = _attend_streams(qs, [kt_ref.at[h] for h in heads for _ in range(2)],
                               [vo_ref.at[h] for h in heads for _ in range(2)], nk)
        for h in heads:
            (o1, l1), (o2, l2) = outs[2 * h], outs[2 * h + 1]
            o = o1 * (1.0 / l1) - o2 * (lam / l2)
            o_ref[:, h * LANES:(h + 1) * LANES] = (_rms(o) * sub_ref[...] * (1.0 - lam_init)).astype(o_ref.dtype)

    @pl.when(qi < n_ctx_tiles)
    def _():
        attend(n_ctx)

    @pl.when(qi >= n_ctx_tiles)
    def _():
        attend(k_ref.shape[0])


def diff_attention(p, lam_vecs, q_gain, k_gain, subln, rope, *, tq, n_ctx, lam_init, q_blk0, k_blk0, v_blk0):
    b, t, _ = p.shape
    c, sa, sb = rope
    nq = t // tq
    hps = DIFF_HEADS_PER_STEP
    assert q_blk0 % hps == 0 and k_blk0 % hps == 0 and v_blk0 % hps == 0
    row_q = lambda bi, h, qi: (qi, 0)
    full = lambda bi, h, qi: (0, 0)
    tile2 = lambda g: jnp.tile(g.astype(F32), 2).reshape(1, LANES)
    return pl.pallas_call(
        functools.partial(_diff_attn_kernel, n_ctx=n_ctx, n_ctx_tiles=n_ctx // tq, lam_init=lam_init),
        grid=(b, DIFF_HEADS // hps, nq),
        in_specs=[
            pl.BlockSpec((4, DIFF_HEAD_DIM), full),
            pl.BlockSpec((None, tq, hps * LANES), lambda bi, h, qi: (bi, qi, q_blk0 // hps + h)),
            pl.BlockSpec((None, t, hps * LANES), lambda bi, h, qi: (bi, 0, k_blk0 // hps + h)),
            pl.BlockSpec((None, t, hps * LANES), lambda bi, h, qi: (bi, 0, v_blk0 // hps + h)),
            pl.BlockSpec((tq, LANES), row_q), pl.BlockSpec((tq, LANES), row_q), pl.BlockSpec((tq, LANES), row_q),
            pl.BlockSpec((t, LANES), full), pl.BlockSpec((t, LANES), full), pl.BlockSpec((t, LANES), full),
            pl.BlockSpec((1, LANES), full), pl.BlockSpec((1, LANES), full), pl.BlockSpec((1, LANES), full),
        ],
        out_specs=pl.BlockSpec((None, tq, hps * LANES), lambda bi, h, qi: (bi, qi, h)),
        out_shape=jax.ShapeDtypeStruct((b, t, DIFF_HEADS * LANES), BF16),
        scratch_shapes=[pltpu.VMEM((hps, LANES, t), BF16), pltpu.VMEM((hps, t, 2 * LANES), BF16)],
        compiler_params=_cparams(("arbitrary", "arbitrary", "arbitrary")),
        name="diff_attention",
    )(lam_vecs.astype(F32), p, p, p, c, sa, sb, c, sa, sb, tile2(q_gain), tile2(k_gain),
      subln.astype(F32).reshape(1, LANES))


def _gla_constants(c):
    levels = int(math.log2(c))
    t = np.arange(c)[:, None]
    u = np.arange(c)[None, :]
    stack = [[u <= t], [u >= t]]
    qside = [[], []]
    pair = [[t == u], [t == u]]
    for lv in range(1, levels + 1):
        up_t = ((t >> (lv - 1)) & 1) == 1
        up_u = ((u >> (lv - 1)) & 1) == 1
        same = (t >> lv) == (u >> lv)
        qside[0].append(np.broadcast_to(up_t, (c, LANES)))
        qside[1].append(np.broadcast_to(~up_t, (c, LANES)))
        pair[0].append(same & up_t & ~up_u)
        pair[1].append(same & ~up_t & up_u)
    f32 = lambda x: np.asarray(x, np.float32)
    stack = np.stack([np.tile(f32(m[0]), (1, 3)) for m in stack])
    return (jnp.asarray(stack, BF16), jnp.asarray(np.stack([f32(m) for m in qside])),
            jnp.asarray(np.stack([f32(m) for m in pair])))


def _split_values(cum, d):
    c, w = cum.shape
    levels = int(math.log2(c))
    g = c // 8
    cum3 = cum.reshape(g, 8, w)
    sub = lax.broadcasted_iota(jnp.int32, (g, 8, w), 1)
    row_of_group = lambda off: jnp.broadcast_to(cum3[:, off:off + 1, :], (g, 8, w))
    if d == 0:
        pair_level = jnp.where((sub & 1) == 1, pltpu.roll(cum3, 1, 1), cum3)
        quad_level = jnp.where(sub < 4, row_of_group(1), row_of_group(5))
        oct_level = row_of_group(3)
    else:
        pair_level = jnp.where((sub & 1) == 0, pltpu.roll(cum3, 7, 1), cum3)
        quad_level = jnp.where(sub < 4, row_of_group(2), row_of_group(6))
        oct_level = row_of_group(4)
    out = [x.reshape(c, w) for x in (pair_level, quad_level, oct_level)]
    for lv in range(4, levels + 1):
        size = 1 << lv
        at = size // 2 - 1 if d == 0 else size // 2
        out.append(jnp.concatenate([jnp.broadcast_to(cum[b0 + at:b0 + at + 1, :], (size, w))
                                    for b0 in range(0, c, size)], axis=0))
    return out


def _gla_chunk(q, k, v, g2, s, stack, qside_ref, pair_ref, d):
    c = q.shape[0]
    levels = int(math.log2(c))
    gcat = jnp.concatenate(_split3(g2), axis=0)
    cum = _dot(stack, gcat)
    tot_col = _dot_tn(gcat, jnp.ones((3 * c, LANES), BF16))
    diag = jnp.sum(q * k, axis=-1, keepdims=True)
    yield
    tot = cum[0:1] if d == 1 else cum[c - 1:c]
    split = _split_values(cum, d)
    zz = []
    for lv in range(1, levels + 1):
        e = jnp.exp2(-jnp.abs(cum - split[lv - 1]))
        z = (jnp.where(qside_ref[d, lv - 1] > 0.5, q, k) * e).astype(BF16)
        zz.append(_dot_nt(z, z))
    q_in = (q * jnp.exp2(cum)).astype(BF16)
    ks = (k * jnp.exp2(tot - cum)).astype(BF16)
    ds = _dot_tn(ks, v)
    yield
    a = pair_ref[d, 0] * diag
    for lv in range(1, levels + 1):
        a = a + pair_ref[d, lv] * zz[lv - 1]
    lhs = jnp.concatenate([q_in, a.astype(BF16)], axis=1)
    o = _dot(lhs, jnp.concatenate([s.astype(BF16), v], axis=0))
    s_new = s * jnp.exp2(tot_col) + ds
    yield
    return o, s_new


def _run_interleaved(gens):
    results = [None] * len(gens)
    live = list(range(len(gens)))
    while live:
        for i in list(live):
            try:
                next(gens[i])
            except StopIteration as stop:
                results[i] = stop.value
                live.remove(i)
    return results


def _hgrn_kernel(qz_ref, zf_ref, zb_ref, v_ref, gz_ref, lbf_ref, lbb_ref, og_ref, stack_ref, qside_ref, pair_ref,
                 o_ref, q_ref, kf_ref, gf_ref, kb_ref, gb_ref, of_ref, ob_ref, st_ref, *, n_ctx_chunks):
    c = GLA_CHUNK
    t = qz_ref.shape[0]
    n = t // c
    heads = range(HGRN_HEADS_PER_STEP)
    qz = qz_ref[...].astype(F32)
    q_ref[...] = qz * _sigmoid(qz) * (HGRN_K_DIM ** -0.5)
    for z_ref, lb_ref, k_ref, g_ref in ((zf_ref, lbf_ref, kf_ref, gf_ref), (zb_ref, lbb_ref, kb_ref, gb_ref)):
        lb = jnp.concatenate([lb_ref[h] for h in heads], axis=-1)
        f = lb + (1.0 - lb) * _sigmoid(z_ref[...].astype(F32))
        k_ref[...] = 1.0 - f
        g_ref[...] = jnp.log(f) * LOG2E
    st_ref[...] = jnp.zeros_like(st_ref)

    def body(i, carry):
        rf = pl.multiple_of(i * c, c)
        cb = jnp.where(i < n_ctx_chunks, n_ctx_chunks - 1 - i, n - 1 - i + n_ctx_chunks)
        rb = pl.multiple_of(cb * c, c)
        chains = [(h, d, pl.ds(r0, c), k_ref, g_ref, out_ref) for h in heads
                  for d, r0, k_ref, g_ref, out_ref in ((0, rf, kf_ref, gf_ref, of_ref), (1, rb, kb_ref, gb_ref, ob_ref))]
        col = lambda h: slice(h * LANES, (h + 1) * LANES)
        outs = _run_interleaved([
            _gla_chunk(q_ref[rows, col(h)], k_ref[rows, col(h)], v_ref[rows, col(h)], g_ref[rows, col(h)],
                       st_ref[h, d], stack_ref[d], qside_ref, pair_ref, d)
            for (h, d, rows, k_ref, g_ref, _) in chains])
        for (h, d, rows, _, _, out_ref), (o, s_new) in zip(chains, outs):
            out_ref[rows, col(h)] = o
            st_ref[h, d] = s_new
        return carry

    lax.fori_loop(0, n, body, 0)
    gz = gz_ref[...].astype(F32)
    for h in heads:
        cols = slice(h * LANES, (h + 1) * LANES)
        o = _rms(of_ref[:, cols] + ob_ref[:, cols]) * og_ref[...]
        o_ref[:, cols] = (o * (gz[:, cols] * _sigmoid(gz[:, cols]))).astype(o_ref.dtype)


def hgrn_bidirectional(p, lb_fwd, lb_bwd, out_gain, *, n_ctx, blk0):
    b, t, _ = p.shape
    c = GLA_CHUNK
    consts = _gla_constants(c)
    h = HGRN_HEADS
    hps = HGRN_HEADS_PER_STEP
    assert blk0 % hps == 0 and h % hps == 0
    seg = lambda s: pl.BlockSpec((None, t, hps * LANES), lambda bi, hi: (bi, 0, (blk0 + s * h) // hps + hi))
    per_head = pl.BlockSpec((hps, 1, LANES), lambda bi, hi: (hi, 0, 0))
    const = lambda a: pl.BlockSpec(a.shape, lambda bi, hi: (0,) * a.ndim)
    seq = pltpu.VMEM((t, hps * LANES), F32)
    return pl.pallas_call(
        functools.partial(_hgrn_kernel, n_ctx_chunks=n_ctx // c),
        grid=(b, h // hps),
        in_specs=[seg(0), seg(1), seg(2), seg(3), seg(4), per_head, per_head,
                  pl.BlockSpec((1, LANES), lambda bi, hi: (0, 0))] + [const(a) for a in consts],
        out_specs=pl.BlockSpec((None, t, hps * LANES), lambda bi, hi: (bi, 0, hi)),
        out_shape=jax.ShapeDtypeStruct((b, t, h * LANES), BF16),
        scratch_shapes=[seq, seq, seq, seq, seq, seq, seq, pltpu.VMEM((hps, 2, LANES, LANES), F32)],
        compiler_params=_cparams(("arbitrary", "arbitrary")),
        name="hgrn_bidirectional",
    )(p, p, p, p, p, lb_fwd.reshape(h, 1, LANES), lb_bwd.reshape(h, 1, LANES),
      out_gain.astype(F32).reshape(1, LANES), *consts)


def _mla_attn_kernel(q_ref, kv_ref, kr_ref, cq_ref, saq_ref, sbq_ref, ck_ref, sak_ref, sbk_ref,
                     qn_ref, qr_ref, kn_ref, krg_ref, o_ref, kt_ref, vo_ref):
    qi = pl.program_id(2)
    ones_nope = _block_ones(MLA_NOPE)
    ones_rope = _block_ones(MLA_ROPE)
    hw = 2 * LANES

    @pl.when(qi == 0)
    def _():
        kr = _block_rms(kr_ref[...].astype(F32), ones_rope, MLA_ROPE) * krg_ref[...]
        kr = _rope(kr, ck_ref[...], sak_ref[...], sbk_ref[...]).T.astype(BF16)
        for h in range(MLA_HEADS_PER_STEP):
            kn = _block_rms(kv_ref[:, h * hw:h * hw + MLA_NOPE].astype(F32), ones_nope, MLA_NOPE) * kn_ref[...]
            kt_ref[h, 0:LANES, :] = kn.T.astype(BF16)
            kt_ref[h, LANES:hw, :] = kr
            vo_ref[h, :, 0:LANES] = kv_ref[:, h * hw + MLA_NOPE:(h + 1) * hw]
            vo_ref[h, :, LANES:2 * LANES] = jnp.ones((kv_ref.shape[0], LANES), BF16)

    scale = (MLA_NOPE + MLA_ROPE) ** -0.5 * LOG2E
    qs = []
    for h in range(MLA_HEADS_PER_STEP):
        qn = _block_rms(q_ref[:, h * hw:h * hw + MLA_NOPE].astype(F32), ones_nope, MLA_NOPE) * (qn_ref[...] * scale)
        qr = _block_rms(q_ref[:, h * hw + MLA_NOPE:(h + 1) * hw].astype(F32), ones_rope, MLA_ROPE) * qr_ref[...]
        qr = _rope(qr, cq_ref[...], saq_ref[...], sbq_ref[...]) * scale
        qs.append(jnp.concatenate([qn.astype(BF16), qr.astype(BF16)], axis=-1))
    heads = range(MLA_HEADS_PER_STEP)
    outs = _attend_streams(qs, [kt_ref.at[h] for h in heads], [vo_ref.at[h] for h in heads], kt_ref.shape[2])
    for h, (o, l) in enumerate(outs):
        o_ref[:, h * MLA_V:(h + 1) * MLA_V] = (o * (1.0 / l)).astype(o_ref.dtype)


def mla_attention(q, kv, p1, kr_blk, rope, qn_gain, qr_gain, kn_gain, kr_gain, *, tq, n_ctx):
    b, n_lat, _ = q.shape
    t = kv.shape[1]
    c, sa, sb = rope
    hps = MLA_HEADS_PER_STEP
    row_q = lambda bi, h, qi: (qi, 0)
    full = lambda bi, h, qi: (0, 0)
    pad = lambda g: jnp.concatenate([g.astype(F32), jnp.zeros((LANES - g.shape[0],), F32)]).reshape(1, LANES)
    return pl.pallas_call(
        _mla_attn_kernel,
        grid=(b, MLA_HEADS // hps, n_lat // tq),
        in_specs=[
            pl.BlockSpec((None, tq, hps * 2 * LANES), lambda bi, h, qi: (bi, qi, h)),
            pl.BlockSpec((None, t, hps * 2 * LANES), lambda bi, h, qi: (bi, 0, h)),
            pl.BlockSpec((None, t, LANES), lambda bi, h, qi: (bi, 0, kr_blk)),
            pl.BlockSpec((tq, LANES), row_q), pl.BlockSpec((tq, LANES), row_q), pl.BlockSpec((tq, LANES), row_q),
            pl.BlockSpec((t, LANES), full), pl.BlockSpec((t, LANES), full), pl.BlockSpec((t, LANES), full),
            pl.BlockSpec((1, LANES), full), pl.BlockSpec((1, LANES), full),
            pl.BlockSpec((1, LANES), full), pl.BlockSpec((1, LANES), full),
        ],
        out_specs=pl.BlockSpec((None, tq, hps * MLA_V), lambda bi, h, qi: (bi, qi, h)),
        out_shape=jax.ShapeDtypeStruct((b, n_lat, MLA_HEADS * MLA_V), BF16),
        scratch_shapes=[pltpu.VMEM((hps, 2 * LANES, t), BF16), pltpu.VMEM((hps, t, 2 * LANES), BF16)],
        compiler_params=_cparams(("arbitrary", "arbitrary", "arbitrary")),
        name="mla_attention",
    )(q, kv, p1, c[n_ctx:], sa[n_ctx:], sb[n_ctx:], c, sa, sb, pad(qn_gain), pad(qr_gain), pad(kn_gain), pad(kr_gain))


PAIRS = [(i, j) for i in range(EXPERTS_PER_GROUP) for j in range(i + 1, EXPERTS_PER_GROUP)]
N_CLASSES = N_GROUPS * len(PAIRS)
CLASS_LO = np.array([EXPERTS_PER_GROUP * g + i for g in range(N_GROUPS) for (i, j) in PAIRS], np.int32)
CLASS_HI = np.array([EXPERTS_PER_GROUP * g + j for g in range(N_GROUPS) for (i, j) in PAIRS], np.int32)
EXPERT_TILE = 256
ROW_SLABS = 8
DMA_UNROLL = 8
ROUTE_BATCH = 4
ROW_SLOTS = 4


def _route_kernel(*refs, n_in, n_x, n_ctx_tiles, row_tile0):
    a_refs = refs[:n_in]
    w_refs = refs[n_in:2 * n_in]
    x_refs = refs[2 * n_in:2 * n_in + n_x]
    m_ref, g_ref, rw_ref, rb_ref, xmid_ref, hf_ref, rows_ref, info_ref, count_ref, cnt_ref = refs[2 * n_in + n_x:]
    w1, w2, _ = _split3(rw_ref[...])
    is_ctx = pl.program_id(1) + row_tile0 < n_ctx_tiles
    logits = []
    for e in range(ROUTE_BATCH):
        acc = _dot(a_refs[0][e], w_refs[0][...])
        for a_ref, w_ref in zip(a_refs[1:], w_refs[1:]):
            acc = acc + _dot(a_ref[e], w_ref[...])
        x_in = x_refs[0][e] if n_x == 1 else jnp.where(is_ctx, x_refs[0][e], x_refs[1][e])
        x_mid = x_in + m_ref[e, G_M:G_M + 1, :] * acc
        xmid_ref[e] = x_mid
        h = _rms(x_mid) * g_ref[...]
        h = h * (1.0 + m_ref[e, SC_F:SC_F + 1, :]) + m_ref[e, SH_F:SH_F + 1, :]
        hf_ref[e] = h.astype(hf_ref.dtype)
        rows_ref[e] = _rows_to_slabs(h)
        h1, h2, _ = _split3(h)
        logits.append(_dot_nt(w1, h1) + (_dot_nt(w1, h2) + _dot_nt(w2, h1)))
    biased = _sigmoid(jnp.concatenate(logits, axis=1)) + rb_ref[...]
    row = [biased[e:e + 1, :] for e in range(N_EXPERTS)]
    gscore = []
    for g in range(N_GROUPS):
        m = row[4 * g:4 * g + 4]
        gscore.append(functools.reduce(jnp.maximum, [m[i] + m[j] for (i, j) in PAIRS]))
    hits = []
    for g in range(N_GROUPS):
        best = None
        for g2 in range(N_GROUPS):
            if g2 == g:
                continue
            wins = (gscore[g] > gscore[g2]) if g2 < g else (gscore[g] >= gscore[g2])
            best = wins if best is None else jnp.logical_and(best, wins)
        chosen = []
        for i in range(EXPERTS_PER_GROUP):
            rank = None
            for j in range(EXPERTS_PER_GROUP):
                if j == i:
                    continue
                mi, mj = row[4 * g + i], row[4 * g + j]
                ahead = ((mj >= mi) if j < i else (mj > mi)).astype(jnp.int32)
                rank = ahead if rank is None else rank + ahead
            chosen.append(rank < 2)
        for (i, j) in PAIRS:
            hits.append(jnp.where(best & chosen[i] & chosen[j], 1.0, 0.0))
    onehot = jnp.concatenate(hits, axis=0)
    nt = onehot.shape[1]
    upper = (lax.broadcasted_iota(jnp.int32, (nt, nt), 0) <= lax.broadcasted_iota(jnp.int32, (nt, nt), 1))
    prefix = _dot(onehot.astype(BF16), upper.astype(BF16))

    @pl.when((pl.program_id(0) == 0) & (pl.program_id(1) == 0))
    def _():
        cnt_ref[...] = jnp.zeros_like(cnt_ref)

    seen = cnt_ref[...]
    cls_id = lax.broadcasted_iota(jnp.int32, onehot.shape, 0).astype(F32)
    cls = jnp.sum(onehot * cls_id, axis=0, keepdims=True)
    rank = jnp.sum(onehot * (seen[:, 0:1] + prefix - 1.0), axis=0, keepdims=True)
    info = jnp.concatenate([cls, rank, jnp.zeros((6, nt), F32)], axis=0).astype(jnp.int32)
    tm = nt // ROUTE_BATCH
    for e in range(ROUTE_BATCH):
        info_ref[e] = info[:, e * tm:(e + 1) * tm]
    seen = seen + jnp.sum(onehot, axis=1, keepdims=True)
    cnt_ref[...] = seen
    count_ref[...] = seen


def route(acts, weights, xs, mods, gain, router_w, router_bias, *, tm, n_ctx_tiles, row_tile0):
    b, r, _ = acts[0].shape
    nt = r // tm
    rb = ROUTE_BATCH
    assert b % rb == 0
    parts = xs if isinstance(xs, tuple) else (xs,)
    d = parts[0].shape[-1]
    if len(parts) == 1:
        x_specs = [pl.BlockSpec((rb, tm, d), lambda bi, ti: (bi, ti + row_tile0, 0))]
    else:
        x_specs = [pl.BlockSpec((rb, tm, d), lambda bi, ti: (bi, jnp.minimum(ti + row_tile0, n_ctx_tiles - 1), 0)),
                   pl.BlockSpec((rb, tm, d), lambda bi, ti: (bi, jnp.maximum(ti + row_tile0 - n_ctx_tiles, 0), 0))]
    row = lambda bi, ti: (bi, ti, 0)
    const = lambda bi, ti: (0, 0)
    in_specs = [pl.BlockSpec((rb, tm, a.shape[-1]), row) for a in acts]
    in_specs += [pl.BlockSpec(w.shape, const) for w in weights]
    in_specs += x_specs + [
        pl.BlockSpec((rb, None, 6, d), lambda bi, ti: (bi, ((ti + row_tile0) >= n_ctx_tiles).astype(jnp.int32), 0, 0)),
        pl.BlockSpec((1, d), const), pl.BlockSpec((N_EXPERTS, d), const), pl.BlockSpec((N_EXPERTS, 1), const)]
    x_mid, hf, rows, info, counts = pl.pallas_call(
        functools.partial(_route_kernel, n_in=len(acts), n_x=len(parts), n_ctx_tiles=n_ctx_tiles,
                          row_tile0=row_tile0),
        grid=(b // rb, nt),
        in_specs=in_specs,
        out_specs=[
            pl.BlockSpec((rb, tm, d), row),
            pl.BlockSpec((rb, tm, d), row),
            pl.BlockSpec((rb, tm, ROW_SLABS, LANES), lambda bi, ti: (bi, ti, 0, 0)),
            pl.BlockSpec((rb, None, 8, tm), lambda bi, ti: (bi, ti, 0, 0)),
            pl.BlockSpec((N_CLASSES, LANES), const),
        ],
        out_shape=[jax.ShapeDtypeStruct((b, r, d), F32),
                   jax.ShapeDtypeStruct((b, r, d), BF16),
                   jax.ShapeDtypeStruct((b, r, ROW_SLABS, LANES), F32),
                   jax.ShapeDtypeStruct((b, nt, 8, tm), jnp.int32),
                   jax.ShapeDtypeStruct((N_CLASSES, LANES), F32)],
        scratch_shapes=[pltpu.VMEM((N_CLASSES, LANES), F32)],
        compiler_params=_cparams(("arbitrary", "arbitrary")),
        name="route",
    )(*acts, *weights, *parts, mods, gain.reshape(1, d).astype(F32), router_w.T.astype(F32),
      router_bias.reshape(N_EXPERTS, 1).astype(F32))
    return x_mid, hf, rows.reshape(b * r, ROW_SLABS, LANES), info, counts


def _transpose8(parts):
    sub = lax.broadcasted_iota(jnp.int32, parts[0].shape, 1)
    for s in (4, 2, 1):
        keep = (sub & s) == 0
        new = list(parts)
        for i in range(8):
            if i & s:
                continue
            a, b = parts[i], parts[i | s]
            new[i] = jnp.where(keep, a, pltpu.roll(b, s, 1))
            new[i | s] = jnp.where(keep, pltpu.roll(a, 8 - s, 1), b)
        parts = new
    return parts


def _slabs_to_rows(slabs):
    g = slabs.shape[0] // 8
    x4 = slabs.reshape(g, 8, ROW_SLABS, LANES)
    parts = _transpose8([x4[:, t] for t in range(8)])
    return jnp.concatenate([p.reshape(g * 8, LANES) for p in parts], axis=-1)


def _rows_to_slabs(x):
    g = x.shape[0] // 8
    parts = _transpose8([x[:, k * LANES:(k + 1) * LANES].reshape(g, 8, LANES) for k in range(ROW_SLABS)])
    return jnp.stack(parts, axis=1).reshape(g * 8, ROW_SLABS, LANES)


def _invert_kernel(dest_ref, src_ref):
    def clear(i, carry):
        for u in range(2 * DMA_UNROLL):
            src_ref[i * 2 * DMA_UNROLL + u] = 0
        return carry

    def put(i, carry):
        for u in range(2 * DMA_UNROLL):
            t = i * 2 * DMA_UNROLL + u
            src_ref[dest_ref[t]] = t
        return carry

    lax.fori_loop(0, src_ref.shape[0] // (2 * DMA_UNROLL), clear, 0)
    lax.fori_loop(0, dest_ref.shape[0] // (2 * DMA_UNROLL), put, 0)


def invert_permutation(dest, n_rows_out):
    smem = pl.BlockSpec(memory_space=pltpu.SMEM)
    return pl.pallas_call(
        _invert_kernel,
        in_specs=[smem],
        out_specs=smem,
        out_shape=jax.ShapeDtypeStruct((n_rows_out,), jnp.int32),
        name="moe_invert",
    )(dest)


def _expert_kernel(te_ref, tv_ref, nx_ref, src_ref, rows_hbm, rwl_ref, rwh_ref, wg_hbm, wu_hbm, wd_hbm, y_ref,
                   xbuf_ref, sem_ref, stage_g, stage_u, stage_d, cache_g, cache_u, cache_d, wsem_ref, *, layer):
    i = pl.program_id(0)
    n = pl.num_programs(0)
    tg = y_ref.shape[0]
    slot = i % ROW_SLOTS
    valid = tv_ref[i] != 0
    prev = jnp.maximum(i - 1, 0)

    def fetch(tile, s):
        base = tile * tg
        for t in range(tg):
            pltpu.async_copy(rows_hbm.at[src_ref[base + t]], xbuf_ref.at[s, t], sem_ref.at[s], priority=t % 2)

    def wait_rows(s):
        pltpu.make_async_copy(rows_hbm.at[pl.ds(0, tg)], xbuf_ref.at[s], sem_ref.at[s]).wait()

    def weight_copies(s, e):
        return [pltpu.make_async_copy(w_hbm.at[layer, e], stage.at[s], wsem_ref.at[s])
                for w_hbm, stage in ((wg_hbm, stage_g), (wu_hbm, stage_u), (wd_hbm, stage_d))]

    @pl.when(i == 0)
    def _():
        for k in range(ROW_SLOTS - 1):
            fetch(jnp.minimum(k, n - 1), k)
        for s in range(2):
            for copy in weight_copies(s, te_ref[s * n]):
                copy.start()

    for s in range(2):
        expert = te_ref[s * n + i]

        @pl.when(valid & ((i == 0) | (expert != te_ref[s * n + prev])))
        def _():
            for copy in weight_copies(s, expert):
                copy.wait()
            cache_g[s] = stage_g[s].astype(BF16)
            cache_u[s] = stage_u[s].astype(BF16)
            cache_d[s] = stage_d[s].astype(BF16)
            upcoming = nx_ref[s * n + i]

            @pl.when(upcoming >= 0)
            def _():
                for copy in weight_copies(s, upcoming):
                    copy.start()

    @pl.when(valid)
    def _():
        wait_rows(slot)
        x = _slabs_to_rows(xbuf_ref[slot])
        fetch(jnp.minimum(i + ROW_SLOTS - 1, n - 1), (i + ROW_SLOTS - 1) % ROW_SLOTS)
        xb = x.astype(BF16)
        s_lo = _sigmoid(jnp.sum(x * rwl_ref[...], axis=-1, keepdims=True))
        s_hi = _sigmoid(jnp.sum(x * rwh_ref[...], axis=-1, keepdims=True))
        inv = 1.0 / (s_lo + s_hi)

        def ffn(s):
            a = _dot(xb, cache_g[s])
            u = _dot(xb, cache_u[s])
            return _dot((a * _sigmoid(a) * u).astype(BF16), cache_d[s])

        y_ref[...] = _rows_to_slabs((s_lo * inv) * ffn(0) + (s_hi * inv) * ffn(1))

    @pl.when(jnp.logical_not(valid))
    def _():
        y_ref[...] = jnp.zeros_like(y_ref)

    @pl.when((valid & (i == n - 1)) | (jnp.logical_not(valid) & (tv_ref[prev] != 0)))
    def _():
        first = jnp.where(valid, i + 1, i)
        for k in range(ROW_SLOTS - 1):
            wait_rows((first + k) % ROW_SLOTS)


def experts(tile_expert, tile_valid, next_expert, src, rows, router_w_t, wg, wu, wd, layer):
    tg = EXPERT_TILE
    n_tiles = src.shape[0] // tg
    d, ff = wg.shape[2:]
    lo = lambda i, te, tv, nx, src: (te[i], 0, 0)
    hi = lambda i, te, tv, nx, src: (te[n_tiles + i], 0, 0)
    hbm = pl.BlockSpec(memory_space=pl.ANY)
    return pl.pallas_call(
        functools.partial(_expert_kernel, layer=layer),
        grid_spec=pltpu.PrefetchScalarGridSpec(
            num_scalar_prefetch=4,
            grid=(n_tiles,),
            in_specs=[hbm, pl.BlockSpec((None, 1, d), lo), pl.BlockSpec((None, 1, d), hi), hbm, hbm, hbm],
            out_specs=pl.BlockSpec((tg, ROW_SLABS, LANES), lambda i, te, tv, nx, src: (i, 0, 0)),
            scratch_shapes=[pltpu.VMEM((ROW_SLOTS, tg, ROW_SLABS, LANES), F32), pltpu.SemaphoreType.DMA((ROW_SLOTS,)),
                            pltpu.VMEM((2, d, ff), F32), pltpu.VMEM((2, d, ff), F32), pltpu.VMEM((2, ff, d), F32),
                            pltpu.VMEM((2, d, ff), BF16), pltpu.VMEM((2, d, ff), BF16), pltpu.VMEM((2, ff, d), BF16),
                            pltpu.SemaphoreType.DMA((2,))],
        ),
        out_shape=jax.ShapeDtypeStruct((n_tiles * tg, ROW_SLABS, LANES), F32),
        compiler_params=_cparams(("arbitrary",)),
        name="moe_experts",
    )(tile_expert, tile_valid, next_expert, src, rows, router_w_t, router_w_t, wg, wu, wd)


def _combine_kernel(dest_ref, hf_ref, x_ref, m_ref, wg_ref, wu_ref, wd_ref, y_hbm, o_ref, ybuf_ref, sem_ref):
    i = pl.program_id(0)
    n = pl.num_programs(0)
    tm = hf_ref.shape[0]
    slot = i % 2

    def fetch(tile, s):
        base = tile * tm
        for t in range(tm):
            pltpu.async_copy(y_hbm.at[dest_ref[base + t]], ybuf_ref.at[s, t], sem_ref.at[s], priority=t % 2)

    def wait_rows(s):
        pltpu.make_async_copy(y_hbm.at[pl.ds(0, tm)], ybuf_ref.at[s], sem_ref.at[s]).wait()

    @pl.when(i == 0)
    def _():
        fetch(0, 0)

    fetch(jnp.minimum(i + 1, n - 1), 1 - slot)
    h = hf_ref[...]
    a = _dot(h, wg_ref[...])
    u = _dot(h, wu_ref[...])
    shared = _dot((a * _sigmoid(a) * u).astype(BF16), wd_ref[...])
    wait_rows(slot)
    o_ref[...] = x_ref[...] + m_ref[G_F:G_F + 1, :] * (shared + _slabs_to_rows(ybuf_ref[slot]))

    @pl.when(i == n - 1)
    def _():
        wait_rows(1 - slot)


def combine(dest, hf, xs, mods, wg, wu, wd, y_sorted, *, tm, n_ctx_tiles):
    b, r, d = hf.shape
    nt = r // tm
    ff = wg.shape[1]
    row = lambda i, dest: (i // nt, i % nt, 0)
    const = lambda i, dest: (0, 0)
    return pl.pallas_call(
        _combine_kernel,
        grid_spec=pltpu.PrefetchScalarGridSpec(
            num_scalar_prefetch=1,
            grid=(b * nt,),
            in_specs=[
                pl.BlockSpec((None, tm, d), row),
                pl.BlockSpec((None, tm, d), row),
                pl.BlockSpec((None, None, 6, d),
                             lambda i, dest: (i // nt, ((i % nt) >= n_ctx_tiles).astype(jnp.int32), 0, 0)),
                pl.BlockSpec((d, ff), const), pl.BlockSpec((d, ff), const), pl.BlockSpec((ff, d), const),
                pl.BlockSpec(memory_space=pl.ANY),
            ],
            out_specs=pl.BlockSpec((None, tm, d), row),
            scratch_shapes=[pltpu.VMEM((2, tm, ROW_SLABS, LANES), F32), pltpu.SemaphoreType.DMA((2,))],
        ),
        out_shape=jax.ShapeDtypeStruct((b, r, d), F32),
        compiler_params=_cparams(("arbitrary",)),
        name="moe_combine",
    )(dest, hf, xs, mods, wg, wu, wd, y_sorted)


def moe_block(acts, weights, xs, mods, gain, router_w, router_bias, ew_gate, ew_up, ew_down, layer, sw_gate, sw_up,
              sw_down, *, tm, n_ctx_tiles, row_tile0):
    xs_mid, hf, rows, info, counts = route(acts, weights, xs, mods, gain, router_w, router_bias, tm=tm,
                                           n_ctx_tiles=n_ctx_tiles, row_tile0=row_tile0)
    b, r, d = xs_mid.shape
    n = b * r
    tg = EXPERT_TILE
    cls = info[:, :, 0, :].reshape(n)
    rank = info[:, :, 1, :].reshape(n)
    padded = ((counts[:, 0].astype(jnp.int32) + tg - 1) // tg) * tg
    ends = jnp.cumsum(padded)
    dest = (ends - padded)[cls] + rank
    n_tiles = n // tg + N_CLASSES
    tile_start = jnp.arange(n_tiles, dtype=jnp.int32) * tg
    tile_valid = tile_start < ends[-1]
    last_start = jnp.maximum(ends[-1] - tg, 0)
    start = jnp.where(tile_valid, tile_start, last_start)
    tile_cls = jnp.sum((ends[None, :] <= start[:, None]).astype(jnp.int32), axis=1)
    tile_cls = jnp.minimum(tile_cls, N_CLASSES - 1)
    slot_expert = jnp.stack([jnp.asarray(CLASS_LO)[tile_cls], jnp.asarray(CLASS_HI)[tile_cls]])
    tile_expert = slot_expert.reshape(-1)
    later = (tile_start[None, :] > tile_start[:, None]) & tile_valid[None, :]
    differs = slot_expert[:, None, :] != slot_expert[:, :, None]
    first = jnp.min(jnp.where(later[None] & differs, jnp.arange(n_tiles)[None, None, :], n_tiles), axis=-1)
    next_expert = jnp.where(first < n_tiles, jnp.take_along_axis(slot_expert, jnp.minimum(first, n_tiles - 1), axis=1),
                            -1).reshape(-1).astype(jnp.int32)
    src = invert_permutation(dest, n_tiles * tg)
    y_sorted = experts(tile_expert, tile_valid.astype(jnp.int32), next_expert, src, rows,
                       router_w.T.astype(F32).reshape(N_EXPERTS, 1, d), ew_gate, ew_up, ew_down, layer)
    return combine(dest, hf, xs_mid, mods, sw_gate.astype(BF16), sw_up.astype(BF16), sw_down.astype(BF16), y_sorted,
                   tm=tm, n_ctx_tiles=max(n_ctx_tiles - row_tile0, 0))


def kernel(x, c, ctx, c_ctx, mod_w, mod_b, norm_mix, norm_ffn, even_w_in, even_w_out, diff_q_gain, diff_k_gain, diff_lambda, diff_subln, hgrn_lb_logits, hgrn_out_gain, odd_w_in, mla_q_a_gain, mla_kv_a_gain, mla_w_uq, mla_w_ukv, mla_q_nope_gain, mla_q_rope_gain, mla_k_nope_gain, mla_k_rope_gain, odd_w_out, router_w, router_bias, expert_w_gate, expert_w_up, expert_w_down, shared_w_gate, shared_w_up, shared_w_down):
    b, n_lat, d = x.shape
    n_ctx = ctx.shape[1]
    depth = mod_w.shape[0]
    tm = 256 if n_ctx % 256 == 0 else 128
    n_ctx_tiles = n_ctx // tm
    rope = rope_lane_tables(n_ctx, n_lat)

    mod_rows = 16
    c_rows = jnp.concatenate([c, c_ctx[None, :], jnp.zeros((mod_rows - b - 1, d), F32)], axis=0)
    mod_all = modulation(c_rows, mod_w, mod_b).reshape(depth, mod_rows, 6, d)
    lb_all = jnp.cumsum(jax.nn.softmax(hgrn_lb_logits.astype(F32), axis=0), axis=0)

    xs = (ctx, x)
    for layer in range(depth):
        last = layer == depth - 1
        j = layer // 2
        mods = jnp.stack([jnp.broadcast_to(mod_all[layer, b], (b, 6, d)), mod_all[layer, :b]], axis=1)
        row_tile0 = n_ctx_tiles if last else 0
        if layer % 2 == 0:
            lam_init = 0.8 - 0.6 * math.exp(-0.3 * layer)
            p = norm_matmul(xs, mods, norm_mix[layer], even_w_in[j].astype(BF16), tm=tm, n_ctx_tiles=n_ctx_tiles)
            oa = diff_attention(p, diff_lambda[j], diff_q_gain[j], diff_k_gain[j], diff_subln[j], rope,
                                tq=tm, n_ctx=n_ctx, lam_init=lam_init, q_blk0=0, k_blk0=4, v_blk0=8)
            ob = hgrn_bidirectional(p, lb_all[j, 0], lb_all[j, 1], hgrn_out_gain[j], n_ctx=n_ctx, blk0=12)
            if last:
                oa, ob = oa[:, n_ctx:], ob[:, n_ctx:]
            w_out = even_w_out[j].astype(BF16)
            half = oa.shape[-1]
            acts, weights = [oa, ob], [w_out[:half], w_out[half:]]
        else:
            w_in = odd_w_in[j]
            zpad = jnp.zeros((d, LANES - MLA_ROPE), F32)
            w_in_r = jnp.concatenate([w_in[:, MLA_Q_LORA:MLA_Q_LORA + MLA_KV_LORA], w_in[:, MLA_Q_LORA + MLA_KV_LORA:],
                                      zpad, w_in[:, :MLA_Q_LORA]], axis=1).astype(BF16)
            w_uq = mla_w_uq[j].reshape(MLA_Q_LORA, MLA_HEADS, MLA_NOPE + MLA_ROPE)
            w_uq = jnp.concatenate([w_uq, jnp.zeros((MLA_Q_LORA, MLA_HEADS, LANES - MLA_ROPE), F32)], axis=-1)
            w_uq = w_uq.reshape(MLA_Q_LORA, MLA_HEADS * 2 * LANES).astype(BF16)
            if not last:
                raise NotImplementedError("context queries for a non-final latent-attention layer")
            kr, q, kv = mla_projections(xs, mods, norm_mix[layer], w_in_r, mla_q_a_gain[j], mla_kv_a_gain[j], w_uq,
                                        mla_w_ukv[j].astype(BF16), tm=tm, n_ctx_tiles=n_ctx_tiles)
            o = mla_attention(q, kv, kr, 0, rope, mla_q_nope_gain[j], mla_q_rope_gain[j], mla_k_nope_gain[j],
                              mla_k_rope_gain[j], tq=2 * tm, n_ctx=n_ctx)
            acts, weights = [o], [odd_w_out[j].astype(BF16)]
        xs = moe_block(acts, weights, xs, mods, norm_ffn[layer], router_w, router_bias,
                       expert_w_gate, expert_w_up, expert_w_down, layer,
                       shared_w_gate[layer], shared_w_up[layer], shared_w_down[layer],
                       tm=tm, n_ctx_tiles=n_ctx_tiles, row_tile0=row_tile0)
    return xs if xs.shape[1] == n_lat else xs[:, n_ctx:]
```

```python
import functools
import math

import numpy as np
import jax
import jax.numpy as jnp
from jax import lax
from jax.experimental import pallas as pl
from jax.experimental.pallas import tpu as pltpu

F32 = jnp.float32
BF16 = jnp.bfloat16

LANES = 128
VMEM_LIMIT = 56 * 1024 * 1024

GRID_W = 64
DIFF_HEADS = 4
DIFF_HEAD_DIM = 64
HGRN_HEADS = 4
HGRN_K_DIM = 128
MLA_HEADS = 8
MLA_NOPE = 128
MLA_ROPE = 64
MLA_V = 128
MLA_Q_LORA = 384
MLA_KV_LORA = 256
N_EXPERTS = 16
N_GROUPS = 4
EXPERTS_PER_GROUP = 4
ROPE_BASE = 10000.0
EPS = 1e-6
LOG2E = 1.4426950408889634
GLA_CHUNK = 64
ATTN_KEY_CHUNK = 1152
MLA_HEADS_PER_STEP = 2
DIFF_HEADS_PER_STEP = 2
HGRN_HEADS_PER_STEP = 2

SH_M, SC_M, G_M, SH_F, SC_F, G_F = range(6)


def _sigmoid(x):
    return 0.5 * jnp.tanh(0.5 * x) + 0.5


def _dot(a, b):
    return jnp.dot(a, b, preferred_element_type=F32)


def _dot_nt(a, b):
    return lax.dot_general(a, b, (((1,), (1,)), ((), ())), preferred_element_type=F32)


def _dot_tn(a, b):
    return lax.dot_general(a, b, (((0,), (0,)), ((), ())), preferred_element_type=F32)


def _split3(x):
    hi = x.astype(BF16)
    r = x - hi.astype(F32)
    mid = r.astype(BF16)
    lo = (r - mid.astype(F32)).astype(BF16)
    return hi, mid, lo


def _rms(x, width=None):
    n = x.shape[-1] if width is None else width
    return x * lax.rsqrt(jnp.sum(x * x, axis=-1, keepdims=True) * (1.0 / n) + EPS)


def _cparams(sem):
    return pltpu.CompilerParams(dimension_semantics=sem, vmem_limit_bytes=VMEM_LIMIT)


def _mod_kernel(c_ref, w_ref, b_ref, o_ref):
    c = c_ref[...]
    s = c * _sigmoid(c)
    o_ref[...] = _dot(s.astype(BF16), w_ref[...].astype(BF16)) + b_ref[...]


def modulation(c_rows, mod_w, mod_b):
    n_layers, d, n = mod_w.shape
    rows = c_rows.shape[0]
    tn = 1536
    return pl.pallas_call(
        _mod_kernel,
        grid=(n_layers, n // tn),
        in_specs=[
            pl.BlockSpec((rows, d), lambda l, j: (0, 0)),
            pl.BlockSpec((None, d, tn), lambda l, j: (l, 0, j)),
            pl.BlockSpec((None, 1, tn), lambda l, j: (l, 0, j)),
        ],
        out_specs=pl.BlockSpec((None, rows, tn), lambda l, j: (l, 0, j)),
        out_shape=jax.ShapeDtypeStruct((n_layers, rows, n), F32),
        compiler_params=_cparams(("arbitrary", "arbitrary")),
        name="modulation",
    )(c_rows, mod_w, mod_b.reshape(n_layers, 1, n))


def _stream_specs(xs, tm, n_ctx_tiles, row_tile0=0):
    parts = xs if isinstance(xs, tuple) else (xs,)
    d = parts[0].shape[-1]
    if len(parts) == 1:
        return [pl.BlockSpec((None, tm, d), lambda bi, ti: (bi, ti + row_tile0, 0))], list(parts)
    return [pl.BlockSpec((None, tm, d), lambda bi, ti: (bi, jnp.minimum(ti + row_tile0, n_ctx_tiles - 1), 0)),
            pl.BlockSpec((None, tm, d), lambda bi, ti: (bi, jnp.maximum(ti + row_tile0 - n_ctx_tiles, 0), 0))], list(parts)


def _stream_tile(x_refs, n_ctx_tiles, row_tile0=0):
    if len(x_refs) == 1:
        return x_refs[0][...]
    return jnp.where(pl.program_id(1) + row_tile0 < n_ctx_tiles, x_refs[0][...], x_refs[1][...])


def _norm_matmul_kernel(*refs, n_x, n_ctx_tiles):
    m_ref, g_ref, w_ref, o_ref = refs[n_x:]
    h = _rms(_stream_tile(refs[:n_x], n_ctx_tiles)) * g_ref[...]
    h = h * (1.0 + m_ref[SC_M:SC_M + 1, :]) + m_ref[SH_M:SH_M + 1, :]
    o_ref[...] = _dot(h.astype(BF16), w_ref[...]).astype(o_ref.dtype)


def norm_matmul(xs, mods, gain, w, *, tm, n_ctx_tiles):
    x_specs, x_args = _stream_specs(xs, tm, n_ctx_tiles)
    b = x_args[0].shape[0]
    t = sum(a.shape[1] for a in x_args)
    k, n = w.shape
    return pl.pallas_call(
        functools.partial(_norm_matmul_kernel, n_x=len(x_args), n_ctx_tiles=n_ctx_tiles),
        grid=(b, t // tm),
        in_specs=x_specs + [
            pl.BlockSpec((None, None, 6, k), lambda bi, ti: (bi, (ti >= n_ctx_tiles).astype(jnp.int32), 0, 0)),
            pl.BlockSpec((1, k), lambda bi, ti: (0, 0)), pl.BlockSpec((k, n), lambda bi, ti: (0, 0))],
        out_specs=pl.BlockSpec((None, tm, n), lambda bi, ti: (bi, ti, 0)),
        out_shape=jax.ShapeDtypeStruct((b, t, n), BF16),
        compiler_params=_cparams(("arbitrary", "arbitrary")),
        name="norm_matmul",
    )(*x_args, mods, gain.reshape(1, k).astype(F32), w)


def _mla_proj_kernel(x_ref, m_ref, g_ref, win_ref, qg_ref, kvg_ref, wuq_ref, wukv_ref, kr_ref, q_ref, kv_ref,
                     *, n_ctx_tiles):
    h = _rms(x_ref[...]) * g_ref[...]
    h = h * (1.0 + m_ref[SC_M:SC_M + 1, :]) + m_ref[SH_M:SH_M + 1, :]
    p = _dot(h.astype(BF16), win_ref[...])
    kr_ref[...] = p[:, MLA_KV_LORA:MLA_KV_LORA + LANES].astype(kr_ref.dtype)
    ckv = _rms(p[:, 0:MLA_KV_LORA]) * kvg_ref[...]
    kv_ref[...] = _dot(ckv.astype(BF16), wukv_ref[...]).astype(kv_ref.dtype)

    @pl.when(pl.program_id(1) >= n_ctx_tiles)
    def _():
        cq = _rms(p[:, MLA_KV_LORA + LANES:]) * qg_ref[...]
        q_ref[...] = _dot(cq.astype(BF16), wuq_ref[...]).astype(q_ref.dtype)

    @pl.when(pl.program_id(1) < n_ctx_tiles)
    def _():
        q_ref[...] = jnp.zeros_like(q_ref)


def mla_projections(xs, mods, gain, w_in_r, q_a_gain, kv_a_gain, w_uq, w_ukv, *, tm, n_ctx_tiles):
    b, t, d = xs.shape
    nt = t // tm
    n_lat = t - n_ctx_tiles * tm
    const = lambda a: pl.BlockSpec(a.shape, lambda bi, ti: (0, 0))
    row = lambda bi, ti: (bi, ti, 0)
    g2 = lambda g: g.reshape(1, -1).astype(F32)
    args = [g2(gain), w_in_r, g2(q_a_gain), g2(kv_a_gain), w_uq, w_ukv]
    return pl.pallas_call(
        functools.partial(_mla_proj_kernel, n_ctx_tiles=n_ctx_tiles),
        grid=(b, nt),
        in_specs=[pl.BlockSpec((None, tm, d), row),
                  pl.BlockSpec((None, None, 6, d), lambda bi, ti: (bi, (ti >= n_ctx_tiles).astype(jnp.int32), 0, 0))]
                 + [const(a) for a in args],
        out_specs=[pl.BlockSpec((None, tm, LANES), row),
                   pl.BlockSpec((None, tm, w_uq.shape[1]), lambda bi, ti: (bi, jnp.maximum(ti - n_ctx_tiles, 0), 0)),
                   pl.BlockSpec((None, tm, w_ukv.shape[1]), row)],
        out_shape=[jax.ShapeDtypeStruct((b, t, LANES), BF16),
                   jax.ShapeDtypeStruct((b, n_lat, w_uq.shape[1]), BF16),
                   jax.ShapeDtypeStruct((b, t, w_ukv.shape[1]), BF16)],
        compiler_params=_cparams(("arbitrary", "arbitrary")),
        name="mla_projections",
    )(xs, mods, *args)


def rope_lane_tables(n_ctx, n_lat):
    rows = n_lat // GRID_W
    row = np.repeat(np.arange(rows), GRID_W).astype(np.float32)
    col = np.tile(np.arange(GRID_W), rows).astype(np.float32)
    axis_dim = DIFF_HEAD_DIM // 2
    inv_freq = jnp.asarray(ROPE_BASE, F32) ** (-jnp.arange(0, axis_dim, 2, dtype=F32) / axis_dim)
    ang_r = jnp.asarray(row)[:, None] * inv_freq
    ang_c = jnp.asarray(col)[:, None] * inv_freq
    lane = np.arange(LANES)
    freq_idx = lane % 16
    use_col = (lane % 64) >= 32
    first = (lane % 32) < 16
    ang = jnp.where(use_col[None, :], ang_c[:, freq_idx], ang_r[:, freq_idx])
    cos, sin = jnp.cos(ang), jnp.sin(ang)
    c = jnp.concatenate([jnp.ones((n_ctx, LANES), F32), cos], axis=0)
    sa = jnp.concatenate([jnp.zeros((n_ctx, LANES), F32), jnp.where(first[None, :], -sin, 0.0)], axis=0)
    sb = jnp.concatenate([jnp.zeros((n_ctx, LANES), F32), jnp.where(first[None, :], 0.0, sin)], axis=0)
    return c, sa, sb


def _rope(x, c, sa, sb):
    return x * c + pltpu.roll(x, LANES - 16, 1) * sa + pltpu.roll(x, 16, 1) * sb


def _block_ones(width):
    r = lax.broadcasted_iota(jnp.int32, (LANES, LANES), 0) // width
    c = lax.broadcasted_iota(jnp.int32, (LANES, LANES), 1) // width
    return (r == c).astype(BF16)


def _block_rms(x, ones, width):
    xx = x * x
    hi = xx.astype(BF16)
    lo = (xx - hi.astype(F32)).astype(BF16)
    ms = (_dot(hi, ones) + _dot(lo, ones)) * (1.0 / width)
    return x * lax.rsqrt(ms + EPS)


def _attend_streams(qs, kt_refs, vo_refs, nk):
    n = len(qs)
    chunk = ATTN_KEY_CHUNK if nk % ATTN_KEY_CHUNK == 0 else nk
    nc = nk // chunk
    s = [[None] * nc for _ in range(n)]
    p = [[None] * nc for _ in range(n)]
    m, ol = [None] * n, [None] * n

    def logits(i, c):
        s[i][c] = _dot(qs[i], kt_refs[i][:, c * chunk:(c + 1) * chunk])
        mc = jnp.max(s[i][c], axis=-1, keepdims=True)
        m[i] = mc if c == 0 else jnp.maximum(m[i], mc)

    def exps(i, c):
        p[i][c] = jnp.exp2(s[i][c] - m[i]).astype(BF16)

    def values(i, c):
        oc = _dot(p[i][c], vo_refs[i][c * chunk:(c + 1) * chunk, :])
        ol[i] = oc if c == 0 else ol[i] + oc

    for step in range(n + 2):
        for c in range(nc):
            if step < n:
                logits(step, c)
            if 0 <= step - 1 < n:
                exps(step - 1, c)
            if 0 <= step - 2 < n:
                values(step - 2, c)
    return [(x[:, 0:LANES], x[:, LANES:LANES + 1]) for x in ol]


def _diff_attn_kernel(lam_ref, q_ref, k_ref, v_ref, cq_ref, saq_ref, sbq_ref, ck_ref, sak_ref, sbk_ref,
                      qg_ref, kg_ref, sub_ref, o_ref, kt_ref, vo_ref, *, n_ctx, n_ctx_tiles, lam_init):
    qi = pl.program_id(2)
    ones = _block_ones(DIFF_HEAD_DIM)
    heads = range(DIFF_HEADS_PER_STEP)

    @pl.when(qi == 0)
    def _():
        for h in heads:
            k = _block_rms(k_ref[:, h * LANES:(h + 1) * LANES].astype(F32), ones, DIFF_HEAD_DIM) * kg_ref[...]
            kt_ref[h] = _rope(k, ck_ref[...], sak_ref[...], sbk_ref[...]).T.astype(BF16)
            vo_ref[h, :, 0:LANES] = v_ref[:, h * LANES:(h + 1) * LANES]
            vo_ref[h, :, LANES:2 * LANES] = jnp.ones((v_ref.shape[0], LANES), BF16)

    qs = []
    for h in heads:
        q = _block_rms(q_ref[:, h * LANES:(h + 1) * LANES].astype(F32), ones, DIFF_HEAD_DIM) * qg_ref[...]
        q = _rope(q, cq_ref[...], saq_ref[...], sbq_ref[...]) * (DIFF_HEAD_DIM ** -0.5 * LOG2E)
        lane = lax.broadcasted_iota(jnp.int32, q.shape, 1)
        qs.append(jnp.where(lane < DIFF_HEAD_DIM, q, 0.0).astype(BF16))
        qs.append(jnp.where(lane >= DIFF_HEAD_DIM, q, 0.0).astype(BF16))
    lv = lam_ref[...]
    lam = (jnp.exp(jnp.sum(lv[0:1] * lv[1:2], axis=-1, keepdims=True))
           - jnp.exp(jnp.sum(lv[2:3] * lv[3:4], axis=-1, keepdims=True)) + lam_init)

    def attend(nk):
        outs = _attend_streams(qs, [kt_ref.at[h] for h in heads for _ in range(2)],
                               [vo_ref.at[h] for h in heads for _ in range(2)], nk)
        for h in heads:
            (o1, l1), (o2, l2) = outs[2 * h], outs[2 * h + 1]
            o = o1 * (1.0 / l1) - o2 * (lam / l2)
            o_ref[:, h * LANES:(h + 1) * LANES] = (_rms(o) * sub_ref[...] * (1.0 - lam_init)).astype(o_ref.dtype)

    @pl.when(qi < n_ctx_tiles)
    def _():
        attend(n_ctx)

    @pl.when(qi >= n_ctx_tiles)
    def _():
        attend(k_ref.shape[0])


def diff_attention(p, lam_vecs, q_gain, k_gain, subln, rope, *, tq, n_ctx, lam_init, q_blk0, k_blk0, v_blk0):
    b, t, _ = p.shape
    c, sa, sb = rope
    nq = t // tq
    hps = DIFF_HEADS_PER_STEP
    assert q_blk0 % hps == 0 and k_blk0 % hps == 0 and v_blk0 % hps == 0
    row_q = lambda bi, h, qi: (qi, 0)
    full = lambda bi, h, qi: (0, 0)
    tile2 = lambda g: jnp.tile(g.astype(F32), 2).reshape(1, LANES)
    return pl.pallas_call(
        functools.partial(_diff_attn_kernel, n_ctx=n_ctx, n_ctx_tiles=n_ctx // tq, lam_init=lam_init),
        grid=(b, DIFF_HEADS // hps, nq),
        in_specs=[
            pl.BlockSpec((4, DIFF_HEAD_DIM), full),
            pl.BlockSpec((None, tq, hps * LANES), lambda bi, h, qi: (bi, qi, q_blk0 // hps + h)),
            pl.BlockSpec((None, t, hps * LANES), lambda bi, h, qi: (bi, 0, k_blk0 // hps + h)),
            pl.BlockSpec((None, t, hps * LANES), lambda bi, h, qi: (bi, 0, v_blk0 // hps + h)),
            pl.BlockSpec((tq, LANES), row_q), pl.BlockSpec((tq, LANES), row_q), pl.BlockSpec((tq, LANES), row_q),
            pl.BlockSpec((t, LANES), full), pl.BlockSpec((t, LANES), full), pl.BlockSpec((t, LANES), full),
            pl.BlockSpec((1, LANES), full), pl.BlockSpec((1, LANES), full), pl.BlockSpec((1, LANES), full),
        ],
        out_specs=pl.BlockSpec((None, tq, hps * LANES), lambda bi, h, qi: (bi, qi, h)),
        out_shape=jax.ShapeDtypeStruct((b, t, DIFF_HEADS * LANES), BF16),
        scratch_shapes=[pltpu.VMEM((hps, LANES, t), BF16), pltpu.VMEM((hps, t, 2 * LANES), BF16)],
        compiler_params=_cparams(("arbitrary", "arbitrary", "arbitrary")),
        name="diff_attention",
    )(lam_vecs.astype(F32), p, p, p, c, sa, sb, c, sa, sb, tile2(q_gain), tile2(k_gain),
      subln.astype(F32).reshape(1, LANES))


def _gla_constants(c):
    levels = int(math.log2(c))
    t = np.arange(c)[:, None]
    u = np.arange(c)[None, :]
    stack = [[u <= t], [u >= t]]
    qside = [[], []]
    pair = [[t == u], [t == u]]
    for lv in range(1, levels + 1):
        up_t = ((t >> (lv - 1)) & 1) == 1
        up_u = ((u >> (lv - 1)) & 1) == 1
        same = (t >> lv) == (u >> lv)
        qside[0].append(np.broadcast_to(up_t, (c, LANES)))
        qside[1].append(np.broadcast_to(~up_t, (c, LANES)))
        pair[0].append(same & up_t & ~up_u)
        pair[1].append(same & ~up_t & up_u)
    f32 = lambda x: np.asarray(x, np.float32)
    stack = np.stack([np.tile(f32(m[0]), (1, 3)) for m in stack])
    return (jnp.asarray(stack, BF16), jnp.asarray(np.stack([f32(m) for m in qside])),
            jnp.asarray(np.stack([f32(m) for m in pair])))


def _split_values(cum, d):
    c, w = cum.shape
    levels = int(math.log2(c))
    g = c // 8
    cum3 = cum.reshape(g, 8, w)
    sub = lax.broadcasted_iota(jnp.int32, (g, 8, w), 1)
    row_of_group = lambda off: jnp.broadcast_to(cum3[:, off:off + 1, :], (g, 8, w))
    if d == 0:
        pair_level = jnp.where((sub & 1) == 1, pltpu.roll(cum3, 1, 1), cum3)
        quad_level = jnp.where(sub < 4, row_of_group(1), row_of_group(5))
        oct_level = row_of_group(3)
    else:
        pair_level = jnp.where((sub & 1) == 0, pltpu.roll(cum3, 7, 1), cum3)
        quad_level = jnp.where(sub < 4, row_of_group(2), row_of_group(6))
        oct_level = row_of_group(4)
    out = [x.reshape(c, w) for x in (pair_level, quad_level, oct_level)]
    for lv in range(4, levels + 1):
        size = 1 << lv
        at = size // 2 - 1 if d == 0 else size // 2
        out.append(jnp.concatenate([jnp.broadcast_to(cum[b0 + at:b0 + at + 1, :], (size, w))
                                    for b0 in range(0, c, size)], axis=0))
    return out


def _gla_chunk(q, k, v, g2, s, stack, qside_ref, pair_ref, d):
    c = q.shape[0]
    levels = int(math.log2(c))
    gcat = jnp.concatenate(_split3(g2), axis=0)
    cum = _dot(stack, gcat)
    tot_col = _dot_tn(gcat, jnp.ones((3 * c, LANES), BF16))
    diag = jnp.sum(q * k, axis=-1, keepdims=True)
    yield
    tot = cum[0:1] if d == 1 else cum[c - 1:c]
    split = _split_values(cum, d)
    zz = []
    for lv in range(1, levels + 1):
        e = jnp.exp2(-jnp.abs(cum - split[lv - 1]))
        z = (jnp.where(qside_ref[d, lv - 1] > 0.5, q, k) * e).astype(BF16)
        zz.append(_dot_nt(z, z))
    q_in = (q * jnp.exp2(cum)).astype(BF16)
    ks = (k * jnp.exp2(tot - cum)).astype(BF16)
    ds = _dot_tn(ks, v)
    yield
    a = pair_ref[d, 0] * diag
    for lv in range(1, levels + 1):
        a = a + pair_ref[d, lv] * zz[lv - 1]
    lhs = jnp.concatenate([q_in, a.astype(BF16)], axis=1)
    o = _dot(lhs, jnp.concatenate([s.astype(BF16), v], axis=0))
    s_new = s * jnp.exp2(tot_col) + ds
    yield
    return o, s_new


def _run_interleaved(gens):
    results = [None] * len(gens)
    live = list(range(len(gens)))
    while live:
        for i in list(live):
            try:
                next(gens[i])
            except StopIteration as stop:
                results[i] = stop.value
                live.remove(i)
    return results


def _hgrn_kernel(qz_ref, zf_ref, zb_ref, v_ref, gz_ref, lbf_ref, lbb_ref, og_ref, stack_ref, qside_ref, pair_ref,
                 o_ref, q_ref, kf_ref, gf_ref, kb_ref, gb_ref, of_ref, ob_ref, st_ref, *, n_ctx_chunks):
    c = GLA_CHUNK
    t = qz_ref.shape[0]
    n = t // c
    heads = range(HGRN_HEADS_PER_STEP)
    qz = qz_ref[...].astype(F32)
    q_ref[...] = qz * _sigmoid(qz) * (HGRN_K_DIM ** -0.5)
    for z_ref, lb_ref, k_ref, g_ref in ((zf_ref, lbf_ref, kf_ref, gf_ref), (zb_ref, lbb_ref, kb_ref, gb_ref)):
        lb = jnp.concatenate([lb_ref[h] for h in heads], axis=-1)
        f = lb + (1.0 - lb) * _sigmoid(z_ref[...].astype(F32))
        k_ref[...] = 1.0 - f
        g_ref[...] = jnp.log(f) * LOG2E
    st_ref[...] = jnp.zeros_like(st_ref)

    def body(i, carry):
        rf = pl.multiple_of(i * c, c)
        cb = jnp.where(i < n_ctx_chunks, n_ctx_chunks - 1 - i, n - 1 - i + n_ctx_chunks)
        rb = pl.multiple_of(cb * c, c)
        chains = [(h, d, pl.ds(r0, c), k_ref, g_ref, out_ref) for h in heads
                  for d, r0, k_ref, g_ref, out_ref in ((0, rf, kf_ref, gf_ref, of_ref), (1, rb, kb_ref, gb_ref, ob_ref))]
        col = lambda h: slice(h * LANES, (h + 1) * LANES)
        outs = _run_interleaved([
            _gla_chunk(q_ref[rows, col(h)], k_ref[rows, col(h)], v_ref[rows, col(h)], g_ref[rows, col(h)],
                       st_ref[h, d], stack_ref[d], qside_ref, pair_ref, d)
            for (h, d, rows, k_ref, g_ref, _) in chains])
        for (h, d, rows, _, _, out_ref), (o, s_new) in zip(chains, outs):
            out_ref[rows, col(h)] = o
            st_ref[h, d] = s_new
        return carry

    lax.fori_loop(0, n, body, 0)
    gz = gz_ref[...].astype(F32)
    for h in heads:
        cols = slice(h * LANES, (h + 1) * LANES)
        o = _rms(of_ref[:, cols] + ob_ref[:, cols]) * og_ref[...]
        o_ref[:, cols] = (o * (gz[:, cols] * _sigmoid(gz[:, cols]))).astype(o_ref.dtype)


def hgrn_bidirectional(p, lb_fwd, lb_bwd, out_gain, *, n_ctx, blk0):
    b, t, _ = p.shape
    c = GLA_CHUNK
    consts = _gla_constants(c)
    h = HGRN_HEADS
    hps = HGRN_HEADS_PER_STEP
    assert blk0 % hps == 0 and h % hps == 0
    seg = lambda s: pl.BlockSpec((None, t, hps * LANES), lambda bi, hi: (bi, 0, (blk0 + s * h) // hps + hi))
    per_head = pl.BlockSpec((hps, 1, LANES), lambda bi, hi: (hi, 0, 0))
    const = lambda a: pl.BlockSpec(a.shape, lambda bi, hi: (0,) * a.ndim)
    seq = pltpu.VMEM((t, hps * LANES), F32)
    return pl.pallas_call(
        functools.partial(_hgrn_kernel, n_ctx_chunks=n_ctx // c),
        grid=(b, h // hps),
        in_specs=[seg(0), seg(1), seg(2), seg(3), seg(4), per_head, per_head,
                  pl.BlockSpec((1, LANES), lambda bi, hi: (0, 0))] + [const(a) for a in consts],
        out_specs=pl.BlockSpec((None, t, hps * LANES), lambda bi, hi: (bi, 0, hi)),
        out_shape=jax.ShapeDtypeStruct((b, t, h * LANES), BF16),
        scratch_shapes=[seq, seq, seq, seq, seq, seq, seq, pltpu.VMEM((hps, 2, LANES, LANES), F32)],
        compiler_params=_cparams(("arbitrary", "arbitrary")),
        name="hgrn_bidirectional",
    )(p, p, p, p, p, lb_fwd.reshape(h, 1, LANES), lb_bwd.reshape(h, 1, LANES),
      out_gain.astype(F32).reshape(1, LANES), *consts)


def _mla_attn_kernel(q_ref, kv_ref, kr_ref, cq_ref, saq_ref, sbq_ref, ck_ref, sak_ref, sbk_ref,
                     qn_ref, qr_ref, kn_ref, krg_ref, o_ref, kt_ref, vo_ref):
    qi = pl.program_id(2)
    ones_nope = _block_ones(MLA_NOPE)
    ones_rope = _block_ones(MLA_ROPE)
    hw = 2 * LANES

    @pl.when(qi == 0)
    def _():
        kr = _block_rms(kr_ref[...].astype(F32), ones_rope, MLA_ROPE) * krg_ref[...]
        kr = _rope(kr, ck_ref[...], sak_ref[...], sbk_ref[...]).T.astype(BF16)
        for h in range(MLA_HEADS_PER_STEP):
            kn = _block_rms(kv_ref[:, h * hw:h * hw + MLA_NOPE].astype(F32), ones_nope, MLA_NOPE) * kn_ref[...]
            kt_ref[h, 0:LANES, :] = kn.T.astype(BF16)
            kt_ref[h, LANES:hw, :] = kr
            vo_ref[h, :, 0:LANES] = kv_ref[:, h * hw + MLA_NOPE:(h + 1) * hw]
            vo_ref[h, :, LANES:2 * LANES] = jnp.ones((kv_ref.shape[0], LANES), BF16)

    scale = (MLA_NOPE + MLA_ROPE) ** -0.5 * LOG2E
    qs = []
    for h in range(MLA_HEADS_PER_STEP):
        qn = _block_rms(q_ref[:, h * hw:h * hw + MLA_NOPE].astype(F32), ones_nope, MLA_NOPE) * (qn_ref[...] * scale)
        qr = _block_rms(q_ref[:, h * hw + MLA_NOPE:(h + 1) * hw].astype(F32), ones_rope, MLA_ROPE) * qr_ref[...]
        qr = _rope(qr, cq_ref[...], saq_ref[...], sbq_ref[...]) * scale
        qs.append(jnp.concatenate([qn.astype(BF16), qr.astype(BF16)], axis=-1))
    heads = range(MLA_HEADS_PER_STEP)
    outs = _attend_streams(qs, [kt_ref.at[h] for h in heads], [vo_ref.at[h] for h in heads], kt_ref.shape[2])
    for h, (o, l) in enumerate(outs):
        o_ref[:, h * MLA_V:(h + 1) * MLA_V] = (o * (1.0 / l)).astype(o_ref.dtype)


def mla_attention(q, kv, p1, kr_blk, rope, qn_gain, qr_gain, kn_gain, kr_gain, *, tq, n_ctx):
    b, n_lat, _ = q.shape
    t = kv.shape[1]
    c, sa, sb = rope
    hps = MLA_HEADS_PER_STEP
    row_q = lambda bi, h, qi: (qi, 0)
    full = lambda bi, h, qi: (0, 0)
    pad = lambda g: jnp.concatenate([g.astype(F32), jnp.zeros((LANES - g.shape[0],), F32)]).reshape(1, LANES)
    return pl.pallas_call(
        _mla_attn_kernel,
        grid=(b, MLA_HEADS // hps, n_lat // tq),
        in_specs=[
            pl.BlockSpec((None, tq, hps * 2 * LANES), lambda bi, h, qi: (bi, qi, h)),
            pl.BlockSpec((None, t, hps * 2 * LANES), lambda bi, h, qi: (bi, 0, h)),
            pl.BlockSpec((None, t, LANES), lambda bi, h, qi: (bi, 0, kr_blk)),
            pl.BlockSpec((tq, LANES), row_q), pl.BlockSpec((tq, LANES), row_q), pl.BlockSpec((tq, LANES), row_q),
            pl.BlockSpec((t, LANES), full), pl.BlockSpec((t, LANES), full), pl.BlockSpec((t, LANES), full),
            pl.BlockSpec((1, LANES), full), pl.BlockSpec((1, LANES), full),
            pl.BlockSpec((1, LANES), full), pl.BlockSpec((1, LANES), full),
        ],
        out_specs=pl.BlockSpec((None, tq, hps * MLA_V), lambda bi, h, qi: (bi, qi, h)),
        out_shape=jax.ShapeDtypeStruct((b, n_lat, MLA_HEADS * MLA_V), BF16),
        scratch_shapes=[pltpu.VMEM((hps, 2 * LANES, t), BF16), pltpu.VMEM((hps, t, 2 * LANES), BF16)],
        compiler_params=_cparams(("arbitrary", "arbitrary", "arbitrary")),
        name="mla_attention",
    )(q, kv, p1, c[n_ctx:], sa[n_ctx:], sb[n_ctx:], c, sa, sb, pad(qn_gain), pad(qr_gain), pad(kn_gain), pad(kr_gain))


PAIRS = [(i, j) for i in range(EXPERTS_PER_GROUP) for j in range(i + 1, EXPERTS_PER_GROUP)]
N_CLASSES = N_GROUPS * len(PAIRS)
CLASS_LO = np.array([EXPERTS_PER_GROUP * g + i for g in range(N_GROUPS) for (i, j) in PAIRS], np.int32)
CLASS_HI = np.array([EXPERTS_PER_GROUP * g + j for g in range(N_GROUPS) for (i, j) in PAIRS], np.int32)
EXPERT_TILE = 256
ROW_SLABS = 8
DMA_UNROLL = 8
ROUTE_BATCH = 4
ROW_SLOTS = 6


def _route_kernel(*refs, n_in, n_x, n_ctx_tiles, row_tile0):
    a_refs = refs[:n_in]
    w_refs = refs[n_in:2 * n_in]
    x_refs = refs[2 * n_in:2 * n_in + n_x]
    m_ref, g_ref, rw_ref, rb_ref, xmid_ref, hf_ref, rows_ref, info_ref, count_ref, cnt_ref = refs[2 * n_in + n_x:]
    w1, w2, _ = _split3(rw_ref[...])
    is_ctx = pl.program_id(1) + row_tile0 < n_ctx_tiles
    logits = []
    for e in range(ROUTE_BATCH):
        acc = _dot(a_refs[0][e], w_refs[0][...])
        for a_ref, w_ref in zip(a_refs[1:], w_refs[1:]):
            acc = acc + _dot(a_ref[e], w_ref[...])
        x_in = x_refs[0][e] if n_x == 1 else jnp.where(is_ctx, x_refs[0][e], x_refs[1][e])
        x_mid = x_in + m_ref[e, G_M:G_M + 1, :] * acc
        xmid_ref[e] = x_mid
        h = _rms(x_mid) * g_ref[...]
        h = h * (1.0 + m_ref[e, SC_F:SC_F + 1, :]) + m_ref[e, SH_F:SH_F + 1, :]
        hf_ref[e] = h.astype(hf_ref.dtype)
        rows_ref[e] = _rows_to_slabs(h)
        h1, h2, _ = _split3(h)
        logits.append(_dot_nt(w1, h1) + (_dot_nt(w1, h2) + _dot_nt(w2, h1)))
    biased = _sigmoid(jnp.concatenate(logits, axis=1)) + rb_ref[...]
    row = [biased[e:e + 1, :] for e in range(N_EXPERTS)]
    gscore = []
    for g in range(N_GROUPS):
        m = row[4 * g:4 * g + 4]
        gscore.append(functools.reduce(jnp.maximum, [m[i] + m[j] for (i, j) in PAIRS]))
    hits = []
    for g in range(N_GROUPS):
        best = None
        for g2 in range(N_GROUPS):
            if g2 == g:
                continue
            wins = (gscore[g] > gscore[g2]) if g2 < g else (gscore[g] >= gscore[g2])
            best = wins if best is None else jnp.logical_and(best, wins)
        chosen = []
        for i in range(EXPERTS_PER_GROUP):
            rank = None
            for j in range(EXPERTS_PER_GROUP):
                if j == i:
                    continue
                mi, mj = row[4 * g + i], row[4 * g + j]
                ahead = ((mj >= mi) if j < i else (mj > mi)).astype(jnp.int32)
                rank = ahead if rank is None else rank + ahead
            chosen.append(rank < 2)
        for (i, j) in PAIRS:
            hits.append(jnp.where(best & chosen[i] & chosen[j], 1.0, 0.0))
    onehot = jnp.concatenate(hits, axis=0)
    nt = onehot.shape[1]
    upper = (lax.broadcasted_iota(jnp.int32, (nt, nt), 0) <= lax.broadcasted_iota(jnp.int32, (nt, nt), 1))
    prefix = _dot(onehot.astype(BF16), upper.astype(BF16))

    @pl.when((pl.program_id(0) == 0) & (pl.program_id(1) == 0))
    def _():
        cnt_ref[...] = jnp.zeros_like(cnt_ref)

    seen = cnt_ref[...]
    cls_id = lax.broadcasted_iota(jnp.int32, onehot.shape, 0).astype(F32)
    cls = jnp.sum(onehot * cls_id, axis=0, keepdims=True)
    rank = jnp.sum(onehot * (seen[:, 0:1] + prefix - 1.0), axis=0, keepdims=True)
    info = jnp.concatenate([cls, rank, jnp.zeros((6, nt), F32)], axis=0).astype(jnp.int32)
    tm = nt // ROUTE_BATCH
    for e in range(ROUTE_BATCH):
        info_ref[e] = info[:, e * tm:(e + 1) * tm]
    seen = seen + jnp.sum(onehot, axis=1, keepdims=True)
    cnt_ref[...] = seen
    count_ref[...] = seen


def route(acts, weights, xs, mods, gain, router_w, router_bias, *, tm, n_ctx_tiles, row_tile0):
    b, r, _ = acts[0].shape
    nt = r // tm
    rb = ROUTE_BATCH
    assert b % rb == 0
    parts = xs if isinstance(xs, tuple) else (xs,)
    d = parts[0].shape[-1]
    if len(parts) == 1:
        x_specs = [pl.BlockSpec((rb, tm, d), lambda bi, ti: (bi, ti + row_tile0, 0))]
    else:
        x_specs = [pl.BlockSpec((rb, tm, d), lambda bi, ti: (bi, jnp.minimum(ti + row_tile0, n_ctx_tiles - 1), 0)),
                   pl.BlockSpec((rb, tm, d), lambda bi, ti: (bi, jnp.maximum(ti + row_tile0 - n_ctx_tiles, 0), 0))]
    row = lambda bi, ti: (bi, ti, 0)
    const = lambda bi, ti: (0, 0)
    in_specs = [pl.BlockSpec((rb, tm, a.shape[-1]), row) for a in acts]
    in_specs += [pl.BlockSpec(w.shape, const) for w in weights]
    in_specs += x_specs + [
        pl.BlockSpec((rb, None, 6, d), lambda bi, ti: (bi, ((ti + row_tile0) >= n_ctx_tiles).astype(jnp.int32), 0, 0)),
        pl.BlockSpec((1, d), const), pl.BlockSpec((N_EXPERTS, d), const), pl.BlockSpec((N_EXPERTS, 1), const)]
    x_mid, hf, rows, info, counts = pl.pallas_call(
        functools.partial(_route_kernel, n_in=len(acts), n_x=len(parts), n_ctx_tiles=n_ctx_tiles,
                          row_tile0=row_tile0),
        grid=(b // rb, nt),
        in_specs=in_specs,
        out_specs=[
            pl.BlockSpec((rb, tm, d), row),
            pl.BlockSpec((rb, tm, d), row),
            pl.BlockSpec((rb, tm, ROW_SLABS, LANES), lambda bi, ti: (bi, ti, 0, 0)),
            pl.BlockSpec((rb, None, 8, tm), lambda bi, ti: (bi, ti, 0, 0)),
            pl.BlockSpec((N_CLASSES, LANES), const),
        ],
        out_shape=[jax.ShapeDtypeStruct((b, r, d), F32),
                   jax.ShapeDtypeStruct((b, r, d), BF16),
                   jax.ShapeDtypeStruct((b, r, ROW_SLABS, LANES), F32),
                   jax.ShapeDtypeStruct((b, nt, 8, tm), jnp.int32),
                   jax.ShapeDtypeStruct((N_CLASSES, LANES), F32)],
        scratch_shapes=[pltpu.VMEM((N_CLASSES, LANES), F32)],
        compiler_params=_cparams(("arbitrary", "arbitrary")),
        name="route",
    )(*acts, *weights, *parts, mods, gain.reshape(1, d).astype(F32), router_w.T.astype(F32),
      router_bias.reshape(N_EXPERTS, 1).astype(F32))
    return x_mid, hf, rows.reshape(b * r, ROW_SLABS, LANES), info, counts


def _transpose8(parts):
    sub = lax.broadcasted_iota(jnp.int32, parts[0].shape, 1)
    for s in (4, 2, 1):
        keep = (sub & s) == 0
        new = list(parts)
        for i in range(8):
            if i & s:
                continue
            a, b = parts[i], parts[i | s]
            new[i] = jnp.where(keep, a, pltpu.roll(b, s, 1))
            new[i | s] = jnp.where(keep, pltpu.roll(a, 8 - s, 1), b)
        parts = new
    return parts


def _slabs_to_rows(slabs):
    g = slabs.shape[0] // 8
    x4 = slabs.reshape(g, 8, ROW_SLABS, LANES)
    parts = _transpose8([x4[:, t] for t in range(8)])
    return jnp.concatenate([p.reshape(g * 8, LANES) for p in parts], axis=-1)


def _rows_to_slabs(x):
    g = x.shape[0] // 8
    parts = _transpose8([x[:, k * LANES:(k + 1) * LANES].reshape(g, 8, LANES) for k in range(ROW_SLABS)])
    return jnp.stack(parts, axis=1).reshape(g * 8, ROW_SLABS, LANES)


def _invert_kernel(dest_ref, src_ref):
    def clear(i, carry):
        for u in range(2 * DMA_UNROLL):
            src_ref[i * 2 * DMA_UNROLL + u] = 0
        return carry

    def put(i, carry):
        for u in range(2 * DMA_UNROLL):
            t = i * 2 * DMA_UNROLL + u
            src_ref[dest_ref[t]] = t
        return carry

    lax.fori_loop(0, src_ref.shape[0] // (2 * DMA_UNROLL), clear, 0)
    lax.fori_loop(0, dest_ref.shape[0] // (2 * DMA_UNROLL), put, 0)


def invert_permutation(dest, n_rows_out):
    smem = pl.BlockSpec(memory_space=pltpu.SMEM)
    return pl.pallas_call(
        _invert_kernel,
        in_specs=[smem],
        out_specs=smem,
        out_shape=jax.ShapeDtypeStruct((n_rows_out,), jnp.int32),
        name="moe_invert",
    )(dest)


def _expert_kernel(te_ref, tv_ref, nx_ref, src_ref, rows_hbm, rwl_ref, rwh_ref, wg_hbm, wu_hbm, wd_hbm, y_ref,
                   xbuf_ref, sem_ref, stage_g, stage_u, stage_d, cache_g, cache_u, cache_d, wsem_ref, *, layer):
    i = pl.program_id(0)
    n = pl.num_programs(0)
    tg = y_ref.shape[0]
    slot = i % ROW_SLOTS
    valid = tv_ref[i] != 0
    prev = jnp.maximum(i - 1, 0)

    def fetch(tile, s):
        base = tile * tg
        for t in range(tg):
            pltpu.async_copy(rows_hbm.at[src_ref[base + t]], xbuf_ref.at[s, t], sem_ref.at[s], priority=t % 2)

    def wait_rows(s):
        pltpu.make_async_copy(rows_hbm.at[pl.ds(0, tg)], xbuf_ref.at[s], sem_ref.at[s]).wait()

    def weight_copies(s, e):
        return [pltpu.make_async_copy(w_hbm.at[layer, e], stage.at[s], wsem_ref.at[s])
                for w_hbm, stage in ((wg_hbm, stage_g), (wu_hbm, stage_u), (wd_hbm, stage_d))]

    @pl.when(i == 0)
    def _():
        for k in range(ROW_SLOTS - 1):
            fetch(jnp.minimum(k, n - 1), k)
        for s in range(2):
            for copy in weight_copies(s, te_ref[s * n]):
                copy.start()

    for s in range(2):
        expert = te_ref[s * n + i]

        @pl.when(valid & ((i == 0) | (expert != te_ref[s * n + prev])))
        def _():
            for copy in weight_copies(s, expert):
                copy.wait()
            cache_g[s] = stage_g[s].astype(BF16)
            cache_u[s] = stage_u[s].astype(BF16)
            cache_d[s] = stage_d[s].astype(BF16)
            upcoming = nx_ref[s * n + i]

            @pl.when(upcoming >= 0)
            def _():
                for copy in weight_copies(s, upcoming):
                    copy.start()

    @pl.when(valid)
    def _():
        wait_rows(slot)
        x = _slabs_to_rows(xbuf_ref[slot])
        fetch(jnp.minimum(i + ROW_SLOTS - 1, n - 1), (i + ROW_SLOTS - 1) % ROW_SLOTS)
        xb = x.astype(BF16)
        s_lo = _sigmoid(jnp.sum(x * rwl_ref[...], axis=-1, keepdims=True))
        s_hi = _sigmoid(jnp.sum(x * rwh_ref[...], axis=-1, keepdims=True))
        inv = 1.0 / (s_lo + s_hi)

        def ffn(s):
            a = _dot(xb, cache_g[s])
            u = _dot(xb, cache_u[s])
            return _dot((a * _sigmoid(a) * u).astype(BF16), cache_d[s])

        y_ref[...] = _rows_to_slabs((s_lo * inv) * ffn(0) + (s_hi * inv) * ffn(1))

    @pl.when(jnp.logical_not(valid))
    def _():
        y_ref[...] = jnp.zeros_like(y_ref)

    @pl.when((valid & (i == n - 1)) | (jnp.logical_not(valid) & (tv_ref[prev] != 0)))
    def _():
        first = jnp.where(valid, i + 1, i)
        for k in range(ROW_SLOTS - 1):
            wait_rows((first + k) % ROW_SLOTS)


def experts(tile_expert, tile_valid, next_expert, src, rows, router_w_t, wg, wu, wd, layer):
    tg = EXPERT_TILE
    n_tiles = src.shape[0] // tg
    d, ff = wg.shape[2:]
    lo = lambda i, te, tv, nx, src: (te[i], 0, 0)
    hi = lambda i, te, tv, nx, src: (te[n_tiles + i], 0, 0)
    hbm = pl.BlockSpec(memory_space=pl.ANY)
    return pl.pallas_call(
        functools.partial(_expert_kernel, layer=layer),
        grid_spec=pltpu.PrefetchScalarGridSpec(
            num_scalar_prefetch=4,
            grid=(n_tiles,),
            in_specs=[hbm, pl.BlockSpec((None, 1, d), lo), pl.BlockSpec((None, 1, d), hi), hbm, hbm, hbm],
            out_specs=pl.BlockSpec((tg, ROW_SLABS, LANES), lambda i, te, tv, nx, src: (i, 0, 0)),
            scratch_shapes=[pltpu.VMEM((ROW_SLOTS, tg, ROW_SLABS, LANES), F32), pltpu.SemaphoreType.DMA((ROW_SLOTS,)),
                            pltpu.VMEM((2, d, ff), F32), pltpu.VMEM((2, d, ff), F32), pltpu.VMEM((2, ff, d), F32),
                            pltpu.VMEM((2, d, ff), BF16), pltpu.VMEM((2, d, ff), BF16), pltpu.VMEM((2, ff, d), BF16),
                            pltpu.SemaphoreType.DMA((2,))],
        ),
        out_shape=jax.ShapeDtypeStruct((n_tiles * tg, ROW_SLABS, LANES), F32),
        compiler_params=_cparams(("arbitrary",)),
        name="moe_experts",
    )(tile_expert, tile_valid, next_expert, src, rows, router_w_t, router_w_t, wg, wu, wd)


def _combine_kernel(dest_ref, hf_ref, x_ref, m_ref, wg_ref, wu_ref, wd_ref, y_hbm, o_ref, ybuf_ref, sem_ref):
    i = pl.program_id(0)
    n = pl.num_programs(0)
    tm = hf_ref.shape[0]
    slot = i % 2

    def fetch(tile, s):
        base = tile * tm
        for t in range(tm):
            pltpu.async_copy(y_hbm.at[dest_ref[base + t]], ybuf_ref.at[s, t], sem_ref.at[s], priority=t % 2)

    def wait_rows(s):
        pltpu.make_async_copy(y_hbm.at[pl.ds(0, tm)], ybuf_ref.at[s], sem_ref.at[s]).wait()

    @pl.when(i == 0)
    def _():
        fetch(0, 0)

    fetch(jnp.minimum(i + 1, n - 1), 1 - slot)
    h = hf_ref[...]
    a = _dot(h, wg_ref[...])
    u = _dot(h, wu_ref[...])
    shared = _dot((a * _sigmoid(a) * u).astype(BF16), wd_ref[...])
    wait_rows(slot)
    o_ref[...] = x_ref[...] + m_ref[G_F:G_F + 1, :] * (shared + _slabs_to_rows(ybuf_ref[slot]))

    @pl.when(i == n - 1)
    def _():
        wait_rows(1 - slot)


def combine(dest, hf, xs, mods, wg, wu, wd, y_sorted, *, tm, n_ctx_tiles):
    b, r, d = hf.shape
    nt = r // tm
    ff = wg.shape[1]
    row = lambda i, dest: (i // nt, i % nt, 0)
    const = lambda i, dest: (0, 0)
    return pl.pallas_call(
        _combine_kernel,
        grid_spec=pltpu.PrefetchScalarGridSpec(
            num_scalar_prefetch=1,
            grid=(b * nt,),
            in_specs=[
                pl.BlockSpec((None, tm, d), row),
                pl.BlockSpec((None, tm, d), row),
                pl.BlockSpec((None, None, 6, d),
                             lambda i, dest: (i // nt, ((i % nt) >= n_ctx_tiles).astype(jnp.int32), 0, 0)),
                pl.BlockSpec((d, ff), const), pl.BlockSpec((d, ff), const), pl.BlockSpec((ff, d), const),
                pl.BlockSpec(memory_space=pl.ANY),
            ],
            out_specs=pl.BlockSpec((None, tm, d), row),
            scratch_shapes=[pltpu.VMEM((2, tm, ROW_SLABS, LANES), F32), pltpu.SemaphoreType.DMA((2,))],
        ),
        out_shape=jax.ShapeDtypeStruct((b, r, d), F32),
        compiler_params=_cparams(("arbitrary",)),
        name="moe_combine",
    )(dest, hf, xs, mods, wg, wu, wd, y_sorted)


def moe_block(acts, weights, xs, mods, gain, router_w, router_bias, ew_gate, ew_up, ew_down, layer, sw_gate, sw_up,
              sw_down, *, tm, n_ctx_tiles, row_tile0):
    xs_mid, hf, rows, info, counts = route(acts, weights, xs, mods, gain, router_w, router_bias, tm=tm,
                                           n_ctx_tiles=n_ctx_tiles, row_tile0=row_tile0)
    b, r, d = xs_mid.shape
    n = b * r
    tg = EXPERT_TILE
    cls = info[:, :, 0, :].reshape(n)
    rank = info[:, :, 1, :].reshape(n)
    padded = ((counts[:, 0].astype(jnp.int32) + tg - 1) // tg) * tg
    ends = jnp.cumsum(padded)
    dest = (ends - padded)[cls] + rank
    n_tiles = n // tg + N_CLASSES
    tile_start = jnp.arange(n_tiles, dtype=jnp.int32) * tg
    tile_valid = tile_start < ends[-1]
    last_start = jnp.maximum(ends[-1] - tg, 0)
    start = jnp.where(tile_valid, tile_start, last_start)
    tile_cls = jnp.sum((ends[None, :] <= start[:, None]).astype(jnp.int32), axis=1)
    tile_cls = jnp.minimum(tile_cls, N_CLASSES - 1)
    slot_expert = jnp.stack([jnp.asarray(CLASS_LO)[tile_cls], jnp.asarray(CLASS_HI)[tile_cls]])
    tile_expert = slot_expert.reshape(-1)
    later = (tile_start[None, :] > tile_start[:, None]) & tile_valid[None, :]
    differs = slot_expert[:, None, :] != slot_expert[:, :, None]
    first = jnp.min(jnp.where(later[None] & differs, jnp.arange(n_tiles)[None, None, :], n_tiles), axis=-1)
    next_expert = jnp.where(first < n_tiles, jnp.take_along_axis(slot_expert, jnp.minimum(first, n_tiles - 1), axis=1),
                            -1).reshape(-1).astype(jnp.int32)
    src = invert_permutation(dest, n_tiles * tg)
    y_sorted = experts(tile_expert, tile_valid.astype(jnp.int32), next_expert, src, rows,
                       router_w.T.astype(F32).reshape(N_EXPERTS, 1, d), ew_gate, ew_up, ew_down, layer)
    return combine(dest, hf, xs_mid, mods, sw_gate.astype(BF16), sw_up.astype(BF16), sw_down.astype(BF16), y_sorted,
                   tm=tm, n_ctx_tiles=max(n_ctx_tiles - row_tile0, 0))


def kernel(x, c, ctx, c_ctx, mod_w, mod_b, norm_mix, norm_ffn, even_w_in, even_w_out, diff_q_gain, diff_k_gain, diff_lambda, diff_subln, hgrn_lb_logits, hgrn_out_gain, odd_w_in, mla_q_a_gain, mla_kv_a_gain, mla_w_uq, mla_w_ukv, mla_q_nope_gain, mla_q_rope_gain, mla_k_nope_gain, mla_k_rope_gain, odd_w_out, router_w, router_bias, expert_w_gate, expert_w_up, expert_w_down, shared_w_gate, shared_w_up, shared_w_down):
    b, n_lat, d = x.shape
    n_ctx = ctx.shape[1]
    depth = mod_w.shape[0]
    tm = 256 if n_ctx % 256 == 0 else 128
    n_ctx_tiles = n_ctx // tm
    rope = rope_lane_tables(n_ctx, n_lat)

    mod_rows = 16
    c_rows = jnp.concatenate([c, c_ctx[None, :], jnp.zeros((mod_rows - b - 1, d), F32)], axis=0)
    mod_all = modulation(c_rows, mod_w, mod_b).reshape(depth, mod_rows, 6, d)
    lb_all = jnp.cumsum(jax.nn.softmax(hgrn_lb_logits.astype(F32), axis=0), axis=0)

    xs = (ctx, x)
    for layer in range(depth):
        last = layer == depth - 1
        j = layer // 2
        mods = jnp.stack([jnp.broadcast_to(mod_all[layer, b], (b, 6, d)), mod_all[layer, :b]], axis=1)
        row_tile0 = n_ctx_tiles if last else 0
        if layer % 2 == 0:
            lam_init = 0.8 - 0.6 * math.exp(-0.3 * layer)
            p = norm_matmul(xs, mods, norm_mix[layer], even_w_in[j].astype(BF16), tm=tm, n_ctx_tiles=n_ctx_tiles)
            oa = diff_attention(p, diff_lambda[j], diff_q_gain[j], diff_k_gain[j], diff_subln[j], rope,
                                tq=tm, n_ctx=n_ctx, lam_init=lam_init, q_blk0=0, k_blk0=4, v_blk0=8)
            ob = hgrn_bidirectional(p, lb_all[j, 0], lb_all[j, 1], hgrn_out_gain[j], n_ctx=n_ctx, blk0=12)
            if last:
                oa, ob = oa[:, n_ctx:], ob[:, n_ctx:]
            w_out = even_w_out[j].astype(BF16)
            half = oa.shape[-1]
            acts, weights = [oa, ob], [w_out[:half], w_out[half:]]
        else:
            w_in = odd_w_in[j]
            zpad = jnp.zeros((d, LANES - MLA_ROPE), F32)
            w_in_r = jnp.concatenate([w_in[:, MLA_Q_LORA:MLA_Q_LORA + MLA_KV_LORA], w_in[:, MLA_Q_LORA + MLA_KV_LORA:],
                                      zpad, w_in[:, :MLA_Q_LORA]], axis=1).astype(BF16)
            w_uq = mla_w_uq[j].reshape(MLA_Q_LORA, MLA_HEADS, MLA_NOPE + MLA_ROPE)
            w_uq = jnp.concatenate([w_uq, jnp.zeros((MLA_Q_LORA, MLA_HEADS, LANES - MLA_ROPE), F32)], axis=-1)
            w_uq = w_uq.reshape(MLA_Q_LORA, MLA_HEADS * 2 * LANES).astype(BF16)
            if not last:
                raise NotImplementedError("context queries for a non-final latent-attention layer")
            kr, q, kv = mla_projections(xs, mods, norm_mix[layer], w_in_r, mla_q_a_gain[j], mla_kv_a_gain[j], w_uq,
                                        mla_w_ukv[j].astype(BF16), tm=tm, n_ctx_tiles=n_ctx_tiles)
            o = mla_attention(q, kv, kr, 0, rope, mla_q_nope_gain[j], mla_q_rope_gain[j], mla_k_nope_gain[j],
                              mla_k_rope_gain[j], tq=2 * tm, n_ctx=n_ctx)
            acts, weights = [o], [odd_w_out[j].astype(BF16)]
        xs = moe_block(acts, weights, xs, mods, norm_ffn[layer], router_w, router_bias,
                       expert_w_gate, expert_w_up, expert_w_down, layer,
                       shared_w_gate[layer], shared_w_up[layer], shared_w_down[layer],
                       tm=tm, n_ctx_tiles=n_ctx_tiles, row_tile0=row_tile0)
    return xs if xs.shape[1] == n_lat else xs[:, n_ctx:]
```
